```python
import math
import jax, jax.numpy as jnp
from jax import lax
import numpy as np

D_MODEL = 1024
BATCH = 1
SEQ = 16384
DEPTH = 4
DEC_BATCH = 16
DEC_SEQ = 32
PAST_LEN = 2048

CHUNK = 64
QBLOCK = 128
N_MIXERS = 3
HEAD_DIM = 64
DIFF_HEADS = D_MODEL // (2 * HEAD_DIM)
SWA_Q_HEADS = D_MODEL // HEAD_DIM
SWA_KV_HEADS = 4
SWA_GROUP = SWA_Q_HEADS // SWA_KV_HEADS
WINDOW = 128
WIN_CHUNKS = WINDOW // CHUNK
SB_HEADS = D_MODEL // HEAD_DIM
N_BUCKETS = 32
MAX_DISTANCE = 128
BIAS_HEADS = 2 * DIFF_HEADS
D_FF = 4 * D_MODEL
EPS = 1e-6
N_DIFF = (DEPTH + 2) // 3
N_SWA = (DEPTH + 1) // 3
N_SB = DEPTH // 3

kernel_name = 'hybrid_streaming_diff_swa_stickbreak'


def rmsnorm(x, g):
    xf = x.astype(jnp.float32)
    y = xf * lax.rsqrt(jnp.mean(xf * xf, axis=-1, keepdims=True) + EPS)
    return (y * g.astype(jnp.float32)).astype(x.dtype)


def t5_bucket(rel):
    half = N_BUCKETS // 2
    exact = half // 2
    ret = jnp.where(rel > 0, half, 0).astype(jnp.int32)
    n = jnp.abs(rel)
    nf = jnp.maximum(n, 1).astype(jnp.float32)
    large = exact + (jnp.log(nf / exact) / math.log(MAX_DISTANCE / exact) * (half - exact)).astype(jnp.int32)
    large = jnp.minimum(large, half - 1)
    return ret + jnp.where(n < exact, n, large)


def pos_bias(rel_bias, rel):
    return jnp.moveaxis(rel_bias.astype(jnp.float32)[t5_bucket(rel)], -1, 0)


def split_blocks(x):
    b, t = x.shape[:2]
    return jnp.moveaxis(x.reshape(b, t // QBLOCK, QBLOCK, *x.shape[2:]), 1, 0)


def merge_blocks(o):
    nb, b, qb = o.shape[:3]
    return jnp.moveaxis(o, 0, 1).reshape(b, nb * qb, -1)


def mlp(h, w_up, w_down):
    return jnp.square(jax.nn.relu(h @ w_up)) @ w_down


def diff_project(h, w_qkv, q_gain, k_gain):
    b, t, _ = h.shape
    q, k, v = jnp.split(h @ w_qkv, 3, axis=-1)
    q = rmsnorm(q.reshape(b, t, DIFF_HEADS, 2, HEAD_DIM), q_gain) * (HEAD_DIM ** -0.5)
    k = rmsnorm(k.reshape(b, t, DIFF_HEADS, 2, HEAD_DIM), k_gain)
    return q, k, v.reshape(b, t, DIFF_HEADS, 2 * HEAD_DIM)


def diff_lambda_value(lam_p, lam_init):
    lp = lam_p.astype(jnp.float32)
    return jnp.exp(jnp.sum(lp[0] * lp[1])) - jnp.exp(jnp.sum(lp[2] * lp[3])) + lam_init


def diff_core(q, k, v, bias, mask, lam, lam_init, subln):
    logits = jnp.einsum('bqhmd,bshmd->bhmqs', q, k, preferred_element_type=jnp.float32)
    logits = logits + bias.reshape(DIFF_HEADS, 2, *bias.shape[1:])
    p = jax.nn.softmax(jnp.where(mask, logits, -jnp.inf), axis=-1)
    a = p[:, :, 0] - lam * p[:, :, 1]
    o = jnp.einsum('bhqs,bshe->bqhe', a, v.astype(jnp.float32))
    return rmsnorm(o, subln) * (1.0 - lam_init)


def diff_prompt(h, rel_bias, w_qkv, w_o, q_gain, k_gain, lam_p, subln, lam_init):
    b, t, _ = h.shape
    lam = diff_lambda_value(lam_p, lam_init)
    q, k, v = diff_project(h, w_qkv, q_gain, k_gain)
    s = jnp.arange(t)

    def block(args):
        qi, bi = args
        tq = bi * QBLOCK + jnp.arange(QBLOCK)
        mask = (s // CHUNK)[None, :] <= (tq // CHUNK)[:, None]
        return diff_core(qi, k, v, pos_bias(rel_bias, s[None, :] - tq[:, None]), mask, lam, lam_init, subln)

    o = merge_blocks(lax.map(block, (split_blocks(q), jnp.arange(t // QBLOCK))))
    y = o.astype(h.dtype) @ w_o
    return y, k.reshape(b, t, DIFF_HEADS, 2 * HEAD_DIM), v


def diff_sample(h, ck, cv, rel_bias, w_qkv, w_o, q_gain, k_gain, lam_p, subln, lam_init):
    b, t, _ = h.shape
    p_len = ck.shape[1]
    lam = diff_lambda_value(lam_p, lam_init)
    q, k, v = diff_project(h, w_qkv, q_gain, k_gain)
    kk = jnp.concatenate([ck.reshape(b, p_len, DIFF_HEADS, 2, HEAD_DIM), k], axis=1)
    vv = jnp.concatenate([cv, v], axis=1)
    s_pos = jnp.arange(p_len + t) - p_len
    t_pos = jnp.arange(t)
    mask = (s_pos // CHUNK)[None, :] <= (t_pos // CHUNK)[:, None]
    o = diff_core(q, kk, vv, pos_bias(rel_bias, s_pos[None, :] - t_pos[:, None]), mask, lam, lam_init, subln)
    y = o.reshape(b, t, D_MODEL).astype(h.dtype) @ w_o
    return y, k.reshape(b, t, DIFF_HEADS, 2 * HEAD_DIM), v


def swa_project(h, w_qkv, q_gain, k_gain):
    b, t, _ = h.shape
    nq = SWA_Q_HEADS * HEAD_DIM
    nkv = SWA_KV_HEADS * HEAD_DIM
    qkv = h @ w_qkv
    q = qkv[..., :nq].reshape(b, t, SWA_KV_HEADS, SWA_GROUP, HEAD_DIM)
    k = qkv[..., nq:nq + nkv].reshape(b, t, SWA_KV_HEADS, HEAD_DIM)
    v = qkv[..., nq + nkv:].reshape(b, t, SWA_KV_HEADS, HEAD_DIM)
    return rmsnorm(q, q_gain) * (HEAD_DIM ** -0.5), rmsnorm(k, k_gain), v


def swa_core(q, k, v, bias, mask, sinks):
    logits = jnp.einsum('...qkgd,...skd->...kgqs', q, k, preferred_element_type=jnp.float32)
    logits = logits + bias.reshape(SWA_KV_HEADS, SWA_GROUP, *bias.shape[1:])
    logits = jnp.where(mask, logits, -jnp.inf)
    sink = jnp.broadcast_to(sinks.astype(jnp.float32).reshape(SWA_KV_HEADS, SWA_GROUP, 1, 1),
                            logits.shape[:-1] + (1,))
    p = jax.nn.softmax(jnp.concatenate([logits, sink], axis=-1), axis=-1)[..., :-1]
    return jnp.einsum('...kgqs,...skd->...qkgd', p, v.astype(jnp.float32))


def swa_prompt(h, rel_bias, w_qkv, w_o, q_gain, k_gain, sinks):
    b, t, _ = h.shape
    nc = t // CHUNK
    q, k, v = swa_project(h, w_qkv, q_gain, k_gain)
    qc = q.reshape(b, nc, CHUNK, SWA_KV_HEADS, SWA_GROUP, HEAD_DIM)

    def band(x):
        xc = jnp.pad(x.reshape(b, nc, CHUNK, SWA_KV_HEADS, HEAD_DIM),
                     ((0, 0), (WIN_CHUNKS, 0), (0, 0), (0, 0), (0, 0)))
        return jnp.concatenate([xc[:, w:w + nc] for w in range(WIN_CHUNKS + 1)], axis=2)

    kb, vb = band(k), band(v)
    sb = (WIN_CHUNKS + 1) * CHUNK
    j = jnp.arange(sb)
    rel = j[None, :] - WIN_CHUNKS * CHUNK - jnp.arange(CHUNK)[:, None]
    valid = ((jnp.arange(nc)[:, None] - WIN_CHUNKS) * CHUNK + j[None, :]) >= 0
    mask = valid[None, :, None, None, None, :]
    o = swa_core(qc, kb, vb, pos_bias(rel_bias, rel), mask, sinks).reshape(b, t, SWA_Q_HEADS * HEAD_DIM)
    y = o.astype(h.dtype) @ w_o
    buf = min(WINDOW, t)
    return y, k[:, t - buf:], v[:, t - buf:]


def swa_sample(h, ck, cv, rel_bias, w_qkv, w_o, q_gain, k_gain, sinks):
    b, t, _ = h.shape
    buf = ck.shape[1]
    q, k, v = swa_project(h, w_qkv, q_gain, k_gain)
    kk = jnp.concatenate([ck, k], axis=1)
    vv = jnp.concatenate([cv, v], axis=1)
    s_pos = jnp.arange(buf + t) - buf
    t_pos = jnp.arange(t)
    sc, tc = s_pos // CHUNK, t_pos // CHUNK
    mask = (sc[None, :] <= tc[:, None]) & (sc[None, :] >= tc[:, None] - WIN_CHUNKS)
    o = swa_core(q, kk, vv, pos_bias(rel_bias, s_pos[None, :] - t_pos[:, None]), mask, sinks)
    y = o.reshape(b, t, SWA_Q_HEADS * HEAD_DIM).astype(h.dtype) @ w_o
    return y, kk[:, t:], vv[:, t:]


def sb_project(h, w_qkv):
    b, t, _ = h.shape
    q, k, v = jnp.split(h @ w_qkv, 3, axis=-1)
    shp = (b, t, SB_HEADS, HEAD_DIM)
    return q.reshape(shp), k.reshape(shp), v.reshape(shp)


def sb_core(q, k, v, mask):
    z = jnp.einsum('bqhd,bshd->bhqs', q, k, preferred_element_type=jnp.float32) * (HEAD_DIM ** -0.5)
    log_1m = jnp.where(mask, jax.nn.log_sigmoid(-z), 0.0)
    after = lax.cumsum(log_1m, axis=3, reverse=True) - log_1m
    a = jnp.where(mask, jnp.exp(jax.nn.log_sigmoid(z) + after), 0.0)
    return jnp.einsum('bhqs,bshd->bqhd', a, v.astype(jnp.float32))


def sb_prompt(h, w_qkv, w_o):
    b, t, _ = h.shape
    q, k, v = sb_project(h, w_qkv)
    s = jnp.arange(t)

    def block(args):
        qi, bi = args
        tq = bi * QBLOCK + jnp.arange(QBLOCK)
        return sb_core(qi, k, v, s[None, :] < tq[:, None])

    o = merge_blocks(lax.map(block, (split_blocks(q), jnp.arange(t // QBLOCK))))
    return o.astype(h.dtype) @ w_o, k, v


def sb_sample(h, ck, cv, w_qkv, w_o):
    b, t, _ = h.shape
    p_len = ck.shape[1]
    q, k, v = sb_project(h, w_qkv)
    kk = jnp.concatenate([ck, k], axis=1)
    vv = jnp.concatenate([cv, v], axis=1)
    s_pos = jnp.arange(p_len + t) - p_len
    mask = s_pos[None, :] < jnp.arange(t)[:, None]
    o = sb_core(q, kk, vv, mask).reshape(b, t, SB_HEADS * HEAD_DIM)
    return o.astype(h.dtype) @ w_o, k, v


def setup_inputs(seed: int = 0) -> dict:
    key = jax.random.key(seed)
    ks = jax.random.split(key, 26)

    def nrm(kk, shape, scale):
        return jax.random.normal(kk, shape, jnp.float32) * scale

    def gain(kk, shape):
        return 1.0 + 0.02 * jax.random.normal(kk, shape, jnp.float32)

    swa_buf = min(WINDOW, PAST_LEN)
    swa_qkv = (SWA_Q_HEADS + 2 * SWA_KV_HEADS) * HEAD_DIM
    return {
        'x_prompt': nrm(ks[0], (BATCH, SEQ, D_MODEL), 1.0),
        'x_sample': nrm(ks[1], (DEC_BATCH, DEC_SEQ, D_MODEL), 1.0),
        'cache_diff_k': nrm(ks[2], (N_DIFF, DEC_BATCH, PAST_LEN, DIFF_HEADS, 2 * HEAD_DIM), 1.0),
        'cache_diff_v': nrm(ks[3], (N_DIFF, DEC_BATCH, PAST_LEN, DIFF_HEADS, 2 * HEAD_DIM), 1.0),
        'cache_swa_k': nrm(ks[4], (N_SWA, DEC_BATCH, swa_buf, SWA_KV_HEADS, HEAD_DIM), 1.0),
        'cache_swa_v': nrm(ks[5], (N_SWA, DEC_BATCH, swa_buf, SWA_KV_HEADS, HEAD_DIM), 1.0),
        'cache_sb_k': nrm(ks[6], (N_SB, DEC_BATCH, PAST_LEN, SB_HEADS, HEAD_DIM), 1.0),
        'cache_sb_v': nrm(ks[7], (N_SB, DEC_BATCH, PAST_LEN, SB_HEADS, HEAD_DIM), 1.0),
        'rel_bias': nrm(ks[8], (N_BUCKETS, BIAS_HEADS), 0.5),
        'norm_mix': gain(ks[9], (DEPTH, D_MODEL)),
        'norm_mlp': gain(ks[10], (DEPTH, D_MODEL)),
        'w_up': nrm(ks[11], (DEPTH, D_MODEL, D_FF), D_MODEL ** -0.5),
        'w_down': nrm(ks[12], (DEPTH, D_FF, D_MODEL), D_FF ** -0.5),
        'diff_w_qkv': nrm(ks[13], (N_DIFF, D_MODEL, 3 * D_MODEL), D_MODEL ** -0.5),
        'diff_w_o': nrm(ks[14], (N_DIFF, D_MODEL, D_MODEL), D_MODEL ** -0.5),
        'diff_q_norm': gain(ks[15], (N_DIFF, HEAD_DIM)),
        'diff_k_norm': gain(ks[16], (N_DIFF, HEAD_DIM)),
        'diff_lambda': nrm(ks[17], (N_DIFF, 4, HEAD_DIM), 0.1),
        'diff_subln': gain(ks[18], (N_DIFF, 2 * HEAD_DIM)),
        'swa_w_qkv': nrm(ks[19], (N_SWA, D_MODEL, swa_qkv), D_MODEL ** -0.5),
        'swa_w_o': nrm(ks[20], (N_SWA, SWA_Q_HEADS * HEAD_DIM, D_MODEL), (SWA_Q_HEADS * HEAD_DIM) ** -0.5),
        'swa_q_norm': gain(ks[21], (N_SWA, HEAD_DIM)),
        'swa_k_norm': gain(ks[22], (N_SWA, HEAD_DIM)),
        'swa_sinks': nrm(ks[23], (N_SWA, SWA_Q_HEADS), 0.5),
        'sb_w_qkv': nrm(ks[24], (N_SB, D_MODEL, 3 * SB_HEADS * HEAD_DIM), D_MODEL ** -0.5),
        'sb_w_o': nrm(ks[25], (N_SB, SB_HEADS * HEAD_DIM, D_MODEL), (SB_HEADS * HEAD_DIM) ** -0.5),
    }


def reference(x_prompt, x_sample, cache_diff_k, cache_diff_v, cache_swa_k, cache_swa_v,
              cache_sb_k, cache_sb_v, rel_bias, norm_mix, norm_mlp, w_up, w_down,
              diff_w_qkv, diff_w_o, diff_q_norm, diff_k_norm, diff_lambda, diff_subln,
              swa_w_qkv, swa_w_o, swa_q_norm, swa_k_norm, swa_sinks, sb_w_qkv, sb_w_o):
    xp, xs = x_prompt, x_sample
    pdk, pdv, pwk, pwv, pbk, pbv = [], [], [], [], [], []
    sdk, sdv, swk, swv, sbk, sbv = [], [], [], [], [], []
    for i in range(DEPTH):
        j = i // N_MIXERS
        hp = rmsnorm(xp, norm_mix[i])
        hs = rmsnorm(xs, norm_mix[i])
        if i % N_MIXERS == 0:
            lam_init = 0.8 - 0.6 * math.exp(-0.3 * i)
            yp, kp, vp = diff_prompt(hp, rel_bias, diff_w_qkv[j], diff_w_o[j], diff_q_norm[j],
                                     diff_k_norm[j], diff_lambda[j], diff_subln[j], lam_init)
            ys, kn, vn = diff_sample(hs, cache_diff_k[j], cache_diff_v[j], rel_bias, diff_w_qkv[j],
                                     diff_w_o[j], diff_q_norm[j], diff_k_norm[j], diff_lambda[j],
                                     diff_subln[j], lam_init)
            pdk.append(kp); pdv.append(vp); sdk.append(kn); sdv.append(vn)
        elif i % N_MIXERS == 1:
            yp, kp, vp = swa_prompt(hp, rel_bias, swa_w_qkv[j], swa_w_o[j], swa_q_norm[j],
                                    swa_k_norm[j], swa_sinks[j])
            ys, kn, vn = swa_sample(hs, cache_swa_k[j], cache_swa_v[j], rel_bias, swa_w_qkv[j],
                                    swa_w_o[j], swa_q_norm[j], swa_k_norm[j], swa_sinks[j])
            pwk.append(kp); pwv.append(vp); swk.append(kn); swv.append(vn)
        else:
            yp, kp, vp = sb_prompt(hp, sb_w_qkv[j], sb_w_o[j])
            ys, kn, vn = sb_sample(hs, cache_sb_k[j], cache_sb_v[j], sb_w_qkv[j], sb_w_o[j])
            pbk.append(kp); pbv.append(vp); sbk.append(kn); sbv.append(vn)
        xp = xp + yp
        xs = xs + ys
        xp = xp + mlp(rmsnorm(xp, norm_mlp[i]), w_up[i], w_down[i])
        xs = xs + mlp(rmsnorm(xs, norm_mlp[i]), w_up[i], w_down[i])
    return (xp, xs,
            jnp.stack(pdk), jnp.stack(pdv), jnp.stack(pwk), jnp.stack(pwv), jnp.stack(pbk), jnp.stack(pbv),
            jnp.stack(sdk), jnp.stack(sdv), jnp.stack(swk), jnp.stack(swv), jnp.stack(sbk), jnp.stack(sbv))
```

```python
import functools
import math

import jax
import jax.numpy as jnp
from jax import lax
from jax.experimental import pallas as pl
from jax.experimental.pallas import tpu as pltpu

F32 = jnp.float32
BF16 = jnp.bfloat16

D_MODEL = 1024
HEAD_DIM = 64
CHUNK = 64
N_MIXERS = 3
DIFF_HEADS = 8
SWA_Q_HEADS = 16
SWA_KV_HEADS = 4
SWA_GROUP = 4
WINDOW = 128
SB_HEADS = 16
N_BUCKETS = 32
MAX_DISTANCE = 128
D_FF = 4 * D_MODEL
EPS = 1e-6
LOG2E = 1.4426950408889634
NEG = -1e30
FAR_BUCKET = 15
SB_SKIP_MARGIN = 100.0

VMEM_LIMIT = 56 * 1024 * 1024
TOKEN_TILE = 512
DIFF_TILE = 512
SB_TILE = 256
SWA_TILE = 128


def _cparams(n_axes):
    return pltpu.CompilerParams(dimension_semantics=("arbitrary",) * n_axes,
                                vmem_limit_bytes=VMEM_LIMIT)


def _resident(block_shape, index_map):
    return pl.BlockSpec(block_shape, index_map, pipeline_mode=pl.Buffered(1))


def _t5_bucket(rel):
    half = N_BUCKETS // 2
    exact = half // 2
    ret = jnp.where(rel > 0, half, 0).astype(jnp.int32)
    n = jnp.abs(rel)
    nf = jnp.maximum(n, 1).astype(F32)
    large = exact + (jnp.log(nf / exact) / math.log(MAX_DISTANCE / exact) * (half - exact)).astype(jnp.int32)
    large = jnp.minimum(large, half - 1)
    return ret + jnp.where(n < exact, n, large)


def _bias_tile(rel_bias, rel, valid, shift_far):
    tab = rel_bias.astype(F32)
    b = jnp.moveaxis(tab[_t5_bucket(rel)], -1, 0)
    if shift_far:
        b = b - tab[FAR_BUCKET][:, None, None]
    b = b * LOG2E
    if valid is not None:
        b = jnp.where(valid[None], b, NEG)
    return b


def _proj_kernel(x_ref, g_ref, wt_ref, qg_ref, kg_ref, *out_refs, nq, nk, nv, vb, head_norm, want_kmax):
    if want_kmax:
        qT_ref, k32_ref, k16_ref, v32_ref, vT_ref, kmax_ref = out_refs
    else:
        qT_ref, k32_ref, k16_ref, v32_ref, vT_ref = out_refs
    x = x_ref[...]
    ms = jnp.mean(x * x, axis=-1, keepdims=True)
    h = (x * lax.rsqrt(ms + EPS) * g_ref[...]).astype(BF16)
    yt = lax.dot_general(wt_ref[...], h, (((1,), (1,)), ((), ())),
                         preferred_element_type=F32)
    tm = x.shape[0]

    def headnorm(t, gcol):
        n = t.shape[0]
        t3 = t.reshape(n // HEAD_DIM, HEAD_DIM, tm)
        r = lax.rsqrt(jnp.mean(t3 * t3, axis=1, keepdims=True) + EPS)
        return (t3 * r).reshape(n, tm) * gcol

    qt = yt[0:nq]
    kt = yt[nq:nq + nk]
    vt = yt[nq + nk:nq + nk + nv]
    if head_norm:
        qt = headnorm(qt, qg_ref[...])
        kt = headnorm(kt, kg_ref[...])
    else:
        qt = qt * qg_ref[...]
    qT_ref[...] = qt.astype(BF16)
    k = kt.T
    k32_ref[...] = k
    k16 = k.astype(BF16)
    k16_ref[...] = k16
    v32_ref[...] = vt.T
    vt16 = vt.astype(BF16)
    for b in range(tm // vb):
        vT_ref[b] = vt16[:, b * vb:(b + 1) * vb]
    if want_kmax:
        kabs = jnp.max(jnp.abs(kt.astype(BF16).astype(F32)), axis=1, keepdims=True)

        @pl.when(pl.program_id(0) == 0)
        def _():
            kmax_ref[...] = kabs

        @pl.when(pl.program_id(0) > 0)
        def _():
            kmax_ref[...] = jnp.maximum(kmax_ref[...], kabs)


def _project(x, g, wt, qg, kg, *, nq, nk, nv, vb, head_norm, want_kmax=False):
    n_tok = x.shape[0]
    tm = min(TOKEN_TILE, n_tok)
    vb = min(vb, tm)
    grid = (n_tok // tm,)
    ntot = nq + nk + nv
    out_shape = [
        jax.ShapeDtypeStruct((nq, n_tok), BF16),
        jax.ShapeDtypeStruct((n_tok, nk), F32),
        jax.ShapeDtypeStruct((n_tok, nk), BF16),
        jax.ShapeDtypeStruct((n_tok, nv), F32),
        jax.ShapeDtypeStruct((n_tok // vb, nv, vb), BF16),
    ]
    out_specs = [
        pl.BlockSpec((nq, tm), lambda i: (0, i)),
        pl.BlockSpec((tm, nk), lambda i: (i, 0)),
        pl.BlockSpec((tm, nk), lambda i: (i, 0)),
        pl.BlockSpec((tm, nv), lambda i: (i, 0)),
        pl.BlockSpec((tm // vb, nv, vb), lambda i: (i, 0, 0)),
    ]
    if want_kmax:
        out_shape.append(jax.ShapeDtypeStruct((nk, 1), F32))
        out_specs.append(pl.BlockSpec((nk, 1), lambda i: (0, 0)))
    return pl.pallas_call(
        functools.partial(_proj_kernel, nq=nq, nk=nk, nv=nv, vb=vb, head_norm=head_norm, want_kmax=want_kmax),
        grid=grid,
        in_specs=[
            pl.BlockSpec((tm, D_MODEL), lambda i: (i, 0)),
            _resident((1, D_MODEL), lambda i: (0, 0)),
            _resident((ntot, D_MODEL), lambda i: (0, 0)),
            _resident((nq, 1), lambda i: (0, 0)),
            _resident((nk, 1), lambda i: (0, 0)),
        ],
        out_specs=out_specs,
        out_shape=out_shape,
        compiler_params=_cparams(1),
        name="proj",
    )(x, g, wt, qg, kg)


def _mlp_kernel(x_ref, o_ref, wo_ref, g_ref, wup_ref, wdn_ref, y_ref, *, f_chunk):
    x1 = x_ref[...] + jnp.dot(o_ref[...], wo_ref[...], preferred_element_type=F32)
    ms = jnp.mean(x1 * x1, axis=-1, keepdims=True)
    hn = (x1 * lax.rsqrt(ms + EPS) * g_ref[...]).astype(BF16)
    acc = x1
    for f in range(D_FF // f_chunk):
        u = jnp.dot(hn, wup_ref[:, f * f_chunk:(f + 1) * f_chunk], preferred_element_type=F32)
        a = jnp.square(jnp.maximum(u, 0.0)).astype(BF16)
        acc = acc + jnp.dot(a, wdn_ref[f * f_chunk:(f + 1) * f_chunk, :], preferred_element_type=F32)
    y_ref[...] = acc


def _outproj_mlp(x, o, wo, g, wup, wdn):
    n_tok = x.shape[0]
    tm = min(TOKEN_TILE, n_tok)
    return pl.pallas_call(
        functools.partial(_mlp_kernel, f_chunk=1024),
        grid=(n_tok // tm,),
        in_specs=[
            pl.BlockSpec((tm, D_MODEL), lambda i: (i, 0)),
            pl.BlockSpec((tm, D_MODEL), lambda i: (i, 0)),
            _resident((D_MODEL, D_MODEL), lambda i: (0, 0)),
            _resident((1, D_MODEL), lambda i: (0, 0)),
            _resident((D_MODEL, D_FF), lambda i: (0, 0)),
            _resident((D_FF, D_MODEL), lambda i: (0, 0)),
        ],
        out_specs=pl.BlockSpec((tm, D_MODEL), lambda i: (i, 0)),
        out_shape=jax.ShapeDtypeStruct((n_tok, D_MODEL), F32),
        compiler_params=_cparams(1),
        name="outproj_mlp",
    )(x, o, wo, g, wup, wdn)


def _diff_lambda(lam_ref, lam_init):
    lp = lam_ref[...]
    a = jnp.sum(lp[0:1] * lp[1:2], axis=-1, keepdims=True)
    b = jnp.sum(lp[2:3] * lp[3:4], axis=-1, keepdims=True)
    return jnp.exp(a) - jnp.exp(b) + lam_init


def _diff_prompt_kernel(qT_ref, k_ref, vT_ref, bd_ref, bp_ref, lam_ref, sub_ref, o_ref,
                        m_ref, l_ref, acc_ref, *, tq, lam_init):
    qi = pl.program_id(1)
    qT = qT_ref[...]
    row = lax.broadcasted_iota(jnp.int32, qT.shape, 0)
    zero = jnp.zeros_like(qT)
    qm = (jnp.where(row < HEAD_DIM, qT, zero), jnp.where(row >= HEAD_DIM, qT, zero))
    m_ref[...] = jnp.full(m_ref.shape, NEG, F32)
    l_ref[...] = jnp.zeros(l_ref.shape, F32)
    acc_ref[...] = jnp.zeros(acc_ref.shape, F32)

    def update(j, bias_fn):
        k = k_ref[j]
        vT = vT_ref[j]
        for m in range(2):
            s = jnp.dot(k, qm[m], preferred_element_type=F32)
            if bias_fn is not None:
                s = bias_fn(m, s)
            m_old = m_ref[m]
            m_new = jnp.maximum(m_old, jnp.max(s, axis=0, keepdims=True))
            alpha = jnp.exp2(m_old - m_new)
            p = jnp.exp2(s - m_new)
            l_ref[m] = alpha * l_ref[m] + jnp.sum(p, axis=0, keepdims=True)
            acc_ref[m] = alpha * acc_ref[m] + jnp.dot(vT, p.astype(BF16), preferred_element_type=F32)
            m_ref[m] = m_new

    def far_body(j, carry):
        update(j, None)
        return carry

    lax.fori_loop(0, jnp.maximum(qi - 1, 0), far_body, 0)

    def prev_bias(m, s):
        corner = s[tq - 128:, :128] + bp_ref[0, m]
        bottom = jnp.concatenate([corner, s[tq - 128:, 128:]], axis=1)
        return jnp.concatenate([s[:tq - 128], bottom], axis=0)

    @pl.when(qi >= 1)
    def _():
        update(qi - 1, prev_bias)

    update(qi, lambda m, s: s + bd_ref[0, m])

    lam = _diff_lambda(lam_ref, lam_init)
    o0 = acc_ref[0] * (1.0 / l_ref[0])
    o1 = acc_ref[1] * (1.0 / l_ref[1])
    o = o0 - lam * o1
    ms = jnp.mean(o * o, axis=0, keepdims=True)
    o = o * lax.rsqrt(ms + EPS) * sub_ref[...] * (1.0 - lam_init)
    o_ref[...] = o.T.astype(BF16)


def _diff_prompt(qT, k16, vT, bias_d, bias_p, lam_p, subln, lam_init):
    n_tok = k16.shape[0]
    tq = min(DIFF_TILE, n_tok)
    nb = n_tok // tq
    k3 = k16.reshape(nb, tq, D_MODEL)
    return pl.pallas_call(
        functools.partial(_diff_prompt_kernel, tq=tq, lam_init=lam_init),
        grid=(DIFF_HEADS, nb),
        in_specs=[
            pl.BlockSpec((2 * HEAD_DIM, tq), lambda h, i: (h, i)),
            pl.BlockSpec((nb, tq, 2 * HEAD_DIM), lambda h, i: (0, 0, h)),
            pl.BlockSpec((nb, 2 * HEAD_DIM, tq), lambda h, i: (0, h, 0)),
            pl.BlockSpec((1, 2, tq, tq), lambda h, i: (h, 0, 0, 0)),
            pl.BlockSpec((1, 2, 128, 128), lambda h, i: (h, 0, 0, 0)),
            pl.BlockSpec((4, HEAD_DIM), lambda h, i: (0, 0)),
            pl.BlockSpec((2 * HEAD_DIM, 1), lambda h, i: (0, 0)),
        ],
        out_specs=pl.BlockSpec((tq, 2 * HEAD_DIM), lambda h, i: (i, h)),
        out_shape=jax.ShapeDtypeStruct((n_tok, D_MODEL), BF16),
        scratch_shapes=[
            pltpu.VMEM((2, 1, tq), F32),
            pltpu.VMEM((2, 1, tq), F32),
            pltpu.VMEM((2, 2 * HEAD_DIM, tq), F32),
        ],
        compiler_params=_cparams(2),
        name="diff_prompt",
    )(qT, k3, vT, bias_d, bias_p, lam_p, subln)


def _diff_sample_kernel(q_ref, ck_ref, cv_ref, kn_ref, vn_ref, b_ref, lam_ref, sub_ref, o_ref, *, lam_init):
    q = q_ref[0, 0]
    t = q.shape[0]
    lane = lax.broadcasted_iota(jnp.int32, q.shape, 1)
    zero = jnp.zeros_like(q)
    qq = jnp.concatenate([jnp.where(lane < HEAD_DIM, q, zero), jnp.where(lane >= HEAD_DIM, q, zero)], axis=0)
    ck = ck_ref[0].astype(BF16)
    cv = cv_ref[0].astype(BF16)
    kn = kn_ref[0]
    vn = vn_ref[0].astype(BF16)
    past = ck.shape[0]
    nt = (((1,), (1,)), ((), ()))
    s_c = lax.dot_general(qq, ck, nt, preferred_element_type=F32)
    s_n = lax.dot_general(qq, kn, nt, preferred_element_type=F32)
    bias = b_ref[0].reshape(2 * t, 128 + t)
    s_far = s_c[:, :past - 128]
    s_near = s_c[:, past - 128:] + bias[:, :128]
    s_n = s_n + bias[:, 128:]
    m = jnp.maximum(jnp.maximum(jnp.max(s_far, axis=1, keepdims=True), jnp.max(s_near, axis=1, keepdims=True)),
                    jnp.max(s_n, axis=1, keepdims=True))
    p_far = jnp.exp2(s_far - m)
    p_near = jnp.exp2(s_near - m)
    p_n = jnp.exp2(s_n - m)
    l = (jnp.sum(p_far, axis=1, keepdims=True) + jnp.sum(p_near, axis=1, keepdims=True)
         + jnp.sum(p_n, axis=1, keepdims=True))
    o = (jnp.dot(p_far.astype(BF16), cv[:past - 128], preferred_element_type=F32)
         + jnp.dot(p_near.astype(BF16), cv[past - 128:], preferred_element_type=F32)
         + jnp.dot(p_n.astype(BF16), vn, preferred_element_type=F32))
    o = o * (1.0 / l)
    lam = _diff_lambda(lam_ref, lam_init)
    od = o[:t] - lam * o[t:]
    ms = jnp.mean(od * od, axis=-1, keepdims=True)
    od = od * lax.rsqrt(ms + EPS) * sub_ref[...] * (1.0 - lam_init)
    o_ref[0] = od.astype(BF16)


def _diff_sample(q_s, ck, cv, kn, vn, bias_s, lam_p, subln_row, lam_init):
    nb, _, t, _ = q_s.shape
    past = ck.shape[1]
    return pl.pallas_call(
        functools.partial(_diff_sample_kernel, lam_init=lam_init),
        grid=(nb, DIFF_HEADS),
        in_specs=[
            pl.BlockSpec((1, 1, t, 128), lambda b, h: (b, h, 0, 0)),
            pl.BlockSpec((1, past, 128), lambda b, h: (b, 0, h)),
            pl.BlockSpec((1, past, 128), lambda b, h: (b, 0, h)),
            pl.BlockSpec((1, t, 128), lambda b, h: (b, 0, h)),
            pl.BlockSpec((1, t, 128), lambda b, h: (b, 0, h)),
            pl.BlockSpec((1, 2, t, 128 + t), lambda b, h: (h, 0, 0, 0)),
            pl.BlockSpec((4, HEAD_DIM), lambda b, h: (0, 0)),
            pl.BlockSpec((1, 128), lambda b, h: (0, 0)),
        ],
        out_specs=pl.BlockSpec((1, t, 128), lambda b, h: (b, 0, h)),
        out_shape=jax.ShapeDtypeStruct((nb, t, D_MODEL), BF16),
        compiler_params=_cparams(2),
        name="diff_sample",
    )(q_s, ck, cv, kn, vn, bias_s, lam_p, subln_row)


def _swa_prompt_kernel(qT_ref, kc_ref, kp_ref, vc_ref, vp_ref, b_ref, sink_ref, o_ref):
    i = pl.program_id(0)
    qT = qT_ref[...]
    kp = kp_ref[...]
    kc = kc_ref[...]
    vp = vp_ref[0]
    vc = vc_ref[0]
    tq = qT.shape[1]
    key_row = lax.broadcasted_iota(jnp.int32, (2 * tq, tq), 0)
    has_prev = key_row >= jnp.where(i > 0, 0, tq)
    zeros = jnp.zeros((HEAD_DIM, tq), BF16)
    outs = []
    for kv in range(SWA_KV_HEADS):
        pair = kv // 2
        k_band = jnp.concatenate([kp[:, pair * 128:(pair + 1) * 128], kc[:, pair * 128:(pair + 1) * 128]], axis=0)
        v_band = jnp.concatenate([vp[kv * HEAD_DIM:(kv + 1) * HEAD_DIM], vc[kv * HEAD_DIM:(kv + 1) * HEAD_DIM]],
                                 axis=1)
        for g in range(SWA_GROUP):
            h = kv * SWA_GROUP + g
            qh = qT[h * HEAD_DIM:(h + 1) * HEAD_DIM]
            qpad = jnp.concatenate([qh, zeros] if kv % 2 == 0 else [zeros, qh], axis=0)
            s = jnp.dot(k_band, qpad, preferred_element_type=F32) + b_ref[h]
            s = jnp.where(has_prev, s, NEG)
            sink = sink_ref[h]
            m = jnp.maximum(jnp.max(s, axis=0, keepdims=True), sink)
            p = jnp.exp2(s - m)
            den = jnp.sum(p, axis=0, keepdims=True) + jnp.exp2(sink - m)
            oT = jnp.dot(v_band, p.astype(BF16), preferred_element_type=F32) * (1.0 / den)
            outs.append(oT)
    o_ref[...] = jnp.concatenate(outs, axis=0).T.astype(BF16)


def _swa_prompt(qT, k16, vT, bias, sinks):
    n_tok = k16.shape[0]
    tq = SWA_TILE
    nkv = SWA_KV_HEADS * HEAD_DIM
    return pl.pallas_call(
        _swa_prompt_kernel,
        grid=(n_tok // tq,),
        in_specs=[
            pl.BlockSpec((D_MODEL, tq), lambda i: (0, i)),
            pl.BlockSpec((tq, nkv), lambda i: (i, 0)),
            pl.BlockSpec((tq, nkv), lambda i: (jnp.maximum(i - 1, 0), 0)),
            pl.BlockSpec((1, nkv, tq), lambda i: (i, 0, 0)),
            pl.BlockSpec((1, nkv, tq), lambda i: (jnp.maximum(i - 1, 0), 0, 0)),
            _resident((SWA_Q_HEADS, 2 * tq, tq), lambda i: (0, 0, 0)),
            _resident((SWA_Q_HEADS, 1, 1), lambda i: (0, 0, 0)),
        ],
        out_specs=pl.BlockSpec((tq, D_MODEL), lambda i: (i, 0)),
        out_shape=jax.ShapeDtypeStruct((n_tok, D_MODEL), BF16),
        compiler_params=_cparams(1),
        name="swa_prompt",
    )(qT, k16, k16, vT, vT, bias, sinks)


def _swa_sample_kernel(q_ref, kc_ref, vc_ref, kn_ref, vn_ref, b_ref, sink_ref, o_ref, ko_ref, vo_ref):
    t = kn_ref.shape[2]
    buf = kc_ref.shape[2]
    nt = (((1,), (1,)), ((), ()))
    for kv in range(SWA_KV_HEADS):
        kc32 = kc_ref[0, kv]
        vc32 = vc_ref[0, kv]
        kn32 = kn_ref[0, kv]
        vn32 = vn_ref[0, kv]
        ko_ref[0, kv, 0:buf - t, :] = kc32[t:]
        ko_ref[0, kv, buf - t:buf, :] = kn32
        vo_ref[0, kv, 0:buf - t, :] = vc32[t:]
        vo_ref[0, kv, buf - t:buf, :] = vn32
        kc = kc32.astype(BF16)
        vc = vc32.astype(BF16)
        kn = kn32.astype(BF16)
        vn = vn32.astype(BF16)
        for g in range(SWA_GROUP):
            h = kv * SWA_GROUP + g
            q = q_ref[0, h]
            b = b_ref[h]
            s_c = lax.dot_general(q, kc, nt, preferred_element_type=F32) + b[:, :buf]
            s_n = lax.dot_general(q, kn, nt, preferred_element_type=F32) + b[:, buf:]
            sink = sink_ref[h]
            m = jnp.maximum(jnp.maximum(jnp.max(s_c, axis=1, keepdims=True), jnp.max(s_n, axis=1, keepdims=True)),
                            sink)
            p_c = jnp.exp2(s_c - m)
            p_n = jnp.exp2(s_n - m)
            den = (jnp.sum(p_c, axis=1, keepdims=True) + jnp.sum(p_n, axis=1, keepdims=True)
                   + jnp.exp2(sink - m))
            o = (jnp.dot(p_c.astype(BF16), vc, preferred_element_type=F32)
                 + jnp.dot(p_n.astype(BF16), vn, preferred_element_type=F32)) * (1.0 / den)
            o_ref[0, h] = o.astype(BF16)


def _swa_sample(q_s, kc, vc, kn, vn, bias, sinks):
    nb, _, t, _ = q_s.shape
    buf = kc.shape[2]
    kvspec = pl.BlockSpec((1, SWA_KV_HEADS, buf, HEAD_DIM), lambda b: (b, 0, 0, 0))
    nspec = pl.BlockSpec((1, SWA_KV_HEADS, t, HEAD_DIM), lambda b: (b, 0, 0, 0))
    return pl.pallas_call(
        _swa_sample_kernel,
        grid=(nb,),
        in_specs=[
            pl.BlockSpec((1, SWA_Q_HEADS, t, HEAD_DIM), lambda b: (b, 0, 0, 0)),
            kvspec, kvspec, nspec, nspec,
            pl.BlockSpec((SWA_Q_HEADS, t, buf + t), lambda b: (0, 0, 0)),
            pl.BlockSpec((SWA_Q_HEADS, 1, 1), lambda b: (0, 0, 0)),
        ],
        out_specs=[
            pl.BlockSpec((1, SWA_Q_HEADS, t, HEAD_DIM), lambda b: (b, 0, 0, 0)),
            kvspec, kvspec,
        ],
        out_shape=[
            jax.ShapeDtypeStruct((nb, SWA_Q_HEADS, t, HEAD_DIM), BF16),
            jax.ShapeDtypeStruct((nb, SWA_KV_HEADS, buf, HEAD_DIM), F32),
            jax.ShapeDtypeStruct((nb, SWA_KV_HEADS, buf, HEAD_DIM), F32),
        ],
        compiler_params=_cparams(1),
        name="swa_sample",
    )(q_s, kc, vc, kn, vn, bias, sinks)


def _softplus(z):
    return jnp.maximum(z, 0.0) + jnp.log(1.0 + jnp.exp(-jnp.abs(z)))


def _split_bf16(x):
    hi = x.astype(BF16)
    lo = (x - hi.astype(F32)).astype(BF16)
    return hi, lo


def _sb_prompt_kernel(qT_ref, k_ref, vT_ref, kmax_ref, o_ref, carry_ref, acc_ref, *, tq):
    qi = pl.program_id(1)
    qT = qT_ref[...]
    row = lax.broadcasted_iota(jnp.int32, qT.shape, 0)
    zero = jnp.zeros_like(qT)
    kr = lax.broadcasted_iota(jnp.int32, (tq, tq), 0)
    kc = lax.broadcasted_iota(jnp.int32, (tq, tq), 1)
    tri = jnp.where(kc >= kr, 1.0, 0.0).astype(BF16)
    causal = kr < kc
    outs = []
    for hh in range(2):
        qpad = jnp.where((row < HEAD_DIM) if hh == 0 else (row >= HEAD_DIM), qT, zero)
        zb = jnp.sum(jnp.abs(qpad.astype(F32)) * kmax_ref[...], axis=0, keepdims=True)
        carry_ref[...] = jnp.zeros(carry_ref.shape, F32)
        acc_ref[...] = jnp.zeros(acc_ref.shape, F32)

        def tile(j, diag, hh=hh, qpad=qpad):
            k = k_ref[j]
            vT = vT_ref[j, hh * HEAD_DIM:(hh + 1) * HEAD_DIM, :]
            z = jnp.dot(k, qpad, preferred_element_type=F32)
            lsp = _softplus(z)
            if diag:
                lsp = jnp.where(causal, lsp, 0.0)
            hi, lo = _split_bf16(lsp)
            c = jnp.dot(tri, hi, preferred_element_type=F32) + jnp.dot(tri, lo, preferred_element_type=F32)
            a = jnp.exp(z - c - carry_ref[...])
            if diag:
                a = jnp.where(causal, a, 0.0)
            acc_ref[...] += jnp.dot(vT, a.astype(BF16), preferred_element_type=F32)
            carry_ref[...] += c[0:1]

        tile(qi, True)

        def cond(state):
            j, live = state
            return jnp.logical_and(j >= 0, live > 0)

        def body(state, zb=zb, tile=tile):
            j, _ = state
            tile(j, False)
            slack = jnp.min(carry_ref[...] - zb)
            return j - 1, (slack < SB_SKIP_MARGIN).astype(jnp.int32)

        slack0 = jnp.min(carry_ref[...] - zb)
        lax.while_loop(cond, body, (qi - 1, (slack0 < SB_SKIP_MARGIN).astype(jnp.int32)))
        outs.append(acc_ref[...])
    o_ref[...] = jnp.concatenate(outs, axis=0).T.astype(BF16)


def _sb_prompt(qT, k16, vT, kmax):
    n_tok = k16.shape[0]
    tq = min(SB_TILE, n_tok)
    nb = n_tok // tq
    k3 = k16.reshape(nb, tq, D_MODEL)
    return pl.pallas_call(
        functools.partial(_sb_prompt_kernel, tq=tq),
        grid=(SB_HEADS // 2, nb),
        in_specs=[
            pl.BlockSpec((128, tq), lambda p, i: (p, i)),
            pl.BlockSpec((nb, tq, 128), lambda p, i: (0, 0, p)),
            pl.BlockSpec((nb, 128, tq), lambda p, i: (0, p, 0)),
            pl.BlockSpec((128, 1), lambda p, i: (p, 0)),
        ],
        out_specs=pl.BlockSpec((tq, 128), lambda p, i: (i, p)),
        out_shape=jax.ShapeDtypeStruct((n_tok, D_MODEL), BF16),
        scratch_shapes=[
            pltpu.VMEM((1, tq), F32),
            pltpu.VMEM((HEAD_DIM, tq), F32),
        ],
        compiler_params=_cparams(2),
        name="sb_prompt",
    )(qT, k3, vT, kmax)


def _sb_sample_kernel(q_ref, ck_ref, cv_ref, kn_ref, vn_ref, o_ref, *, kb):
    t = kn_ref.shape[1]
    past = ck_ref.shape[1]
    nt = (((1,), (1,)), ((), ()))
    kn = kn_ref[0]
    vn = vn_ref[0].astype(BF16)

    def tri_mat(n):
        r = lax.broadcasted_iota(jnp.int32, (n, n), 0)
        c = lax.broadcasted_iota(jnp.int32, (n, n), 1)
        return jnp.where(r >= c, 1.0, 0.0).astype(BF16)

    tri_t = tri_mat(t)
    tri_b = tri_mat(kb)
    qr = lax.broadcasted_iota(jnp.int32, (t, t), 0)
    kc = lax.broadcasted_iota(jnp.int32, (t, t), 1)
    causal = kc < qr
    lane = lax.broadcasted_iota(jnp.int32, (t, 128), 1)
    outs = []
    for hh in range(2):
        q = q_ref[0, hh]
        z = lax.dot_general(q, kn, nt, preferred_element_type=F32)
        lsp = jnp.where(causal, _softplus(z), 0.0)
        hi, lo = _split_bf16(lsp)
        c = jnp.dot(hi, tri_t, preferred_element_type=F32) + jnp.dot(lo, tri_t, preferred_element_type=F32)
        a = jnp.where(causal, jnp.exp(z - c), 0.0)
        o = jnp.dot(a.astype(BF16), vn, preferred_element_type=F32)
        carry = c[:, 0:1]
        for b in reversed(range(past // kb)):
            ck = ck_ref[0, b * kb:(b + 1) * kb, :].astype(BF16)
            cv = cv_ref[0, b * kb:(b + 1) * kb, :].astype(BF16)
            z = lax.dot_general(q, ck, nt, preferred_element_type=F32)
            hi, lo = _split_bf16(_softplus(z))
            c = jnp.dot(hi, tri_b, preferred_element_type=F32) + jnp.dot(lo, tri_b, preferred_element_type=F32)
            a = jnp.exp(z - c - carry)
            o = o + jnp.dot(a.astype(BF16), cv, preferred_element_type=F32)
            carry = carry + c[:, 0:1]
        outs.append(o)
    o_ref[0] = jnp.where(lane < HEAD_DIM, outs[0], outs[1]).astype(BF16)


def _sb_sample(q_s, ck, cv, kn, vn):
    nb, _, t, _ = q_s.shape
    past = ck.shape[1]
    return pl.pallas_call(
        functools.partial(_sb_sample_kernel, kb=min(256, past)),
        grid=(nb, SB_HEADS // 2),
        in_specs=[
            pl.BlockSpec((1, 2, t, 128), lambda b, p: (b, p, 0, 0)),
            pl.BlockSpec((1, past, 128), lambda b, p: (b, 0, p)),
            pl.BlockSpec((1, past, 128), lambda b, p: (b, 0, p)),
            pl.BlockSpec((1, t, 128), lambda b, p: (b, 0, p)),
            pl.BlockSpec((1, t, 128), lambda b, p: (b, 0, p)),
        ],
        out_specs=pl.BlockSpec((1, t, 128), lambda b, p: (b, 0, p)),
        out_shape=jax.ShapeDtypeStruct((nb, t, D_MODEL), BF16),
        compiler_params=_cparams(2),
        name="sb_sample",
    )(q_s, ck, cv, kn, vn)


def _tile_col(gain, n, scale):
    return (jnp.tile(gain.astype(F32), n // gain.shape[0]) * scale).reshape(n, 1)


def _diff_layer(xp, xs, ck, cv, rel_bias, g, w_qkv, q_gain, k_gain, lam_p, subln, lam_init):
    n_p = xp.shape[0]
    nb, past = ck.shape[0], ck.shape[1]
    t = xs.shape[0] // nb
    wt = w_qkv.T.astype(BF16)
    qg = _tile_col(q_gain, D_MODEL, HEAD_DIM ** -0.5 * LOG2E)
    kg = _tile_col(k_gain, D_MODEL, 1.0)
    g = g.reshape(1, D_MODEL)
    kw = dict(nq=D_MODEL, nk=D_MODEL, nv=D_MODEL, head_norm=True)
    tq = min(DIFF_TILE, n_p)
    qT, k32, k16, v32, vT = _project(xp, g, wt, qg, kg, vb=tq, **kw)
    qT_s, k32_s, k16_s, v32_s, _ = _project(xs, g, wt, qg, kg, vb=min(128, xs.shape[0]), **kw)

    j = jnp.arange(tq)
    rel_d = j[:, None] - j[None, :]
    valid_d = (j // CHUNK)[:, None] <= (j // CHUNK)[None, :]
    bias_d = _bias_tile(rel_bias, rel_d, valid_d, True).reshape(DIFF_HEADS, 2, tq, tq)
    j128 = jnp.arange(128)
    rel_p = (j128[:, None] - 128) - j128[None, :]
    bias_p = _bias_tile(rel_bias, rel_p, None, True).reshape(DIFF_HEADS, 2, 128, 128)
    o_p = _diff_prompt(qT, k16, vT, bias_d, bias_p, lam_p, subln.reshape(2 * HEAD_DIM, 1), lam_init)

    s_pos = jnp.arange(128 + t) - 128
    rel_s = s_pos[None, :] - jnp.arange(t)[:, None]
    bias_s = _bias_tile(rel_bias, rel_s, None, True).reshape(DIFF_HEADS, 2, t, 128 + t)
    q_s = qT_s.T.reshape(nb, t, DIFF_HEADS, 2 * HEAD_DIM).transpose(0, 2, 1, 3)
    o_s = _diff_sample(q_s, ck.reshape(nb, past, D_MODEL), cv.reshape(nb, past, D_MODEL),
                       k16_s.reshape(nb, t, D_MODEL), v32_s.reshape(nb, t, D_MODEL), bias_s, lam_p,
                       subln.reshape(1, 2 * HEAD_DIM), lam_init)
    return (o_p, o_s.reshape(nb * t, D_MODEL),
            k32.reshape(1, n_p, DIFF_HEADS, 2 * HEAD_DIM), v32.reshape(1, n_p, DIFF_HEADS, 2 * HEAD_DIM),
            k32_s.reshape(nb, t, DIFF_HEADS, 2 * HEAD_DIM), v32_s.reshape(nb, t, DIFF_HEADS, 2 * HEAD_DIM))


def _swa_layer(xp, xs, ck, cv, rel_bias, g, w_qkv, q_gain, k_gain, sinks):
    n_p = xp.shape[0]
    nb, buf = ck.shape[0], ck.shape[1]
    t = xs.shape[0] // nb
    nkv = SWA_KV_HEADS * HEAD_DIM
    wt = w_qkv.T.astype(BF16)
    qg = _tile_col(q_gain, D_MODEL, HEAD_DIM ** -0.5 * LOG2E)
    kg = _tile_col(k_gain, nkv, 1.0)
    g = g.reshape(1, D_MODEL)
    kw = dict(nq=D_MODEL, nk=nkv, nv=nkv, vb=SWA_TILE, head_norm=True)
    qT, k32, k16, v32, vT = _project(xp, g, wt, qg, kg, **kw)
    qT_s, k32_s, _, v32_s, _ = _project(xs, g, wt, qg, kg, **kw)
    sink_col = (sinks.astype(F32) * LOG2E).reshape(SWA_Q_HEADS, 1, 1)

    tq = SWA_TILE
    jk = jnp.arange(2 * tq) - tq
    iq = jnp.arange(tq)
    rel = jk[:, None] - iq[None, :]
    kchunk = jnp.floor_divide(jk, CHUNK)[:, None]
    qchunk = (iq // CHUNK)[None, :]
    valid = (kchunk <= qchunk) & (kchunk >= qchunk - WINDOW // CHUNK)
    bias = _bias_tile(rel_bias, rel, valid, False)
    o_p = _swa_prompt(qT, k16, vT, bias, sink_col)

    s_pos = jnp.arange(buf + t) - buf
    t_pos = jnp.arange(t)
    sc = jnp.floor_divide(s_pos, CHUNK)[None, :]
    tc = (t_pos // CHUNK)[:, None]
    valid_s = (sc <= tc) & (sc >= tc - WINDOW // CHUNK)
    bias_s = _bias_tile(rel_bias, s_pos[None, :] - t_pos[:, None], valid_s, False)
    q_s = qT_s.T.reshape(nb, t, SWA_Q_HEADS, HEAD_DIM).transpose(0, 2, 1, 3)
    kn = k32_s.reshape(nb, t, SWA_KV_HEADS, HEAD_DIM).transpose(0, 2, 1, 3)
    vn = v32_s.reshape(nb, t, SWA_KV_HEADS, HEAD_DIM).transpose(0, 2, 1, 3)
    o_s, ko, vo = _swa_sample(q_s, ck.transpose(0, 2, 1, 3), cv.transpose(0, 2, 1, 3), kn, vn, bias_s, sink_col)
    o_s = o_s.transpose(0, 2, 1, 3).reshape(nb * t, D_MODEL)
    wbuf = min(WINDOW, n_p)
    return (o_p, o_s,
            k32[n_p - wbuf:].reshape(1, wbuf, SWA_KV_HEADS, HEAD_DIM),
            v32[n_p - wbuf:].reshape(1, wbuf, SWA_KV_HEADS, HEAD_DIM),
            ko.transpose(0, 2, 1, 3), vo.transpose(0, 2, 1, 3))


def _sb_layer(xp, xs, ck, cv, g, w_qkv):
    n_p = xp.shape[0]
    nb, past = ck.shape[0], ck.shape[1]
    t = xs.shape[0] // nb
    wt = w_qkv.T.astype(BF16)
    qg = jnp.full((D_MODEL, 1), HEAD_DIM ** -0.5, F32)
    kg = jnp.ones((D_MODEL, 1), F32)
    g = g.reshape(1, D_MODEL)
    kw = dict(nq=D_MODEL, nk=D_MODEL, nv=D_MODEL, head_norm=False)
    tq = min(SB_TILE, n_p)
    qT, k32, k16, v32, vT, kmax = _project(xp, g, wt, qg, kg, vb=tq, want_kmax=True, **kw)
    qT_s, k32_s, k16_s, v32_s, _ = _project(xs, g, wt, qg, kg, vb=min(128, xs.shape[0]), **kw)
    o_p = _sb_prompt(qT, k16, vT, kmax)

    q4 = qT_s.T.reshape(nb, t, SB_HEADS // 2, 2, HEAD_DIM)
    zq = jnp.zeros_like(q4[:, :, :, 0])
    q_even = jnp.concatenate([q4[:, :, :, 0], zq], axis=-1)
    q_odd = jnp.concatenate([zq, q4[:, :, :, 1]], axis=-1)
    q_s = jnp.stack([q_even, q_odd], axis=3).reshape(nb, t, SB_HEADS, 128).transpose(0, 2, 1, 3)
    o_s = _sb_sample(q_s, ck.reshape(nb, past, D_MODEL), cv.reshape(nb, past, D_MODEL),
                     k16_s.reshape(nb, t, D_MODEL), v32_s.reshape(nb, t, D_MODEL))
    return (o_p, o_s.reshape(nb * t, D_MODEL),
            k32.reshape(1, n_p, SB_HEADS, HEAD_DIM), v32.reshape(1, n_p, SB_HEADS, HEAD_DIM),
            k32_s.reshape(nb, t, SB_HEADS, HEAD_DIM), v32_s.reshape(nb, t, SB_HEADS, HEAD_DIM))


def kernel(x_prompt, x_sample, cache_diff_k, cache_diff_v, cache_swa_k, cache_swa_v, cache_sb_k, cache_sb_v, rel_bias, norm_mix, norm_mlp, w_up, w_down, diff_w_qkv, diff_w_o, diff_q_norm, diff_k_norm, diff_lambda, diff_subln, swa_w_qkv, swa_w_o, swa_q_norm, swa_k_norm, swa_sinks, sb_w_qkv, sb_w_o):
    bp, n_p, _ = x_prompt.shape
    assert bp == 1
    nb, t, _ = x_sample.shape
    depth = norm_mix.shape[0]
    xp = x_prompt.reshape(n_p, D_MODEL)
    xs = x_sample.reshape(nb * t, D_MODEL)
    outs = {name: [] for name in ("pdk", "pdv", "pwk", "pwv", "pbk", "pbv", "sdk", "sdv", "swk", "swv", "sbk", "sbv")}
    for i in range(depth):
        j = i // N_MIXERS
        if i % N_MIXERS == 0:
            lam_init = 0.8 - 0.6 * math.exp(-0.3 * i)
            o_p, o_s, kp, vp, kn, vn = _diff_layer(
                xp, xs, cache_diff_k[j], cache_diff_v[j], rel_bias, norm_mix[i], diff_w_qkv[j],
                diff_q_norm[j], diff_k_norm[j], diff_lambda[j], diff_subln[j], lam_init)
            w_o = diff_w_o[j]
            names = ("pdk", "pdv", "sdk", "sdv")
        elif i % N_MIXERS == 1:
            o_p, o_s, kp, vp, kn, vn = _swa_layer(
                xp, xs, cache_swa_k[j], cache_swa_v[j], rel_bias, norm_mix[i], swa_w_qkv[j],
                swa_q_norm[j], swa_k_norm[j], swa_sinks[j])
            w_o = swa_w_o[j]
            names = ("pwk", "pwv", "swk", "swv")
        else:
            o_p, o_s, kp, vp, kn, vn = _sb_layer(xp, xs, cache_sb_k[j], cache_sb_v[j], norm_mix[i], sb_w_qkv[j])
            w_o = sb_w_o[j]
            names = ("pbk", "pbv", "sbk", "sbv")
        for name, val in zip(names, (kp, vp, kn, vn)):
            outs[name].append(val)
        wo16 = w_o.astype(BF16)
        g_mlp = norm_mlp[i].reshape(1, D_MODEL)
        wup16 = w_up[i].astype(BF16)
        wdn16 = w_down[i].astype(BF16)
        xp = _outproj_mlp(xp, o_p, wo16, g_mlp, wup16, wdn16)
        xs = _outproj_mlp(xs, o_s, wo16, g_mlp, wup16, wdn16)
    st = {name: jnp.stack(v) for name, v in outs.items()}
    return (xp.reshape(1, n_p, D_MODEL), xs.reshape(nb, t, D_MODEL),
            st["pdk"], st["pdv"], st["pwk"], st["pwv"], st["pbk"], st["pbv"],
            st["sdk"], st["sdv"], st["swk"], st["swv"], st["sbk"], st["sbv"])
```

```python
import functools
import math

import jax
import jax.numpy as jnp
from jax import lax
from jax.experimental import pallas as pl
from jax.experimental.pallas import tpu as pltpu

F32 = jnp.float32
BF16 = jnp.bfloat16

D_MODEL = 1024
HEAD_DIM = 64
CHUNK = 64
N_MIXERS = 3
DIFF_HEADS = 8
SWA_Q_HEADS = 16
SWA_KV_HEADS = 4
SWA_GROUP = 4
WINDOW = 128
SB_HEADS = 16
N_BUCKETS = 32
MAX_DISTANCE = 128
D_FF = 4 * D_MODEL
EPS = 1e-6
LOG2E = 1.4426950408889634
NEG = -1e30
FAR_BUCKET = 15
SB_SKIP_MARGIN = 100.0
SOFTMAX_BOUND_LIMIT = 100.0

VMEM_LIMIT = 56 * 1024 * 1024
TOKEN_TILE = 512
DIFF_TILE = 512
SB_TILE = 256
SB_SAMPLE_CHUNK = 512
SWA_TILE = 128

NT_DIMS = (((1,), (1,)), ((), ()))


def _cparams(n_axes):
    return pltpu.CompilerParams(dimension_semantics=("arbitrary",) * n_axes,
                                vmem_limit_bytes=VMEM_LIMIT)


def _resident(block_shape, index_map):
    return pl.BlockSpec(block_shape, index_map, pipeline_mode=pl.Buffered(1))


def _t5_bucket(rel):
    half = N_BUCKETS // 2
    exact = half // 2
    ret = jnp.where(rel > 0, half, 0).astype(jnp.int32)
    n = jnp.abs(rel)
    nf = jnp.maximum(n, 1).astype(F32)
    large = exact + (jnp.log(nf / exact) / math.log(MAX_DISTANCE / exact) * (half - exact)).astype(jnp.int32)
    large = jnp.minimum(large, half - 1)
    return ret + jnp.where(n < exact, n, large)


def _bias_tile(rel_bias, n_j, n_i, r0, valid, shift_far):
    tab = rel_bias.astype(F32)
    rel = jnp.arange(n_i + n_j - 1) + (r0 - (n_i - 1))
    vals = jnp.moveaxis(tab[_t5_bucket(rel)], -1, 0)
    if shift_far:
        vals = vals - tab[FAR_BUCKET][:, None]
    vals = vals * LOG2E
    period = n_i + n_j
    y = jnp.concatenate([jnp.flip(vals, axis=-1), jnp.zeros((vals.shape[0], 1), F32)], axis=-1)
    r = jnp.tile(y, (1, n_j))[:, :n_j * (period - 1)].reshape(vals.shape[0], n_j, period - 1)
    b = r[:, :, n_j - 1:n_j - 1 + n_i]
    if valid is not None:
        b = jnp.where(valid[None], b, NEG)
    return b


def _bias_stats(rel_bias):
    tab = rel_bias.astype(F32)
    sh = (tab - tab[FAR_BUCKET][None, :]) * LOG2E
    bmax = jnp.max(sh, axis=0)
    return jnp.stack([bmax, bmax - jnp.min(sh, axis=0)], axis=-1)


def _proj_kernel(x_ref, g_ref, wt_ref, qg_ref, kg_ref, *out_refs, nq, nk, nv, vb, head_norm, kstat):
    qT_ref, k32_ref, k16_ref, v32_ref, vT_ref = out_refs[:5]
    x = x_ref[...]
    ms = jnp.mean(x * x, axis=-1, keepdims=True)
    h = (x * lax.rsqrt(ms + EPS) * g_ref[...]).astype(BF16)
    yt = lax.dot_general(wt_ref[...], h, NT_DIMS, preferred_element_type=F32)
    tm = x.shape[0]

    def headnorm(t, gcol):
        n = t.shape[0]
        t3 = t.reshape(n // HEAD_DIM, HEAD_DIM, tm)
        r = lax.rsqrt(jnp.mean(t3 * t3, axis=1, keepdims=True) + EPS)
        return (t3 * r).reshape(n, tm) * gcol

    qt = yt[0:nq]
    kt = yt[nq:nq + nk]
    vt = yt[nq + nk:nq + nk + nv]
    if head_norm:
        qt = headnorm(qt, qg_ref[...])
        kt = headnorm(kt, kg_ref[...])
    else:
        qt = qt * qg_ref[...]
    qT_ref[...] = qt.astype(BF16)
    k = kt.T
    k32_ref[...] = k
    k16_ref[...] = k.astype(BF16)
    v32_ref[...] = vt.T
    vt16 = vt.astype(BF16)
    for b in range(tm // vb):
        vT_ref[b] = vt16[:, b * vb:(b + 1) * vb]
    if kstat is not None:
        stat_ref = out_refs[5]
        kr = kt.astype(BF16).astype(F32)
        if kstat == "absmax":
            stat = jnp.max(jnp.abs(kr), axis=1, keepdims=True)
        else:
            k3 = kr.reshape(nk // HEAD_DIM, HEAD_DIM, tm)
            stat = jnp.max(jnp.sum(k3 * k3, axis=1), axis=1, keepdims=True)

        @pl.when(pl.program_id(0) == 0)
        def _():
            stat_ref[...] = stat

        @pl.when(pl.program_id(0) > 0)
        def _():
            stat_ref[...] = jnp.maximum(stat_ref[...], stat)


def _project(x, g, wt, qg, kg, *, nq, nk, nv, vb, head_norm, kstat=None):
    n_tok = x.shape[0]
    tm = min(TOKEN_TILE, n_tok)
    vb = min(vb, tm)
    grid = (n_tok // tm,)
    ntot = nq + nk + nv
    out_shape = [
        jax.ShapeDtypeStruct((nq, n_tok), BF16),
        jax.ShapeDtypeStruct((n_tok, nk), F32),
        jax.ShapeDtypeStruct((n_tok, nk), BF16),
        jax.ShapeDtypeStruct((n_tok, nv), F32),
        jax.ShapeDtypeStruct((n_tok // vb, nv, vb), BF16),
    ]
    out_specs = [
        pl.BlockSpec((nq, tm), lambda i: (0, i)),
        pl.BlockSpec((tm, nk), lambda i: (i, 0)),
        pl.BlockSpec((tm, nk), lambda i: (i, 0)),
        pl.BlockSpec((tm, nv), lambda i: (i, 0)),
        pl.BlockSpec((tm // vb, nv, vb), lambda i: (i, 0, 0)),
    ]
    if kstat is not None:
        n_stat = nk if kstat == "absmax" else nk // HEAD_DIM
        out_shape.append(jax.ShapeDtypeStruct((n_stat, 1), F32))
        out_specs.append(pl.BlockSpec((n_stat, 1), lambda i: (0, 0)))
    return pl.pallas_call(
        functools.partial(_proj_kernel, nq=nq, nk=nk, nv=nv, vb=vb, head_norm=head_norm, kstat=kstat),
        grid=grid,
        in_specs=[
            pl.BlockSpec((tm, D_MODEL), lambda i: (i, 0)),
            _resident((1, D_MODEL), lambda i: (0, 0)),
            _resident((ntot, D_MODEL), lambda i: (0, 0)),
            _resident((nq, 1), lambda i: (0, 0)),
            _resident((nk, 1), lambda i: (0, 0)),
        ],
        out_specs=out_specs,
        out_shape=out_shape,
        compiler_params=_cparams(1),
        name="proj",
    )(x, g, wt, qg, kg)


def _mlp_kernel(x_ref, o_ref, wo_ref, g_ref, wup_ref, wdn_ref, y_ref, *, f_chunk):
    x1 = x_ref[...] + jnp.dot(o_ref[...], wo_ref[...], preferred_element_type=F32)
    ms = jnp.mean(x1 * x1, axis=-1, keepdims=True)
    hn = (x1 * lax.rsqrt(ms + EPS) * g_ref[...]).astype(BF16)
    acc = x1
    for f in range(D_FF // f_chunk):
        u = jnp.dot(hn, wup_ref[:, f * f_chunk:(f + 1) * f_chunk], preferred_element_type=F32)
        a = jnp.square(jnp.maximum(u, 0.0)).astype(BF16)
        acc = acc + jnp.dot(a, wdn_ref[f * f_chunk:(f + 1) * f_chunk, :], preferred_element_type=F32)
    y_ref[...] = acc


def _outproj_mlp(x, o, wo, g, wup, wdn):
    n_tok = x.shape[0]
    tm = min(TOKEN_TILE, n_tok)
    return pl.pallas_call(
        functools.partial(_mlp_kernel, f_chunk=1024),
        grid=(n_tok // tm,),
        in_specs=[
            pl.BlockSpec((tm, D_MODEL), lambda i: (i, 0)),
            pl.BlockSpec((tm, D_MODEL), lambda i: (i, 0)),
            _resident((D_MODEL, D_MODEL), lambda i: (0, 0)),
            _resident((1, D_MODEL), lambda i: (0, 0)),
            _resident((D_MODEL, D_FF), lambda i: (0, 0)),
            _resident((D_FF, D_MODEL), lambda i: (0, 0)),
        ],
        out_specs=pl.BlockSpec((tm, D_MODEL), lambda i: (i, 0)),
        out_shape=jax.ShapeDtypeStruct((n_tok, D_MODEL), F32),
        compiler_params=_cparams(1),
        name="outproj_mlp",
    )(x, o, wo, g, wup, wdn)


def _diff_lambda(lam_ref, lam_init):
    lp = lam_ref[...]
    a = jnp.sum(lp[0:1] * lp[1:2], axis=-1, keepdims=True)
    b = jnp.sum(lp[2:3] * lp[3:4], axis=-1, keepdims=True)
    return jnp.exp(a) - jnp.exp(b) + lam_init


def _diff_prompt_kernel(qT_ref, k_ref, vT_ref, bd_ref, bp_ref, kn_ref, bs_ref, lam_ref, sub_ref, o_ref,
                        m_ref, l_ref, acc_ref, *, tq, lam_init):
    qi = pl.program_id(1)
    qT = qT_ref[...]
    row = lax.broadcasted_iota(jnp.int32, qT.shape, 0)
    zero = jnp.zeros_like(qT)
    qm = (jnp.where(row < HEAD_DIM, qT, zero), jnp.where(row >= HEAD_DIM, qT, zero))

    def prev_bias(m, s):
        corner = s[tq - 128:, :128] + bp_ref[0, m]
        bottom = jnp.concatenate([corner, s[tq - 128:, 128:]], axis=1)
        return jnp.concatenate([s[:tq - 128], bottom], axis=0)

    def diag_bias(m, s):
        return s + bd_ref[0, m]

    def prev_diag_bias(m, s):
        return jnp.concatenate([prev_bias(m, s[:tq]), diag_bias(m, s[tq:])], axis=0)

    n_far = jnp.maximum(qi - 1, 0)

    def walk(tile_fn):
        def far_body(j, carry):
            tile_fn(j, None)
            return carry

        lax.fori_loop(0, n_far, far_body, 0)

        @pl.when(qi >= 1)
        def _():
            tile_fn(qi - 1, prev_bias)

        tile_fn(qi, diag_bias)

    span = []
    for m in range(2):
        qf = qm[m].astype(F32)
        qn = jnp.sqrt(jnp.sum(qf * qf, axis=0, keepdims=True))
        reach = qn * (jnp.sqrt(kn_ref[m:m + 1, :]) * 1.001)
        m_ref[m] = reach + bs_ref[m:m + 1, 0:1]
        span.append(jnp.max(2.0 * reach + bs_ref[m:m + 1, 1:2]))
    bound_is_tight = jnp.maximum(span[0], span[1]) <= SOFTMAX_BOUND_LIMIT

    @pl.when(jnp.logical_not(bound_is_tight))
    def _():
        m_ref[...] = jnp.full(m_ref.shape, NEG, F32)

        def max_tile(j, bias_fn):
            k = k_ref[j]
            for m in range(2):
                s = jnp.dot(k, qm[m], preferred_element_type=F32)
                if bias_fn is not None:
                    s = bias_fn(m, s)
                m_ref[m] = jnp.maximum(m_ref[m], jnp.max(s, axis=0, keepdims=True))

        walk(max_tile)

    l_ref[...] = jnp.zeros(l_ref.shape, F32)
    acc_ref[...] = jnp.zeros(acc_ref.shape, F32)

    def acc_tiles(js, bias_fn):
        k = jnp.concatenate([k_ref[j] for j in js], axis=0)
        vT = jnp.concatenate([vT_ref[j] for j in js], axis=1)
        nk = len(js) * tq
        for m in range(2):
            s = jnp.dot(k, qm[m], preferred_element_type=F32)
            if bias_fn is not None:
                s = bias_fn(m, s)
            p = jnp.exp2(s - m_ref[m])
            l_ref[m] += jnp.sum(p.reshape(nk // 8, 8, tq), axis=0)
            acc_ref[m] += jnp.dot(vT, p.astype(BF16), preferred_element_type=F32)

    def far_pair(jj, carry):
        acc_tiles([2 * jj, 2 * jj + 1], None)
        return carry

    lax.fori_loop(0, n_far // 2, far_pair, 0)

    @pl.when(n_far % 2 == 1)
    def _():
        acc_tiles([n_far - 1], None)

    @pl.when(qi >= 1)
    def _():
        acc_tiles([qi - 1, qi], prev_diag_bias)

    @pl.when(qi == 0)
    def _():
        acc_tiles([qi], diag_bias)

    lam = _diff_lambda(lam_ref, lam_init)
    o0 = acc_ref[0] * (1.0 / jnp.sum(l_ref[0], axis=0, keepdims=True))
    o1 = acc_ref[1] * (1.0 / jnp.sum(l_ref[1], axis=0, keepdims=True))
    o = o0 - lam * o1
    ms = jnp.mean(o * o, axis=0, keepdims=True)
    o = o * lax.rsqrt(ms + EPS) * sub_ref[...] * (1.0 - lam_init)
    o_ref[...] = o.T.astype(BF16)


def _diff_prompt(qT, k16, vT, bias_d, bias_p, knorm2, bstats, lam_p, subln, lam_init):
    n_tok = k16.shape[0]
    tq = min(DIFF_TILE, n_tok)
    nb = n_tok // tq
    k3 = k16.reshape(nb, tq, D_MODEL)
    return pl.pallas_call(
        functools.partial(_diff_prompt_kernel, tq=tq, lam_init=lam_init),
        grid=(DIFF_HEADS, nb),
        in_specs=[
            pl.BlockSpec((2 * HEAD_DIM, tq), lambda h, i: (h, i)),
            pl.BlockSpec((nb, tq, 2 * HEAD_DIM), lambda h, i: (0, 0, h)),
            pl.BlockSpec((nb, 2 * HEAD_DIM, tq), lambda h, i: (0, h, 0)),
            pl.BlockSpec((1, 2, tq, tq), lambda h, i: (h, 0, 0, 0)),
            pl.BlockSpec((1, 2, 128, 128), lambda h, i: (h, 0, 0, 0)),
            pl.BlockSpec((None, 2, 1), lambda h, i: (h, 0, 0)),
            pl.BlockSpec((None, 2, 2), lambda h, i: (h, 0, 0)),
            pl.BlockSpec((4, HEAD_DIM), lambda h, i: (0, 0)),
            pl.BlockSpec((2 * HEAD_DIM, 1), lambda h, i: (0, 0)),
        ],
        out_specs=pl.BlockSpec((tq, 2 * HEAD_DIM), lambda h, i: (i, h)),
        out_shape=jax.ShapeDtypeStruct((n_tok, D_MODEL), BF16),
        scratch_shapes=[
            pltpu.VMEM((2, 1, tq), F32),
            pltpu.VMEM((2, 8, tq), F32),
            pltpu.VMEM((2, 2 * HEAD_DIM, tq), F32),
        ],
        compiler_params=_cparams(2),
        name="diff_prompt",
    )(qT, k3, vT, bias_d, bias_p, knorm2, bstats, lam_p, subln)


def _diff_sample_kernel(q_ref, ck_ref, cv_ref, kn_ref, vn_ref, b_ref, lam_ref, sub_ref, o_ref, *, lam_init, past):
    t = q_ref.shape[2]
    lane = lax.broadcasted_iota(jnp.int32, (t, 2 * HEAD_DIM), 1)
    lam = _diff_lambda(lam_ref, lam_init)
    for h in range(DIFF_HEADS):
        q = q_ref[0, h]
        zero = jnp.zeros_like(q)
        qq = jnp.concatenate([jnp.where(lane < HEAD_DIM, q, zero), jnp.where(lane >= HEAD_DIM, q, zero)], axis=0)
        ck = ck_ref[pl.ds(h, past, stride=DIFF_HEADS), :].astype(BF16)
        cv = cv_ref[pl.ds(h, past, stride=DIFF_HEADS), :].astype(BF16)
        kn = kn_ref[0, h]
        vn = vn_ref[0, h].astype(BF16)
        s_c = lax.dot_general(qq, ck, NT_DIMS, preferred_element_type=F32)
        s_n = lax.dot_general(qq, kn, NT_DIMS, preferred_element_type=F32)
        bias = b_ref[h].reshape(2 * t, 128 + t)
        s_far = s_c[:, :past - 128]
        s_near = s_c[:, past - 128:] + bias[:, :128]
        s_n = s_n + bias[:, 128:]
        m = jnp.maximum(jnp.maximum(jnp.max(s_far, axis=1, keepdims=True), jnp.max(s_near, axis=1, keepdims=True)),
                        jnp.max(s_n, axis=1, keepdims=True))
        p_far = jnp.exp2(s_far - m)
        p_near = jnp.exp2(s_near - m)
        p_n = jnp.exp2(s_n - m)
        l = (jnp.sum(p_far, axis=1, keepdims=True) + jnp.sum(p_near, axis=1, keepdims=True)
             + jnp.sum(p_n, axis=1, keepdims=True))
        o = (jnp.dot(p_far.astype(BF16), cv[:past - 128], preferred_element_type=F32)
             + jnp.dot(p_near.astype(BF16), cv[past - 128:], preferred_element_type=F32)
             + jnp.dot(p_n.astype(BF16), vn, preferred_element_type=F32))
        o = o * (1.0 / l)
        od = o[:t] - lam * o[t:]
        ms = jnp.mean(od * od, axis=-1, keepdims=True)
        od = od * lax.rsqrt(ms + EPS) * sub_ref[...] * (1.0 - lam_init)
        o_ref[0, h] = od.astype(BF16)


def _diff_sample(q_s, cache_k, cache_v, layer, kn, vn, bias_s, lam_p, subln_row, lam_init):
    nb, _, t, _ = q_s.shape
    past = cache_k.shape[2]
    rows = past * DIFF_HEADS
    ck = cache_k.reshape(cache_k.shape[0], nb, rows, 2 * HEAD_DIM)
    cv = cache_v.reshape(cache_v.shape[0], nb, rows, 2 * HEAD_DIM)
    cache_spec = pl.BlockSpec((None, None, rows, 2 * HEAD_DIM), lambda b: (layer, b, 0, 0))
    head_spec = pl.BlockSpec((1, DIFF_HEADS, t, 2 * HEAD_DIM), lambda b: (b, 0, 0, 0))
    return pl.pallas_call(
        functools.partial(_diff_sample_kernel, lam_init=lam_init, past=past),
        grid=(nb,),
        in_specs=[
            head_spec, cache_spec, cache_spec, head_spec, head_spec,
            pl.BlockSpec((DIFF_HEADS, 2, t, 128 + t), lambda b: (0, 0, 0, 0)),
            pl.BlockSpec((4, HEAD_DIM), lambda b: (0, 0)),
            pl.BlockSpec((1, 2 * HEAD_DIM), lambda b: (0, 0)),
        ],
        out_specs=head_spec,
        out_shape=jax.ShapeDtypeStruct((nb, DIFF_HEADS, t, 2 * HEAD_DIM), BF16),
        compiler_params=_cparams(1),
        name="diff_sample",
    )(q_s, ck, cv, kn, vn, bias_s, lam_p, subln_row)


def _swa_prompt_kernel(qT_ref, kc_ref, kp_ref, vc_ref, vp_ref, b_ref, sink_ref, o_ref):
    i = pl.program_id(0)
    qT = qT_ref[...]
    kp = kp_ref[...]
    kc = kc_ref[...]
    vp = vp_ref[0]
    vc = vc_ref[0]
    tq = qT.shape[1]
    key_row = lax.broadcasted_iota(jnp.int32, (2 * tq, tq), 0)
    has_prev = key_row >= jnp.where(i > 0, 0, tq)
    zeros = jnp.zeros((HEAD_DIM, tq), BF16)
    outs = []
    for kv in range(SWA_KV_HEADS):
        pair = kv // 2
        k_band = jnp.concatenate([kp[:, pair * 128:(pair + 1) * 128], kc[:, pair * 128:(pair + 1) * 128]], axis=0)
        v_band = jnp.concatenate([vp[kv * HEAD_DIM:(kv + 1) * HEAD_DIM], vc[kv * HEAD_DIM:(kv + 1) * HEAD_DIM]],
                                 axis=1)
        for g in range(SWA_GROUP):
            h = kv * SWA_GROUP + g
            qh = qT[h * HEAD_DIM:(h + 1) * HEAD_DIM]
            qpad = jnp.concatenate([qh, zeros] if kv % 2 == 0 else [zeros, qh], axis=0)
            s = jnp.dot(k_band, qpad, preferred_element_type=F32) + b_ref[h]
            s = jnp.where(has_prev, s, NEG)
            sink = sink_ref[h]
            m = jnp.maximum(jnp.max(s, axis=0, keepdims=True), sink)
            p = jnp.exp2(s - m)
            den = jnp.sum(p, axis=0, keepdims=True) + jnp.exp2(sink - m)
            oT = jnp.dot(v_band, p.astype(BF16), preferred_element_type=F32) * (1.0 / den)
            outs.append(oT)
    o_ref[...] = jnp.concatenate(outs, axis=0).T.astype(BF16)


def _swa_prompt(qT, k16, vT, bias, sinks):
    n_tok = k16.shape[0]
    tq = SWA_TILE
    nkv = SWA_KV_HEADS * HEAD_DIM
    return pl.pallas_call(
        _swa_prompt_kernel,
        grid=(n_tok // tq,),
        in_specs=[
            pl.BlockSpec((D_MODEL, tq), lambda i: (0, i)),
            pl.BlockSpec((tq, nkv), lambda i: (i, 0)),
            pl.BlockSpec((tq, nkv), lambda i: (jnp.maximum(i - 1, 0), 0)),
            pl.BlockSpec((1, nkv, tq), lambda i: (i, 0, 0)),
            pl.BlockSpec((1, nkv, tq), lambda i: (jnp.maximum(i - 1, 0), 0, 0)),
            _resident((SWA_Q_HEADS, 2 * tq, tq), lambda i: (0, 0, 0)),
            _resident((SWA_Q_HEADS, 1, 1), lambda i: (0, 0, 0)),
        ],
        out_specs=pl.BlockSpec((tq, D_MODEL), lambda i: (i, 0)),
        out_shape=jax.ShapeDtypeStruct((n_tok, D_MODEL), BF16),
        compiler_params=_cparams(1),
        name="swa_prompt",
    )(qT, k16, k16, vT, vT, bias, sinks)


def _swa_sample_kernel(q_ref, kc_ref, vc_ref, kn_ref, vn_ref, b_ref, sink_ref, o_ref, ko_ref, vo_ref):
    t = kn_ref.shape[2]
    buf = kc_ref.shape[2]
    for kv in range(SWA_KV_HEADS):
        kc32 = kc_ref[0, kv]
        vc32 = vc_ref[0, kv]
        kn32 = kn_ref[0, kv]
        vn32 = vn_ref[0, kv]
        ko_ref[0, kv, 0:buf - t, :] = kc32[t:]
        ko_ref[0, kv, buf - t:buf, :] = kn32
        vo_ref[0, kv, 0:buf - t, :] = vc32[t:]
        vo_ref[0, kv, buf - t:buf, :] = vn32
        kc = kc32.astype(BF16)
        vc = vc32.astype(BF16)
        kn = kn32.astype(BF16)
        vn = vn32.astype(BF16)
        for g in range(SWA_GROUP):
            h = kv * SWA_GROUP + g
            q = q_ref[0, h]
            b = b_ref[h]
            s_c = lax.dot_general(q, kc, NT_DIMS, preferred_element_type=F32) + b[:, :buf]
            s_n = lax.dot_general(q, kn, NT_DIMS, preferred_element_type=F32) + b[:, buf:]
            sink = sink_ref[h]
            m = jnp.maximum(jnp.maximum(jnp.max(s_c, axis=1, keepdims=True), jnp.max(s_n, axis=1, keepdims=True)),
                            sink)
            p_c = jnp.exp2(s_c - m)
            p_n = jnp.exp2(s_n - m)
            den = (jnp.sum(p_c, axis=1, keepdims=True) + jnp.sum(p_n, axis=1, keepdims=True)
                   + jnp.exp2(sink - m))
            o = (jnp.dot(p_c.astype(BF16), vc, preferred_element_type=F32)
                 + jnp.dot(p_n.astype(BF16), vn, preferred_element_type=F32)) * (1.0 / den)
            o_ref[0, h] = o.astype(BF16)


def _swa_sample(q_s, kc, vc, kn, vn, bias, sinks):
    nb, _, t, _ = q_s.shape
    buf = kc.shape[2]
    kvspec = pl.BlockSpec((1, SWA_KV_HEADS, buf, HEAD_DIM), lambda b: (b, 0, 0, 0))
    nspec = pl.BlockSpec((1, SWA_KV_HEADS, t, HEAD_DIM), lambda b: (b, 0, 0, 0))
    return pl.pallas_call(
        _swa_sample_kernel,
        grid=(nb,),
        in_specs=[
            pl.BlockSpec((1, SWA_Q_HEADS, t, HEAD_DIM), lambda b: (b, 0, 0, 0)),
            kvspec, kvspec, nspec, nspec,
            pl.BlockSpec((SWA_Q_HEADS, t, buf + t), lambda b: (0, 0, 0)),
            pl.BlockSpec((SWA_Q_HEADS, 1, 1), lambda b: (0, 0, 0)),
        ],
        out_specs=[
            pl.BlockSpec((1, SWA_Q_HEADS, t, HEAD_DIM), lambda b: (b, 0, 0, 0)),
            kvspec, kvspec,
        ],
        out_shape=[
            jax.ShapeDtypeStruct((nb, SWA_Q_HEADS, t, HEAD_DIM), BF16),
            jax.ShapeDtypeStruct((nb, SWA_KV_HEADS, buf, HEAD_DIM), F32),
            jax.ShapeDtypeStruct((nb, SWA_KV_HEADS, buf, HEAD_DIM), F32),
        ],
        compiler_params=_cparams(1),
        name="swa_sample",
    )(q_s, kc, vc, kn, vn, bias, sinks)


def _softplus(z):
    return jnp.maximum(z, 0.0) + jnp.log(1.0 + jnp.exp(-jnp.abs(z)))


def _split_bf16(x):
    hi = x.astype(BF16)
    lo = (x - hi.astype(F32)).astype(BF16)
    return hi, lo


def _sb_prompt_kernel(qT_ref, k_ref, vT_ref, kmax_ref, o_ref, carry_ref, acc_ref, *, tq):
    qi = pl.program_id(1)
    qT = qT_ref[...]
    row = lax.broadcasted_iota(jnp.int32, qT.shape, 0)
    zero = jnp.zeros_like(qT)
    qpair = jnp.concatenate([jnp.where(row < HEAD_DIM, qT, zero), jnp.where(row >= HEAD_DIM, qT, zero)], axis=1)
    zb = jnp.sum(jnp.abs(qpair.astype(F32)) * kmax_ref[...], axis=0, keepdims=True)
    kr = lax.broadcasted_iota(jnp.int32, (tq, tq), 0)
    kc = lax.broadcasted_iota(jnp.int32, (tq, tq), 1)
    tri = jnp.where(kc >= kr, 1.0, 0.0).astype(BF16)
    kr2 = lax.broadcasted_iota(jnp.int32, (tq, 2 * tq), 0)
    kc2 = lax.broadcasted_iota(jnp.int32, (tq, 2 * tq), 1)
    causal = kr2 < jnp.where(kc2 >= tq, kc2 - tq, kc2)
    carry_ref[...] = jnp.zeros(carry_ref.shape, F32)
    acc_ref[...] = jnp.zeros(acc_ref.shape, F32)

    def tile(j, diag):
        k = k_ref[j]
        vT = vT_ref[j]
        z = jnp.dot(k, qpair, preferred_element_type=F32)
        lsp = _softplus(z)
        if diag:
            lsp = jnp.where(causal, lsp, 0.0)
        hi, lo = _split_bf16(lsp)
        c = jnp.dot(tri, hi, preferred_element_type=F32) + jnp.dot(tri, lo, preferred_element_type=F32)
        a = jnp.exp(z - c - carry_ref[...])
        if diag:
            a = jnp.where(causal, a, 0.0)
        acc_ref[...] += jnp.dot(vT, a.astype(BF16), preferred_element_type=F32)
        carry_ref[...] += c[0:1]

    def still_live():
        return (jnp.min(carry_ref[...] - zb) < SB_SKIP_MARGIN).astype(jnp.int32)

    tile(qi, True)

    def cond(state):
        j, live = state
        return jnp.logical_and(j >= 0, live > 0)

    def body(state):
        j, _ = state
        tile(j, False)
        return j - 1, still_live()

    lax.while_loop(cond, body, (qi - 1, still_live()))
    o = jnp.concatenate([acc_ref[0:HEAD_DIM, 0:tq], acc_ref[HEAD_DIM:2 * HEAD_DIM, tq:2 * tq]], axis=0)
    o_ref[...] = o.T.astype(BF16)


def _sb_prompt(qT, k16, vT, kmax):
    n_tok = k16.shape[0]
    tq = min(SB_TILE, n_tok)
    nb = n_tok // tq
    k3 = k16.reshape(nb, tq, D_MODEL)
    return pl.pallas_call(
        functools.partial(_sb_prompt_kernel, tq=tq),
        grid=(SB_HEADS // 2, nb),
        in_specs=[
            pl.BlockSpec((128, tq), lambda p, i: (p, i)),
            pl.BlockSpec((nb, tq, 128), lambda p, i: (0, 0, p)),
            pl.BlockSpec((nb, 128, tq), lambda p, i: (0, p, 0)),
            pl.BlockSpec((128, 1), lambda p, i: (p, 0)),
        ],
        out_specs=pl.BlockSpec((tq, 128), lambda p, i: (i, p)),
        out_shape=jax.ShapeDtypeStruct((n_tok, D_MODEL), BF16),
        scratch_shapes=[
            pltpu.VMEM((1, 2 * tq), F32),
            pltpu.VMEM((2 * HEAD_DIM, 2 * tq), F32),
        ],
        compiler_params=_cparams(2),
        name="sb_prompt",
    )(qT, k3, vT, kmax)


def _tri_lanes(n):
    r = lax.broadcasted_iota(jnp.int32, (n, n), 0)
    c = lax.broadcasted_iota(jnp.int32, (n, n), 1)
    return jnp.where(r >= c, 1.0, 0.0).astype(BF16)


def _sb_sample_kernel(q_ref, ck_ref, cv_ref, kn_ref, vn_ref, o_ref, carry_ref, acc_ref, *, ch, kb):
    c = pl.program_id(1)
    nh = q_ref.shape[1]
    t = q_ref.shape[2]

    @pl.when(c == 0)
    def _():
        z = jnp.concatenate([lax.dot_general(q_ref[0, h], kn_ref[0, h], NT_DIMS, preferred_element_type=F32)
                             for h in range(nh)], axis=0)
        r = lax.broadcasted_iota(jnp.int32, (nh * t, t), 0)
        kc = lax.broadcasted_iota(jnp.int32, (nh * t, t), 1)
        causal = kc < lax.rem(r, t)
        hi, lo = _split_bf16(jnp.where(causal, _softplus(z), 0.0))
        tri = _tri_lanes(t)
        cs = jnp.dot(hi, tri, preferred_element_type=F32) + jnp.dot(lo, tri, preferred_element_type=F32)
        a = jnp.where(causal, jnp.exp(z - cs), 0.0).astype(BF16)
        carry_ref[...] = cs[:, 0:1]
        for h in range(nh):
            acc_ref[h * t:(h + 1) * t, :] = jnp.dot(a[h * t:(h + 1) * t], vn_ref[0, h].astype(BF16),
                                                    preferred_element_type=F32)

    z = jnp.concatenate([
        lax.dot_general(q_ref[0, h], ck_ref[pl.ds(h, ch, stride=nh), :].astype(BF16), NT_DIMS,
                        preferred_element_type=F32) for h in range(nh)], axis=0)
    nblk = ch // kb
    rows = nh * t
    zst = jnp.concatenate([z[:, b * kb:(b + 1) * kb] for b in range(nblk)], axis=0)
    hi, lo = _split_bf16(_softplus(zst))
    tri = _tri_lanes(kb)
    cs = jnp.dot(hi, tri, preferred_element_type=F32) + jnp.dot(lo, tri, preferred_element_type=F32)
    run = carry_ref[...]
    carries = [None] * nblk
    for b in reversed(range(nblk)):
        carries[b] = run
        run = run + cs[b * rows:(b + 1) * rows, 0:1]
    carry_ref[...] = run
    a = jnp.exp(zst - cs - jnp.concatenate(carries, axis=0))
    a = jnp.concatenate([a[b * rows:(b + 1) * rows] for b in range(nblk)], axis=1).astype(BF16)
    for h in range(nh):
        acc_ref[h * t:(h + 1) * t, :] += jnp.dot(a[h * t:(h + 1) * t],
                                                 cv_ref[pl.ds(h, ch, stride=nh), :].astype(BF16),
                                                 preferred_element_type=F32)

    @pl.when(c == pl.num_programs(1) - 1)
    def _():
        for h in range(nh):
            o_ref[0, h] = acc_ref[h * t:(h + 1) * t, :].astype(BF16)


def _sb_sample(q_s, cache_k, cache_v, layer, kn, vn):
    nb, nh, t, _ = q_s.shape
    past = cache_k.shape[2]
    ch = min(SB_SAMPLE_CHUNK, past)
    nch = past // ch
    ck = cache_k.reshape(cache_k.shape[0], nb, past * nh, HEAD_DIM)
    cv = cache_v.reshape(cache_v.shape[0], nb, past * nh, HEAD_DIM)
    cache_spec = pl.BlockSpec((None, None, ch * nh, HEAD_DIM), lambda b, c: (layer, b, nch - 1 - c, 0))
    head_spec = pl.BlockSpec((1, nh, t, HEAD_DIM), lambda b, c: (b, 0, 0, 0))
    return pl.pallas_call(
        functools.partial(_sb_sample_kernel, ch=ch, kb=min(256, ch)),
        grid=(nb, nch),
        in_specs=[head_spec, cache_spec, cache_spec, head_spec, head_spec],
        out_specs=head_spec,
        out_shape=jax.ShapeDtypeStruct((nb, nh, t, HEAD_DIM), BF16),
        scratch_shapes=[
            pltpu.VMEM((nh * t, 1), F32),
            pltpu.VMEM((nh * t, HEAD_DIM), F32),
        ],
        compiler_params=_cparams(2),
        name="sb_sample",
    )(q_s, ck, cv, kn, vn)


def _tile_col(gain, n, scale):
    return (jnp.tile(gain.astype(F32), n // gain.shape[0]) * scale).reshape(n, 1)


def _heads_major(x, nb, t, nh, hd):
    return x.reshape(nb, t, nh, hd).transpose(0, 2, 1, 3)


def _diff_layer(xp, xs, cache_k, cache_v, layer, rel_bias, g, w_qkv, q_gain, k_gain, lam_p, subln, lam_init):
    n_p = xp.shape[0]
    nb = cache_k.shape[1]
    t = xs.shape[0] // nb
    wt = w_qkv.T.astype(BF16)
    qg = _tile_col(q_gain, D_MODEL, HEAD_DIM ** -0.5 * LOG2E)
    kg = _tile_col(k_gain, D_MODEL, 1.0)
    g = g.reshape(1, D_MODEL)
    kw = dict(nq=D_MODEL, nk=D_MODEL, nv=D_MODEL, head_norm=True)
    tq = min(DIFF_TILE, n_p)
    qT, k32, k16, v32, vT, knorm2 = _project(xp, g, wt, qg, kg, vb=tq, kstat="normsq", **kw)
    qT_s, k32_s, k16_s, v32_s, _ = _project(xs, g, wt, qg, kg, vb=128, **kw)

    j = jnp.arange(tq)
    valid_d = (j // CHUNK)[:, None] <= (j // CHUNK)[None, :]
    bias_d = _bias_tile(rel_bias, tq, tq, 0, valid_d, True).reshape(DIFF_HEADS, 2, tq, tq)
    bias_p = _bias_tile(rel_bias, 128, 128, -128, None, True).reshape(DIFF_HEADS, 2, 128, 128)
    bstats = _bias_stats(rel_bias).reshape(DIFF_HEADS, 2, 2)
    o_p = _diff_prompt(qT, k16, vT, bias_d, bias_p, knorm2.reshape(DIFF_HEADS, 2, 1), bstats, lam_p,
                       subln.reshape(2 * HEAD_DIM, 1), lam_init)

    bias_s = jnp.swapaxes(_bias_tile(rel_bias, 128 + t, t, -128, None, True), 1, 2)
    bias_s = bias_s.reshape(DIFF_HEADS, 2, t, 128 + t)
    hm = functools.partial(_heads_major, nb=nb, t=t, nh=DIFF_HEADS, hd=2 * HEAD_DIM)
    o_s = _diff_sample(hm(qT_s.T), cache_k, cache_v, layer, hm(k16_s), hm(v32_s), bias_s, lam_p,
                       subln.reshape(1, 2 * HEAD_DIM), lam_init)
    o_s = o_s.transpose(0, 2, 1, 3).reshape(nb * t, D_MODEL)
    return (o_p, o_s,
            k32.reshape(1, n_p, DIFF_HEADS, 2 * HEAD_DIM), v32.reshape(1, n_p, DIFF_HEADS, 2 * HEAD_DIM),
            k32_s.reshape(nb, t, DIFF_HEADS, 2 * HEAD_DIM), v32_s.reshape(nb, t, DIFF_HEADS, 2 * HEAD_DIM))


def _swa_layer(xp, xs, ck, cv, rel_bias, g, w_qkv, q_gain, k_gain, sinks):
    n_p = xp.shape[0]
    nb, buf = ck.shape[0], ck.shape[1]
    t = xs.shape[0] // nb
    nkv = SWA_KV_HEADS * HEAD_DIM
    wt = w_qkv.T.astype(BF16)
    qg = _tile_col(q_gain, D_MODEL, HEAD_DIM ** -0.5 * LOG2E)
    kg = _tile_col(k_gain, nkv, 1.0)
    g = g.reshape(1, D_MODEL)
    kw = dict(nq=D_MODEL, nk=nkv, nv=nkv, vb=SWA_TILE, head_norm=True)
    qT, k32, k16, v32, vT = _project(xp, g, wt, qg, kg, **kw)
    qT_s, k32_s, _, v32_s, _ = _project(xs, g, wt, qg, kg, **kw)
    sink_col = (sinks.astype(F32) * LOG2E).reshape(SWA_Q_HEADS, 1, 1)
    wchunks = WINDOW // CHUNK

    tq = SWA_TILE
    kchunk = jnp.floor_divide(jnp.arange(2 * tq) - tq, CHUNK)[:, None]
    qchunk = (jnp.arange(tq) // CHUNK)[None, :]
    valid = (kchunk <= qchunk) & (kchunk >= qchunk - wchunks)
    bias = _bias_tile(rel_bias, 2 * tq, tq, -tq, valid, False)
    o_p = _swa_prompt(qT, k16, vT, bias, sink_col)

    sc = jnp.floor_divide(jnp.arange(buf + t) - buf, CHUNK)[:, None]
    tc = (jnp.arange(t) // CHUNK)[None, :]
    valid_s = (sc <= tc) & (sc >= tc - wchunks)
    bias_s = jnp.swapaxes(_bias_tile(rel_bias, buf + t, t, -buf, valid_s, False), 1, 2)
    q_s = _heads_major(qT_s.T, nb, t, SWA_Q_HEADS, HEAD_DIM)
    kn = _heads_major(k32_s, nb, t, SWA_KV_HEADS, HEAD_DIM)
    vn = _heads_major(v32_s, nb, t, SWA_KV_HEADS, HEAD_DIM)
    o_s, ko, vo = _swa_sample(q_s, ck.transpose(0, 2, 1, 3), cv.transpose(0, 2, 1, 3), kn, vn, bias_s, sink_col)
    o_s = o_s.transpose(0, 2, 1, 3).reshape(nb * t, D_MODEL)
    wbuf = min(WINDOW, n_p)
    return (o_p, o_s,
            k32[n_p - wbuf:].reshape(1, wbuf, SWA_KV_HEADS, HEAD_DIM),
            v32[n_p - wbuf:].reshape(1, wbuf, SWA_KV_HEADS, HEAD_DIM),
            ko.transpose(0, 2, 1, 3), vo.transpose(0, 2, 1, 3))


def _sb_layer(xp, xs, cache_k, cache_v, layer, g, w_qkv):
    n_p = xp.shape[0]
    nb = cache_k.shape[1]
    t = xs.shape[0] // nb
    wt = w_qkv.T.astype(BF16)
    qg = jnp.full((D_MODEL, 1), HEAD_DIM ** -0.5, F32)
    kg = jnp.ones((D_MODEL, 1), F32)
    g = g.reshape(1, D_MODEL)
    kw = dict(nq=D_MODEL, nk=D_MODEL, nv=D_MODEL, head_norm=False)
    tq = min(SB_TILE, n_p)
    qT, k32, k16, v32, vT, kmax = _project(xp, g, wt, qg, kg, vb=tq, kstat="absmax", **kw)
    qT_s, k32_s, k16_s, v32_s, _ = _project(xs, g, wt, qg, kg, vb=128, **kw)
    o_p = _sb_prompt(qT, k16, vT, kmax)
    hm = functools.partial(_heads_major, nb=nb, t=t, nh=SB_HEADS, hd=HEAD_DIM)
    o_s = _sb_sample(hm(qT_s.T), cache_k, cache_v, layer, hm(k16_s), hm(v32_s))
    o_s = o_s.transpose(0, 2, 1, 3).reshape(nb * t, D_MODEL)
    return (o_p, o_s,
            k32.reshape(1, n_p, SB_HEADS, HEAD_DIM), v32.reshape(1, n_p, SB_HEADS, HEAD_DIM),
            k32_s.reshape(nb, t, SB_HEADS, HEAD_DIM), v32_s.reshape(nb, t, SB_HEADS, HEAD_DIM))


def kernel(x_prompt, x_sample, cache_diff_k, cache_diff_v, cache_swa_k, cache_swa_v, cache_sb_k, cache_sb_v, rel_bias, norm_mix, norm_mlp, w_up, w_down, diff_w_qkv, diff_w_o, diff_q_norm, diff_k_norm, diff_lambda, diff_subln, swa_w_qkv, swa_w_o, swa_q_norm, swa_k_norm, swa_sinks, sb_w_qkv, sb_w_o):
    bp, n_p, _ = x_prompt.shape
    assert bp == 1
    nb, t, _ = x_sample.shape
    depth = norm_mix.shape[0]
    xp = x_prompt.reshape(n_p, D_MODEL)
    xs = x_sample.reshape(nb * t, D_MODEL)
    outs = {name: [] for name in ("pdk", "pdv", "pwk", "pwv", "pbk", "pbv", "sdk", "sdv", "swk", "swv", "sbk", "sbv")}
    for i in range(depth):
        j = i // N_MIXERS
        if i % N_MIXERS == 0:
            lam_init = 0.8 - 0.6 * math.exp(-0.3 * i)
            o_p, o_s, kp, vp, kn, vn = _diff_layer(
                xp, xs, cache_diff_k, cache_diff_v, j, rel_bias, norm_mix[i], diff_w_qkv[j],
                diff_q_norm[j], diff_k_norm[j], diff_lambda[j], diff_subln[j], lam_init)
            w_o = diff_w_o[j]
            names = ("pdk", "pdv", "sdk", "sdv")
        elif i % N_MIXERS == 1:
            o_p, o_s, kp, vp, kn, vn = _swa_layer(
                xp, xs, cache_swa_k[j], cache_swa_v[j], rel_bias, norm_mix[i], swa_w_qkv[j],
                swa_q_norm[j], swa_k_norm[j], swa_sinks[j])
            w_o = swa_w_o[j]
            names = ("pwk", "pwv", "swk", "swv")
        else:
            o_p, o_s, kp, vp, kn, vn = _sb_layer(xp, xs, cache_sb_k, cache_sb_v, j, norm_mix[i], sb_w_qkv[j])
            w_o = sb_w_o[j]
            names = ("pbk", "pbv", "sbk", "sbv")
        for name, val in zip(names, (kp, vp, kn, vn)):
            outs[name].append(val)
        wo16 = w_o.astype(BF16)
        g_mlp = norm_mlp[i].reshape(1, D_MODEL)
        wup16 = w_up[i].astype(BF16)
        wdn16 = w_down[i].astype(BF16)
        xp = _outproj_mlp(xp, o_p, wo16, g_mlp, wup16, wdn16)
        xs = _outproj_mlp(xs, o_s, wo16, g_mlp, wup16, wdn16)
    st = {name: jnp.stack(v) for name, v in outs.items()}
    return (xp.reshape(1, n_p, D_MODEL), xs.reshape(nb, t, D_MODEL),
            st["pdk"], st["pdv"], st["pwk"], st["pwv"], st["pbk"], st["pbv"],
            st["sdk"], st["sdv"], st["swk"], st["swv"], st["sbk"], st["sbv"])
```

```python
import functools
import math

import jax
import jax.numpy as jnp
from jax import lax
from jax.experimental import pallas as pl
from jax.experimental.pallas import tpu as pltpu

F32 = jnp.float32
BF16 = jnp.bfloat16

D_MODEL = 1024
HEAD_DIM = 64
CHUNK = 64
N_MIXERS = 3
DIFF_HEADS = 8
SWA_Q_HEADS = 16
SWA_KV_HEADS = 4
SWA_GROUP = 4
WINDOW = 128
SB_HEADS = 16
N_BUCKETS = 32
MAX_DISTANCE = 128
D_FF = 4 * D_MODEL
EPS = 1e-6
LOG2E = 1.4426950408889634
NEG = -1e30
FAR_BUCKET = 15
SB_SKIP_MARGIN = 100.0
SOFTMAX_BOUND_LIMIT = 100.0

VMEM_LIMIT = 56 * 1024 * 1024
TOKEN_TILE = 512
PROJ_CHUNK = 512
DIFF_TILE = 512
SB_TILE = 256
SB_SAMPLE_CHUNK = 512
SWA_TILE = 128

NT_DIMS = (((1,), (1,)), ((), ()))


def _cparams(n_axes):
    return pltpu.CompilerParams(dimension_semantics=("arbitrary",) * n_axes,
                                vmem_limit_bytes=VMEM_LIMIT)


def _resident(block_shape, index_map):
    return pl.BlockSpec(block_shape, index_map, pipeline_mode=pl.Buffered(1))


def _t5_bucket(rel):
    half = N_BUCKETS // 2
    exact = half // 2
    ret = jnp.where(rel > 0, half, 0).astype(jnp.int32)
    n = jnp.abs(rel)
    nf = jnp.maximum(n, 1).astype(F32)
    large = exact + (jnp.log(nf / exact) / math.log(MAX_DISTANCE / exact) * (half - exact)).astype(jnp.int32)
    large = jnp.minimum(large, half - 1)
    return ret + jnp.where(n < exact, n, large)


def _bias_tile(rel_bias, n_j, n_i, r0, valid, shift_far):
    tab = rel_bias.astype(F32)
    rel = jnp.arange(n_i + n_j - 1) + (r0 - (n_i - 1))
    vals = jnp.moveaxis(tab[_t5_bucket(rel)], -1, 0)
    if shift_far:
        vals = vals - tab[FAR_BUCKET][:, None]
    vals = vals * LOG2E
    period = n_i + n_j
    y = jnp.concatenate([jnp.flip(vals, axis=-1), jnp.zeros((vals.shape[0], 1), F32)], axis=-1)
    r = jnp.tile(y, (1, n_j))[:, :n_j * (period - 1)].reshape(vals.shape[0], n_j, period - 1)
    b = r[:, :, n_j - 1:n_j - 1 + n_i]
    if valid is not None:
        b = jnp.where(valid[None], b, NEG)
    return b


def _bias_stats(rel_bias):
    tab = rel_bias.astype(F32)
    sh = (tab - tab[FAR_BUCKET][None, :]) * LOG2E
    bmax = jnp.max(sh, axis=0)
    return jnp.stack([bmax, bmax - jnp.min(sh, axis=0)], axis=-1)


def _proj_kernel(x_ref, g_ref, wt_ref, qg_ref, kg_ref, *out_refs, nq, nk, nv, vb, head_norm, kstat):
    qT_ref, k32_ref, k16_ref, v32_ref, vT_ref = out_refs[:5]
    x = x_ref[...]
    ms = jnp.mean(x * x, axis=-1, keepdims=True)
    h = (x * lax.rsqrt(ms + EPS) * g_ref[...]).astype(BF16)
    tm = x.shape[0]

    def rows(r0, n):
        return lax.dot_general(wt_ref[r0:r0 + n, :], h, NT_DIMS, preferred_element_type=F32)

    def headnorm(t, gcol):
        n = t.shape[0]
        t3 = t.reshape(n // HEAD_DIM, HEAD_DIM, tm)
        r = lax.rsqrt(jnp.mean(t3 * t3, axis=1, keepdims=True) + EPS)
        return (t3 * r).reshape(n, tm) * gcol

    for r0 in range(0, nq, PROJ_CHUNK):
        n = min(PROJ_CHUNK, nq - r0)
        qt = rows(r0, n)
        qt = headnorm(qt, qg_ref[r0:r0 + n, :]) if head_norm else qt * qg_ref[r0:r0 + n, :]
        qT_ref[r0:r0 + n, :] = qt.astype(BF16)

    for r0 in range(0, nk, PROJ_CHUNK):
        n = min(PROJ_CHUNK, nk - r0)
        kt = rows(nq + r0, n)
        if head_norm:
            kt = headnorm(kt, kg_ref[r0:r0 + n, :])
        k = kt.T
        k32_ref[:, r0:r0 + n] = k
        k16_ref[:, r0:r0 + n] = k.astype(BF16)
        if kstat is not None:
            stat_ref = out_refs[5]
            kr = kt.astype(BF16).astype(F32)
            if kstat == "absmax":
                stat = jnp.max(jnp.abs(kr), axis=1, keepdims=True)
                srows = slice(r0, r0 + n)
            else:
                k3 = kr.reshape(n // HEAD_DIM, HEAD_DIM, tm)
                stat = jnp.max(jnp.sum(k3 * k3, axis=1), axis=1, keepdims=True)
                srows = slice(r0 // HEAD_DIM, (r0 + n) // HEAD_DIM)

            @pl.when(pl.program_id(0) == 0)
            def _(stat=stat, srows=srows):
                stat_ref[srows, :] = stat

            @pl.when(pl.program_id(0) > 0)
            def _(stat=stat, srows=srows):
                stat_ref[srows, :] = jnp.maximum(stat_ref[srows, :], stat)

    for r0 in range(0, nv, PROJ_CHUNK):
        n = min(PROJ_CHUNK, nv - r0)
        vt = rows(nq + nk + r0, n)
        v32_ref[:, r0:r0 + n] = vt.T
        vt16 = vt.astype(BF16)
        for b in range(tm // vb):
            vT_ref[b, r0:r0 + n, :] = vt16[:, b * vb:(b + 1) * vb]


def _project(x, g, wt, qg, kg, *, nq, nk, nv, vb, head_norm, kstat=None):
    n_tok = x.shape[0]
    tm = min(TOKEN_TILE, n_tok)
    vb = min(vb, tm)
    grid = (n_tok // tm,)
    ntot = nq + nk + nv
    out_shape = [
        jax.ShapeDtypeStruct((nq, n_tok), BF16),
        jax.ShapeDtypeStruct((n_tok, nk), F32),
        jax.ShapeDtypeStruct((n_tok, nk), BF16),
        jax.ShapeDtypeStruct((n_tok, nv), F32),
        jax.ShapeDtypeStruct((n_tok // vb, nv, vb), BF16),
    ]
    out_specs = [
        pl.BlockSpec((nq, tm), lambda i: (0, i)),
        pl.BlockSpec((tm, nk), lambda i: (i, 0)),
        pl.BlockSpec((tm, nk), lambda i: (i, 0)),
        pl.BlockSpec((tm, nv), lambda i: (i, 0)),
        pl.BlockSpec((tm // vb, nv, vb), lambda i: (i, 0, 0)),
    ]
    if kstat is not None:
        n_stat = nk if kstat == "absmax" else nk // HEAD_DIM
        out_shape.append(jax.ShapeDtypeStruct((n_stat, 1), F32))
        out_specs.append(pl.BlockSpec((n_stat, 1), lambda i: (0, 0)))
    return pl.pallas_call(
        functools.partial(_proj_kernel, nq=nq, nk=nk, nv=nv, vb=vb, head_norm=head_norm, kstat=kstat),
        grid=grid,
        in_specs=[
            pl.BlockSpec((tm, D_MODEL), lambda i: (i, 0)),
            _resident((1, D_MODEL), lambda i: (0, 0)),
            _resident((ntot, D_MODEL), lambda i: (0, 0)),
            _resident((nq, 1), lambda i: (0, 0)),
            _resident((nk, 1), lambda i: (0, 0)),
        ],
        out_specs=out_specs,
        out_shape=out_shape,
        compiler_params=_cparams(1),
        name="proj",
    )(x, g, wt, qg, kg)


def _mlp_kernel(x_ref, o_ref, wo_ref, g_ref, wup_ref, wdn_ref, y_ref, *, f_chunk):
    x1 = x_ref[...] + jnp.dot(o_ref[...], wo_ref[...], preferred_element_type=F32)
    ms = jnp.mean(x1 * x1, axis=-1, keepdims=True)
    hn = (x1 * lax.rsqrt(ms + EPS) * g_ref[...]).astype(BF16)
    acc = x1
    for f in range(D_FF // f_chunk):
        u = jnp.dot(hn, wup_ref[:, f * f_chunk:(f + 1) * f_chunk], preferred_element_type=F32)
        a = jnp.square(jnp.maximum(u, 0.0)).astype(BF16)
        acc = acc + jnp.dot(a, wdn_ref[f * f_chunk:(f + 1) * f_chunk, :], preferred_element_type=F32)
    y_ref[...] = acc


def _outproj_mlp(x, o, wo, g, wup, wdn):
    n_tok = x.shape[0]
    tm = min(TOKEN_TILE, n_tok)
    return pl.pallas_call(
        functools.partial(_mlp_kernel, f_chunk=1024),
        grid=(n_tok // tm,),
        in_specs=[
            pl.BlockSpec((tm, D_MODEL), lambda i: (i, 0)),
            pl.BlockSpec((tm, D_MODEL), lambda i: (i, 0)),
            _resident((D_MODEL, D_MODEL), lambda i: (0, 0)),
            _resident((1, D_MODEL), lambda i: (0, 0)),
            _resident((D_MODEL, D_FF), lambda i: (0, 0)),
            _resident((D_FF, D_MODEL), lambda i: (0, 0)),
        ],
        out_specs=pl.BlockSpec((tm, D_MODEL), lambda i: (i, 0)),
        out_shape=jax.ShapeDtypeStruct((n_tok, D_MODEL), F32),
        compiler_params=_cparams(1),
        name="outproj_mlp",
    )(x, o, wo, g, wup, wdn)


def _diff_lambda(lam_ref, lam_init):
    lp = lam_ref[...]
    a = jnp.sum(lp[0:1] * lp[1:2], axis=-1, keepdims=True)
    b = jnp.sum(lp[2:3] * lp[3:4], axis=-1, keepdims=True)
    return jnp.exp(a) - jnp.exp(b) + lam_init


def _diff_prompt_kernel(qT_ref, k_ref, vT_ref, bd_ref, bp_ref, kn_ref, bs_ref, lam_ref, sub_ref, o_ref,
                        m_ref, l_ref, acc_ref, *, tq, lam_init):
    qi = pl.program_id(1)
    qT = qT_ref[...]
    row = lax.broadcasted_iota(jnp.int32, qT.shape, 0)
    zero = jnp.zeros_like(qT)
    qm = (jnp.where(row < HEAD_DIM, qT, zero), jnp.where(row >= HEAD_DIM, qT, zero))

    def prev_bias(m, s):
        corner = s[tq - 128:, :128] + bp_ref[0, m]
        bottom = jnp.concatenate([corner, s[tq - 128:, 128:]], axis=1)
        return jnp.concatenate([s[:tq - 128], bottom], axis=0)

    def diag_bias(m, s):
        return s + bd_ref[0, m]

    def prev_diag_bias(m, s):
        return jnp.concatenate([prev_bias(m, s[:tq]), diag_bias(m, s[tq:])], axis=0)

    n_far = jnp.maximum(qi - 1, 0)

    def walk(tile_fn):
        def far_body(j, carry):
            tile_fn(j, None)
            return carry

        lax.fori_loop(0, n_far, far_body, 0)

        @pl.when(qi >= 1)
        def _():
            tile_fn(qi - 1, prev_bias)

        tile_fn(qi, diag_bias)

    span = []
    for m in range(2):
        qf = qm[m].astype(F32)
        qn = jnp.sqrt(jnp.sum(qf * qf, axis=0, keepdims=True))
        reach = qn * (jnp.sqrt(kn_ref[m:m + 1, :]) * 1.001)
        m_ref[m] = reach + bs_ref[m:m + 1, 0:1]
        span.append(jnp.max(2.0 * reach + bs_ref[m:m + 1, 1:2]))
    bound_is_tight = jnp.maximum(span[0], span[1]) <= SOFTMAX_BOUND_LIMIT

    @pl.when(jnp.logical_not(bound_is_tight))
    def _():
        m_ref[...] = jnp.full(m_ref.shape, NEG, F32)

        def max_tile(j, bias_fn):
            k = k_ref[j]
            for m in range(2):
                s = jnp.dot(k, qm[m], preferred_element_type=F32)
                if bias_fn is not None:
                    s = bias_fn(m, s)
                m_ref[m] = jnp.maximum(m_ref[m], jnp.max(s, axis=0, keepdims=True))

        walk(max_tile)

    l_ref[...] = jnp.zeros(l_ref.shape, F32)
    acc_ref[...] = jnp.zeros(acc_ref.shape, F32)

    def acc_tiles(js, bias_fn):
        k = jnp.concatenate([k_ref[j] for j in js], axis=0)
        vT = jnp.concatenate([vT_ref[j] for j in js], axis=1)
        nk = len(js) * tq
        for m in range(2):
            s = jnp.dot(k, qm[m], preferred_element_type=F32)
            if bias_fn is not None:
                s = bias_fn(m, s)
            p = jnp.exp2(s - m_ref[m])
            l_ref[m] += jnp.sum(p.reshape(nk // 8, 8, tq), axis=0)
            acc_ref[m] += jnp.dot(vT, p.astype(BF16), preferred_element_type=F32)

    def far_pair(jj, carry):
        acc_tiles([2 * jj, 2 * jj + 1], None)
        return carry

    lax.fori_loop(0, n_far // 2, far_pair, 0)

    @pl.when(n_far % 2 == 1)
    def _():
        acc_tiles([n_far - 1], None)

    @pl.when(qi >= 1)
    def _():
        acc_tiles([qi - 1, qi], prev_diag_bias)

    @pl.when(qi == 0)
    def _():
        acc_tiles([qi], diag_bias)

    lam = _diff_lambda(lam_ref, lam_init)
    o0 = acc_ref[0] * (1.0 / jnp.sum(l_ref[0], axis=0, keepdims=True))
    o1 = acc_ref[1] * (1.0 / jnp.sum(l_ref[1], axis=0, keepdims=True))
    o = o0 - lam * o1
    ms = jnp.mean(o * o, axis=0, keepdims=True)
    o = o * lax.rsqrt(ms + EPS) * sub_ref[...] * (1.0 - lam_init)
    o_ref[...] = o.T.astype(BF16)


def _diff_prompt(qT, k16, vT, bias_d, bias_p, knorm2, bstats, lam_p, subln, lam_init):
    n_tok = k16.shape[0]
    tq = min(DIFF_TILE, n_tok)
    nb = n_tok // tq
    k3 = k16.reshape(nb, tq, D_MODEL)
    return pl.pallas_call(
        functools.partial(_diff_prompt_kernel, tq=tq, lam_init=lam_init),
        grid=(DIFF_HEADS, nb),
        in_specs=[
            pl.BlockSpec((2 * HEAD_DIM, tq), lambda h, i: (h, i)),
            pl.BlockSpec((nb, tq, 2 * HEAD_DIM), lambda h, i: (0, 0, h)),
            pl.BlockSpec((nb, 2 * HEAD_DIM, tq), lambda h, i: (0, h, 0)),
            pl.BlockSpec((1, 2, tq, tq), lambda h, i: (h, 0, 0, 0)),
            pl.BlockSpec((1, 2, 128, 128), lambda h, i: (h, 0, 0, 0)),
            pl.BlockSpec((None, 2, 1), lambda h, i: (h, 0, 0)),
            pl.BlockSpec((None, 2, 2), lambda h, i: (h, 0, 0)),
            pl.BlockSpec((4, HEAD_DIM), lambda h, i: (0, 0)),
            pl.BlockSpec((2 * HEAD_DIM, 1), lambda h, i: (0, 0)),
        ],
        out_specs=pl.BlockSpec((tq, 2 * HEAD_DIM), lambda h, i: (i, h)),
        out_shape=jax.ShapeDtypeStruct((n_tok, D_MODEL), BF16),
        scratch_shapes=[
            pltpu.VMEM((2, 1, tq), F32),
            pltpu.VMEM((2, 8, tq), F32),
            pltpu.VMEM((2, 2 * HEAD_DIM, tq), F32),
        ],
        compiler_params=_cparams(2),
        name="diff_prompt",
    )(qT, k3, vT, bias_d, bias_p, knorm2, bstats, lam_p, subln)


def _diff_sample_kernel(q_ref, ck_ref, cv_ref, kn_ref, vn_ref, b_ref, lam_ref, sub_ref, o_ref, *, lam_init, past):
    t = q_ref.shape[2]
    lane = lax.broadcasted_iota(jnp.int32, (t, 2 * HEAD_DIM), 1)
    lam = _diff_lambda(lam_ref, lam_init)
    for h in range(DIFF_HEADS):
        q = q_ref[0, h]
        zero = jnp.zeros_like(q)
        qq = jnp.concatenate([jnp.where(lane < HEAD_DIM, q, zero), jnp.where(lane >= HEAD_DIM, q, zero)], axis=0)
        ck = ck_ref[pl.ds(h, past, stride=DIFF_HEADS), :].astype(BF16)
        cv = cv_ref[pl.ds(h, past, stride=DIFF_HEADS), :].astype(BF16)
        kn = kn_ref[0, h]
        vn = vn_ref[0, h].astype(BF16)
        s_c = lax.dot_general(qq, ck, NT_DIMS, preferred_element_type=F32)
        s_n = lax.dot_general(qq, kn, NT_DIMS, preferred_element_type=F32)
        bias = b_ref[h].reshape(2 * t, 128 + t)
        s_far = s_c[:, :past - 128]
        s_near = s_c[:, past - 128:] + bias[:, :128]
        s_n = s_n + bias[:, 128:]
        m = jnp.maximum(jnp.maximum(jnp.max(s_far, axis=1, keepdims=True), jnp.max(s_near, axis=1, keepdims=True)),
                        jnp.max(s_n, axis=1, keepdims=True))
        p_far = jnp.exp2(s_far - m)
        p_near = jnp.exp2(s_near - m)
        p_n = jnp.exp2(s_n - m)
        l = (jnp.sum(p_far, axis=1, keepdims=True) + jnp.sum(p_near, axis=1, keepdims=True)
             + jnp.sum(p_n, axis=1, keepdims=True))
        o = (jnp.dot(p_far.astype(BF16), cv[:past - 128], preferred_element_type=F32)
             + jnp.dot(p_near.astype(BF16), cv[past - 128:], preferred_element_type=F32)
             + jnp.dot(p_n.astype(BF16), vn, preferred_element_type=F32))
        o = o * (1.0 / l)
        od = o[:t] - lam * o[t:]
        ms = jnp.mean(od * od, axis=-1, keepdims=True)
        od = od * lax.rsqrt(ms + EPS) * sub_ref[...] * (1.0 - lam_init)
        o_ref[0, h] = od.astype(BF16)


def _diff_sample(q_s, cache_k, cache_v, layer, kn, vn, bias_s, lam_p, subln_row, lam_init):
    nb, _, t, _ = q_s.shape
    past = cache_k.shape[2]
    rows = past * DIFF_HEADS
    ck = cache_k.reshape(cache_k.shape[0], nb, rows, 2 * HEAD_DIM)
    cv = cache_v.reshape(cache_v.shape[0], nb, rows, 2 * HEAD_DIM)
    cache_spec = pl.BlockSpec((None, None, rows, 2 * HEAD_DIM), lambda b: (layer, b, 0, 0))
    head_spec = pl.BlockSpec((1, DIFF_HEADS, t, 2 * HEAD_DIM), lambda b: (b, 0, 0, 0))
    return pl.pallas_call(
        functools.partial(_diff_sample_kernel, lam_init=lam_init, past=past),
        grid=(nb,),
        in_specs=[
            head_spec, cache_spec, cache_spec, head_spec, head_spec,
            pl.BlockSpec((DIFF_HEADS, 2, t, 128 + t), lambda b: (0, 0, 0, 0)),
            pl.BlockSpec((4, HEAD_DIM), lambda b: (0, 0)),
            pl.BlockSpec((1, 2 * HEAD_DIM), lambda b: (0, 0)),
        ],
        out_specs=head_spec,
        out_shape=jax.ShapeDtypeStruct((nb, DIFF_HEADS, t, 2 * HEAD_DIM), BF16),
        compiler_params=_cparams(1),
        name="diff_sample",
    )(q_s, ck, cv, kn, vn, bias_s, lam_p, subln_row)


def _swa_prompt_kernel(qT_ref, kc_ref, kp_ref, vc_ref, vp_ref, b_ref, sink_ref, o_ref):
    i = pl.program_id(0)
    qT = qT_ref[...]
    kp = kp_ref[...]
    kc = kc_ref[...]
    vp = vp_ref[0]
    vc = vc_ref[0]
    tq = qT.shape[1]
    key_row = lax.broadcasted_iota(jnp.int32, (2 * tq, SWA_GROUP * tq), 0)
    has_prev = key_row >= jnp.where(i > 0, 0, tq)
    zeros = jnp.zeros((HEAD_DIM, tq), BF16)
    outs = []
    for kv in range(SWA_KV_HEADS):
        pair = kv // 2
        k_band = jnp.concatenate([kp[:, pair * 128:(pair + 1) * 128], kc[:, pair * 128:(pair + 1) * 128]], axis=0)
        v_band = jnp.concatenate([vp[kv * HEAD_DIM:(kv + 1) * HEAD_DIM], vc[kv * HEAD_DIM:(kv + 1) * HEAD_DIM]],
                                 axis=1)
        qg = []
        for g in range(SWA_GROUP):
            h = kv * SWA_GROUP + g
            qh = qT[h * HEAD_DIM:(h + 1) * HEAD_DIM]
            qg.append(jnp.concatenate([qh, zeros] if kv % 2 == 0 else [zeros, qh], axis=0))
        q4 = jnp.concatenate(qg, axis=1)
        s = jnp.dot(k_band, q4, preferred_element_type=F32) + b_ref[kv]
        s = jnp.where(has_prev, s, NEG)
        sink = sink_ref[kv]
        m = jnp.maximum(jnp.max(s, axis=0, keepdims=True), sink)
        p = jnp.exp2(s - m)
        den = jnp.sum(p, axis=0, keepdims=True) + jnp.exp2(sink - m)
        oT4 = jnp.dot(v_band, p.astype(BF16), preferred_element_type=F32) * (1.0 / den)
        for g in range(SWA_GROUP):
            outs.append(oT4[:, g * tq:(g + 1) * tq])
    o_ref[...] = jnp.concatenate(outs, axis=0).T.astype(BF16)


def _swa_prompt(qT, k16, vT, bias, sinks):
    n_tok = k16.shape[0]
    tq = SWA_TILE
    nkv = SWA_KV_HEADS * HEAD_DIM
    return pl.pallas_call(
        _swa_prompt_kernel,
        grid=(n_tok // tq,),
        in_specs=[
            pl.BlockSpec((D_MODEL, tq), lambda i: (0, i)),
            pl.BlockSpec((tq, nkv), lambda i: (i, 0)),
            pl.BlockSpec((tq, nkv), lambda i: (jnp.maximum(i - 1, 0), 0)),
            pl.BlockSpec((1, nkv, tq), lambda i: (i, 0, 0)),
            pl.BlockSpec((1, nkv, tq), lambda i: (jnp.maximum(i - 1, 0), 0, 0)),
            _resident((SWA_KV_HEADS, 2 * tq, SWA_GROUP * tq), lambda i: (0, 0, 0)),
            _resident((SWA_KV_HEADS, 1, SWA_GROUP * tq), lambda i: (0, 0, 0)),
        ],
        out_specs=pl.BlockSpec((tq, D_MODEL), lambda i: (i, 0)),
        out_shape=jax.ShapeDtypeStruct((n_tok, D_MODEL), BF16),
        compiler_params=_cparams(1),
        name="swa_prompt",
    )(qT, k16, k16, vT, vT, bias, sinks)


def _swa_sample_kernel(q_ref, kc_ref, vc_ref, kn_ref, vn_ref, b_ref, sink_ref, o_ref, ko_ref, vo_ref):
    t = kn_ref.shape[2]
    buf = kc_ref.shape[2]
    for kv in range(SWA_KV_HEADS):
        kc32 = kc_ref[0, kv]
        vc32 = vc_ref[0, kv]
        kn32 = kn_ref[0, kv]
        vn32 = vn_ref[0, kv]
        ko_ref[0, kv, 0:buf - t, :] = kc32[t:]
        ko_ref[0, kv, buf - t:buf, :] = kn32
        vo_ref[0, kv, 0:buf - t, :] = vc32[t:]
        vo_ref[0, kv, buf - t:buf, :] = vn32
        kc = kc32.astype(BF16)
        vc = vc32.astype(BF16)
        kn = kn32.astype(BF16)
        vn = vn32.astype(BF16)
        for g in range(SWA_GROUP):
            h = kv * SWA_GROUP + g
            q = q_ref[0, h]
            b = b_ref[h]
            s_c = lax.dot_general(q, kc, NT_DIMS, preferred_element_type=F32) + b[:, :buf]
            s_n = lax.dot_general(q, kn, NT_DIMS, preferred_element_type=F32) + b[:, buf:]
            sink = sink_ref[h]
            m = jnp.maximum(jnp.maximum(jnp.max(s_c, axis=1, keepdims=True), jnp.max(s_n, axis=1, keepdims=True)),
                            sink)
            p_c = jnp.exp2(s_c - m)
            p_n = jnp.exp2(s_n - m)
            den = (jnp.sum(p_c, axis=1, keepdims=True) + jnp.sum(p_n, axis=1, keepdims=True)
                   + jnp.exp2(sink - m))
            o = (jnp.dot(p_c.astype(BF16), vc, preferred_element_type=F32)
                 + jnp.dot(p_n.astype(BF16), vn, preferred_element_type=F32)) * (1.0 / den)
            o_ref[0, h] = o.astype(BF16)


def _swa_sample(q_s, kc, vc, kn, vn, bias, sinks):
    nb, _, t, _ = q_s.shape
    buf = kc.shape[2]
    kvspec = pl.BlockSpec((1, SWA_KV_HEADS, buf, HEAD_DIM), lambda b: (b, 0, 0, 0))
    nspec = pl.BlockSpec((1, SWA_KV_HEADS, t, HEAD_DIM), lambda b: (b, 0, 0, 0))
    return pl.pallas_call(
        _swa_sample_kernel,
        grid=(nb,),
        in_specs=[
            pl.BlockSpec((1, SWA_Q_HEADS, t, HEAD_DIM), lambda b: (b, 0, 0, 0)),
            kvspec, kvspec, nspec, nspec,
            pl.BlockSpec((SWA_Q_HEADS, t, buf + t), lambda b: (0, 0, 0)),
            pl.BlockSpec((SWA_Q_HEADS, 1, 1), lambda b: (0, 0, 0)),
        ],
        out_specs=[
            pl.BlockSpec((1, SWA_Q_HEADS, t, HEAD_DIM), lambda b: (b, 0, 0, 0)),
            kvspec, kvspec,
        ],
        out_shape=[
            jax.ShapeDtypeStruct((nb, SWA_Q_HEADS, t, HEAD_DIM), BF16),
            jax.ShapeDtypeStruct((nb, SWA_KV_HEADS, buf, HEAD_DIM), F32),
            jax.ShapeDtypeStruct((nb, SWA_KV_HEADS, buf, HEAD_DIM), F32),
        ],
        compiler_params=_cparams(1),
        name="swa_sample",
    )(q_s, kc, vc, kn, vn, bias, sinks)


def _softplus(z):
    return jnp.maximum(z, 0.0) + jnp.log(1.0 + jnp.exp(-jnp.abs(z)))


def _split_bf16(x):
    hi = x.astype(BF16)
    lo = (x - hi.astype(F32)).astype(BF16)
    return hi, lo


def _sb_prompt_kernel(qT_ref, k_ref, vT_ref, kmax_ref, o_ref, carry_ref, acc_ref, *, tq):
    qi = pl.program_id(1)
    qT = qT_ref[...]
    row = lax.broadcasted_iota(jnp.int32, qT.shape, 0)
    zero = jnp.zeros_like(qT)
    qpair = jnp.concatenate([jnp.where(row < HEAD_DIM, qT, zero), jnp.where(row >= HEAD_DIM, qT, zero)], axis=1)
    zb = jnp.sum(jnp.abs(qpair.astype(F32)) * kmax_ref[...], axis=0, keepdims=True)
    kr = lax.broadcasted_iota(jnp.int32, (tq, tq), 0)
    kc = lax.broadcasted_iota(jnp.int32, (tq, tq), 1)
    tri = jnp.where(kc >= kr, 1.0, 0.0).astype(BF16)
    kr2 = lax.broadcasted_iota(jnp.int32, (tq, 2 * tq), 0)
    kc2 = lax.broadcasted_iota(jnp.int32, (tq, 2 * tq), 1)
    causal = kr2 < jnp.where(kc2 >= tq, kc2 - tq, kc2)
    carry_ref[...] = jnp.zeros(carry_ref.shape, F32)
    acc_ref[...] = jnp.zeros(acc_ref.shape, F32)

    def tiles(js, first_is_diag):
        zs, cs = [], []
        for n, j in enumerate(js):
            z = jnp.dot(k_ref[j], qpair, preferred_element_type=F32)
            lsp = _softplus(z)
            if first_is_diag and n == 0:
                lsp = jnp.where(causal, lsp, 0.0)
            hi, lo = _split_bf16(lsp)
            zs.append(z)
            cs.append(jnp.dot(tri, hi, preferred_element_type=F32) + jnp.dot(tri, lo, preferred_element_type=F32))
        carry = carry_ref[...]
        weights = []
        for n in range(len(js)):
            a = jnp.exp(zs[n] - cs[n] - carry)
            if first_is_diag and n == 0:
                a = jnp.where(causal, a, 0.0)
            weights.append(a.astype(BF16))
            carry = carry + cs[n][0:1]
        carry_ref[...] = carry
        vT = jnp.concatenate([vT_ref[j] for j in js], axis=1)
        acc_ref[...] += jnp.dot(vT, jnp.concatenate(weights, axis=0), preferred_element_type=F32)

    def still_live():
        return (jnp.min(carry_ref[...] - zb) < SB_SKIP_MARGIN).astype(jnp.int32)

    @pl.when(qi == 0)
    def _():
        tiles([qi], True)

    @pl.when(qi >= 1)
    def _():
        tiles([qi, qi - 1], True)

    def cond(state):
        j, live = state
        return jnp.logical_and(j >= 0, live > 0)

    def body(state):
        j, _ = state
        tiles([j], False)
        return j - 1, still_live()

    lax.while_loop(cond, body, (qi - 2, still_live()))
    o = jnp.concatenate([acc_ref[0:HEAD_DIM, 0:tq], acc_ref[HEAD_DIM:2 * HEAD_DIM, tq:2 * tq]], axis=0)
    o_ref[...] = o.T.astype(BF16)


def _sb_prompt(qT, k16, vT, kmax):
    n_tok = k16.shape[0]
    tq = min(SB_TILE, n_tok)
    nb = n_tok // tq
    k3 = k16.reshape(nb, tq, D_MODEL)
    return pl.pallas_call(
        functools.partial(_sb_prompt_kernel, tq=tq),
        grid=(SB_HEADS // 2, nb),
        in_specs=[
            pl.BlockSpec((128, tq), lambda p, i: (p, i)),
            pl.BlockSpec((nb, tq, 128), lambda p, i: (0, 0, p)),
            pl.BlockSpec((nb, 128, tq), lambda p, i: (0, p, 0)),
            pl.BlockSpec((128, 1), lambda p, i: (p, 0)),
        ],
        out_specs=pl.BlockSpec((tq, 128), lambda p, i: (i, p)),
        out_shape=jax.ShapeDtypeStruct((n_tok, D_MODEL), BF16),
        scratch_shapes=[
            pltpu.VMEM((1, 2 * tq), F32),
            pltpu.VMEM((2 * HEAD_DIM, 2 * tq), F32),
        ],
        compiler_params=_cparams(2),
        name="sb_prompt",
    )(qT, k3, vT, kmax)


def _tri_lanes(n):
    r = lax.broadcasted_iota(jnp.int32, (n, n), 0)
    c = lax.broadcasted_iota(jnp.int32, (n, n), 1)
    return jnp.where(r >= c, 1.0, 0.0).astype(BF16)


def _sb_sample_kernel(q_ref, ck_ref, cv_ref, kn_ref, vn_ref, o_ref, carry_ref, acc_ref, *, ch, kb):
    c = pl.program_id(1)
    nh = q_ref.shape[1]
    t = q_ref.shape[2]

    @pl.when(c == 0)
    def _():
        z = jnp.concatenate([lax.dot_general(q_ref[0, h], kn_ref[0, h], NT_DIMS, preferred_element_type=F32)
                             for h in range(nh)], axis=0)
        r = lax.broadcasted_iota(jnp.int32, (nh * t, t), 0)
        kc = lax.broadcasted_iota(jnp.int32, (nh * t, t), 1)
        causal = kc < lax.rem(r, t)
        hi, lo = _split_bf16(jnp.where(causal, _softplus(z), 0.0))
        tri = _tri_lanes(t)
        cs = jnp.dot(hi, tri, preferred_element_type=F32) + jnp.dot(lo, tri, preferred_element_type=F32)
        a = jnp.where(causal, jnp.exp(z - cs), 0.0).astype(BF16)
        carry_ref[...] = cs[:, 0:1]
        for h in range(nh):
            acc_ref[h * t:(h + 1) * t, :] = jnp.dot(a[h * t:(h + 1) * t], vn_ref[0, h].astype(BF16),
                                                    preferred_element_type=F32)

    z = jnp.concatenate([jnp.dot(q_ref[0, h], ck_ref[h].astype(BF16), preferred_element_type=F32)
                         for h in range(nh)], axis=0)
    nblk = ch // kb
    rows = nh * t
    zst = jnp.concatenate([z[:, b * kb:(b + 1) * kb] for b in range(nblk)], axis=0)
    hi, lo = _split_bf16(_softplus(zst))
    tri = _tri_lanes(kb)
    cs = jnp.dot(hi, tri, preferred_element_type=F32) + jnp.dot(lo, tri, preferred_element_type=F32)
    run = carry_ref[...]
    carries = [None] * nblk
    for b in reversed(range(nblk)):
        carries[b] = run
        run = run + cs[b * rows:(b + 1) * rows, 0:1]
    carry_ref[...] = run
    a = jnp.exp(zst - cs - jnp.concatenate(carries, axis=0))
    a = jnp.concatenate([a[b * rows:(b + 1) * rows] for b in range(nblk)], axis=1).astype(BF16)
    for h in range(nh):
        acc_ref[h * t:(h + 1) * t, :] += lax.dot_general(a[h * t:(h + 1) * t], cv_ref[h].astype(BF16), NT_DIMS,
                                                         preferred_element_type=F32)

    @pl.when(c == pl.num_programs(1) - 1)
    def _():
        for h in range(nh):
            o_ref[0, h] = acc_ref[h * t:(h + 1) * t, :].astype(BF16)


def _sb_sample(q_s, cache_k, cache_v, layer, kn, vn):
    nb, nh, t, _ = q_s.shape
    past = cache_k.shape[2]
    ch = min(SB_SAMPLE_CHUNK, past)
    nch = past // ch
    ck = cache_k.transpose(0, 1, 3, 4, 2)
    cv = cache_v.transpose(0, 1, 3, 4, 2)
    cache_spec = pl.BlockSpec((None, None, nh, HEAD_DIM, ch), lambda b, c: (layer, b, 0, 0, nch - 1 - c))
    head_spec = pl.BlockSpec((1, nh, t, HEAD_DIM), lambda b, c: (b, 0, 0, 0))
    return pl.pallas_call(
        functools.partial(_sb_sample_kernel, ch=ch, kb=min(256, ch)),
        grid=(nb, nch),
        in_specs=[head_spec, cache_spec, cache_spec, head_spec, head_spec],
        out_specs=head_spec,
        out_shape=jax.ShapeDtypeStruct((nb, nh, t, HEAD_DIM), BF16),
        scratch_shapes=[
            pltpu.VMEM((nh * t, 1), F32),
            pltpu.VMEM((nh * t, HEAD_DIM), F32),
        ],
        compiler_params=_cparams(2),
        name="sb_sample",
    )(q_s, ck, cv, kn, vn)


def _tile_col(gain, n, scale):
    return (jnp.tile(gain.astype(F32), n // gain.shape[0]) * scale).reshape(n, 1)


def _heads_major(x, nb, t, nh, hd):
    return x.reshape(nb, t, nh, hd).transpose(0, 2, 1, 3)


def _diff_layer(xp, xs, cache_k, cache_v, layer, rel_bias, g, w_qkv, q_gain, k_gain, lam_p, subln, lam_init):
    n_p = xp.shape[0]
    nb = cache_k.shape[1]
    t = xs.shape[0] // nb
    wt = w_qkv.T.astype(BF16)
    qg = _tile_col(q_gain, D_MODEL, HEAD_DIM ** -0.5 * LOG2E)
    kg = _tile_col(k_gain, D_MODEL, 1.0)
    g = g.reshape(1, D_MODEL)
    kw = dict(nq=D_MODEL, nk=D_MODEL, nv=D_MODEL, head_norm=True)
    tq = min(DIFF_TILE, n_p)
    qT, k32, k16, v32, vT, knorm2 = _project(xp, g, wt, qg, kg, vb=tq, kstat="normsq", **kw)
    qT_s, k32_s, k16_s, v32_s, _ = _project(xs, g, wt, qg, kg, vb=128, **kw)

    j = jnp.arange(tq)
    valid_d = (j // CHUNK)[:, None] <= (j // CHUNK)[None, :]
    bias_d = _bias_tile(rel_bias, tq, tq, 0, valid_d, True).reshape(DIFF_HEADS, 2, tq, tq)
    bias_p = _bias_tile(rel_bias, 128, 128, -128, None, True).reshape(DIFF_HEADS, 2, 128, 128)
    bstats = _bias_stats(rel_bias).reshape(DIFF_HEADS, 2, 2)
    o_p = _diff_prompt(qT, k16, vT, bias_d, bias_p, knorm2.reshape(DIFF_HEADS, 2, 1), bstats, lam_p,
                       subln.reshape(2 * HEAD_DIM, 1), lam_init)

    bias_s = jnp.swapaxes(_bias_tile(rel_bias, 128 + t, t, -128, None, True), 1, 2)
    bias_s = bias_s.reshape(DIFF_HEADS, 2, t, 128 + t)
    hm = functools.partial(_heads_major, nb=nb, t=t, nh=DIFF_HEADS, hd=2 * HEAD_DIM)
    o_s = _diff_sample(hm(qT_s.T), cache_k, cache_v, layer, hm(k16_s), hm(v32_s), bias_s, lam_p,
                       subln.reshape(1, 2 * HEAD_DIM), lam_init)
    o_s = o_s.transpose(0, 2, 1, 3).reshape(nb * t, D_MODEL)
    return (o_p, o_s,
            k32.reshape(1, n_p, DIFF_HEADS, 2 * HEAD_DIM), v32.reshape(1, n_p, DIFF_HEADS, 2 * HEAD_DIM),
            k32_s.reshape(nb, t, DIFF_HEADS, 2 * HEAD_DIM), v32_s.reshape(nb, t, DIFF_HEADS, 2 * HEAD_DIM))


def _swa_layer(xp, xs, ck, cv, rel_bias, g, w_qkv, q_gain, k_gain, sinks):
    n_p = xp.shape[0]
    nb, buf = ck.shape[0], ck.shape[1]
    t = xs.shape[0] // nb
    nkv = SWA_KV_HEADS * HEAD_DIM
    wt = w_qkv.T.astype(BF16)
    qg = _tile_col(q_gain, D_MODEL, HEAD_DIM ** -0.5 * LOG2E)
    kg = _tile_col(k_gain, nkv, 1.0)
    g = g.reshape(1, D_MODEL)
    kw = dict(nq=D_MODEL, nk=nkv, nv=nkv, vb=SWA_TILE, head_norm=True)
    qT, k32, k16, v32, vT = _project(xp, g, wt, qg, kg, **kw)
    qT_s, k32_s, _, v32_s, _ = _project(xs, g, wt, qg, kg, **kw)
    sink_col = (sinks.astype(F32) * LOG2E).reshape(SWA_Q_HEADS, 1, 1)
    wchunks = WINDOW // CHUNK

    tq = SWA_TILE
    kchunk = jnp.floor_divide(jnp.arange(2 * tq) - tq, CHUNK)[:, None]
    qchunk = (jnp.arange(tq) // CHUNK)[None, :]
    valid = (kchunk <= qchunk) & (kchunk >= qchunk - wchunks)
    bias = _bias_tile(rel_bias, 2 * tq, tq, -tq, valid, False)
    bias_g = bias.reshape(SWA_KV_HEADS, SWA_GROUP, 2 * tq, tq).transpose(0, 2, 1, 3)
    bias_g = bias_g.reshape(SWA_KV_HEADS, 2 * tq, SWA_GROUP * tq)
    sink_g = jnp.repeat(sink_col.reshape(SWA_KV_HEADS, 1, SWA_GROUP), tq, axis=2)
    o_p = _swa_prompt(qT, k16, vT, bias_g, sink_g)

    sc = jnp.floor_divide(jnp.arange(buf + t) - buf, CHUNK)[:, None]
    tc = (jnp.arange(t) // CHUNK)[None, :]
    valid_s = (sc <= tc) & (sc >= tc - wchunks)
    bias_s = jnp.swapaxes(_bias_tile(rel_bias, buf + t, t, -buf, valid_s, False), 1, 2)
    q_s = _heads_major(qT_s.T, nb, t, SWA_Q_HEADS, HEAD_DIM)
    kn = _heads_major(k32_s, nb, t, SWA_KV_HEADS, HEAD_DIM)
    vn = _heads_major(v32_s, nb, t, SWA_KV_HEADS, HEAD_DIM)
    o_s, ko, vo = _swa_sample(q_s, ck.transpose(0, 2, 1, 3), cv.transpose(0, 2, 1, 3), kn, vn, bias_s, sink_col)
    o_s = o_s.transpose(0, 2, 1, 3).reshape(nb * t, D_MODEL)
    wbuf = min(WINDOW, n_p)
    return (o_p, o_s,
            k32[n_p - wbuf:].reshape(1, wbuf, SWA_KV_HEADS, HEAD_DIM),
            v32[n_p - wbuf:].reshape(1, wbuf, SWA_KV_HEADS, HEAD_DIM),
            ko.transpose(0, 2, 1, 3), vo.transpose(0, 2, 1, 3))


def _sb_layer(xp, xs, cache_k, cache_v, layer, g, w_qkv):
    n_p = xp.shape[0]
    nb = cache_k.shape[1]
    t = xs.shape[0] // nb
    wt = w_qkv.T.astype(BF16)
    qg = jnp.full((D_MODEL, 1), HEAD_DIM ** -0.5, F32)
    kg = jnp.ones((D_MODEL, 1), F32)
    g = g.reshape(1, D_MODEL)
    kw = dict(nq=D_MODEL, nk=D_MODEL, nv=D_MODEL, head_norm=False)
    tq = min(SB_TILE, n_p)
    qT, k32, k16, v32, vT, kmax = _project(xp, g, wt, qg, kg, vb=tq, kstat="absmax", **kw)
    qT_s, k32_s, k16_s, v32_s, _ = _project(xs, g, wt, qg, kg, vb=128, **kw)
    o_p = _sb_prompt(qT, k16, vT, kmax)
    hm = functools.partial(_heads_major, nb=nb, t=t, nh=SB_HEADS, hd=HEAD_DIM)
    o_s = _sb_sample(hm(qT_s.T), cache_k, cache_v, layer, hm(k16_s), hm(v32_s))
    o_s = o_s.transpose(0, 2, 1, 3).reshape(nb * t, D_MODEL)
    return (o_p, o_s,
            k32.reshape(1, n_p, SB_HEADS, HEAD_DIM), v32.reshape(1, n_p, SB_HEADS, HEAD_DIM),
            k32_s.reshape(nb, t, SB_HEADS, HEAD_DIM), v32_s.reshape(nb, t, SB_HEADS, HEAD_DIM))


def kernel(x_prompt, x_sample, cache_diff_k, cache_diff_v, cache_swa_k, cache_swa_v, cache_sb_k, cache_sb_v, rel_bias, norm_mix, norm_mlp, w_up, w_down, diff_w_qkv, diff_w_o, diff_q_norm, diff_k_norm, diff_lambda, diff_subln, swa_w_qkv, swa_w_o, swa_q_norm, swa_k_norm, swa_sinks, sb_w_qkv, sb_w_o):
    bp, n_p, _ = x_prompt.shape
    assert bp == 1
    nb, t, _ = x_sample.shape
    depth = norm_mix.shape[0]
    xp = x_prompt.reshape(n_p, D_MODEL)
    xs = x_sample.reshape(nb * t, D_MODEL)
    outs = {name: [] for name in ("pdk", "pdv", "pwk", "pwv", "pbk", "pbv", "sdk", "sdv", "swk", "swv", "sbk", "sbv")}
    for i in range(depth):
        j = i // N_MIXERS
        if i % N_MIXERS == 0:
            lam_init = 0.8 - 0.6 * math.exp(-0.3 * i)
            o_p, o_s, kp, vp, kn, vn = _diff_layer(
                xp, xs, cache_diff_k, cache_diff_v, j, rel_bias, norm_mix[i], diff_w_qkv[j],
                diff_q_norm[j], diff_k_norm[j], diff_lambda[j], diff_subln[j], lam_init)
            w_o = diff_w_o[j]
            names = ("pdk", "pdv", "sdk", "sdv")
        elif i % N_MIXERS == 1:
            o_p, o_s, kp, vp, kn, vn = _swa_layer(
                xp, xs, cache_swa_k[j], cache_swa_v[j], rel_bias, norm_mix[i], swa_w_qkv[j],
                swa_q_norm[j], swa_k_norm[j], swa_sinks[j])
            w_o = swa_w_o[j]
            names = ("pwk", "pwv", "swk", "swv")
        else:
            o_p, o_s, kp, vp, kn, vn = _sb_layer(xp, xs, cache_sb_k, cache_sb_v, j, norm_mix[i], sb_w_qkv[j])
            w_o = sb_w_o[j]
            names = ("pbk", "pbv", "sbk", "sbv")
        for name, val in zip(names, (kp, vp, kn, vn)):
            outs[name].append(val)
        wo16 = w_o.astype(BF16)
        g_mlp = norm_mlp[i].reshape(1, D_MODEL)
        wup16 = w_up[i].astype(BF16)
        wdn16 = w_down[i].astype(BF16)
        xp = _outproj_mlp(xp, o_p, wo16, g_mlp, wup16, wdn16)
        xs = _outproj_mlp(xs, o_s, wo16, g_mlp, wup16, wdn16)
    st = {name: jnp.stack(v) for name, v in outs.items()}
    return (xp.reshape(1, n_p, D_MODEL), xs.reshape(nb, t, D_MODEL),
            st["pdk"], st["pdv"], st["pwk"], st["pwv"], st["pbk"], st["pbv"],
            st["sdk"], st["sdv"], st["swk"], st["swv"], st["sbk"], st["sbv"])
```

```python
import functools
import math

import jax
import jax.numpy as jnp
from jax import lax
from jax.experimental import pallas as pl
from jax.experimental.pallas import tpu as pltpu

F32 = jnp.float32
BF16 = jnp.bfloat16

D_MODEL = 1024
HEAD_DIM = 64
CHUNK = 64
N_MIXERS = 3
DIFF_HEADS = 8
SWA_Q_HEADS = 16
SWA_KV_HEADS = 4
SWA_GROUP = 4
WINDOW = 128
SB_HEADS = 16
N_BUCKETS = 32
MAX_DISTANCE = 128
D_FF = 4 * D_MODEL
EPS = 1e-6
LOG2E = 1.4426950408889634
NEG = -1e30
FAR_BUCKET = 15
SB_SKIP_MARGIN = 100.0
SOFTMAX_BOUND_LIMIT = 100.0

VMEM_LIMIT = 56 * 1024 * 1024
TOKEN_TILE = 512
PROJ_CHUNK = 512
DIFF_TILE = 512
DIFF_FAR_GROUP = 4
SB_TILE = 256
SB_SAMPLE_CHUNK = 512
SWA_TILE = 128

NT_DIMS = (((1,), (1,)), ((), ()))


def _cparams(n_axes):
    return pltpu.CompilerParams(dimension_semantics=("arbitrary",) * n_axes,
                                vmem_limit_bytes=VMEM_LIMIT)


def _resident(block_shape, index_map):
    return pl.BlockSpec(block_shape, index_map, pipeline_mode=pl.Buffered(1))


def _t5_bucket(rel):
    half = N_BUCKETS // 2
    exact = half // 2
    ret = jnp.where(rel > 0, half, 0).astype(jnp.int32)
    n = jnp.abs(rel)
    nf = jnp.maximum(n, 1).astype(F32)
    large = exact + (jnp.log(nf / exact) / math.log(MAX_DISTANCE / exact) * (half - exact)).astype(jnp.int32)
    large = jnp.minimum(large, half - 1)
    return ret + jnp.where(n < exact, n, large)


def _bias_tile(rel_bias, n_j, n_i, r0, valid, shift_far):
    tab = rel_bias.astype(F32)
    rel = jnp.arange(n_i + n_j - 1) + (r0 - (n_i - 1))
    vals = jnp.moveaxis(tab[_t5_bucket(rel)], -1, 0)
    if shift_far:
        vals = vals - tab[FAR_BUCKET][:, None]
    vals = vals * LOG2E
    period = n_i + n_j
    y = jnp.concatenate([jnp.flip(vals, axis=-1), jnp.zeros((vals.shape[0], 1), F32)], axis=-1)
    r = jnp.tile(y, (1, n_j))[:, :n_j * (period - 1)].reshape(vals.shape[0], n_j, period - 1)
    b = r[:, :, n_j - 1:n_j - 1 + n_i]
    if valid is not None:
        b = jnp.where(valid[None], b, NEG)
    return b


def _bias_stats(rel_bias):
    tab = rel_bias.astype(F32)
    sh = (tab - tab[FAR_BUCKET][None, :]) * LOG2E
    bmax = jnp.max(sh, axis=0)
    return jnp.stack([bmax, bmax - jnp.min(sh, axis=0)], axis=-1)


def _proj_kernel(x_ref, g_ref, wt_ref, qg_ref, kg_ref, *out_refs, nq, nk, nv, vb, head_norm, kstat):
    qT_ref, k32_ref, k16_ref, v32_ref, vT_ref = out_refs[:5]
    x = x_ref[...]
    ms = jnp.mean(x * x, axis=-1, keepdims=True)
    h = (x * lax.rsqrt(ms + EPS) * g_ref[...]).astype(BF16)
    tm = x.shape[0]

    def rows(r0, n):
        return lax.dot_general(wt_ref[r0:r0 + n, :], h, NT_DIMS, preferred_element_type=F32)

    def headnorm(t, gcol):
        n = t.shape[0]
        t3 = t.reshape(n // HEAD_DIM, HEAD_DIM, tm)
        r = lax.rsqrt(jnp.mean(t3 * t3, axis=1, keepdims=True) + EPS)
        return (t3 * r).reshape(n, tm) * gcol

    for r0 in range(0, nq, PROJ_CHUNK):
        n = min(PROJ_CHUNK, nq - r0)
        qt = rows(r0, n)
        qt = headnorm(qt, qg_ref[r0:r0 + n, :]) if head_norm else qt * qg_ref[r0:r0 + n, :]
        qT_ref[r0:r0 + n, :] = qt.astype(BF16)

    for r0 in range(0, nk, PROJ_CHUNK):
        n = min(PROJ_CHUNK, nk - r0)
        kt = rows(nq + r0, n)
        if head_norm:
            kt = headnorm(kt, kg_ref[r0:r0 + n, :])
        k = kt.T
        k32_ref[:, r0:r0 + n] = k
        k16_ref[:, r0:r0 + n] = k.astype(BF16)
        if kstat is not None:
            stat_ref = out_refs[5]
            kr = kt.astype(BF16).astype(F32)
            if kstat == "absmax":
                stat = jnp.max(jnp.abs(kr), axis=1, keepdims=True)
                srows = slice(r0, r0 + n)
            else:
                k3 = kr.reshape(n // HEAD_DIM, HEAD_DIM, tm)
                stat = jnp.max(jnp.sum(k3 * k3, axis=1), axis=1, keepdims=True)
                srows = slice(r0 // HEAD_DIM, (r0 + n) // HEAD_DIM)

            @pl.when(pl.program_id(0) == 0)
            def _(stat=stat, srows=srows):
                stat_ref[srows, :] = stat

            @pl.when(pl.program_id(0) > 0)
            def _(stat=stat, srows=srows):
                stat_ref[srows, :] = jnp.maximum(stat_ref[srows, :], stat)

    for r0 in range(0, nv, PROJ_CHUNK):
        n = min(PROJ_CHUNK, nv - r0)
        vt = rows(nq + nk + r0, n)
        v32_ref[:, r0:r0 + n] = vt.T
        vt16 = vt.astype(BF16)
        for b in range(tm // vb):
            vT_ref[b, r0:r0 + n, :] = vt16[:, b * vb:(b + 1) * vb]


def _project(x, g, wt, qg, kg, *, nq, nk, nv, vb, head_norm, kstat=None):
    n_tok = x.shape[0]
    tm = min(TOKEN_TILE, n_tok)
    vb = min(vb, tm)
    grid = (n_tok // tm,)
    ntot = nq + nk + nv
    out_shape = [
        jax.ShapeDtypeStruct((nq, n_tok), BF16),
        jax.ShapeDtypeStruct((n_tok, nk), F32),
        jax.ShapeDtypeStruct((n_tok, nk), BF16),
        jax.ShapeDtypeStruct((n_tok, nv), F32),
        jax.ShapeDtypeStruct((n_tok // vb, nv, vb), BF16),
    ]
    out_specs = [
        pl.BlockSpec((nq, tm), lambda i: (0, i)),
        pl.BlockSpec((tm, nk), lambda i: (i, 0)),
        pl.BlockSpec((tm, nk), lambda i: (i, 0)),
        pl.BlockSpec((tm, nv), lambda i: (i, 0)),
        pl.BlockSpec((tm // vb, nv, vb), lambda i: (i, 0, 0)),
    ]
    if kstat is not None:
        n_stat = nk if kstat == "absmax" else nk // HEAD_DIM
        out_shape.append(jax.ShapeDtypeStruct((n_stat, 1), F32))
        out_specs.append(pl.BlockSpec((n_stat, 1), lambda i: (0, 0)))
    return pl.pallas_call(
        functools.partial(_proj_kernel, nq=nq, nk=nk, nv=nv, vb=vb, head_norm=head_norm, kstat=kstat),
        grid=grid,
        in_specs=[
            pl.BlockSpec((tm, D_MODEL), lambda i: (i, 0)),
            _resident((1, D_MODEL), lambda i: (0, 0)),
            _resident((ntot, D_MODEL), lambda i: (0, 0)),
            _resident((nq, 1), lambda i: (0, 0)),
            _resident((nk, 1), lambda i: (0, 0)),
        ],
        out_specs=out_specs,
        out_shape=out_shape,
        compiler_params=_cparams(1),
        name="proj",
    )(x, g, wt, qg, kg)


def _mlp_kernel(x_ref, o_ref, wo_ref, g_ref, wup_ref, wdn_ref, y_ref, *, f_chunk):
    x1 = x_ref[...] + jnp.dot(o_ref[...], wo_ref[...], preferred_element_type=F32)
    ms = jnp.mean(x1 * x1, axis=-1, keepdims=True)
    hn = (x1 * lax.rsqrt(ms + EPS) * g_ref[...]).astype(BF16)
    acc = x1
    for f in range(D_FF // f_chunk):
        u = jnp.dot(hn, wup_ref[:, f * f_chunk:(f + 1) * f_chunk], preferred_element_type=F32)
        a = jnp.square(jnp.maximum(u, 0.0)).astype(BF16)
        acc = acc + jnp.dot(a, wdn_ref[f * f_chunk:(f + 1) * f_chunk, :], preferred_element_type=F32)
    y_ref[...] = acc


def _outproj_mlp(x, o, wo, g, wup, wdn):
    n_tok = x.shape[0]
    tm = min(TOKEN_TILE, n_tok)
    return pl.pallas_call(
        functools.partial(_mlp_kernel, f_chunk=1024),
        grid=(n_tok // tm,),
        in_specs=[
            pl.BlockSpec((tm, D_MODEL), lambda i: (i, 0)),
            pl.BlockSpec((tm, D_MODEL), lambda i: (i, 0)),
            _resident((D_MODEL, D_MODEL), lambda i: (0, 0)),
            _resident((1, D_MODEL), lambda i: (0, 0)),
            _resident((D_MODEL, D_FF), lambda i: (0, 0)),
            _resident((D_FF, D_MODEL), lambda i: (0, 0)),
        ],
        out_specs=pl.BlockSpec((tm, D_MODEL), lambda i: (i, 0)),
        out_shape=jax.ShapeDtypeStruct((n_tok, D_MODEL), F32),
        compiler_params=_cparams(1),
        name="outproj_mlp",
    )(x, o, wo, g, wup, wdn)


def _diff_lambda(lam_ref, lam_init):
    lp = lam_ref[...]
    a = jnp.sum(lp[0:1] * lp[1:2], axis=-1, keepdims=True)
    b = jnp.sum(lp[2:3] * lp[3:4], axis=-1, keepdims=True)
    return jnp.exp(a) - jnp.exp(b) + lam_init


def _diff_prompt_kernel(qT_ref, k_ref, vT_ref, bd_ref, bp_ref, kn_ref, bs_ref, lam_ref, sub_ref, o_ref,
                        m_ref, l_ref, acc_ref, *, tq, lam_init):
    qi = pl.program_id(1)
    qT = qT_ref[...]
    row = lax.broadcasted_iota(jnp.int32, qT.shape, 0)
    zero = jnp.zeros_like(qT)
    qm = (jnp.where(row < HEAD_DIM, qT, zero), jnp.where(row >= HEAD_DIM, qT, zero))

    def prev_bias(m, s):
        corner = s[tq - 128:, :128] + bp_ref[0, m]
        bottom = jnp.concatenate([corner, s[tq - 128:, 128:]], axis=1)
        return jnp.concatenate([s[:tq - 128], bottom], axis=0)

    def diag_bias(m, s):
        return s + bd_ref[0, m]

    def prev_diag_bias(m, s):
        return jnp.concatenate([prev_bias(m, s[:tq]), diag_bias(m, s[tq:])], axis=0)

    n_far = jnp.maximum(qi - 1, 0)

    def walk(tile_fn):
        def far_body(j, carry):
            tile_fn(j, None)
            return carry

        lax.fori_loop(0, n_far, far_body, 0)

        @pl.when(qi >= 1)
        def _():
            tile_fn(qi - 1, prev_bias)

        tile_fn(qi, diag_bias)

    span = []
    for m in range(2):
        qf = qm[m].astype(F32)
        qn = jnp.sqrt(jnp.sum(qf * qf, axis=0, keepdims=True))
        reach = qn * (jnp.sqrt(kn_ref[m:m + 1, :]) * 1.001)
        m_ref[m] = reach + bs_ref[m:m + 1, 0:1]
        span.append(jnp.max(2.0 * reach + bs_ref[m:m + 1, 1:2]))
    bound_is_tight = jnp.maximum(span[0], span[1]) <= SOFTMAX_BOUND_LIMIT

    @pl.when(jnp.logical_not(bound_is_tight))
    def _():
        m_ref[...] = jnp.full(m_ref.shape, NEG, F32)

        def max_tile(j, bias_fn):
            k = k_ref[j]
            for m in range(2):
                s = jnp.dot(k, qm[m], preferred_element_type=F32)
                if bias_fn is not None:
                    s = bias_fn(m, s)
                m_ref[m] = jnp.maximum(m_ref[m], jnp.max(s, axis=0, keepdims=True))

        walk(max_tile)

    l_ref[...] = jnp.zeros(l_ref.shape, F32)
    acc_ref[...] = jnp.zeros(acc_ref.shape, F32)

    def acc_tiles(js, bias_fn):
        k = jnp.concatenate([k_ref[j] for j in js], axis=0)
        vT = jnp.concatenate([vT_ref[j] for j in js], axis=1)
        nk = len(js) * tq
        for m in range(2):
            s = jnp.dot(k, qm[m], preferred_element_type=F32)
            if bias_fn is not None:
                s = bias_fn(m, s)
            p = jnp.exp2(s - m_ref[m])
            l_ref[m] += jnp.sum(p.reshape(nk // 8, 8, tq), axis=0)
            acc_ref[m] += jnp.dot(vT, p.astype(BF16), preferred_element_type=F32)

    def far_group(jj, carry):
        acc_tiles([DIFF_FAR_GROUP * jj + u for u in range(DIFF_FAR_GROUP)], None)
        return carry

    n_groups = n_far // DIFF_FAR_GROUP
    lax.fori_loop(0, n_groups, far_group, 0)
    start = n_groups * DIFF_FAR_GROUP
    left = n_far - start
    size = DIFF_FAR_GROUP // 2
    while size >= 1:
        @pl.when((left & size) != 0)
        def _(start=start, size=size):
            acc_tiles([start + u for u in range(size)], None)

        start = start + (left & size)
        size //= 2

    @pl.when(qi >= 1)
    def _():
        acc_tiles([qi - 1, qi], prev_diag_bias)

    @pl.when(qi == 0)
    def _():
        acc_tiles([qi], diag_bias)

    lam = _diff_lambda(lam_ref, lam_init)
    o0 = acc_ref[0] * (1.0 / jnp.sum(l_ref[0], axis=0, keepdims=True))
    o1 = acc_ref[1] * (1.0 / jnp.sum(l_ref[1], axis=0, keepdims=True))
    o = o0 - lam * o1
    ms = jnp.mean(o * o, axis=0, keepdims=True)
    o = o * lax.rsqrt(ms + EPS) * sub_ref[...] * (1.0 - lam_init)
    o_ref[...] = o.T.astype(BF16)


def _diff_prompt(qT, k16, vT, bias_d, bias_p, knorm2, bstats, lam_p, subln, lam_init):
    n_tok = k16.shape[0]
    tq = min(DIFF_TILE, n_tok)
    nb = n_tok // tq
    k3 = k16.reshape(nb, tq, D_MODEL)
    return pl.pallas_call(
        functools.partial(_diff_prompt_kernel, tq=tq, lam_init=lam_init),
        grid=(DIFF_HEADS, nb),
        in_specs=[
            pl.BlockSpec((2 * HEAD_DIM, tq), lambda h, i: (h, i)),
            pl.BlockSpec((nb, tq, 2 * HEAD_DIM), lambda h, i: (0, 0, h)),
            pl.BlockSpec((nb, 2 * HEAD_DIM, tq), lambda h, i: (0, h, 0)),
            pl.BlockSpec((1, 2, tq, tq), lambda h, i: (h, 0, 0, 0)),
            pl.BlockSpec((1, 2, 128, 128), lambda h, i: (h, 0, 0, 0)),
            pl.BlockSpec((None, 2, 1), lambda h, i: (h, 0, 0)),
            pl.BlockSpec((None, 2, 2), lambda h, i: (h, 0, 0)),
            pl.BlockSpec((4, HEAD_DIM), lambda h, i: (0, 0)),
            pl.BlockSpec((2 * HEAD_DIM, 1), lambda h, i: (0, 0)),
        ],
        out_specs=pl.BlockSpec((tq, 2 * HEAD_DIM), lambda h, i: (i, h)),
        out_shape=jax.ShapeDtypeStruct((n_tok, D_MODEL), BF16),
        scratch_shapes=[
            pltpu.VMEM((2, 1, tq), F32),
            pltpu.VMEM((2, 8, tq), F32),
            pltpu.VMEM((2, 2 * HEAD_DIM, tq), F32),
        ],
        compiler_params=_cparams(2),
        name="diff_prompt",
    )(qT, k3, vT, bias_d, bias_p, knorm2, bstats, lam_p, subln)


def _diff_sample_kernel(q_ref, ck_ref, cv_ref, kn_ref, vn_ref, b_ref, lam_ref, sub_ref, o_ref, *, lam_init, past):
    t = q_ref.shape[2]
    lane = lax.broadcasted_iota(jnp.int32, (t, 2 * HEAD_DIM), 1)
    lam = _diff_lambda(lam_ref, lam_init)
    for h in range(DIFF_HEADS):
        q = q_ref[0, h]
        zero = jnp.zeros_like(q)
        qq = jnp.concatenate([jnp.where(lane < HEAD_DIM, q, zero), jnp.where(lane >= HEAD_DIM, q, zero)], axis=0)
        ck = ck_ref[pl.ds(h, past, stride=DIFF_HEADS), :].astype(BF16)
        cv = cv_ref[pl.ds(h, past, stride=DIFF_HEADS), :].astype(BF16)
        kn = kn_ref[0, h]
        vn = vn_ref[0, h].astype(BF16)
        s_c = lax.dot_general(qq, ck, NT_DIMS, preferred_element_type=F32)
        s_n = lax.dot_general(qq, kn, NT_DIMS, preferred_element_type=F32)
        bias = b_ref[h].reshape(2 * t, 128 + t)
        s_far = s_c[:, :past - 128]
        s_near = s_c[:, past - 128:] + bias[:, :128]
        s_n = s_n + bias[:, 128:]
        m = jnp.maximum(jnp.maximum(jnp.max(s_far, axis=1, keepdims=True), jnp.max(s_near, axis=1, keepdims=True)),
                        jnp.max(s_n, axis=1, keepdims=True))
        p_far = jnp.exp2(s_far - m)
        p_near = jnp.exp2(s_near - m)
        p_n = jnp.exp2(s_n - m)
        l = (jnp.sum(p_far, axis=1, keepdims=True) + jnp.sum(p_near, axis=1, keepdims=True)
             + jnp.sum(p_n, axis=1, keepdims=True))
        o = (jnp.dot(p_far.astype(BF16), cv[:past - 128], preferred_element_type=F32)
             + jnp.dot(p_near.astype(BF16), cv[past - 128:], preferred_element_type=F32)
             + jnp.dot(p_n.astype(BF16), vn, preferred_element_type=F32))
        o = o * (1.0 / l)
        od = o[:t] - lam * o[t:]
        ms = jnp.mean(od * od, axis=-1, keepdims=True)
        od = od * lax.rsqrt(ms + EPS) * sub_ref[...] * (1.0 - lam_init)
        o_ref[0, h] = od.astype(BF16)


def _diff_sample(q_s, cache_k, cache_v, layer, kn, vn, bias_s, lam_p, subln_row, lam_init):
    nb, _, t, _ = q_s.shape
    past = cache_k.shape[2]
    rows = past * DIFF_HEADS
    ck = cache_k.reshape(cache_k.shape[0], nb, rows, 2 * HEAD_DIM)
    cv = cache_v.reshape(cache_v.shape[0], nb, rows, 2 * HEAD_DIM)
    cache_spec = pl.BlockSpec((None, None, rows, 2 * HEAD_DIM), lambda b: (layer, b, 0, 0))
    head_spec = pl.BlockSpec((1, DIFF_HEADS, t, 2 * HEAD_DIM), lambda b: (b, 0, 0, 0))
    return pl.pallas_call(
        functools.partial(_diff_sample_kernel, lam_init=lam_init, past=past),
        grid=(nb,),
        in_specs=[
            head_spec, cache_spec, cache_spec, head_spec, head_spec,
            pl.BlockSpec((DIFF_HEADS, 2, t, 128 + t), lambda b: (0, 0, 0, 0)),
            pl.BlockSpec((4, HEAD_DIM), lambda b: (0, 0)),
            pl.BlockSpec((1, 2 * HEAD_DIM), lambda b: (0, 0)),
        ],
        out_specs=head_spec,
        out_shape=jax.ShapeDtypeStruct((nb, DIFF_HEADS, t, 2 * HEAD_DIM), BF16),
        compiler_params=_cparams(1),
        name="diff_sample",
    )(q_s, ck, cv, kn, vn, bias_s, lam_p, subln_row)


def _swa_prompt_kernel(qT_ref, kc_ref, kp_ref, vc_ref, vp_ref, b_ref, sink_ref, o_ref):
    i = pl.program_id(0)
    qT = qT_ref[...]
    kp = kp_ref[...]
    kc = kc_ref[...]
    vp = vp_ref[0]
    vc = vc_ref[0]
    tq = qT.shape[1]
    key_row = lax.broadcasted_iota(jnp.int32, (2 * tq, SWA_GROUP * tq), 0)
    has_prev = key_row >= jnp.where(i > 0, 0, tq)
    zeros = jnp.zeros((HEAD_DIM, tq), BF16)
    outs = []
    for kv in range(SWA_KV_HEADS):
        pair = kv // 2
        k_band = jnp.concatenate([kp[:, pair * 128:(pair + 1) * 128], kc[:, pair * 128:(pair + 1) * 128]], axis=0)
        v_band = jnp.concatenate([vp[kv * HEAD_DIM:(kv + 1) * HEAD_DIM], vc[kv * HEAD_DIM:(kv + 1) * HEAD_DIM]],
                                 axis=1)
        qg = []
        for g in range(SWA_GROUP):
            h = kv * SWA_GROUP + g
            qh = qT[h * HEAD_DIM:(h + 1) * HEAD_DIM]
            qg.append(jnp.concatenate([qh, zeros] if kv % 2 == 0 else [zeros, qh], axis=0))
        q4 = jnp.concatenate(qg, axis=1)
        s = jnp.dot(k_band, q4, preferred_element_type=F32) + b_ref[kv]
        s = jnp.where(has_prev, s, NEG)
        sink = sink_ref[kv]
        m = jnp.maximum(jnp.max(s, axis=0, keepdims=True), sink)
        p = jnp.exp2(s - m)
        den = jnp.sum(p, axis=0, keepdims=True) + jnp.exp2(sink - m)
        oT4 = jnp.dot(v_band, p.astype(BF16), preferred_element_type=F32) * (1.0 / den)
        for g in range(SWA_GROUP):
            outs.append(oT4[:, g * tq:(g + 1) * tq])
    o_ref[...] = jnp.concatenate(outs, axis=0).T.astype(BF16)


def _swa_prompt(qT, k16, vT, bias, sinks):
    n_tok = k16.shape[0]
    tq = SWA_TILE
    nkv = SWA_KV_HEADS * HEAD_DIM
    return pl.pallas_call(
        _swa_prompt_kernel,
        grid=(n_tok // tq,),
        in_specs=[
            pl.BlockSpec((D_MODEL, tq), lambda i: (0, i)),
            pl.BlockSpec((tq, nkv), lambda i: (i, 0)),
            pl.BlockSpec((tq, nkv), lambda i: (jnp.maximum(i - 1, 0), 0)),
            pl.BlockSpec((1, nkv, tq), lambda i: (i, 0, 0)),
            pl.BlockSpec((1, nkv, tq), lambda i: (jnp.maximum(i - 1, 0), 0, 0)),
            _resident((SWA_KV_HEADS, 2 * tq, SWA_GROUP * tq), lambda i: (0, 0, 0)),
            _resident((SWA_KV_HEADS, 1, SWA_GROUP * tq), lambda i: (0, 0, 0)),
        ],
        out_specs=pl.BlockSpec((tq, D_MODEL), lambda i: (i, 0)),
        out_shape=jax.ShapeDtypeStruct((n_tok, D_MODEL), BF16),
        compiler_params=_cparams(1),
        name="swa_prompt",
    )(qT, k16, k16, vT, vT, bias, sinks)


def _swa_sample_kernel(q_ref, kc_ref, vc_ref, kn_ref, vn_ref, b_ref, sink_ref, o_ref, ko_ref, vo_ref):
    t = kn_ref.shape[2]
    buf = kc_ref.shape[2]
    for kv in range(SWA_KV_HEADS):
        kc32 = kc_ref[0, kv]
        vc32 = vc_ref[0, kv]
        kn32 = kn_ref[0, kv]
        vn32 = vn_ref[0, kv]
        ko_ref[0, kv, 0:buf - t, :] = kc32[t:]
        ko_ref[0, kv, buf - t:buf, :] = kn32
        vo_ref[0, kv, 0:buf - t, :] = vc32[t:]
        vo_ref[0, kv, buf - t:buf, :] = vn32
        kc = kc32.astype(BF16)
        vc = vc32.astype(BF16)
        kn = kn32.astype(BF16)
        vn = vn32.astype(BF16)
        for g in range(SWA_GROUP):
            h = kv * SWA_GROUP + g
            q = q_ref[0, h]
            b = b_ref[h]
            s_c = lax.dot_general(q, kc, NT_DIMS, preferred_element_type=F32) + b[:, :buf]
            s_n = lax.dot_general(q, kn, NT_DIMS, preferred_element_type=F32) + b[:, buf:]
            sink = sink_ref[h]
            m = jnp.maximum(jnp.maximum(jnp.max(s_c, axis=1, keepdims=True), jnp.max(s_n, axis=1, keepdims=True)),
                            sink)
            p_c = jnp.exp2(s_c - m)
            p_n = jnp.exp2(s_n - m)
            den = (jnp.sum(p_c, axis=1, keepdims=True) + jnp.sum(p_n, axis=1, keepdims=True)
                   + jnp.exp2(sink - m))
            o = (jnp.dot(p_c.astype(BF16), vc, preferred_element_type=F32)
                 + jnp.dot(p_n.astype(BF16), vn, preferred_element_type=F32)) * (1.0 / den)
            o_ref[0, h] = o.astype(BF16)


def _swa_sample(q_s, kc, vc, kn, vn, bias, sinks):
    nb, _, t, _ = q_s.shape
    buf = kc.shape[2]
    kvspec = pl.BlockSpec((1, SWA_KV_HEADS, buf, HEAD_DIM), lambda b: (b, 0, 0, 0))
    nspec = pl.BlockSpec((1, SWA_KV_HEADS, t, HEAD_DIM), lambda b: (b, 0, 0, 0))
    return pl.pallas_call(
        _swa_sample_kernel,
        grid=(nb,),
        in_specs=[
            pl.BlockSpec((1, SWA_Q_HEADS, t, HEAD_DIM), lambda b: (b, 0, 0, 0)),
            kvspec, kvspec, nspec, nspec,
            pl.BlockSpec((SWA_Q_HEADS, t, buf + t), lambda b: (0, 0, 0)),
            pl.BlockSpec((SWA_Q_HEADS, 1, 1), lambda b: (0, 0, 0)),
        ],
        out_specs=[
            pl.BlockSpec((1, SWA_Q_HEADS, t, HEAD_DIM), lambda b: (b, 0, 0, 0)),
            kvspec, kvspec,
        ],
        out_shape=[
            jax.ShapeDtypeStruct((nb, SWA_Q_HEADS, t, HEAD_DIM), BF16),
            jax.ShapeDtypeStruct((nb, SWA_KV_HEADS, buf, HEAD_DIM), F32),
            jax.ShapeDtypeStruct((nb, SWA_KV_HEADS, buf, HEAD_DIM), F32),
        ],
        compiler_params=_cparams(1),
        name="swa_sample",
    )(q_s, kc, vc, kn, vn, bias, sinks)


def _softplus(z):
    return jnp.maximum(z, 0.0) + jnp.log(1.0 + jnp.exp(-jnp.abs(z)))


def _split_bf16(x):
    hi = x.astype(BF16)
    lo = (x - hi.astype(F32)).astype(BF16)
    return hi, lo


def _sb_prompt_kernel(qT_ref, k_ref, vT_ref, kmax_ref, o_ref, carry_ref, acc_ref, *, tq):
    qi = pl.program_id(1)
    qT = qT_ref[...]
    row = lax.broadcasted_iota(jnp.int32, qT.shape, 0)
    zero = jnp.zeros_like(qT)
    qpair = jnp.concatenate([jnp.where(row < HEAD_DIM, qT, zero), jnp.where(row >= HEAD_DIM, qT, zero)], axis=1)
    zb = jnp.sum(jnp.abs(qpair.astype(F32)) * kmax_ref[...], axis=0, keepdims=True)
    kr = lax.broadcasted_iota(jnp.int32, (tq, tq), 0)
    kc = lax.broadcasted_iota(jnp.int32, (tq, tq), 1)
    tri = jnp.where(kc >= kr, 1.0, 0.0).astype(BF16)
    kr2 = lax.broadcasted_iota(jnp.int32, (tq, 2 * tq), 0)
    kc2 = lax.broadcasted_iota(jnp.int32, (tq, 2 * tq), 1)
    causal = kr2 < jnp.where(kc2 >= tq, kc2 - tq, kc2)
    carry_ref[...] = jnp.zeros(carry_ref.shape, F32)
    acc_ref[...] = jnp.zeros(acc_ref.shape, F32)

    def tiles(js, first_is_diag):
        zs, cs = [], []
        for n, j in enumerate(js):
            z = jnp.dot(k_ref[j], qpair, preferred_element_type=F32)
            lsp = _softplus(z)
            if first_is_diag and n == 0:
                lsp = jnp.where(causal, lsp, 0.0)
            hi, lo = _split_bf16(lsp)
            zs.append(z)
            cs.append(jnp.dot(tri, hi, preferred_element_type=F32) + jnp.dot(tri, lo, preferred_element_type=F32))
        carry = carry_ref[...]
        weights = []
        for n in range(len(js)):
            a = jnp.exp(zs[n] - cs[n] - carry)
            if first_is_diag and n == 0:
                a = jnp.where(causal, a, 0.0)
            weights.append(a.astype(BF16))
            carry = carry + cs[n][0:1]
        carry_ref[...] = carry
        vT = jnp.concatenate([vT_ref[j] for j in js], axis=1)
        acc_ref[...] += jnp.dot(vT, jnp.concatenate(weights, axis=0), preferred_element_type=F32)

    def still_live():
        return (jnp.min(carry_ref[...] - zb) < SB_SKIP_MARGIN).astype(jnp.int32)

    @pl.when(qi == 0)
    def _():
        tiles([qi], True)

    @pl.when(qi == 1)
    def _():
        tiles([qi, qi - 1], True)

    @pl.when(qi >= 2)
    def _():
        tiles([qi, qi - 1, qi - 2], True)

    def cond(state):
        j, live = state
        return jnp.logical_and(j >= 0, live > 0)

    def body(state):
        j, _ = state
        tiles([j], False)
        return j - 1, still_live()

    lax.while_loop(cond, body, (qi - 3, still_live()))
    o = jnp.concatenate([acc_ref[0:HEAD_DIM, 0:tq], acc_ref[HEAD_DIM:2 * HEAD_DIM, tq:2 * tq]], axis=0)
    o_ref[...] = o.T.astype(BF16)


def _sb_prompt(qT, k16, vT, kmax):
    n_tok = k16.shape[0]
    tq = min(SB_TILE, n_tok)
    nb = n_tok // tq
    k3 = k16.reshape(nb, tq, D_MODEL)
    return pl.pallas_call(
        functools.partial(_sb_prompt_kernel, tq=tq),
        grid=(SB_HEADS // 2, nb),
        in_specs=[
            pl.BlockSpec((128, tq), lambda p, i: (p, i)),
            pl.BlockSpec((nb, tq, 128), lambda p, i: (0, 0, p)),
            pl.BlockSpec((nb, 128, tq), lambda p, i: (0, p, 0)),
            pl.BlockSpec((128, 1), lambda p, i: (p, 0)),
        ],
        out_specs=pl.BlockSpec((tq, 128), lambda p, i: (i, p)),
        out_shape=jax.ShapeDtypeStruct((n_tok, D_MODEL), BF16),
        scratch_shapes=[
            pltpu.VMEM((1, 2 * tq), F32),
            pltpu.VMEM((2 * HEAD_DIM, 2 * tq), F32),
        ],
        compiler_params=_cparams(2),
        name="sb_prompt",
    )(qT, k3, vT, kmax)


def _tri_lanes(n):
    r = lax.broadcasted_iota(jnp.int32, (n, n), 0)
    c = lax.broadcasted_iota(jnp.int32, (n, n), 1)
    return jnp.where(r >= c, 1.0, 0.0).astype(BF16)


def _sb_sample_kernel(q_ref, ck_ref, cv_ref, kn_ref, vn_ref, o_ref, carry_ref, acc_ref, *, ch, kb):
    c = pl.program_id(1)
    nh = q_ref.shape[1]
    t = q_ref.shape[2]

    @pl.when(c == 0)
    def _():
        z = jnp.concatenate([lax.dot_general(q_ref[0, h], kn_ref[0, h], NT_DIMS, preferred_element_type=F32)
                             for h in range(nh)], axis=0)
        r = lax.broadcasted_iota(jnp.int32, (nh * t, t), 0)
        kc = lax.broadcasted_iota(jnp.int32, (nh * t, t), 1)
        causal = kc < lax.rem(r, t)
        hi, lo = _split_bf16(jnp.where(causal, _softplus(z), 0.0))
        tri = _tri_lanes(t)
        cs = jnp.dot(hi, tri, preferred_element_type=F32) + jnp.dot(lo, tri, preferred_element_type=F32)
        a = jnp.where(causal, jnp.exp(z - cs), 0.0).astype(BF16)
        carry_ref[...] = cs[:, 0:1]
        for h in range(nh):
            acc_ref[h * t:(h + 1) * t, :] = jnp.dot(a[h * t:(h + 1) * t], vn_ref[0, h].astype(BF16),
                                                    preferred_element_type=F32)

    z = jnp.concatenate([jnp.dot(q_ref[0, h], ck_ref[h].astype(BF16), preferred_element_type=F32)
                         for h in range(nh)], axis=0)
    nblk = ch // kb
    rows = nh * t
    zst = jnp.concatenate([z[:, b * kb:(b + 1) * kb] for b in range(nblk)], axis=0)
    hi, lo = _split_bf16(_softplus(zst))
    tri = _tri_lanes(kb)
    cs = jnp.dot(hi, tri, preferred_element_type=F32) + jnp.dot(lo, tri, preferred_element_type=F32)
    run = carry_ref[...]
    carries = [None] * nblk
    for b in reversed(range(nblk)):
        carries[b] = run
        run = run + cs[b * rows:(b + 1) * rows, 0:1]
    carry_ref[...] = run
    a = jnp.exp(zst - cs - jnp.concatenate(carries, axis=0))
    a = jnp.concatenate([a[b * rows:(b + 1) * rows] for b in range(nblk)], axis=1).astype(BF16)
    for h in range(nh):
        acc_ref[h * t:(h + 1) * t, :] += lax.dot_general(a[h * t:(h + 1) * t], cv_ref[h].astype(BF16), NT_DIMS,
                                                         preferred_element_type=F32)

    @pl.when(c == pl.num_programs(1) - 1)
    def _():
        for h in range(nh):
            o_ref[0, h] = acc_ref[h * t:(h + 1) * t, :].astype(BF16)


def _sb_sample(q_s, cache_k, cache_v, layer, kn, vn):
    nb, nh, t, _ = q_s.shape
    past = cache_k.shape[2]
    ch = min(SB_SAMPLE_CHUNK, past)
    nch = past // ch
    ck = cache_k.transpose(0, 1, 3, 4, 2)
    cv = cache_v.transpose(0, 1, 3, 4, 2)
    cache_spec = pl.BlockSpec((None, None, nh, HEAD_DIM, ch), lambda b, c: (layer, b, 0, 0, nch - 1 - c))
    head_spec = pl.BlockSpec((1, nh, t, HEAD_DIM), lambda b, c: (b, 0, 0, 0))
    return pl.pallas_call(
        functools.partial(_sb_sample_kernel, ch=ch, kb=min(256, ch)),
        grid=(nb, nch),
        in_specs=[head_spec, cache_spec, cache_spec, head_spec, head_spec],
        out_specs=head_spec,
        out_shape=jax.ShapeDtypeStruct((nb, nh, t, HEAD_DIM), BF16),
        scratch_shapes=[
            pltpu.VMEM((nh * t, 1), F32),
            pltpu.VMEM((nh * t, HEAD_DIM), F32),
        ],
        compiler_params=_cparams(2),
        name="sb_sample",
    )(q_s, ck, cv, kn, vn)


def _tile_col(gain, n, scale):
    return (jnp.tile(gain.astype(F32), n // gain.shape[0]) * scale).reshape(n, 1)


def _heads_major(x, nb, t, nh, hd):
    return x.reshape(nb, t, nh, hd).transpose(0, 2, 1, 3)


def _diff_layer(xp, xs, cache_k, cache_v, layer, rel_bias, g, w_qkv, q_gain, k_gain, lam_p, subln, lam_init):
    n_p = xp.shape[0]
    nb = cache_k.shape[1]
    t = xs.shape[0] // nb
    wt = w_qkv.T.astype(BF16)
    qg = _tile_col(q_gain, D_MODEL, HEAD_DIM ** -0.5 * LOG2E)
    kg = _tile_col(k_gain, D_MODEL, 1.0)
    g = g.reshape(1, D_MODEL)
    kw = dict(nq=D_MODEL, nk=D_MODEL, nv=D_MODEL, head_norm=True)
    tq = min(DIFF_TILE, n_p)
    qT, k32, k16, v32, vT, knorm2 = _project(xp, g, wt, qg, kg, vb=tq, kstat="normsq", **kw)
    qT_s, k32_s, k16_s, v32_s, _ = _project(xs, g, wt, qg, kg, vb=128, **kw)

    j = jnp.arange(tq)
    valid_d = (j // CHUNK)[:, None] <= (j // CHUNK)[None, :]
    bias_d = _bias_tile(rel_bias, tq, tq, 0, valid_d, True).reshape(DIFF_HEADS, 2, tq, tq)
    bias_p = _bias_tile(rel_bias, 128, 128, -128, None, True).reshape(DIFF_HEADS, 2, 128, 128)
    bstats = _bias_stats(rel_bias).reshape(DIFF_HEADS, 2, 2)
    o_p = _diff_prompt(qT, k16, vT, bias_d, bias_p, knorm2.reshape(DIFF_HEADS, 2, 1), bstats, lam_p,
                       subln.reshape(2 * HEAD_DIM, 1), lam_init)

    bias_s = jnp.swapaxes(_bias_tile(rel_bias, 128 + t, t, -128, None, True), 1, 2)
    bias_s = bias_s.reshape(DIFF_HEADS, 2, t, 128 + t)
    hm = functools.partial(_heads_major, nb=nb, t=t, nh=DIFF_HEADS, hd=2 * HEAD_DIM)
    o_s = _diff_sample(hm(qT_s.T), cache_k, cache_v, layer, hm(k16_s), hm(v32_s), bias_s, lam_p,
                       subln.reshape(1, 2 * HEAD_DIM), lam_init)
    o_s = o_s.transpose(0, 2, 1, 3).reshape(nb * t, D_MODEL)
    return (o_p, o_s,
            k32.reshape(1, n_p, DIFF_HEADS, 2 * HEAD_DIM), v32.reshape(1, n_p, DIFF_HEADS, 2 * HEAD_DIM),
            k32_s.reshape(nb, t, DIFF_HEADS, 2 * HEAD_DIM), v32_s.reshape(nb, t, DIFF_HEADS, 2 * HEAD_DIM))


def _swa_layer(xp, xs, ck, cv, rel_bias, g, w_qkv, q_gain, k_gain, sinks):
    n_p = xp.shape[0]
    nb, buf = ck.shape[0], ck.shape[1]
    t = xs.shape[0] // nb
    nkv = SWA_KV_HEADS * HEAD_DIM
    wt = w_qkv.T.astype(BF16)
    qg = _tile_col(q_gain, D_MODEL, HEAD_DIM ** -0.5 * LOG2E)
    kg = _tile_col(k_gain, nkv, 1.0)
    g = g.reshape(1, D_MODEL)
    kw = dict(nq=D_MODEL, nk=nkv, nv=nkv, vb=SWA_TILE, head_norm=True)
    qT, k32, k16, v32, vT = _project(xp, g, wt, qg, kg, **kw)
    qT_s, k32_s, _, v32_s, _ = _project(xs, g, wt, qg, kg, **kw)
    sink_col = (sinks.astype(F32) * LOG2E).reshape(SWA_Q_HEADS, 1, 1)
    wchunks = WINDOW // CHUNK

    tq = SWA_TILE
    kchunk = jnp.floor_divide(jnp.arange(2 * tq) - tq, CHUNK)[:, None]
    qchunk = (jnp.arange(tq) // CHUNK)[None, :]
    valid = (kchunk <= qchunk) & (kchunk >= qchunk - wchunks)
    bias = _bias_tile(rel_bias, 2 * tq, tq, -tq, valid, False)
    bias_g = bias.reshape(SWA_KV_HEADS, SWA_GROUP, 2 * tq, tq).transpose(0, 2, 1, 3)
    bias_g = bias_g.reshape(SWA_KV_HEADS, 2 * tq, SWA_GROUP * tq)
    sink_g = jnp.repeat(sink_col.reshape(SWA_KV_HEADS, 1, SWA_GROUP), tq, axis=2)
    o_p = _swa_prompt(qT, k16, vT, bias_g, sink_g)

    sc = jnp.floor_divide(jnp.arange(buf + t) - buf, CHUNK)[:, None]
    tc = (jnp.arange(t) // CHUNK)[None, :]
    valid_s = (sc <= tc) & (sc >= tc - wchunks)
    bias_s = jnp.swapaxes(_bias_tile(rel_bias, buf + t, t, -buf, valid_s, False), 1, 2)
    q_s = _heads_major(qT_s.T, nb, t, SWA_Q_HEADS, HEAD_DIM)
    kn = _heads_major(k32_s, nb, t, SWA_KV_HEADS, HEAD_DIM)
    vn = _heads_major(v32_s, nb, t, SWA_KV_HEADS, HEAD_DIM)
    o_s, ko, vo = _swa_sample(q_s, ck.transpose(0, 2, 1, 3), cv.transpose(0, 2, 1, 3), kn, vn, bias_s, sink_col)
    o_s = o_s.transpose(0, 2, 1, 3).reshape(nb * t, D_MODEL)
    wbuf = min(WINDOW, n_p)
    return (o_p, o_s,
            k32[n_p - wbuf:].reshape(1, wbuf, SWA_KV_HEADS, HEAD_DIM),
            v32[n_p - wbuf:].reshape(1, wbuf, SWA_KV_HEADS, HEAD_DIM),
            ko.transpose(0, 2, 1, 3), vo.transpose(0, 2, 1, 3))


def _sb_layer(xp, xs, cache_k, cache_v, layer, g, w_qkv):
    n_p = xp.shape[0]
    nb = cache_k.shape[1]
    t = xs.shape[0] // nb
    wt = w_qkv.T.astype(BF16)
    qg = jnp.full((D_MODEL, 1), HEAD_DIM ** -0.5, F32)
    kg = jnp.ones((D_MODEL, 1), F32)
    g = g.reshape(1, D_MODEL)
    kw = dict(nq=D_MODEL, nk=D_MODEL, nv=D_MODEL, head_norm=False)
    tq = min(SB_TILE, n_p)
    qT, k32, k16, v32, vT, kmax = _project(xp, g, wt, qg, kg, vb=tq, kstat="absmax", **kw)
    qT_s, k32_s, k16_s, v32_s, _ = _project(xs, g, wt, qg, kg, vb=128, **kw)
    o_p = _sb_prompt(qT, k16, vT, kmax)
    hm = functools.partial(_heads_major, nb=nb, t=t, nh=SB_HEADS, hd=HEAD_DIM)
    o_s = _sb_sample(hm(qT_s.T), cache_k, cache_v, layer, hm(k16_s), hm(v32_s))
    o_s = o_s.transpose(0, 2, 1, 3).reshape(nb * t, D_MODEL)
    return (o_p, o_s,
            k32.reshape(1, n_p, SB_HEADS, HEAD_DIM), v32.reshape(1, n_p, SB_HEADS, HEAD_DIM),
            k32_s.reshape(nb, t, SB_HEADS, HEAD_DIM), v32_s.reshape(nb, t, SB_HEADS, HEAD_DIM))


def kernel(x_prompt, x_sample, cache_diff_k, cache_diff_v, cache_swa_k, cache_swa_v, cache_sb_k, cache_sb_v, rel_bias, norm_mix, norm_mlp, w_up, w_down, diff_w_qkv, diff_w_o, diff_q_norm, diff_k_norm, diff_lambda, diff_subln, swa_w_qkv, swa_w_o, swa_q_norm, swa_k_norm, swa_sinks, sb_w_qkv, sb_w_o):
    bp, n_p, _ = x_prompt.shape
    assert bp == 1
    nb, t, _ = x_sample.shape
    depth = norm_mix.shape[0]
    xp = x_prompt.reshape(n_p, D_MODEL)
    xs = x_sample.reshape(nb * t, D_MODEL)
    outs = {name: [] for name in ("pdk", "pdv", "pwk", "pwv", "pbk", "pbv", "sdk", "sdv", "swk", "swv", "sbk", "sbv")}
    for i in range(depth):
        j = i // N_MIXERS
        if i % N_MIXERS == 0:
            lam_init = 0.8 - 0.6 * math.exp(-0.3 * i)
            o_p, o_s, kp, vp, kn, vn = _diff_layer(
                xp, xs, cache_diff_k, cache_diff_v, j, rel_bias, norm_mix[i], diff_w_qkv[j],
                diff_q_norm[j], diff_k_norm[j], diff_lambda[j], diff_subln[j], lam_init)
            w_o = diff_w_o[j]
            names = ("pdk", "pdv", "sdk", "sdv")
        elif i % N_MIXERS == 1:
            o_p, o_s, kp, vp, kn, vn = _swa_layer(
                xp, xs, cache_swa_k[j], cache_swa_v[j], rel_bias, norm_mix[i], swa_w_qkv[j],
                swa_q_norm[j], swa_k_norm[j], swa_sinks[j])
            w_o = swa_w_o[j]
            names = ("pwk", "pwv", "swk", "swv")
        else:
            o_p, o_s, kp, vp, kn, vn = _sb_layer(xp, xs, cache_sb_k, cache_sb_v, j, norm_mix[i], sb_w_qkv[j])
            w_o = sb_w_o[j]
            names = ("pbk", "pbv", "sbk", "sbv")
        for name, val in zip(names, (kp, vp, kn, vn)):
            outs[name].append(val)
        wo16 = w_o.astype(BF16)
        g_mlp = norm_mlp[i].reshape(1, D_MODEL)
        wup16 = w_up[i].astype(BF16)
        wdn16 = w_down[i].astype(BF16)
        xp = _outproj_mlp(xp, o_p, wo16, g_mlp, wup16, wdn16)
        xs = _outproj_mlp(xs, o_s, wo16, g_mlp, wup16, wdn16)
    st = {name: jnp.stack(v) for name, v in outs.items()}
    return (xp.reshape(1, n_p, D_MODEL), xs.reshape(nb, t, D_MODEL),
            st["pdk"], st["pdv"], st["pwk"], st["pwv"], st["pbk"], st["pbv"],
            st["sdk"], st["sdv"], st["swk"], st["swv"], st["sbk"], st["sbv"])
```

```python
import functools
import math

import jax
import jax.numpy as jnp
from jax import lax
from jax.experimental import pallas as pl
from jax.experimental.pallas import tpu as pltpu

F32 = jnp.float32
BF16 = jnp.bfloat16

D_MODEL = 1024
HEAD_DIM = 64
CHUNK = 64
N_MIXERS = 3
DIFF_HEADS = 8
SWA_Q_HEADS = 16
SWA_KV_HEADS = 4
SWA_GROUP = 4
WINDOW = 128
SB_HEADS = 16
N_BUCKETS = 32
MAX_DISTANCE = 128
D_FF = 4 * D_MODEL
EPS = 1e-6
LOG2E = 1.4426950408889634
NEG = -1e30
FAR_BUCKET = 15
SB_SKIP_MARGIN = 145.0
SOFTMAX_BOUND_LIMIT = 100.0

VMEM_LIMIT = 56 * 1024 * 1024
TOKEN_TILE = 512
PROJ_CHUNK = 512
DIFF_TILE = 512
DIFF_FAR_GROUP = 4
SB_TILE = 256
SB_SAMPLE_CHUNK = 512
SWA_TILE = 128

NT_DIMS = (((1,), (1,)), ((), ()))


def _cparams(n_axes):
    return pltpu.CompilerParams(dimension_semantics=("arbitrary",) * n_axes,
                                vmem_limit_bytes=VMEM_LIMIT)


def _resident(block_shape, index_map):
    return pl.BlockSpec(block_shape, index_map, pipeline_mode=pl.Buffered(1))


def _t5_bucket(rel):
    half = N_BUCKETS // 2
    exact = half // 2
    ret = jnp.where(rel > 0, half, 0).astype(jnp.int32)
    n = jnp.abs(rel)
    nf = jnp.maximum(n, 1).astype(F32)
    large = exact + (jnp.log(nf / exact) / math.log(MAX_DISTANCE / exact) * (half - exact)).astype(jnp.int32)
    large = jnp.minimum(large, half - 1)
    return ret + jnp.where(n < exact, n, large)


def _bias_tile(rel_bias, n_j, n_i, r0, valid, shift_far):
    tab = rel_bias.astype(F32)
    period = -(-(n_i + n_j - 1) // 128) * 128
    u = jnp.arange(period)
    rel = jnp.where(u < n_i, r0 - u, r0 + (period - u))
    vals = jnp.moveaxis(tab[_t5_bucket(rel)], -1, 0)
    if shift_far:
        vals = vals - tab[FAR_BUCKET][:, None]
    vals = (vals * LOG2E).reshape(vals.shape[0], 1, period)

    def expand(y_ref, o_ref):
        rows = jnp.broadcast_to(y_ref[0], (n_j, period))
        o_ref[0] = pltpu.roll(rows, 0, 1, stride=1, stride_axis=0)[:, :n_i]

    b = pl.pallas_call(
        expand,
        grid=(vals.shape[0],),
        in_specs=[pl.BlockSpec((1, 1, period), lambda c: (c, 0, 0))],
        out_specs=pl.BlockSpec((1, n_j, n_i), lambda c: (c, 0, 0)),
        out_shape=jax.ShapeDtypeStruct((vals.shape[0], n_j, n_i), F32),
        compiler_params=_cparams(1),
        name="bias_toeplitz",
    )(vals)
    if valid is not None:
        b = jnp.where(valid[None], b, NEG)
    return b


def _bias_stats(rel_bias):
    tab = rel_bias.astype(F32)
    sh = (tab - tab[FAR_BUCKET][None, :]) * LOG2E
    bmax = jnp.max(sh, axis=0)
    return jnp.stack([bmax, bmax - jnp.min(sh, axis=0)], axis=-1)


def _proj_kernel(x_ref, g_ref, wt_ref, qg_ref, kg_ref, *refs, nq, nk, nv, vb, head_norm, kstat, n_alias,
                 slot, all_slots):
    out_refs = refs[n_alias:]
    qT_ref, k32_ref, k16_ref, v32_ref, vT_ref = out_refs[:5]
    if all_slots > 1:
        for s in range(all_slots):
            if s != slot:
                k32_ref[s] = jnp.zeros(k32_ref.shape[1:], F32)
                v32_ref[s] = jnp.zeros(v32_ref.shape[1:], F32)
        k32_ref = k32_ref.at[slot]
        v32_ref = v32_ref.at[slot]
    x = x_ref[...]
    ms = jnp.mean(x * x, axis=-1, keepdims=True)
    h = (x * lax.rsqrt(ms + EPS) * g_ref[...]).astype(BF16)
    tm = x.shape[0]

    def rows(r0, n):
        return lax.dot_general(wt_ref[r0:r0 + n, :], h, NT_DIMS, preferred_element_type=F32)

    def headnorm(t, gcol):
        n = t.shape[0]
        t3 = t.reshape(n // HEAD_DIM, HEAD_DIM, tm)
        r = lax.rsqrt(jnp.mean(t3 * t3, axis=1, keepdims=True) + EPS)
        return (t3 * r).reshape(n, tm) * gcol

    for r0 in range(0, nq, PROJ_CHUNK):
        n = min(PROJ_CHUNK, nq - r0)
        qt = rows(r0, n)
        qt = headnorm(qt, qg_ref[r0:r0 + n, :]) if head_norm else qt * qg_ref[r0:r0 + n, :]
        qT_ref[r0:r0 + n, :] = qt.astype(BF16)

    for r0 in range(0, nk, PROJ_CHUNK):
        n = min(PROJ_CHUNK, nk - r0)
        kt = rows(nq + r0, n)
        if head_norm:
            kt = headnorm(kt, kg_ref[r0:r0 + n, :])
        k = kt.T
        k32_ref[:, r0:r0 + n] = k
        k16_ref[:, r0:r0 + n] = k.astype(BF16)
        if kstat is not None:
            stat_ref = out_refs[5]
            kr = kt.astype(BF16).astype(F32)
            if kstat == "absmax":
                stat = jnp.max(jnp.abs(kr), axis=1, keepdims=True)
                srows = slice(r0, r0 + n)
            else:
                k3 = kr.reshape(n // HEAD_DIM, HEAD_DIM, tm)
                stat = jnp.max(jnp.sum(k3 * k3, axis=1), axis=1, keepdims=True)
                srows = slice(r0 // HEAD_DIM, (r0 + n) // HEAD_DIM)

            @pl.when(pl.program_id(0) == 0)
            def _(stat=stat, srows=srows):
                stat_ref[srows, :] = stat

            @pl.when(pl.program_id(0) > 0)
            def _(stat=stat, srows=srows):
                stat_ref[srows, :] = jnp.maximum(stat_ref[srows, :], stat)

    for r0 in range(0, nv, PROJ_CHUNK):
        n = min(PROJ_CHUNK, nv - r0)
        vt = rows(nq + nk + r0, n)
        v32_ref[:, r0:r0 + n] = vt.T
        vt16 = vt.astype(BF16)
        for b in range(tm // vb):
            vT_ref[b, r0:r0 + n, :] = vt16[:, b * vb:(b + 1) * vb]


def _project(x, g, wt, qg, kg, *, nq, nk, nv, vb, head_norm, kstat=None, slot=0, n_slots=1, earlier=None):
    n_tok = x.shape[0]
    tm = min(TOKEN_TILE, n_tok)
    vb = min(vb, tm)
    grid = (n_tok // tm,)
    ntot = nq + nk + nv
    out_shape = [
        jax.ShapeDtypeStruct((nq, n_tok), BF16),
        jax.ShapeDtypeStruct((n_slots, n_tok, nk), F32),
        jax.ShapeDtypeStruct((n_tok, nk), BF16),
        jax.ShapeDtypeStruct((n_slots, n_tok, nv), F32),
        jax.ShapeDtypeStruct((n_tok // vb, nv, vb), BF16),
    ]
    all_slots = n_slots if (earlier is None and n_slots > 1) else 1

    def kv_spec(width):
        if all_slots > 1:
            return pl.BlockSpec((n_slots, tm, width), lambda i: (0, i, 0))
        return pl.BlockSpec((None, tm, width), lambda i: (slot, i, 0))

    out_specs = [
        pl.BlockSpec((nq, tm), lambda i: (0, i)),
        kv_spec(nk),
        pl.BlockSpec((tm, nk), lambda i: (i, 0)),
        kv_spec(nv),
        pl.BlockSpec((tm // vb, nv, vb), lambda i: (i, 0, 0)),
    ]
    if kstat is not None:
        n_stat = nk if kstat == "absmax" else nk // HEAD_DIM
        out_shape.append(jax.ShapeDtypeStruct((n_stat, 1), F32))
        out_specs.append(pl.BlockSpec((n_stat, 1), lambda i: (0, 0)))
    in_specs = [
        pl.BlockSpec((tm, D_MODEL), lambda i: (i, 0)),
        _resident((1, D_MODEL), lambda i: (0, 0)),
        _resident((ntot, D_MODEL), lambda i: (0, 0)),
        _resident((nq, 1), lambda i: (0, 0)),
        _resident((nk, 1), lambda i: (0, 0)),
    ]
    args = [x, g, wt, qg, kg]
    aliases = {}
    if earlier is not None:
        aliases = {len(args): 1, len(args) + 1: 3}
        in_specs += [pl.BlockSpec(memory_space=pl.ANY)] * 2
        args += list(earlier)
    return pl.pallas_call(
        functools.partial(_proj_kernel, nq=nq, nk=nk, nv=nv, vb=vb, head_norm=head_norm, kstat=kstat,
                          n_alias=len(aliases), slot=slot, all_slots=all_slots),
        grid=grid,
        in_specs=in_specs,
        out_specs=out_specs,
        out_shape=out_shape,
        input_output_aliases=aliases,
        compiler_params=_cparams(1),
        name="proj",
    )(*args)


def _mlp_kernel(x_ref, o_ref, wo_ref, g_ref, wup_ref, wdn_ref, y_ref, *, f_chunk):
    x1 = x_ref[...] + jnp.dot(o_ref[...], wo_ref[...], preferred_element_type=F32)
    ms = jnp.mean(x1 * x1, axis=-1, keepdims=True)
    hn = (x1 * lax.rsqrt(ms + EPS) * g_ref[...]).astype(BF16)
    acc = x1
    for f in range(D_FF // f_chunk):
        u = jnp.dot(hn, wup_ref[:, f * f_chunk:(f + 1) * f_chunk], preferred_element_type=F32)
        a = jnp.square(jnp.maximum(u, 0.0)).astype(BF16)
        acc = acc + jnp.dot(a, wdn_ref[f * f_chunk:(f + 1) * f_chunk, :], preferred_element_type=F32)
    y_ref[...] = acc


def _outproj_mlp(x, o, wo, g, wup, wdn):
    n_tok = x.shape[0]
    tm = min(TOKEN_TILE, n_tok)
    return pl.pallas_call(
        functools.partial(_mlp_kernel, f_chunk=1024),
        grid=(n_tok // tm,),
        in_specs=[
            pl.BlockSpec((tm, D_MODEL), lambda i: (i, 0)),
            pl.BlockSpec((tm, D_MODEL), lambda i: (i, 0)),
            _resident((D_MODEL, D_MODEL), lambda i: (0, 0)),
            _resident((1, D_MODEL), lambda i: (0, 0)),
            _resident((D_MODEL, D_FF), lambda i: (0, 0)),
            _resident((D_FF, D_MODEL), lambda i: (0, 0)),
        ],
        out_specs=pl.BlockSpec((tm, D_MODEL), lambda i: (i, 0)),
        out_shape=jax.ShapeDtypeStruct((n_tok, D_MODEL), F32),
        compiler_params=_cparams(1),
        name="outproj_mlp",
    )(x, o, wo, g, wup, wdn)


def _diff_lambda(lam_ref, lam_init):
    lp = lam_ref[...]
    a = jnp.sum(lp[0:1] * lp[1:2], axis=-1, keepdims=True)
    b = jnp.sum(lp[2:3] * lp[3:4], axis=-1, keepdims=True)
    return jnp.exp(a) - jnp.exp(b) + lam_init


def _diff_prompt_kernel(qT_ref, k_ref, vT_ref, bd_ref, bp_ref, kn_ref, bs_ref, lam_ref, sub_ref, o_ref,
                        m_ref, l_ref, acc_ref, *, tq, lam_init):
    qi = pl.program_id(1)
    qT = qT_ref[...]
    row = lax.broadcasted_iota(jnp.int32, qT.shape, 0)
    zero = jnp.zeros_like(qT)
    qm = (jnp.where(row < HEAD_DIM, qT, zero), jnp.where(row >= HEAD_DIM, qT, zero))

    def prev_bias(m, s):
        corner = s[tq - 128:, :128] + bp_ref[0, m]
        bottom = jnp.concatenate([corner, s[tq - 128:, 128:]], axis=1)
        return jnp.concatenate([s[:tq - 128], bottom], axis=0)

    def diag_bias(m, s):
        return s + bd_ref[0, m]

    def prev_diag_bias(m, s):
        return jnp.concatenate([prev_bias(m, s[:tq]), diag_bias(m, s[tq:])], axis=0)

    n_far = jnp.maximum(qi - 1, 0)

    def walk(tile_fn):
        def far_body(j, carry):
            tile_fn(j, None)
            return carry

        lax.fori_loop(0, n_far, far_body, 0)

        @pl.when(qi >= 1)
        def _():
            tile_fn(qi - 1, prev_bias)

        tile_fn(qi, diag_bias)

    span = []
    for m in range(2):
        qf = qm[m].astype(F32)
        qn = jnp.sqrt(jnp.sum(qf * qf, axis=0, keepdims=True))
        reach = qn * (jnp.sqrt(kn_ref[m:m + 1, :]) * 1.001)
        m_ref[m] = reach + bs_ref[m:m + 1, 0:1]
        span.append(jnp.max(2.0 * reach + bs_ref[m:m + 1, 1:2]))
    bound_is_tight = jnp.maximum(span[0], span[1]) <= SOFTMAX_BOUND_LIMIT

    @pl.when(jnp.logical_not(bound_is_tight))
    def _():
        m_ref[...] = jnp.full(m_ref.shape, NEG, F32)

        def max_tile(j, bias_fn):
            k = k_ref[j]
            for m in range(2):
                s = jnp.dot(k, qm[m], preferred_element_type=F32)
                if bias_fn is not None:
                    s = bias_fn(m, s)
                m_ref[m] = jnp.maximum(m_ref[m], jnp.max(s, axis=0, keepdims=True))

        walk(max_tile)

    l_ref[...] = jnp.zeros(l_ref.shape, F32)
    acc_ref[...] = jnp.zeros(acc_ref.shape, F32)

    def acc_tiles(js, bias_fn):
        k = jnp.concatenate([k_ref[j] for j in js], axis=0)
        vT = jnp.concatenate([vT_ref[j] for j in js], axis=1)
        nk = len(js) * tq
        for m in range(2):
            s = jnp.dot(k, qm[m], preferred_element_type=F32)
            if bias_fn is not None:
                s = bias_fn(m, s)
            p = jnp.exp2(s - m_ref[m])
            l_ref[m] += jnp.sum(p.reshape(nk // 8, 8, tq), axis=0)
            acc_ref[m] += jnp.dot(vT, p.astype(BF16), preferred_element_type=F32)

    def far_group(jj, carry):
        acc_tiles([DIFF_FAR_GROUP * jj + u for u in range(DIFF_FAR_GROUP)], None)
        return carry

    n_groups = n_far // DIFF_FAR_GROUP
    lax.fori_loop(0, n_groups, far_group, 0)
    start = n_groups * DIFF_FAR_GROUP
    left = n_far - start
    size = DIFF_FAR_GROUP // 2
    while size >= 1:
        @pl.when((left & size) != 0)
        def _(start=start, size=size):
            acc_tiles([start + u for u in range(size)], None)

        start = start + (left & size)
        size //= 2

    @pl.when(qi >= 1)
    def _():
        acc_tiles([qi - 1, qi], prev_diag_bias)

    @pl.when(qi == 0)
    def _():
        acc_tiles([qi], diag_bias)

    lam = _diff_lambda(lam_ref, lam_init)
    o0 = acc_ref[0] * (1.0 / jnp.sum(l_ref[0], axis=0, keepdims=True))
    o1 = acc_ref[1] * (1.0 / jnp.sum(l_ref[1], axis=0, keepdims=True))
    o = o0 - lam * o1
    ms = jnp.mean(o * o, axis=0, keepdims=True)
    o = o * lax.rsqrt(ms + EPS) * sub_ref[...] * (1.0 - lam_init)
    o_ref[...] = o.T.astype(BF16)


def _diff_prompt(qT, k16, vT, bias_d, bias_p, knorm2, bstats, lam_p, subln, lam_init):
    n_tok = k16.shape[0]
    tq = min(DIFF_TILE, n_tok)
    nb = n_tok // tq
    k3 = k16.reshape(nb, tq, D_MODEL)
    return pl.pallas_call(
        functools.partial(_diff_prompt_kernel, tq=tq, lam_init=lam_init),
        grid=(DIFF_HEADS, nb),
        in_specs=[
            pl.BlockSpec((2 * HEAD_DIM, tq), lambda h, i: (h, i)),
            pl.BlockSpec((nb, tq, 2 * HEAD_DIM), lambda h, i: (0, 0, h)),
            pl.BlockSpec((nb, 2 * HEAD_DIM, tq), lambda h, i: (0, h, 0)),
            pl.BlockSpec((1, 2, tq, tq), lambda h, i: (h, 0, 0, 0)),
            pl.BlockSpec((1, 2, 128, 128), lambda h, i: (h, 0, 0, 0)),
            pl.BlockSpec((None, 2, 1), lambda h, i: (h, 0, 0)),
            pl.BlockSpec((None, 2, 2), lambda h, i: (h, 0, 0)),
            pl.BlockSpec((4, HEAD_DIM), lambda h, i: (0, 0)),
            pl.BlockSpec((2 * HEAD_DIM, 1), lambda h, i: (0, 0)),
        ],
        out_specs=pl.BlockSpec((tq, 2 * HEAD_DIM), lambda h, i: (i, h)),
        out_shape=jax.ShapeDtypeStruct((n_tok, D_MODEL), BF16),
        scratch_shapes=[
            pltpu.VMEM((2, 1, tq), F32),
            pltpu.VMEM((2, 8, tq), F32),
            pltpu.VMEM((2, 2 * HEAD_DIM, tq), F32),
        ],
        compiler_params=_cparams(2),
        name="diff_prompt",
    )(qT, k3, vT, bias_d, bias_p, knorm2, bstats, lam_p, subln)


def _diff_sample_kernel(q_ref, ck_ref, cv_ref, kn_ref, vn_ref, b_ref, lam_ref, sub_ref, o_ref, *, lam_init, past):
    t = q_ref.shape[2]
    lane = lax.broadcasted_iota(jnp.int32, (t, 2 * HEAD_DIM), 1)
    lam = _diff_lambda(lam_ref, lam_init)
    for h in range(DIFF_HEADS):
        q = q_ref[0, h]
        zero = jnp.zeros_like(q)
        qq = jnp.concatenate([jnp.where(lane < HEAD_DIM, q, zero), jnp.where(lane >= HEAD_DIM, q, zero)], axis=0)
        ck = ck_ref[pl.ds(h, past, stride=DIFF_HEADS), :].astype(BF16)
        cv = cv_ref[pl.ds(h, past, stride=DIFF_HEADS), :].astype(BF16)
        kn = kn_ref[0, h]
        vn = vn_ref[0, h].astype(BF16)
        s_c = lax.dot_general(qq, ck, NT_DIMS, preferred_element_type=F32)
        s_n = lax.dot_general(qq, kn, NT_DIMS, preferred_element_type=F32)
        bias = b_ref[h].reshape(2 * t, 128 + t)
        s_far = s_c[:, :past - 128]
        s_near = s_c[:, past - 128:] + bias[:, :128]
        s_n = s_n + bias[:, 128:]
        m = jnp.maximum(jnp.maximum(jnp.max(s_far, axis=1, keepdims=True), jnp.max(s_near, axis=1, keepdims=True)),
                        jnp.max(s_n, axis=1, keepdims=True))
        p_far = jnp.exp2(s_far - m)
        p_near = jnp.exp2(s_near - m)
        p_n = jnp.exp2(s_n - m)
        l = (jnp.sum(p_far, axis=1, keepdims=True) + jnp.sum(p_near, axis=1, keepdims=True)
             + jnp.sum(p_n, axis=1, keepdims=True))
        o = (jnp.dot(p_far.astype(BF16), cv[:past - 128], preferred_element_type=F32)
             + jnp.dot(p_near.astype(BF16), cv[past - 128:], preferred_element_type=F32)
             + jnp.dot(p_n.astype(BF16), vn, preferred_element_type=F32))
        o = o * (1.0 / l)
        od = o[:t] - lam * o[t:]
        ms = jnp.mean(od * od, axis=-1, keepdims=True)
        od = od * lax.rsqrt(ms + EPS) * sub_ref[...] * (1.0 - lam_init)
        o_ref[0, h] = od.astype(BF16)


def _diff_sample(q_s, cache_k, cache_v, layer, kn, vn, bias_s, lam_p, subln_row, lam_init):
    nb, _, t, _ = q_s.shape
    past = cache_k.shape[2]
    rows = past * DIFF_HEADS
    ck = cache_k.reshape(cache_k.shape[0], nb, rows, 2 * HEAD_DIM)
    cv = cache_v.reshape(cache_v.shape[0], nb, rows, 2 * HEAD_DIM)
    cache_spec = pl.BlockSpec((None, None, rows, 2 * HEAD_DIM), lambda b: (layer, b, 0, 0))
    head_spec = pl.BlockSpec((1, DIFF_HEADS, t, 2 * HEAD_DIM), lambda b: (b, 0, 0, 0))
    return pl.pallas_call(
        functools.partial(_diff_sample_kernel, lam_init=lam_init, past=past),
        grid=(nb,),
        in_specs=[
            head_spec, cache_spec, cache_spec, head_spec, head_spec,
            pl.BlockSpec((DIFF_HEADS, 2, t, 128 + t), lambda b: (0, 0, 0, 0)),
            pl.BlockSpec((4, HEAD_DIM), lambda b: (0, 0)),
            pl.BlockSpec((1, 2 * HEAD_DIM), lambda b: (0, 0)),
        ],
        out_specs=head_spec,
        out_shape=jax.ShapeDtypeStruct((nb, DIFF_HEADS, t, 2 * HEAD_DIM), BF16),
        compiler_params=_cparams(1),
        name="diff_sample",
    )(q_s, ck, cv, kn, vn, bias_s, lam_p, subln_row)


def _swa_prompt_kernel(qT_ref, kc_ref, kp_ref, vc_ref, vp_ref, b_ref, sink_ref, o_ref):
    i = pl.program_id(0)
    qT = qT_ref[...]
    kp = kp_ref[...]
    kc = kc_ref[...]
    vp = vp_ref[0]
    vc = vc_ref[0]
    tq = qT.shape[1]
    key_row = lax.broadcasted_iota(jnp.int32, (2 * tq, SWA_GROUP * tq), 0)
    has_prev = key_row >= jnp.where(i > 0, 0, tq)
    zeros = jnp.zeros((HEAD_DIM, tq), BF16)
    outs = []
    for kv in range(SWA_KV_HEADS):
        pair = kv // 2
        k_band = jnp.concatenate([kp[:, pair * 128:(pair + 1) * 128], kc[:, pair * 128:(pair + 1) * 128]], axis=0)
        v_band = jnp.concatenate([vp[kv * HEAD_DIM:(kv + 1) * HEAD_DIM], vc[kv * HEAD_DIM:(kv + 1) * HEAD_DIM]],
                                 axis=1)
        qg = []
        for g in range(SWA_GROUP):
            h = kv * SWA_GROUP + g
            qh = qT[h * HEAD_DIM:(h + 1) * HEAD_DIM]
            qg.append(jnp.concatenate([qh, zeros] if kv % 2 == 0 else [zeros, qh], axis=0))
        q4 = jnp.concatenate(qg, axis=1)
        s = jnp.dot(k_band, q4, preferred_element_type=F32) + b_ref[kv]
        s = jnp.where(has_prev, s, NEG)
        sink = sink_ref[kv]
        m = jnp.maximum(jnp.max(s, axis=0, keepdims=True), sink)
        p = jnp.exp2(s - m)
        den = jnp.sum(p, axis=0, keepdims=True) + jnp.exp2(sink - m)
        oT4 = jnp.dot(v_band, p.astype(BF16), preferred_element_type=F32) * (1.0 / den)
        for g in range(SWA_GROUP):
            outs.append(oT4[:, g * tq:(g + 1) * tq])
    o_ref[...] = jnp.concatenate(outs, axis=0).T.astype(BF16)


def _swa_prompt(qT, k16, vT, bias, sinks):
    n_tok = k16.shape[0]
    tq = SWA_TILE
    nkv = SWA_KV_HEADS * HEAD_DIM
    return pl.pallas_call(
        _swa_prompt_kernel,
        grid=(n_tok // tq,),
        in_specs=[
            pl.BlockSpec((D_MODEL, tq), lambda i: (0, i)),
            pl.BlockSpec((tq, nkv), lambda i: (i, 0)),
            pl.BlockSpec((tq, nkv), lambda i: (jnp.maximum(i - 1, 0), 0)),
            pl.BlockSpec((1, nkv, tq), lambda i: (i, 0, 0)),
            pl.BlockSpec((1, nkv, tq), lambda i: (jnp.maximum(i - 1, 0), 0, 0)),
            _resident((SWA_KV_HEADS, 2 * tq, SWA_GROUP * tq), lambda i: (0, 0, 0)),
            _resident((SWA_KV_HEADS, 1, SWA_GROUP * tq), lambda i: (0, 0, 0)),
        ],
        out_specs=pl.BlockSpec((tq, D_MODEL), lambda i: (i, 0)),
        out_shape=jax.ShapeDtypeStruct((n_tok, D_MODEL), BF16),
        compiler_params=_cparams(1),
        name="swa_prompt",
    )(qT, k16, k16, vT, vT, bias, sinks)


def _swa_sample_kernel(q_ref, kc_ref, vc_ref, kn_ref, vn_ref, b_ref, sink_ref, o_ref, ko_ref, vo_ref):
    t = kn_ref.shape[2]
    buf = kc_ref.shape[2]
    for kv in range(SWA_KV_HEADS):
        kc32 = kc_ref[0, kv]
        vc32 = vc_ref[0, kv]
        kn32 = kn_ref[0, kv]
        vn32 = vn_ref[0, kv]
        ko_ref[0, kv, 0:buf - t, :] = kc32[t:]
        ko_ref[0, kv, buf - t:buf, :] = kn32
        vo_ref[0, kv, 0:buf - t, :] = vc32[t:]
        vo_ref[0, kv, buf - t:buf, :] = vn32
        kc = kc32.astype(BF16)
        vc = vc32.astype(BF16)
        kn = kn32.astype(BF16)
        vn = vn32.astype(BF16)
        for g in range(SWA_GROUP):
            h = kv * SWA_GROUP + g
            q = q_ref[0, h]
            b = b_ref[h]
            s_c = lax.dot_general(q, kc, NT_DIMS, preferred_element_type=F32) + b[:, :buf]
            s_n = lax.dot_general(q, kn, NT_DIMS, preferred_element_type=F32) + b[:, buf:]
            sink = sink_ref[h]
            m = jnp.maximum(jnp.maximum(jnp.max(s_c, axis=1, keepdims=True), jnp.max(s_n, axis=1, keepdims=True)),
                            sink)
            p_c = jnp.exp2(s_c - m)
            p_n = jnp.exp2(s_n - m)
            den = (jnp.sum(p_c, axis=1, keepdims=True) + jnp.sum(p_n, axis=1, keepdims=True)
                   + jnp.exp2(sink - m))
            o = (jnp.dot(p_c.astype(BF16), vc, preferred_element_type=F32)
                 + jnp.dot(p_n.astype(BF16), vn, preferred_element_type=F32)) * (1.0 / den)
            o_ref[0, h] = o.astype(BF16)


def _swa_sample(q_s, kc, vc, kn, vn, bias, sinks):
    nb, _, t, _ = q_s.shape
    buf = kc.shape[2]
    kvspec = pl.BlockSpec((1, SWA_KV_HEADS, buf, HEAD_DIM), lambda b: (b, 0, 0, 0))
    nspec = pl.BlockSpec((1, SWA_KV_HEADS, t, HEAD_DIM), lambda b: (b, 0, 0, 0))
    return pl.pallas_call(
        _swa_sample_kernel,
        grid=(nb,),
        in_specs=[
            pl.BlockSpec((1, SWA_Q_HEADS, t, HEAD_DIM), lambda b: (b, 0, 0, 0)),
            kvspec, kvspec, nspec, nspec,
            pl.BlockSpec((SWA_Q_HEADS, t, buf + t), lambda b: (0, 0, 0)),
            pl.BlockSpec((SWA_Q_HEADS, 1, 1), lambda b: (0, 0, 0)),
        ],
        out_specs=[
            pl.BlockSpec((1, SWA_Q_HEADS, t, HEAD_DIM), lambda b: (b, 0, 0, 0)),
            kvspec, kvspec,
        ],
        out_shape=[
            jax.ShapeDtypeStruct((nb, SWA_Q_HEADS, t, HEAD_DIM), BF16),
            jax.ShapeDtypeStruct((nb, SWA_KV_HEADS, buf, HEAD_DIM), F32),
            jax.ShapeDtypeStruct((nb, SWA_KV_HEADS, buf, HEAD_DIM), F32),
        ],
        compiler_params=_cparams(1),
        name="swa_sample",
    )(q_s, kc, vc, kn, vn, bias, sinks)


def _softplus(z):
    return jnp.maximum(z, 0.0) + jnp.log2(1.0 + jnp.exp2(-jnp.abs(z)))


def _split_bf16(x):
    hi = x.astype(BF16)
    lo = (x - hi.astype(F32)).astype(BF16)
    return hi, lo


def _sb_prompt_kernel(qT_ref, k_ref, vT_ref, kmax_ref, o_ref, carry_ref, acc_ref, *, tq):
    qi = pl.program_id(1)
    qT = qT_ref[...]
    row = lax.broadcasted_iota(jnp.int32, qT.shape, 0)
    zero = jnp.zeros_like(qT)
    qpair = jnp.concatenate([jnp.where(row < HEAD_DIM, qT, zero), jnp.where(row >= HEAD_DIM, qT, zero)], axis=1)
    zb = jnp.sum(jnp.abs(qpair.astype(F32)) * kmax_ref[...], axis=0, keepdims=True)
    kr = lax.broadcasted_iota(jnp.int32, (tq, tq), 0)
    kc = lax.broadcasted_iota(jnp.int32, (tq, tq), 1)
    tri = jnp.where(kc >= kr, 1.0, 0.0).astype(BF16)
    kr2 = lax.broadcasted_iota(jnp.int32, (tq, 2 * tq), 0)
    kc2 = lax.broadcasted_iota(jnp.int32, (tq, 2 * tq), 1)
    causal = kr2 < jnp.where(kc2 >= tq, kc2 - tq, kc2)
    carry_ref[...] = jnp.zeros(carry_ref.shape, F32)
    acc_ref[...] = jnp.zeros(acc_ref.shape, F32)

    def tiles(js, first_is_diag):
        zs, cs = [], []
        for n, j in enumerate(js):
            z = jnp.dot(k_ref[j], qpair, preferred_element_type=F32)
            lsp = _softplus(z)
            if first_is_diag and n == 0:
                lsp = jnp.where(causal, lsp, 0.0)
            hi, lo = _split_bf16(lsp)
            zs.append(z)
            cs.append(jnp.dot(tri, hi, preferred_element_type=F32) + jnp.dot(tri, lo, preferred_element_type=F32))
        carry = carry_ref[...]
        weights = []
        for n in range(len(js)):
            a = jnp.exp2(zs[n] - cs[n] - carry)
            if first_is_diag and n == 0:
                a = jnp.where(causal, a, 0.0)
            weights.append(a.astype(BF16))
            carry = carry + cs[n][0:1]
        carry_ref[...] = carry
        vT = jnp.concatenate([vT_ref[j] for j in js], axis=1)
        acc_ref[...] += jnp.dot(vT, jnp.concatenate(weights, axis=0), preferred_element_type=F32)

    def still_live():
        return (jnp.min(carry_ref[...] - zb) < SB_SKIP_MARGIN).astype(jnp.int32)

    @pl.when(qi == 0)
    def _():
        tiles([qi], True)

    @pl.when(qi == 1)
    def _():
        tiles([qi, qi - 1], True)

    @pl.when(qi >= 2)
    def _():
        tiles([qi, qi - 1, qi - 2], True)

    def cond(state):
        j, live = state
        return jnp.logical_and(j >= 0, live > 0)

    def body(state):
        j, _ = state
        tiles([j], False)
        return j - 1, still_live()

    lax.while_loop(cond, body, (qi - 3, still_live()))
    o = jnp.concatenate([acc_ref[0:HEAD_DIM, 0:tq], acc_ref[HEAD_DIM:2 * HEAD_DIM, tq:2 * tq]], axis=0)
    o_ref[...] = o.T.astype(BF16)


def _sb_prompt(qT, k16, vT, kmax):
    n_tok = k16.shape[0]
    tq = min(SB_TILE, n_tok)
    nb = n_tok // tq
    k3 = k16.reshape(nb, tq, D_MODEL)
    return pl.pallas_call(
        functools.partial(_sb_prompt_kernel, tq=tq),
        grid=(SB_HEADS // 2, nb),
        in_specs=[
            pl.BlockSpec((128, tq), lambda p, i: (p, i)),
            pl.BlockSpec((nb, tq, 128), lambda p, i: (0, 0, p)),
            pl.BlockSpec((nb, 128, tq), lambda p, i: (0, p, 0)),
            pl.BlockSpec((128, 1), lambda p, i: (p, 0)),
        ],
        out_specs=pl.BlockSpec((tq, 128), lambda p, i: (i, p)),
        out_shape=jax.ShapeDtypeStruct((n_tok, D_MODEL), BF16),
        scratch_shapes=[
            pltpu.VMEM((1, 2 * tq), F32),
            pltpu.VMEM((2 * HEAD_DIM, 2 * tq), F32),
        ],
        compiler_params=_cparams(2),
        name="sb_prompt",
    )(qT, k3, vT, kmax)


def _tri_lanes(n):
    r = lax.broadcasted_iota(jnp.int32, (n, n), 0)
    c = lax.broadcasted_iota(jnp.int32, (n, n), 1)
    return jnp.where(r >= c, 1.0, 0.0).astype(BF16)


def _sb_sample_kernel(q_ref, ck_ref, cv_ref, kn_ref, vn_ref, o_ref, carry_ref, acc_ref, *, ch, kb):
    c = pl.program_id(1)
    nh = q_ref.shape[1]
    t = q_ref.shape[2]

    @pl.when(c == 0)
    def _():
        z = jnp.concatenate([lax.dot_general(q_ref[0, h], kn_ref[0, h], NT_DIMS, preferred_element_type=F32)
                             for h in range(nh)], axis=0)
        r = lax.broadcasted_iota(jnp.int32, (nh * t, t), 0)
        kc = lax.broadcasted_iota(jnp.int32, (nh * t, t), 1)
        causal = kc < lax.rem(r, t)
        hi, lo = _split_bf16(jnp.where(causal, _softplus(z), 0.0))
        tri = _tri_lanes(t)
        cs = jnp.dot(hi, tri, preferred_element_type=F32) + jnp.dot(lo, tri, preferred_element_type=F32)
        a = jnp.where(causal, jnp.exp2(z - cs), 0.0).astype(BF16)
        carry_ref[...] = cs[:, 0:1]
        for h in range(nh):
            acc_ref[h * t:(h + 1) * t, :] = jnp.dot(a[h * t:(h + 1) * t], vn_ref[0, h].astype(BF16),
                                                    preferred_element_type=F32)

    z = jnp.concatenate([jnp.dot(q_ref[0, h], ck_ref[h].astype(BF16), preferred_element_type=F32)
                         for h in range(nh)], axis=0)
    nblk = ch // kb
    rows = nh * t
    zst = jnp.concatenate([z[:, b * kb:(b + 1) * kb] for b in range(nblk)], axis=0)
    hi, lo = _split_bf16(_softplus(zst))
    tri = _tri_lanes(kb)
    cs = jnp.dot(hi, tri, preferred_element_type=F32) + jnp.dot(lo, tri, preferred_element_type=F32)
    run = carry_ref[...]
    carries = [None] * nblk
    for b in reversed(range(nblk)):
        carries[b] = run
        run = run + cs[b * rows:(b + 1) * rows, 0:1]
    carry_ref[...] = run
    a = jnp.exp2(zst - cs - jnp.concatenate(carries, axis=0))
    a = jnp.concatenate([a[b * rows:(b + 1) * rows] for b in range(nblk)], axis=1).astype(BF16)
    for h in range(nh):
        acc_ref[h * t:(h + 1) * t, :] += lax.dot_general(a[h * t:(h + 1) * t], cv_ref[h].astype(BF16), NT_DIMS,
                                                         preferred_element_type=F32)

    @pl.when(c == pl.num_programs(1) - 1)
    def _():
        for h in range(nh):
            o_ref[0, h] = acc_ref[h * t:(h + 1) * t, :].astype(BF16)


def _sb_sample(q_s, cache_k, cache_v, layer, kn, vn):
    nb, nh, t, _ = q_s.shape
    past = cache_k.shape[2]
    ch = min(SB_SAMPLE_CHUNK, past)
    nch = past // ch
    ck = cache_k.transpose(0, 1, 3, 4, 2)
    cv = cache_v.transpose(0, 1, 3, 4, 2)
    cache_spec = pl.BlockSpec((None, None, nh, HEAD_DIM, ch), lambda b, c: (layer, b, 0, 0, nch - 1 - c))
    head_spec = pl.BlockSpec((1, nh, t, HEAD_DIM), lambda b, c: (b, 0, 0, 0))
    return pl.pallas_call(
        functools.partial(_sb_sample_kernel, ch=ch, kb=min(256, ch)),
        grid=(nb, nch),
        in_specs=[head_spec, cache_spec, cache_spec, head_spec, head_spec],
        out_specs=head_spec,
        out_shape=jax.ShapeDtypeStruct((nb, nh, t, HEAD_DIM), BF16),
        scratch_shapes=[
            pltpu.VMEM((nh * t, 1), F32),
            pltpu.VMEM((nh * t, HEAD_DIM), F32),
        ],
        compiler_params=_cparams(2),
        name="sb_sample",
    )(q_s, ck, cv, kn, vn)


def _tile_col(gain, n, scale):
    return (jnp.tile(gain.astype(F32), n // gain.shape[0]) * scale).reshape(n, 1)


def _heads_major(x, nb, t, nh, hd):
    return x.reshape(nb, t, nh, hd).transpose(0, 2, 1, 3)


def _diff_layer(xp, xs, cache_k, cache_v, layer, n_layers, earlier, rel_bias, g, w_qkv, q_gain, k_gain, lam_p,
                subln, lam_init):
    n_p = xp.shape[0]
    nb = cache_k.shape[1]
    t = xs.shape[0] // nb
    wt = w_qkv.T.astype(BF16)
    qg = _tile_col(q_gain, D_MODEL, HEAD_DIM ** -0.5 * LOG2E)
    kg = _tile_col(k_gain, D_MODEL, 1.0)
    g = g.reshape(1, D_MODEL)
    kw = dict(nq=D_MODEL, nk=D_MODEL, nv=D_MODEL, head_norm=True)
    tq = min(DIFF_TILE, n_p)
    qT, k32, k16, v32, vT, knorm2 = _project(xp, g, wt, qg, kg, vb=tq, kstat="normsq", slot=layer,
                                             n_slots=n_layers, earlier=earlier, **kw)
    qT_s, k32_s, k16_s, v32_s, _ = _project(xs, g, wt, qg, kg, vb=128, **kw)
    k32_s, v32_s = k32_s[0], v32_s[0]

    j = jnp.arange(tq)
    valid_d = (j // CHUNK)[:, None] <= (j // CHUNK)[None, :]
    bias_d = _bias_tile(rel_bias, tq, tq, 0, valid_d, True).reshape(DIFF_HEADS, 2, tq, tq)
    bias_p = _bias_tile(rel_bias, 128, 128, -128, None, True).reshape(DIFF_HEADS, 2, 128, 128)
    bstats = _bias_stats(rel_bias).reshape(DIFF_HEADS, 2, 2)
    o_p = _diff_prompt(qT, k16, vT, bias_d, bias_p, knorm2.reshape(DIFF_HEADS, 2, 1), bstats, lam_p,
                       subln.reshape(2 * HEAD_DIM, 1), lam_init)

    bias_s = jnp.swapaxes(_bias_tile(rel_bias, 128 + t, t, -128, None, True), 1, 2)
    bias_s = bias_s.reshape(DIFF_HEADS, 2, t, 128 + t)
    hm = functools.partial(_heads_major, nb=nb, t=t, nh=DIFF_HEADS, hd=2 * HEAD_DIM)
    o_s = _diff_sample(hm(qT_s.T), cache_k, cache_v, layer, hm(k16_s), hm(v32_s), bias_s, lam_p,
                       subln.reshape(1, 2 * HEAD_DIM), lam_init)
    o_s = o_s.transpose(0, 2, 1, 3).reshape(nb * t, D_MODEL)
    return (o_p, o_s,
            k32, v32,
            k32_s.reshape(nb, t, DIFF_HEADS, 2 * HEAD_DIM), v32_s.reshape(nb, t, DIFF_HEADS, 2 * HEAD_DIM))


def _swa_layer(xp, xs, ck, cv, rel_bias, g, w_qkv, q_gain, k_gain, sinks):
    n_p = xp.shape[0]
    nb, buf = ck.shape[0], ck.shape[1]
    t = xs.shape[0] // nb
    nkv = SWA_KV_HEADS * HEAD_DIM
    wt = w_qkv.T.astype(BF16)
    qg = _tile_col(q_gain, D_MODEL, HEAD_DIM ** -0.5 * LOG2E)
    kg = _tile_col(k_gain, nkv, 1.0)
    g = g.reshape(1, D_MODEL)
    kw = dict(nq=D_MODEL, nk=nkv, nv=nkv, vb=SWA_TILE, head_norm=True)
    qT, k32, k16, v32, vT = _project(xp, g, wt, qg, kg, **kw)
    qT_s, k32_s, _, v32_s, _ = _project(xs, g, wt, qg, kg, **kw)
    k32, v32, k32_s, v32_s = k32[0], v32[0], k32_s[0], v32_s[0]
    sink_col = (sinks.astype(F32) * LOG2E).reshape(SWA_Q_HEADS, 1, 1)
    wchunks = WINDOW // CHUNK

    tq = SWA_TILE
    kchunk = jnp.floor_divide(jnp.arange(2 * tq) - tq, CHUNK)[:, None]
    qchunk = (jnp.arange(tq) // CHUNK)[None, :]
    valid = (kchunk <= qchunk) & (kchunk >= qchunk - wchunks)
    bias = _bias_tile(rel_bias, 2 * tq, tq, -tq, valid, False)
    bias_g = bias.reshape(SWA_KV_HEADS, SWA_GROUP, 2 * tq, tq).transpose(0, 2, 1, 3)
    bias_g = bias_g.reshape(SWA_KV_HEADS, 2 * tq, SWA_GROUP * tq)
    sink_g = jnp.repeat(sink_col.reshape(SWA_KV_HEADS, 1, SWA_GROUP), tq, axis=2)
    o_p = _swa_prompt(qT, k16, vT, bias_g, sink_g)

    sc = jnp.floor_divide(jnp.arange(buf + t) - buf, CHUNK)[:, None]
    tc = (jnp.arange(t) // CHUNK)[None, :]
    valid_s = (sc <= tc) & (sc >= tc - wchunks)
    bias_s = jnp.swapaxes(_bias_tile(rel_bias, buf + t, t, -buf, valid_s, False), 1, 2)
    q_s = _heads_major(qT_s.T, nb, t, SWA_Q_HEADS, HEAD_DIM)
    kn = _heads_major(k32_s, nb, t, SWA_KV_HEADS, HEAD_DIM)
    vn = _heads_major(v32_s, nb, t, SWA_KV_HEADS, HEAD_DIM)
    o_s, ko, vo = _swa_sample(q_s, ck.transpose(0, 2, 1, 3), cv.transpose(0, 2, 1, 3), kn, vn, bias_s, sink_col)
    o_s = o_s.transpose(0, 2, 1, 3).reshape(nb * t, D_MODEL)
    wbuf = min(WINDOW, n_p)
    return (o_p, o_s,
            k32[n_p - wbuf:].reshape(1, wbuf, SWA_KV_HEADS, HEAD_DIM),
            v32[n_p - wbuf:].reshape(1, wbuf, SWA_KV_HEADS, HEAD_DIM),
            ko.transpose(0, 2, 1, 3), vo.transpose(0, 2, 1, 3))


def _sb_layer(xp, xs, cache_k, cache_v, layer, n_layers, earlier, g, w_qkv):
    n_p = xp.shape[0]
    nb = cache_k.shape[1]
    t = xs.shape[0] // nb
    wt = w_qkv.T.astype(BF16)
    qg = jnp.full((D_MODEL, 1), HEAD_DIM ** -0.5 * LOG2E, F32)
    kg = jnp.ones((D_MODEL, 1), F32)
    g = g.reshape(1, D_MODEL)
    kw = dict(nq=D_MODEL, nk=D_MODEL, nv=D_MODEL, head_norm=False)
    tq = min(SB_TILE, n_p)
    qT, k32, k16, v32, vT, kmax = _project(xp, g, wt, qg, kg, vb=tq, kstat="absmax", slot=layer,
                                           n_slots=n_layers, earlier=earlier, **kw)
    qT_s, k32_s, k16_s, v32_s, _ = _project(xs, g, wt, qg, kg, vb=128, **kw)
    k32_s, v32_s = k32_s[0], v32_s[0]
    o_p = _sb_prompt(qT, k16, vT, kmax)
    hm = functools.partial(_heads_major, nb=nb, t=t, nh=SB_HEADS, hd=HEAD_DIM)
    o_s = _sb_sample(hm(qT_s.T), cache_k, cache_v, layer, hm(k16_s), hm(v32_s))
    o_s = o_s.transpose(0, 2, 1, 3).reshape(nb * t, D_MODEL)
    return (o_p, o_s,
            k32, v32,
            k32_s.reshape(nb, t, SB_HEADS, HEAD_DIM), v32_s.reshape(nb, t, SB_HEADS, HEAD_DIM))


def kernel(x_prompt, x_sample, cache_diff_k, cache_diff_v, cache_swa_k, cache_swa_v, cache_sb_k, cache_sb_v, rel_bias, norm_mix, norm_mlp, w_up, w_down, diff_w_qkv, diff_w_o, diff_q_norm, diff_k_norm, diff_lambda, diff_subln, swa_w_qkv, swa_w_o, swa_q_norm, swa_k_norm, swa_sinks, sb_w_qkv, sb_w_o):
    bp, n_p, _ = x_prompt.shape
    assert bp == 1
    nb, t, _ = x_sample.shape
    depth = norm_mix.shape[0]
    xp = x_prompt.reshape(n_p, D_MODEL)
    xs = x_sample.reshape(nb * t, D_MODEL)
    n_diff, n_sb = cache_diff_k.shape[0], cache_sb_k.shape[0]
    outs = {name: [] for name in ("pwk", "pwv", "sdk", "sdv", "swk", "swv", "sbk", "sbv")}
    diff_kv = sb_kv = None
    for i in range(depth):
        j = i // N_MIXERS
        if i % N_MIXERS == 0:
            lam_init = 0.8 - 0.6 * math.exp(-0.3 * i)
            o_p, o_s, kp, vp, kn, vn = _diff_layer(
                xp, xs, cache_diff_k, cache_diff_v, j, n_diff, diff_kv, rel_bias, norm_mix[i], diff_w_qkv[j],
                diff_q_norm[j], diff_k_norm[j], diff_lambda[j], diff_subln[j], lam_init)
            diff_kv = (kp, vp)
            w_o = diff_w_o[j]
            names = ("sdk", "sdv")
        elif i % N_MIXERS == 1:
            o_p, o_s, kp, vp, kn, vn = _swa_layer(
                xp, xs, cache_swa_k[j], cache_swa_v[j], rel_bias, norm_mix[i], swa_w_qkv[j],
                swa_q_norm[j], swa_k_norm[j], swa_sinks[j])
            outs["pwk"].append(kp)
            outs["pwv"].append(vp)
            w_o = swa_w_o[j]
            names = ("swk", "swv")
        else:
            o_p, o_s, kp, vp, kn, vn = _sb_layer(xp, xs, cache_sb_k, cache_sb_v, j, n_sb, sb_kv, norm_mix[i],
                                                 sb_w_qkv[j])
            sb_kv = (kp, vp)
            w_o = sb_w_o[j]
            names = ("sbk", "sbv")
        for name, val in zip(names, (kn, vn)):
            outs[name].append(val)
        wo16 = w_o.astype(BF16)
        g_mlp = norm_mlp[i].reshape(1, D_MODEL)
        wup16 = w_up[i].astype(BF16)
        wdn16 = w_down[i].astype(BF16)
        xp = _outproj_mlp(xp, o_p, wo16, g_mlp, wup16, wdn16)
        xs = _outproj_mlp(xs, o_s, wo16, g_mlp, wup16, wdn16)
    st = {name: jnp.stack(v) for name, v in outs.items()}
    diff_shape = (n_diff, 1, n_p, DIFF_HEADS, 2 * HEAD_DIM)
    sb_shape = (n_sb, 1, n_p, SB_HEADS, HEAD_DIM)
    return (xp.reshape(1, n_p, D_MODEL), xs.reshape(nb, t, D_MODEL),
            diff_kv[0].reshape(diff_shape), diff_kv[1].reshape(diff_shape), st["pwk"], st["pwv"],
            sb_kv[0].reshape(sb_shape), sb_kv[1].reshape(sb_shape),
            st["sdk"], st["sdv"], st["swk"], st["swv"], st["sbk"], st["sbv"])
```

```python
import functools
import math

import jax
import jax.numpy as jnp
from jax import lax
from jax.experimental import pallas as pl
from jax.experimental.pallas import tpu as pltpu

F32 = jnp.float32
BF16 = jnp.bfloat16

D_MODEL = 1024
HEAD_DIM = 64
CHUNK = 64
N_MIXERS = 3
DIFF_HEADS = 8
SWA_Q_HEADS = 16
SWA_KV_HEADS = 4
SWA_GROUP = 4
WINDOW = 128
SB_HEADS = 16
N_BUCKETS = 32
MAX_DISTANCE = 128
D_FF = 4 * D_MODEL
EPS = 1e-6
LOG2E = 1.4426950408889634
NEG = -1e30
FAR_BUCKET = 15
SB_SKIP_MARGIN = 145.0
SOFTMAX_BOUND_LIMIT = 100.0

VMEM_LIMIT = 56 * 1024 * 1024
TOKEN_TILE = 512
PROJ_CHUNK = 512
DIFF_TILE = 512
DIFF_FAR_GROUP = 4
SB_TILE = 256
SB_PAIRS_PER_STEP = 2
SB_SAMPLE_CHUNK = 512
SWA_TILE = 128

NT_DIMS = (((1,), (1,)), ((), ()))


def _cparams(n_axes):
    return pltpu.CompilerParams(dimension_semantics=("arbitrary",) * n_axes,
                                vmem_limit_bytes=VMEM_LIMIT)


def _resident(block_shape, index_map):
    return pl.BlockSpec(block_shape, index_map, pipeline_mode=pl.Buffered(1))


def _t5_bucket(rel):
    half = N_BUCKETS // 2
    exact = half // 2
    ret = jnp.where(rel > 0, half, 0).astype(jnp.int32)
    n = jnp.abs(rel)
    nf = jnp.maximum(n, 1).astype(F32)
    large = exact + (jnp.log(nf / exact) / math.log(MAX_DISTANCE / exact) * (half - exact)).astype(jnp.int32)
    large = jnp.minimum(large, half - 1)
    return ret + jnp.where(n < exact, n, large)


def _bias_tile(rel_bias, n_j, n_i, r0, valid, shift_far):
    tab = rel_bias.astype(F32)
    period = -(-(n_i + n_j - 1) // 128) * 128
    u = jnp.arange(period)
    rel = jnp.where(u < n_i, r0 - u, r0 + (period - u))
    vals = jnp.moveaxis(tab[_t5_bucket(rel)], -1, 0)
    if shift_far:
        vals = vals - tab[FAR_BUCKET][:, None]
    vals = (vals * LOG2E).reshape(vals.shape[0], 1, period)

    def expand(y_ref, o_ref):
        rows = jnp.broadcast_to(y_ref[0], (n_j, period))
        o_ref[0] = pltpu.roll(rows, 0, 1, stride=1, stride_axis=0)[:, :n_i]

    b = pl.pallas_call(
        expand,
        grid=(vals.shape[0],),
        in_specs=[pl.BlockSpec((1, 1, period), lambda c: (c, 0, 0))],
        out_specs=pl.BlockSpec((1, n_j, n_i), lambda c: (c, 0, 0)),
        out_shape=jax.ShapeDtypeStruct((vals.shape[0], n_j, n_i), F32),
        compiler_params=_cparams(1),
        name="bias_toeplitz",
    )(vals)
    if valid is not None:
        b = jnp.where(valid[None], b, NEG)
    return b


def _bias_stats(rel_bias):
    tab = rel_bias.astype(F32)
    sh = (tab - tab[FAR_BUCKET][None, :]) * LOG2E
    bmax = jnp.max(sh, axis=0)
    return jnp.stack([bmax, bmax - jnp.min(sh, axis=0)], axis=-1)


def _proj_kernel(x_ref, g_ref, wt_ref, qg_ref, kg_ref, *refs, nq, nk, nv, vb, head_norm, kstat, n_alias,
                 slot, all_slots):
    out_refs = refs[n_alias:]
    qT_ref, k32_ref, k16_ref, v32_ref, vT_ref = out_refs[:5]
    if all_slots > 1:
        for s in range(all_slots):
            if s != slot:
                k32_ref[s] = jnp.zeros(k32_ref.shape[1:], F32)
                v32_ref[s] = jnp.zeros(v32_ref.shape[1:], F32)
        k32_ref = k32_ref.at[slot]
        v32_ref = v32_ref.at[slot]
    x = x_ref[...]
    ms = jnp.mean(x * x, axis=-1, keepdims=True)
    h = (x * lax.rsqrt(ms + EPS) * g_ref[...]).astype(BF16)
    tm = x.shape[0]

    def rows(r0, n):
        return lax.dot_general(wt_ref[r0:r0 + n, :], h, NT_DIMS, preferred_element_type=F32)

    def headnorm(t, gcol):
        n = t.shape[0]
        t3 = t.reshape(n // HEAD_DIM, HEAD_DIM, tm)
        r = lax.rsqrt(jnp.mean(t3 * t3, axis=1, keepdims=True) + EPS)
        return (t3 * r).reshape(n, tm) * gcol

    for r0 in range(0, nq, PROJ_CHUNK):
        n = min(PROJ_CHUNK, nq - r0)
        qt = rows(r0, n)
        qt = headnorm(qt, qg_ref[r0:r0 + n, :]) if head_norm else qt * qg_ref[r0:r0 + n, :]
        qT_ref[r0:r0 + n, :] = qt.astype(BF16)

    for r0 in range(0, nk, PROJ_CHUNK):
        n = min(PROJ_CHUNK, nk - r0)
        kt = rows(nq + r0, n)
        if head_norm:
            kt = headnorm(kt, kg_ref[r0:r0 + n, :])
        k = kt.T
        k32_ref[:, r0:r0 + n] = k
        k16_ref[:, r0:r0 + n] = k.astype(BF16)
        if kstat is not None:
            stat_ref = out_refs[5]
            kr = kt.astype(BF16).astype(F32)
            if kstat == "absmax":
                stat = jnp.max(jnp.abs(kr), axis=1, keepdims=True)
                srows = slice(r0, r0 + n)
            else:
                k3 = kr.reshape(n // HEAD_DIM, HEAD_DIM, tm)
                stat = jnp.max(jnp.sum(k3 * k3, axis=1), axis=1, keepdims=True)
                srows = slice(r0 // HEAD_DIM, (r0 + n) // HEAD_DIM)

            @pl.when(pl.program_id(0) == 0)
            def _(stat=stat, srows=srows):
                stat_ref[srows, :] = stat

            @pl.when(pl.program_id(0) > 0)
            def _(stat=stat, srows=srows):
                stat_ref[srows, :] = jnp.maximum(stat_ref[srows, :], stat)

    for r0 in range(0, nv, PROJ_CHUNK):
        n = min(PROJ_CHUNK, nv - r0)
        vt = rows(nq + nk + r0, n)
        v32_ref[:, r0:r0 + n] = vt.T
        vt16 = vt.astype(BF16)
        for b in range(tm // vb):
            vT_ref[b, r0:r0 + n, :] = vt16[:, b * vb:(b + 1) * vb]


def _project(x, g, wt, qg, kg, *, nq, nk, nv, vb, head_norm, kstat=None, slot=0, n_slots=1, earlier=None):
    n_tok = x.shape[0]
    tm = min(TOKEN_TILE, n_tok)
    vb = min(vb, tm)
    grid = (n_tok // tm,)
    ntot = nq + nk + nv
    out_shape = [
        jax.ShapeDtypeStruct((nq, n_tok), BF16),
        jax.ShapeDtypeStruct((n_slots, n_tok, nk), F32),
        jax.ShapeDtypeStruct((n_tok, nk), BF16),
        jax.ShapeDtypeStruct((n_slots, n_tok, nv), F32),
        jax.ShapeDtypeStruct((n_tok // vb, nv, vb), BF16),
    ]
    all_slots = n_slots if (earlier is None and n_slots > 1) else 1

    def kv_spec(width):
        if all_slots > 1:
            return pl.BlockSpec((n_slots, tm, width), lambda i: (0, i, 0))
        return pl.BlockSpec((None, tm, width), lambda i: (slot, i, 0))

    out_specs = [
        pl.BlockSpec((nq, tm), lambda i: (0, i)),
        kv_spec(nk),
        pl.BlockSpec((tm, nk), lambda i: (i, 0)),
        kv_spec(nv),
        pl.BlockSpec((tm // vb, nv, vb), lambda i: (i, 0, 0)),
    ]
    if kstat is not None:
        n_stat = nk if kstat == "absmax" else nk // HEAD_DIM
        out_shape.append(jax.ShapeDtypeStruct((n_stat, 1), F32))
        out_specs.append(pl.BlockSpec((n_stat, 1), lambda i: (0, 0)))
    in_specs = [
        pl.BlockSpec((tm, D_MODEL), lambda i: (i, 0)),
        _resident((1, D_MODEL), lambda i: (0, 0)),
        _resident((ntot, D_MODEL), lambda i: (0, 0)),
        _resident((nq, 1), lambda i: (0, 0)),
        _resident((nk, 1), lambda i: (0, 0)),
    ]
    args = [x, g, wt, qg, kg]
    aliases = {}
    if earlier is not None:
        aliases = {len(args): 1, len(args) + 1: 3}
        in_specs += [pl.BlockSpec(memory_space=pl.ANY)] * 2
        args += list(earlier)
    return pl.pallas_call(
        functools.partial(_proj_kernel, nq=nq, nk=nk, nv=nv, vb=vb, head_norm=head_norm, kstat=kstat,
                          n_alias=len(aliases), slot=slot, all_slots=all_slots),
        grid=grid,
        in_specs=in_specs,
        out_specs=out_specs,
        out_shape=out_shape,
        input_output_aliases=aliases,
        compiler_params=_cparams(1),
        name="proj",
    )(*args)


def _mlp_kernel(x_ref, o_ref, wo_ref, g_ref, wup_ref, wdn_ref, y_ref, *, f_chunk):
    x1 = x_ref[...] + jnp.dot(o_ref[...], wo_ref[...], preferred_element_type=F32)
    ms = jnp.mean(x1 * x1, axis=-1, keepdims=True)
    hn = (x1 * lax.rsqrt(ms + EPS) * g_ref[...]).astype(BF16)
    acc = x1
    for f in range(D_FF // f_chunk):
        u = jnp.dot(hn, wup_ref[:, f * f_chunk:(f + 1) * f_chunk], preferred_element_type=F32)
        a = jnp.square(jnp.maximum(u, 0.0)).astype(BF16)
        acc = acc + jnp.dot(a, wdn_ref[f * f_chunk:(f + 1) * f_chunk, :], preferred_element_type=F32)
    y_ref[...] = acc


def _outproj_mlp(x, o, wo, g, wup, wdn):
    n_tok = x.shape[0]
    tm = min(TOKEN_TILE, n_tok)
    return pl.pallas_call(
        functools.partial(_mlp_kernel, f_chunk=1024),
        grid=(n_tok // tm,),
        in_specs=[
            pl.BlockSpec((tm, D_MODEL), lambda i: (i, 0)),
            pl.BlockSpec((tm, D_MODEL), lambda i: (i, 0)),
            _resident((D_MODEL, D_MODEL), lambda i: (0, 0)),
            _resident((1, D_MODEL), lambda i: (0, 0)),
            _resident((D_MODEL, D_FF), lambda i: (0, 0)),
            _resident((D_FF, D_MODEL), lambda i: (0, 0)),
        ],
        out_specs=pl.BlockSpec((tm, D_MODEL), lambda i: (i, 0)),
        out_shape=jax.ShapeDtypeStruct((n_tok, D_MODEL), F32),
        compiler_params=_cparams(1),
        name="outproj_mlp",
    )(x, o, wo, g, wup, wdn)


def _diff_lambda(lam_ref, lam_init):
    lp = lam_ref[...]
    a = jnp.sum(lp[0:1] * lp[1:2], axis=-1, keepdims=True)
    b = jnp.sum(lp[2:3] * lp[3:4], axis=-1, keepdims=True)
    return jnp.exp(a) - jnp.exp(b) + lam_init


def _diff_prompt_kernel(qT_ref, k_ref, vT_ref, bd_ref, bp_ref, kn_ref, bs_ref, lam_ref, sub_ref, o_ref,
                        m_ref, l_ref, acc_ref, *, tq, lam_init):
    qi = pl.program_id(1)
    qT = qT_ref[...]
    row = lax.broadcasted_iota(jnp.int32, qT.shape, 0)
    zero = jnp.zeros_like(qT)
    qm = (jnp.where(row < HEAD_DIM, qT, zero), jnp.where(row >= HEAD_DIM, qT, zero))

    def prev_bias(m, s):
        corner = s[tq - 128:, :128] + bp_ref[0, m]
        bottom = jnp.concatenate([corner, s[tq - 128:, 128:]], axis=1)
        return jnp.concatenate([s[:tq - 128], bottom], axis=0)

    def diag_bias(m, s):
        return s + bd_ref[0, m]

    def prev_diag_bias(m, s):
        return jnp.concatenate([prev_bias(m, s[:tq]), diag_bias(m, s[tq:])], axis=0)

    n_far = jnp.maximum(qi - 1, 0)

    def walk(tile_fn):
        def far_body(j, carry):
            tile_fn(j, None)
            return carry

        lax.fori_loop(0, n_far, far_body, 0)

        @pl.when(qi >= 1)
        def _():
            tile_fn(qi - 1, prev_bias)

        tile_fn(qi, diag_bias)

    span = []
    for m in range(2):
        qf = qm[m].astype(F32)
        qn = jnp.sqrt(jnp.sum(qf * qf, axis=0, keepdims=True))
        reach = qn * (jnp.sqrt(kn_ref[m:m + 1, :]) * 1.001)
        m_ref[m] = reach + bs_ref[m:m + 1, 0:1]
        span.append(jnp.max(2.0 * reach + bs_ref[m:m + 1, 1:2]))
    bound_is_tight = jnp.maximum(span[0], span[1]) <= SOFTMAX_BOUND_LIMIT

    @pl.when(jnp.logical_not(bound_is_tight))
    def _():
        m_ref[...] = jnp.full(m_ref.shape, NEG, F32)

        def max_tile(j, bias_fn):
            k = k_ref[j]
            for m in range(2):
                s = jnp.dot(k, qm[m], preferred_element_type=F32)
                if bias_fn is not None:
                    s = bias_fn(m, s)
                m_ref[m] = jnp.maximum(m_ref[m], jnp.max(s, axis=0, keepdims=True))

        walk(max_tile)

    l_ref[...] = jnp.zeros(l_ref.shape, F32)
    acc_ref[...] = jnp.zeros(acc_ref.shape, F32)

    def acc_tiles(js, bias_fn):
        k = jnp.concatenate([k_ref[j] for j in js], axis=0)
        vT = jnp.concatenate([vT_ref[j] for j in js], axis=1)
        nk = len(js) * tq
        for m in range(2):
            s = jnp.dot(k, qm[m], preferred_element_type=F32)
            if bias_fn is not None:
                s = bias_fn(m, s)
            p = jnp.exp2(s - m_ref[m])
            l_ref[m] += jnp.sum(p.reshape(nk // 8, 8, tq), axis=0)
            acc_ref[m] += jnp.dot(vT, p.astype(BF16), preferred_element_type=F32)

    def far_group(jj, carry):
        acc_tiles([DIFF_FAR_GROUP * jj + u for u in range(DIFF_FAR_GROUP)], None)
        return carry

    n_groups = n_far // DIFF_FAR_GROUP
    lax.fori_loop(0, n_groups, far_group, 0)
    start = n_groups * DIFF_FAR_GROUP
    left = n_far - start
    size = DIFF_FAR_GROUP // 2
    while size >= 1:
        @pl.when((left & size) != 0)
        def _(start=start, size=size):
            acc_tiles([start + u for u in range(size)], None)

        start = start + (left & size)
        size //= 2

    @pl.when(qi >= 1)
    def _():
        acc_tiles([qi - 1, qi], prev_diag_bias)

    @pl.when(qi == 0)
    def _():
        acc_tiles([qi], diag_bias)

    lam = _diff_lambda(lam_ref, lam_init)
    o0 = acc_ref[0] * (1.0 / jnp.sum(l_ref[0], axis=0, keepdims=True))
    o1 = acc_ref[1] * (1.0 / jnp.sum(l_ref[1], axis=0, keepdims=True))
    o = o0 - lam * o1
    ms = jnp.mean(o * o, axis=0, keepdims=True)
    o = o * lax.rsqrt(ms + EPS) * sub_ref[...] * (1.0 - lam_init)
    o_ref[...] = o.T.astype(BF16)


def _diff_prompt(qT, k16, vT, bias_d, bias_p, knorm2, bstats, lam_p, subln, lam_init):
    n_tok = k16.shape[0]
    tq = min(DIFF_TILE, n_tok)
    nb = n_tok // tq
    k3 = k16.reshape(nb, tq, D_MODEL)
    return pl.pallas_call(
        functools.partial(_diff_prompt_kernel, tq=tq, lam_init=lam_init),
        grid=(DIFF_HEADS, nb),
        in_specs=[
            pl.BlockSpec((2 * HEAD_DIM, tq), lambda h, i: (h, i)),
            pl.BlockSpec((nb, tq, 2 * HEAD_DIM), lambda h, i: (0, 0, h)),
            pl.BlockSpec((nb, 2 * HEAD_DIM, tq), lambda h, i: (0, h, 0)),
            pl.BlockSpec((1, 2, tq, tq), lambda h, i: (h, 0, 0, 0)),
            pl.BlockSpec((1, 2, 128, 128), lambda h, i: (h, 0, 0, 0)),
            pl.BlockSpec((None, 2, 1), lambda h, i: (h, 0, 0)),
            pl.BlockSpec((None, 2, 2), lambda h, i: (h, 0, 0)),
            pl.BlockSpec((4, HEAD_DIM), lambda h, i: (0, 0)),
            pl.BlockSpec((2 * HEAD_DIM, 1), lambda h, i: (0, 0)),
        ],
        out_specs=pl.BlockSpec((tq, 2 * HEAD_DIM), lambda h, i: (i, h)),
        out_shape=jax.ShapeDtypeStruct((n_tok, D_MODEL), BF16),
        scratch_shapes=[
            pltpu.VMEM((2, 1, tq), F32),
            pltpu.VMEM((2, 8, tq), F32),
            pltpu.VMEM((2, 2 * HEAD_DIM, tq), F32),
        ],
        compiler_params=_cparams(2),
        name="diff_prompt",
    )(qT, k3, vT, bias_d, bias_p, knorm2, bstats, lam_p, subln)


def _diff_sample_kernel(q_ref, ck_ref, cv_ref, kn_ref, vn_ref, b_ref, lam_ref, sub_ref, o_ref, *, lam_init, past):
    t = q_ref.shape[2]
    lane = lax.broadcasted_iota(jnp.int32, (t, 2 * HEAD_DIM), 1)
    lam = _diff_lambda(lam_ref, lam_init)
    for h in range(DIFF_HEADS):
        q = q_ref[0, h]
        zero = jnp.zeros_like(q)
        qq = jnp.concatenate([jnp.where(lane < HEAD_DIM, q, zero), jnp.where(lane >= HEAD_DIM, q, zero)], axis=0)
        ck = ck_ref[pl.ds(h, past, stride=DIFF_HEADS), :].astype(BF16)
        cv = cv_ref[pl.ds(h, past, stride=DIFF_HEADS), :].astype(BF16)
        kn = kn_ref[0, h]
        vn = vn_ref[0, h].astype(BF16)
        s_c = lax.dot_general(qq, ck, NT_DIMS, preferred_element_type=F32)
        s_n = lax.dot_general(qq, kn, NT_DIMS, preferred_element_type=F32)
        bias = b_ref[h].reshape(2 * t, 128 + t)
        s_far = s_c[:, :past - 128]
        s_near = s_c[:, past - 128:] + bias[:, :128]
        s_n = s_n + bias[:, 128:]
        m = jnp.maximum(jnp.maximum(jnp.max(s_far, axis=1, keepdims=True), jnp.max(s_near, axis=1, keepdims=True)),
                        jnp.max(s_n, axis=1, keepdims=True))
        p_far = jnp.exp2(s_far - m)
        p_near = jnp.exp2(s_near - m)
        p_n = jnp.exp2(s_n - m)
        l = (jnp.sum(p_far, axis=1, keepdims=True) + jnp.sum(p_near, axis=1, keepdims=True)
             + jnp.sum(p_n, axis=1, keepdims=True))
        o = (jnp.dot(p_far.astype(BF16), cv[:past - 128], preferred_element_type=F32)
             + jnp.dot(p_near.astype(BF16), cv[past - 128:], preferred_element_type=F32)
             + jnp.dot(p_n.astype(BF16), vn, preferred_element_type=F32))
        o = o * (1.0 / l)
        od = o[:t] - lam * o[t:]
        ms = jnp.mean(od * od, axis=-1, keepdims=True)
        od = od * lax.rsqrt(ms + EPS) * sub_ref[...] * (1.0 - lam_init)
        o_ref[0, h] = od.astype(BF16)


def _diff_sample(q_s, cache_k, cache_v, layer, kn, vn, bias_s, lam_p, subln_row, lam_init):
    nb, _, t, _ = q_s.shape
    past = cache_k.shape[2]
    rows = past * DIFF_HEADS
    ck = cache_k.reshape(cache_k.shape[0], nb, rows, 2 * HEAD_DIM)
    cv = cache_v.reshape(cache_v.shape[0], nb, rows, 2 * HEAD_DIM)
    cache_spec = pl.BlockSpec((None, None, rows, 2 * HEAD_DIM), lambda b: (layer, b, 0, 0))
    head_spec = pl.BlockSpec((1, DIFF_HEADS, t, 2 * HEAD_DIM), lambda b: (b, 0, 0, 0))
    return pl.pallas_call(
        functools.partial(_diff_sample_kernel, lam_init=lam_init, past=past),
        grid=(nb,),
        in_specs=[
            head_spec, cache_spec, cache_spec, head_spec, head_spec,
            pl.BlockSpec((DIFF_HEADS, 2, t, 128 + t), lambda b: (0, 0, 0, 0)),
            pl.BlockSpec((4, HEAD_DIM), lambda b: (0, 0)),
            pl.BlockSpec((1, 2 * HEAD_DIM), lambda b: (0, 0)),
        ],
        out_specs=head_spec,
        out_shape=jax.ShapeDtypeStruct((nb, DIFF_HEADS, t, 2 * HEAD_DIM), BF16),
        compiler_params=_cparams(1),
        name="diff_sample",
    )(q_s, ck, cv, kn, vn, bias_s, lam_p, subln_row)


def _swa_prompt_kernel(qT_ref, kc_ref, kp_ref, vc_ref, vp_ref, b_ref, sink_ref, o_ref):
    i = pl.program_id(0)
    qT = qT_ref[...]
    kp = kp_ref[...]
    kc = kc_ref[...]
    vp = vp_ref[0]
    vc = vc_ref[0]
    tq = qT.shape[1]
    key_row = lax.broadcasted_iota(jnp.int32, (2 * tq, SWA_GROUP * tq), 0)
    has_prev = key_row >= jnp.where(i > 0, 0, tq)
    zeros = jnp.zeros((HEAD_DIM, tq), BF16)
    outs = []
    for kv in range(SWA_KV_HEADS):
        pair = kv // 2
        k_band = jnp.concatenate([kp[:, pair * 128:(pair + 1) * 128], kc[:, pair * 128:(pair + 1) * 128]], axis=0)
        v_band = jnp.concatenate([vp[kv * HEAD_DIM:(kv + 1) * HEAD_DIM], vc[kv * HEAD_DIM:(kv + 1) * HEAD_DIM]],
                                 axis=1)
        qg = []
        for g in range(SWA_GROUP):
            h = kv * SWA_GROUP + g
            qh = qT[h * HEAD_DIM:(h + 1) * HEAD_DIM]
            qg.append(jnp.concatenate([qh, zeros] if kv % 2 == 0 else [zeros, qh], axis=0))
        q4 = jnp.concatenate(qg, axis=1)
        s = jnp.dot(k_band, q4, preferred_element_type=F32) + b_ref[kv]
        s = jnp.where(has_prev, s, NEG)
        sink = sink_ref[kv]
        m = jnp.maximum(jnp.max(s, axis=0, keepdims=True), sink)
        p = jnp.exp2(s - m)
        den = jnp.sum(p, axis=0, keepdims=True) + jnp.exp2(sink - m)
        oT4 = jnp.dot(v_band, p.astype(BF16), preferred_element_type=F32) * (1.0 / den)
        for g in range(SWA_GROUP):
            outs.append(oT4[:, g * tq:(g + 1) * tq])
    o_ref[...] = jnp.concatenate(outs, axis=0).T.astype(BF16)


def _swa_prompt(qT, k16, vT, bias, sinks):
    n_tok = k16.shape[0]
    tq = SWA_TILE
    nkv = SWA_KV_HEADS * HEAD_DIM
    return pl.pallas_call(
        _swa_prompt_kernel,
        grid=(n_tok // tq,),
        in_specs=[
            pl.BlockSpec((D_MODEL, tq), lambda i: (0, i)),
            pl.BlockSpec((tq, nkv), lambda i: (i, 0)),
            pl.BlockSpec((tq, nkv), lambda i: (jnp.maximum(i - 1, 0), 0)),
            pl.BlockSpec((1, nkv, tq), lambda i: (i, 0, 0)),
            pl.BlockSpec((1, nkv, tq), lambda i: (jnp.maximum(i - 1, 0), 0, 0)),
            _resident((SWA_KV_HEADS, 2 * tq, SWA_GROUP * tq), lambda i: (0, 0, 0)),
            _resident((SWA_KV_HEADS, 1, SWA_GROUP * tq), lambda i: (0, 0, 0)),
        ],
        out_specs=pl.BlockSpec((tq, D_MODEL), lambda i: (i, 0)),
        out_shape=jax.ShapeDtypeStruct((n_tok, D_MODEL), BF16),
        compiler_params=_cparams(1),
        name="swa_prompt",
    )(qT, k16, k16, vT, vT, bias, sinks)


def _swa_sample_kernel(q_ref, kc_ref, vc_ref, kn_ref, vn_ref, b_ref, sink_ref, o_ref, ko_ref, vo_ref):
    t = kn_ref.shape[2]
    buf = kc_ref.shape[2]
    for kv in range(SWA_KV_HEADS):
        kc32 = kc_ref[0, kv]
        vc32 = vc_ref[0, kv]
        kn32 = kn_ref[0, kv]
        vn32 = vn_ref[0, kv]
        ko_ref[0, kv, 0:buf - t, :] = kc32[t:]
        ko_ref[0, kv, buf - t:buf, :] = kn32
        vo_ref[0, kv, 0:buf - t, :] = vc32[t:]
        vo_ref[0, kv, buf - t:buf, :] = vn32
        kc = kc32.astype(BF16)
        vc = vc32.astype(BF16)
        kn = kn32.astype(BF16)
        vn = vn32.astype(BF16)
        for g in range(SWA_GROUP):
            h = kv * SWA_GROUP + g
            q = q_ref[0, h]
            b = b_ref[h]
            s_c = lax.dot_general(q, kc, NT_DIMS, preferred_element_type=F32) + b[:, :buf]
            s_n = lax.dot_general(q, kn, NT_DIMS, preferred_element_type=F32) + b[:, buf:]
            sink = sink_ref[h]
            m = jnp.maximum(jnp.maximum(jnp.max(s_c, axis=1, keepdims=True), jnp.max(s_n, axis=1, keepdims=True)),
                            sink)
            p_c = jnp.exp2(s_c - m)
            p_n = jnp.exp2(s_n - m)
            den = (jnp.sum(p_c, axis=1, keepdims=True) + jnp.sum(p_n, axis=1, keepdims=True)
                   + jnp.exp2(sink - m))
            o = (jnp.dot(p_c.astype(BF16), vc, preferred_element_type=F32)
                 + jnp.dot(p_n.astype(BF16), vn, preferred_element_type=F32)) * (1.0 / den)
            o_ref[0, h] = o.astype(BF16)


def _swa_sample(q_s, kc, vc, kn, vn, bias, sinks):
    nb, _, t, _ = q_s.shape
    buf = kc.shape[2]
    kvspec = pl.BlockSpec((1, SWA_KV_HEADS, buf, HEAD_DIM), lambda b: (b, 0, 0, 0))
    nspec = pl.BlockSpec((1, SWA_KV_HEADS, t, HEAD_DIM), lambda b: (b, 0, 0, 0))
    return pl.pallas_call(
        _swa_sample_kernel,
        grid=(nb,),
        in_specs=[
            pl.BlockSpec((1, SWA_Q_HEADS, t, HEAD_DIM), lambda b: (b, 0, 0, 0)),
            kvspec, kvspec, nspec, nspec,
            pl.BlockSpec((SWA_Q_HEADS, t, buf + t), lambda b: (0, 0, 0)),
            pl.BlockSpec((SWA_Q_HEADS, 1, 1), lambda b: (0, 0, 0)),
        ],
        out_specs=[
            pl.BlockSpec((1, SWA_Q_HEADS, t, HEAD_DIM), lambda b: (b, 0, 0, 0)),
            kvspec, kvspec,
        ],
        out_shape=[
            jax.ShapeDtypeStruct((nb, SWA_Q_HEADS, t, HEAD_DIM), BF16),
            jax.ShapeDtypeStruct((nb, SWA_KV_HEADS, buf, HEAD_DIM), F32),
            jax.ShapeDtypeStruct((nb, SWA_KV_HEADS, buf, HEAD_DIM), F32),
        ],
        compiler_params=_cparams(1),
        name="swa_sample",
    )(q_s, kc, vc, kn, vn, bias, sinks)


def _softplus(z):
    return jnp.maximum(z, 0.0) + jnp.log2(1.0 + jnp.exp2(-jnp.abs(z)))


def _split_bf16(x):
    hi = x.astype(BF16)
    lo = (x - hi.astype(F32)).astype(BF16)
    return hi, lo


def _sb_prompt_kernel(qT_ref, k_ref, vT_ref, kmax_ref, o_ref, carry_ref, acc_ref, *, tq, pairs):
    qi = pl.program_id(1)
    width = 2 * pairs * tq
    row = lax.broadcasted_iota(jnp.int32, (2 * HEAD_DIM, tq), 0)
    qpairs, zbs = [], []
    for pr in range(pairs):
        qT = qT_ref[2 * HEAD_DIM * pr:2 * HEAD_DIM * (pr + 1), :]
        zero = jnp.zeros_like(qT)
        qp = jnp.concatenate([jnp.where(row < HEAD_DIM, qT, zero), jnp.where(row >= HEAD_DIM, qT, zero)], axis=1)
        qpairs.append(qp)
        kmax = kmax_ref[2 * HEAD_DIM * pr:2 * HEAD_DIM * (pr + 1), :]
        zbs.append(jnp.sum(jnp.abs(qp.astype(F32)) * kmax, axis=0, keepdims=True))
    zb = jnp.concatenate(zbs, axis=1)
    kr = lax.broadcasted_iota(jnp.int32, (tq, tq), 0)
    kc = lax.broadcasted_iota(jnp.int32, (tq, tq), 1)
    tri = jnp.where(kc >= kr, 1.0, 0.0).astype(BF16)
    krw = lax.broadcasted_iota(jnp.int32, (tq, width), 0)
    kcw = lax.broadcasted_iota(jnp.int32, (tq, width), 1)
    causal = krw < lax.rem(kcw, tq)
    carry_ref[...] = jnp.zeros(carry_ref.shape, F32)
    acc_ref[...] = jnp.zeros(acc_ref.shape, F32)

    def tiles(js, first_is_diag):
        zs, cs = [], []
        for n, j in enumerate(js):
            k = k_ref[j]
            z = jnp.concatenate([jnp.dot(k[:, 2 * HEAD_DIM * pr:2 * HEAD_DIM * (pr + 1)], qpairs[pr],
                                         preferred_element_type=F32) for pr in range(pairs)], axis=1)
            lsp = _softplus(z)
            if first_is_diag and n == 0:
                lsp = jnp.where(causal, lsp, 0.0)
            hi, lo = _split_bf16(lsp)
            zs.append(z)
            cs.append(jnp.dot(tri, hi, preferred_element_type=F32) + jnp.dot(tri, lo, preferred_element_type=F32))
        carry = carry_ref[...]
        weights = []
        for n in range(len(js)):
            a = jnp.exp2(zs[n] - cs[n] - carry)
            if first_is_diag and n == 0:
                a = jnp.where(causal, a, 0.0)
            weights.append(a.astype(BF16))
            carry = carry + cs[n][0:1]
        carry_ref[...] = carry
        for pr in range(pairs):
            vT = jnp.concatenate([vT_ref[j, 2 * HEAD_DIM * pr:2 * HEAD_DIM * (pr + 1), :] for j in js],
                                 axis=1)
            w = jnp.concatenate([a[:, 2 * tq * pr:2 * tq * (pr + 1)] for a in weights], axis=0)
            acc_ref[pr] += jnp.dot(vT, w, preferred_element_type=F32)

    def still_live():
        return (jnp.min(carry_ref[...] - zb) < SB_SKIP_MARGIN).astype(jnp.int32)

    @pl.when(qi == 0)
    def _():
        tiles([qi], True)

    @pl.when(qi == 1)
    def _():
        tiles([qi, qi - 1], True)

    @pl.when(qi >= 2)
    def _():
        tiles([qi, qi - 1, qi - 2], True)

    def cond(state):
        j, live = state
        return jnp.logical_and(j >= 0, live > 0)

    def body(state):
        j, _ = state
        tiles([j], False)
        return j - 1, still_live()

    lax.while_loop(cond, body, (qi - 3, still_live()))
    outs = []
    for pr in range(pairs):
        outs += [acc_ref[pr, 0:HEAD_DIM, 0:tq], acc_ref[pr, HEAD_DIM:2 * HEAD_DIM, tq:2 * tq]]
    o_ref[...] = jnp.concatenate(outs, axis=0).T.astype(BF16)


def _sb_prompt(qT, k16, vT, kmax):
    n_tok = k16.shape[0]
    tq = min(SB_TILE, n_tok)
    nb = n_tok // tq
    k3 = k16.reshape(nb, tq, D_MODEL)
    pairs = SB_PAIRS_PER_STEP
    wide = 2 * HEAD_DIM * pairs
    return pl.pallas_call(
        functools.partial(_sb_prompt_kernel, tq=tq, pairs=pairs),
        grid=(SB_HEADS // (2 * pairs), nb),
        in_specs=[
            pl.BlockSpec((wide, tq), lambda p, i: (p, i)),
            pl.BlockSpec((nb, tq, wide), lambda p, i: (0, 0, p)),
            pl.BlockSpec((nb, wide, tq), lambda p, i: (0, p, 0)),
            pl.BlockSpec((wide, 1), lambda p, i: (p, 0)),
        ],
        out_specs=pl.BlockSpec((tq, wide), lambda p, i: (i, p)),
        out_shape=jax.ShapeDtypeStruct((n_tok, D_MODEL), BF16),
        scratch_shapes=[
            pltpu.VMEM((1, 2 * pairs * tq), F32),
            pltpu.VMEM((pairs, 2 * HEAD_DIM, 2 * tq), F32),
        ],
        compiler_params=_cparams(2),
        name="sb_prompt",
    )(qT, k3, vT, kmax)


def _tri_lanes(n):
    r = lax.broadcasted_iota(jnp.int32, (n, n), 0)
    c = lax.broadcasted_iota(jnp.int32, (n, n), 1)
    return jnp.where(r >= c, 1.0, 0.0).astype(BF16)


def _sb_sample_kernel(q_ref, ck_ref, cv_ref, kn_ref, vn_ref, o_ref, carry_ref, acc_ref, *, ch, kb):
    c = pl.program_id(1)
    nh = q_ref.shape[1]
    t = q_ref.shape[2]

    @pl.when(c == 0)
    def _():
        z = jnp.concatenate([lax.dot_general(q_ref[0, h], kn_ref[0, h], NT_DIMS, preferred_element_type=F32)
                             for h in range(nh)], axis=0)
        r = lax.broadcasted_iota(jnp.int32, (nh * t, t), 0)
        kc = lax.broadcasted_iota(jnp.int32, (nh * t, t), 1)
        causal = kc < lax.rem(r, t)
        hi, lo = _split_bf16(jnp.where(causal, _softplus(z), 0.0))
        tri = _tri_lanes(t)
        cs = jnp.dot(hi, tri, preferred_element_type=F32) + jnp.dot(lo, tri, preferred_element_type=F32)
        a = jnp.where(causal, jnp.exp2(z - cs), 0.0).astype(BF16)
        carry_ref[...] = cs[:, 0:1]
        for h in range(nh):
            acc_ref[h * t:(h + 1) * t, :] = jnp.dot(a[h * t:(h + 1) * t], vn_ref[0, h].astype(BF16),
                                                    preferred_element_type=F32)

    z = jnp.concatenate([jnp.dot(q_ref[0, h], ck_ref[h].astype(BF16), preferred_element_type=F32)
                         for h in range(nh)], axis=0)
    nblk = ch // kb
    rows = nh * t
    zst = jnp.concatenate([z[:, b * kb:(b + 1) * kb] for b in range(nblk)], axis=0)
    hi, lo = _split_bf16(_softplus(zst))
    tri = _tri_lanes(kb)
    cs = jnp.dot(hi, tri, preferred_element_type=F32) + jnp.dot(lo, tri, preferred_element_type=F32)
    run = carry_ref[...]
    carries = [None] * nblk
    for b in reversed(range(nblk)):
        carries[b] = run
        run = run + cs[b * rows:(b + 1) * rows, 0:1]
    carry_ref[...] = run
    a = jnp.exp2(zst - cs - jnp.concatenate(carries, axis=0))
    a = jnp.concatenate([a[b * rows:(b + 1) * rows] for b in range(nblk)], axis=1).astype(BF16)
    for h in range(nh):
        acc_ref[h * t:(h + 1) * t, :] += lax.dot_general(a[h * t:(h + 1) * t], cv_ref[h].astype(BF16), NT_DIMS,
                                                         preferred_element_type=F32)

    @pl.when(c == pl.num_programs(1) - 1)
    def _():
        for h in range(nh):
            o_ref[0, h] = acc_ref[h * t:(h + 1) * t, :].astype(BF16)


def _sb_sample(q_s, cache_k, cache_v, layer, kn, vn):
    nb, nh, t, _ = q_s.shape
    past = cache_k.shape[2]
    ch = min(SB_SAMPLE_CHUNK, past)
    nch = past // ch
    ck = cache_k.transpose(0, 1, 3, 4, 2)
    cv = cache_v.transpose(0, 1, 3, 4, 2)
    cache_spec = pl.BlockSpec((None, None, nh, HEAD_DIM, ch), lambda b, c: (layer, b, 0, 0, nch - 1 - c))
    head_spec = pl.BlockSpec((1, nh, t, HEAD_DIM), lambda b, c: (b, 0, 0, 0))
    return pl.pallas_call(
        functools.partial(_sb_sample_kernel, ch=ch, kb=min(256, ch)),
        grid=(nb, nch),
        in_specs=[head_spec, cache_spec, cache_spec, head_spec, head_spec],
        out_specs=head_spec,
        out_shape=jax.ShapeDtypeStruct((nb, nh, t, HEAD_DIM), BF16),
        scratch_shapes=[
            pltpu.VMEM((nh * t, 1), F32),
            pltpu.VMEM((nh * t, HEAD_DIM), F32),
        ],
        compiler_params=_cparams(2),
        name="sb_sample",
    )(q_s, ck, cv, kn, vn)


def _tile_col(gain, n, scale):
    return (jnp.tile(gain.astype(F32), n // gain.shape[0]) * scale).reshape(n, 1)


def _heads_major(x, nb, t, nh, hd):
    return x.reshape(nb, t, nh, hd).transpose(0, 2, 1, 3)


def _diff_bias_tiles(rel_bias, tq, t):
    j = jnp.arange(tq)
    valid_d = (j // CHUNK)[:, None] <= (j // CHUNK)[None, :]
    bias_d = _bias_tile(rel_bias, tq, tq, 0, valid_d, True).reshape(DIFF_HEADS, 2, tq, tq)
    bias_p = _bias_tile(rel_bias, 128, 128, -128, None, True).reshape(DIFF_HEADS, 2, 128, 128)
    bstats = _bias_stats(rel_bias).reshape(DIFF_HEADS, 2, 2)
    bias_s = jnp.swapaxes(_bias_tile(rel_bias, 128 + t, t, -128, None, True), 1, 2)
    bias_s = bias_s.reshape(DIFF_HEADS, 2, t, 128 + t)
    return bias_d, bias_p, bstats, bias_s


def _diff_layer(xp, xs, cache_k, cache_v, layer, n_layers, earlier, bias_tiles, g, w_qkv, q_gain, k_gain, lam_p,
                subln, lam_init):
    bias_d, bias_p, bstats, bias_s = bias_tiles
    n_p = xp.shape[0]
    nb = cache_k.shape[1]
    t = xs.shape[0] // nb
    wt = w_qkv.T.astype(BF16)
    qg = _tile_col(q_gain, D_MODEL, HEAD_DIM ** -0.5 * LOG2E)
    kg = _tile_col(k_gain, D_MODEL, 1.0)
    g = g.reshape(1, D_MODEL)
    kw = dict(nq=D_MODEL, nk=D_MODEL, nv=D_MODEL, head_norm=True)
    tq = min(DIFF_TILE, n_p)
    qT, k32, k16, v32, vT, knorm2 = _project(xp, g, wt, qg, kg, vb=tq, kstat="normsq", slot=layer,
                                             n_slots=n_layers, earlier=earlier, **kw)
    qT_s, k32_s, k16_s, v32_s, _ = _project(xs, g, wt, qg, kg, vb=128, **kw)
    k32_s, v32_s = k32_s[0], v32_s[0]

    o_p = _diff_prompt(qT, k16, vT, bias_d, bias_p, knorm2.reshape(DIFF_HEADS, 2, 1), bstats, lam_p,
                       subln.reshape(2 * HEAD_DIM, 1), lam_init)
    hm = functools.partial(_heads_major, nb=nb, t=t, nh=DIFF_HEADS, hd=2 * HEAD_DIM)
    o_s = _diff_sample(hm(qT_s.T), cache_k, cache_v, layer, hm(k16_s), hm(v32_s), bias_s, lam_p,
                       subln.reshape(1, 2 * HEAD_DIM), lam_init)
    o_s = o_s.transpose(0, 2, 1, 3).reshape(nb * t, D_MODEL)
    return (o_p, o_s,
            k32, v32,
            k32_s.reshape(nb, t, DIFF_HEADS, 2 * HEAD_DIM), v32_s.reshape(nb, t, DIFF_HEADS, 2 * HEAD_DIM))


def _swa_layer(xp, xs, ck, cv, rel_bias, g, w_qkv, q_gain, k_gain, sinks):
    n_p = xp.shape[0]
    nb, buf = ck.shape[0], ck.shape[1]
    t = xs.shape[0] // nb
    nkv = SWA_KV_HEADS * HEAD_DIM
    wt = w_qkv.T.astype(BF16)
    qg = _tile_col(q_gain, D_MODEL, HEAD_DIM ** -0.5 * LOG2E)
    kg = _tile_col(k_gain, nkv, 1.0)
    g = g.reshape(1, D_MODEL)
    kw = dict(nq=D_MODEL, nk=nkv, nv=nkv, vb=SWA_TILE, head_norm=True)
    qT, k32, k16, v32, vT = _project(xp, g, wt, qg, kg, **kw)
    qT_s, k32_s, _, v32_s, _ = _project(xs, g, wt, qg, kg, **kw)
    k32, v32, k32_s, v32_s = k32[0], v32[0], k32_s[0], v32_s[0]
    sink_col = (sinks.astype(F32) * LOG2E).reshape(SWA_Q_HEADS, 1, 1)
    wchunks = WINDOW // CHUNK

    tq = SWA_TILE
    kchunk = jnp.floor_divide(jnp.arange(2 * tq) - tq, CHUNK)[:, None]
    qchunk = (jnp.arange(tq) // CHUNK)[None, :]
    valid = (kchunk <= qchunk) & (kchunk >= qchunk - wchunks)
    bias = _bias_tile(rel_bias, 2 * tq, tq, -tq, valid, False)
    bias_g = bias.reshape(SWA_KV_HEADS, SWA_GROUP, 2 * tq, tq).transpose(0, 2, 1, 3)
    bias_g = bias_g.reshape(SWA_KV_HEADS, 2 * tq, SWA_GROUP * tq)
    sink_g = jnp.repeat(sink_col.reshape(SWA_KV_HEADS, 1, SWA_GROUP), tq, axis=2)
    o_p = _swa_prompt(qT, k16, vT, bias_g, sink_g)

    sc = jnp.floor_divide(jnp.arange(buf + t) - buf, CHUNK)[:, None]
    tc = (jnp.arange(t) // CHUNK)[None, :]
    valid_s = (sc <= tc) & (sc >= tc - wchunks)
    bias_s = jnp.swapaxes(_bias_tile(rel_bias, buf + t, t, -buf, valid_s, False), 1, 2)
    q_s = _heads_major(qT_s.T, nb, t, SWA_Q_HEADS, HEAD_DIM)
    kn = _heads_major(k32_s, nb, t, SWA_KV_HEADS, HEAD_DIM)
    vn = _heads_major(v32_s, nb, t, SWA_KV_HEADS, HEAD_DIM)
    o_s, ko, vo = _swa_sample(q_s, ck.transpose(0, 2, 1, 3), cv.transpose(0, 2, 1, 3), kn, vn, bias_s, sink_col)
    o_s = o_s.transpose(0, 2, 1, 3).reshape(nb * t, D_MODEL)
    wbuf = min(WINDOW, n_p)
    return (o_p, o_s,
            k32[n_p - wbuf:].reshape(1, wbuf, SWA_KV_HEADS, HEAD_DIM),
            v32[n_p - wbuf:].reshape(1, wbuf, SWA_KV_HEADS, HEAD_DIM),
            ko.transpose(0, 2, 1, 3), vo.transpose(0, 2, 1, 3))


def _sb_layer(xp, xs, cache_k, cache_v, layer, n_layers, earlier, g, w_qkv):
    n_p = xp.shape[0]
    nb = cache_k.shape[1]
    t = xs.shape[0] // nb
    wt = w_qkv.T.astype(BF16)
    qg = jnp.full((D_MODEL, 1), HEAD_DIM ** -0.5 * LOG2E, F32)
    kg = jnp.ones((D_MODEL, 1), F32)
    g = g.reshape(1, D_MODEL)
    kw = dict(nq=D_MODEL, nk=D_MODEL, nv=D_MODEL, head_norm=False)
    tq = min(SB_TILE, n_p)
    qT, k32, k16, v32, vT, kmax = _project(xp, g, wt, qg, kg, vb=tq, kstat="absmax", slot=layer,
                                           n_slots=n_layers, earlier=earlier, **kw)
    qT_s, k32_s, k16_s, v32_s, _ = _project(xs, g, wt, qg, kg, vb=128, **kw)
    k32_s, v32_s = k32_s[0], v32_s[0]
    o_p = _sb_prompt(qT, k16, vT, kmax)
    hm = functools.partial(_heads_major, nb=nb, t=t, nh=SB_HEADS, hd=HEAD_DIM)
    o_s = _sb_sample(hm(qT_s.T), cache_k, cache_v, layer, hm(k16_s), hm(v32_s))
    o_s = o_s.transpose(0, 2, 1, 3).reshape(nb * t, D_MODEL)
    return (o_p, o_s,
            k32, v32,
            k32_s.reshape(nb, t, SB_HEADS, HEAD_DIM), v32_s.reshape(nb, t, SB_HEADS, HEAD_DIM))


def kernel(x_prompt, x_sample, cache_diff_k, cache_diff_v, cache_swa_k, cache_swa_v, cache_sb_k, cache_sb_v, rel_bias, norm_mix, norm_mlp, w_up, w_down, diff_w_qkv, diff_w_o, diff_q_norm, diff_k_norm, diff_lambda, diff_subln, swa_w_qkv, swa_w_o, swa_q_norm, swa_k_norm, swa_sinks, sb_w_qkv, sb_w_o):
    bp, n_p, _ = x_prompt.shape
    assert bp == 1
    nb, t, _ = x_sample.shape
    depth = norm_mix.shape[0]
    xp = x_prompt.reshape(n_p, D_MODEL)
    xs = x_sample.reshape(nb * t, D_MODEL)
    n_diff, n_sb = cache_diff_k.shape[0], cache_sb_k.shape[0]
    outs = {name: [] for name in ("pwk", "pwv", "sdk", "sdv", "swk", "swv", "sbk", "sbv")}
    diff_kv = sb_kv = None
    diff_bias = _diff_bias_tiles(rel_bias, min(DIFF_TILE, n_p), t)
    for i in range(depth):
        j = i // N_MIXERS
        if i % N_MIXERS == 0:
            lam_init = 0.8 - 0.6 * math.exp(-0.3 * i)
            o_p, o_s, kp, vp, kn, vn = _diff_layer(
                xp, xs, cache_diff_k, cache_diff_v, j, n_diff, diff_kv, diff_bias, norm_mix[i], diff_w_qkv[j],
                diff_q_norm[j], diff_k_norm[j], diff_lambda[j], diff_subln[j], lam_init)
            diff_kv = (kp, vp)
            w_o = diff_w_o[j]
            names = ("sdk", "sdv")
        elif i % N_MIXERS == 1:
            o_p, o_s, kp, vp, kn, vn = _swa_layer(
                xp, xs, cache_swa_k[j], cache_swa_v[j], rel_bias, norm_mix[i], swa_w_qkv[j],
                swa_q_norm[j], swa_k_norm[j], swa_sinks[j])
            outs["pwk"].append(kp)
            outs["pwv"].append(vp)
            w_o = swa_w_o[j]
            names = ("swk", "swv")
        else:
            o_p, o_s, kp, vp, kn, vn = _sb_layer(xp, xs, cache_sb_k, cache_sb_v, j, n_sb, sb_kv, norm_mix[i],
                                                 sb_w_qkv[j])
            sb_kv = (kp, vp)
            w_o = sb_w_o[j]
            names = ("sbk", "sbv")
        for name, val in zip(names, (kn, vn)):
            outs[name].append(val)
        wo16 = w_o.astype(BF16)
        g_mlp = norm_mlp[i].reshape(1, D_MODEL)
        wup16 = w_up[i].astype(BF16)
        wdn16 = w_down[i].astype(BF16)
        xp = _outproj_mlp(xp, o_p, wo16, g_mlp, wup16, wdn16)
        xs = _outproj_mlp(xs, o_s, wo16, g_mlp, wup16, wdn16)
    st = {name: jnp.stack(v) for name, v in outs.items()}
    diff_shape = (n_diff, 1, n_p, DIFF_HEADS, 2 * HEAD_DIM)
    sb_shape = (n_sb, 1, n_p, SB_HEADS, HEAD_DIM)
    return (xp.reshape(1, n_p, D_MODEL), xs.reshape(nb, t, D_MODEL),
            diff_kv[0].reshape(diff_shape), diff_kv[1].reshape(diff_shape), st["pwk"], st["pwv"],
            sb_kv[0].reshape(sb_shape), sb_kv[1].reshape(sb_shape),
            st["sdk"], st["sdv"], st["swk"], st["swv"], st["sbk"], st["sbv"])
```

```python
import functools
import math

import jax
import jax.numpy as jnp
from jax import lax
from jax.experimental import pallas as pl
from jax.experimental.pallas import tpu as pltpu

F32 = jnp.float32
BF16 = jnp.bfloat16

D_MODEL = 1024
HEAD_DIM = 64
CHUNK = 64
N_MIXERS = 3
DIFF_HEADS = 8
SWA_Q_HEADS = 16
SWA_KV_HEADS = 4
SWA_GROUP = 4
WINDOW = 128
SB_HEADS = 16
N_BUCKETS = 32
MAX_DISTANCE = 128
D_FF = 4 * D_MODEL
EPS = 1e-6
LOG2E = 1.4426950408889634
NEG = -1e30
FAR_BUCKET = 15
SB_SKIP_MARGIN = 145.0
SOFTMAX_BOUND_LIMIT = 100.0

VMEM_LIMIT = 56 * 1024 * 1024
TOKEN_TILE = 512
PROJ_CHUNK = 512
DIFF_TILE = 512
DIFF_FAR_GROUP = 8
SB_TILE = 256
SB_PAIRS_PER_STEP = 2
SB_SAMPLE_CHUNK = 512
SWA_TILE = 128

NT_DIMS = (((1,), (1,)), ((), ()))


def _cparams(n_axes):
    return pltpu.CompilerParams(dimension_semantics=("arbitrary",) * n_axes,
                                vmem_limit_bytes=VMEM_LIMIT)


def _resident(block_shape, index_map):
    return pl.BlockSpec(block_shape, index_map, pipeline_mode=pl.Buffered(1))


def _t5_bucket(rel):
    half = N_BUCKETS // 2
    exact = half // 2
    ret = jnp.where(rel > 0, half, 0).astype(jnp.int32)
    n = jnp.abs(rel)
    nf = jnp.maximum(n, 1).astype(F32)
    large = exact + (jnp.log(nf / exact) / math.log(MAX_DISTANCE / exact) * (half - exact)).astype(jnp.int32)
    large = jnp.minimum(large, half - 1)
    return ret + jnp.where(n < exact, n, large)


def _bias_tile(rel_bias, n_j, n_i, r0, valid, shift_far):
    tab = rel_bias.astype(F32)
    period = -(-(n_i + n_j - 1) // 128) * 128
    u = jnp.arange(period)
    rel = jnp.where(u < n_i, r0 - u, r0 + (period - u))
    vals = jnp.moveaxis(tab[_t5_bucket(rel)], -1, 0)
    if shift_far:
        vals = vals - tab[FAR_BUCKET][:, None]
    vals = (vals * LOG2E).reshape(vals.shape[0], 1, period)

    def expand(y_ref, o_ref):
        rows = jnp.broadcast_to(y_ref[0], (n_j, period))
        o_ref[0] = pltpu.roll(rows, 0, 1, stride=1, stride_axis=0)[:, :n_i]

    b = pl.pallas_call(
        expand,
        grid=(vals.shape[0],),
        in_specs=[pl.BlockSpec((1, 1, period), lambda c: (c, 0, 0))],
        out_specs=pl.BlockSpec((1, n_j, n_i), lambda c: (c, 0, 0)),
        out_shape=jax.ShapeDtypeStruct((vals.shape[0], n_j, n_i), F32),
        compiler_params=_cparams(1),
        name="bias_toeplitz",
    )(vals)
    if valid is not None:
        b = jnp.where(valid[None], b, NEG)
    return b


def _bias_stats(rel_bias):
    tab = rel_bias.astype(F32)
    sh = (tab - tab[FAR_BUCKET][None, :]) * LOG2E
    bmax = jnp.max(sh, axis=0)
    return jnp.stack([bmax, bmax - jnp.min(sh, axis=0)], axis=-1)


def _proj_kernel(x_ref, g_ref, wt_ref, qg_ref, kg_ref, *refs, nq, nk, nv, vb, head_norm, kstat, n_alias,
                 slot, all_slots):
    out_refs = refs[n_alias:]
    qT_ref, k32_ref, k16_ref, v32_ref, vT_ref = out_refs[:5]
    if all_slots > 1:
        for s in range(all_slots):
            if s != slot:
                k32_ref[s] = jnp.zeros(k32_ref.shape[1:], F32)
                v32_ref[s] = jnp.zeros(v32_ref.shape[1:], F32)
        k32_ref = k32_ref.at[slot]
        v32_ref = v32_ref.at[slot]
    x = x_ref[...]
    ms = jnp.mean(x * x, axis=-1, keepdims=True)
    h = (x * lax.rsqrt(ms + EPS) * g_ref[...]).astype(BF16)
    tm = x.shape[0]

    def rows(r0, n):
        return lax.dot_general(wt_ref[r0:r0 + n, :], h, NT_DIMS, preferred_element_type=F32)

    def headnorm(t, gcol):
        n = t.shape[0]
        t3 = t.reshape(n // HEAD_DIM, HEAD_DIM, tm)
        r = lax.rsqrt(jnp.mean(t3 * t3, axis=1, keepdims=True) + EPS)
        return (t3 * r).reshape(n, tm) * gcol

    for r0 in range(0, nq, PROJ_CHUNK):
        n = min(PROJ_CHUNK, nq - r0)
        qt = rows(r0, n)
        qt = headnorm(qt, qg_ref[r0:r0 + n, :]) if head_norm else qt * qg_ref[r0:r0 + n, :]
        qT_ref[r0:r0 + n, :] = qt.astype(BF16)

    for r0 in range(0, nk, PROJ_CHUNK):
        n = min(PROJ_CHUNK, nk - r0)
        kt = rows(nq + r0, n)
        if head_norm:
            kt = headnorm(kt, kg_ref[r0:r0 + n, :])
        k = kt.T
        k32_ref[:, r0:r0 + n] = k
        k16_ref[:, r0:r0 + n] = k.astype(BF16)
        if kstat is not None:
            stat_ref = out_refs[5]
            kr = kt.astype(BF16).astype(F32)
            if kstat == "absmax":
                stat = jnp.max(jnp.abs(kr), axis=1, keepdims=True)
                srows = slice(r0, r0 + n)
            else:
                k3 = kr.reshape(n // HEAD_DIM, HEAD_DIM, tm)
                stat = jnp.max(jnp.sum(k3 * k3, axis=1), axis=1, keepdims=True)
                srows = slice(r0 // HEAD_DIM, (r0 + n) // HEAD_DIM)

            @pl.when(pl.program_id(0) == 0)
            def _(stat=stat, srows=srows):
                stat_ref[srows, :] = stat

            @pl.when(pl.program_id(0) > 0)
            def _(stat=stat, srows=srows):
                stat_ref[srows, :] = jnp.maximum(stat_ref[srows, :], stat)

    for r0 in range(0, nv, PROJ_CHUNK):
        n = min(PROJ_CHUNK, nv - r0)
        vt = rows(nq + nk + r0, n)
        v32_ref[:, r0:r0 + n] = vt.T
        vt16 = vt.astype(BF16)
        for b in range(tm // vb):
            vT_ref[b, r0:r0 + n, :] = vt16[:, b * vb:(b + 1) * vb]


def _project(x, g, wt, qg, kg, *, nq, nk, nv, vb, head_norm, kstat=None, slot=0, n_slots=1, earlier=None):
    n_tok = x.shape[0]
    tm = min(TOKEN_TILE, n_tok)
    vb = min(vb, tm)
    grid = (n_tok // tm,)
    ntot = nq + nk + nv
    out_shape = [
        jax.ShapeDtypeStruct((nq, n_tok), BF16),
        jax.ShapeDtypeStruct((n_slots, n_tok, nk), F32),
        jax.ShapeDtypeStruct((n_tok, nk), BF16),
        jax.ShapeDtypeStruct((n_slots, n_tok, nv), F32),
        jax.ShapeDtypeStruct((n_tok // vb, nv, vb), BF16),
    ]
    all_slots = n_slots if (earlier is None and n_slots > 1) else 1

    def kv_spec(width):
        if all_slots > 1:
            return pl.BlockSpec((n_slots, tm, width), lambda i: (0, i, 0))
        return pl.BlockSpec((None, tm, width), lambda i: (slot, i, 0))

    out_specs = [
        pl.BlockSpec((nq, tm), lambda i: (0, i)),
        kv_spec(nk),
        pl.BlockSpec((tm, nk), lambda i: (i, 0)),
        kv_spec(nv),
        pl.BlockSpec((tm // vb, nv, vb), lambda i: (i, 0, 0)),
    ]
    if kstat is not None:
        n_stat = nk if kstat == "absmax" else nk // HEAD_DIM
        out_shape.append(jax.ShapeDtypeStruct((n_stat, 1), F32))
        out_specs.append(pl.BlockSpec((n_stat, 1), lambda i: (0, 0)))
    in_specs = [
        pl.BlockSpec((tm, D_MODEL), lambda i: (i, 0)),
        _resident((1, D_MODEL), lambda i: (0, 0)),
        _resident((ntot, D_MODEL), lambda i: (0, 0)),
        _resident((nq, 1), lambda i: (0, 0)),
        _resident((nk, 1), lambda i: (0, 0)),
    ]
    args = [x, g, wt, qg, kg]
    aliases = {}
    if earlier is not None:
        aliases = {len(args): 1, len(args) + 1: 3}
        in_specs += [pl.BlockSpec(memory_space=pl.ANY)] * 2
        args += list(earlier)
    return pl.pallas_call(
        functools.partial(_proj_kernel, nq=nq, nk=nk, nv=nv, vb=vb, head_norm=head_norm, kstat=kstat,
                          n_alias=len(aliases), slot=slot, all_slots=all_slots),
        grid=grid,
        in_specs=in_specs,
        out_specs=out_specs,
        out_shape=out_shape,
        input_output_aliases=aliases,
        compiler_params=_cparams(1),
        name="proj",
    )(*args)


def _mlp_kernel(x_ref, o_ref, wo_ref, g_ref, wup_ref, wdn_ref, y_ref, *, f_chunk):
    x1 = x_ref[...] + jnp.dot(o_ref[...], wo_ref[...], preferred_element_type=F32)
    ms = jnp.mean(x1 * x1, axis=-1, keepdims=True)
    hn = (x1 * lax.rsqrt(ms + EPS) * g_ref[...]).astype(BF16)
    acc = x1
    for f in range(D_FF // f_chunk):
        u = jnp.dot(hn, wup_ref[:, f * f_chunk:(f + 1) * f_chunk], preferred_element_type=F32)
        a = jnp.square(jnp.maximum(u, 0.0)).astype(BF16)
        acc = acc + jnp.dot(a, wdn_ref[f * f_chunk:(f + 1) * f_chunk, :], preferred_element_type=F32)
    y_ref[...] = acc


def _outproj_mlp(x, o, wo, g, wup, wdn, layer):
    n_tok = x.shape[0]
    tm = min(TOKEN_TILE, n_tok)
    return pl.pallas_call(
        functools.partial(_mlp_kernel, f_chunk=1024),
        grid=(n_tok // tm,),
        in_specs=[
            pl.BlockSpec((tm, D_MODEL), lambda i: (i, 0)),
            pl.BlockSpec((tm, D_MODEL), lambda i: (i, 0)),
            _resident((D_MODEL, D_MODEL), lambda i: (0, 0)),
            _resident((1, D_MODEL), lambda i: (0, 0)),
            _resident((None, D_MODEL, D_FF), lambda i: (layer, 0, 0)),
            _resident((None, D_FF, D_MODEL), lambda i: (layer, 0, 0)),
        ],
        out_specs=pl.BlockSpec((tm, D_MODEL), lambda i: (i, 0)),
        out_shape=jax.ShapeDtypeStruct((n_tok, D_MODEL), F32),
        compiler_params=_cparams(1),
        name="outproj_mlp",
    )(x, o, wo, g, wup, wdn)


def _diff_lambda(lam_ref, lam_init):
    lp = lam_ref[...]
    a = jnp.sum(lp[0:1] * lp[1:2], axis=-1, keepdims=True)
    b = jnp.sum(lp[2:3] * lp[3:4], axis=-1, keepdims=True)
    return jnp.exp(a) - jnp.exp(b) + lam_init


def _diff_prompt_kernel(qT_ref, k_ref, vT_ref, bd_ref, bp_ref, kn_ref, bs_ref, lam_ref, sub_ref, o_ref,
                        m_ref, l_ref, acc_ref, *, tq, lam_init):
    qi = pl.program_id(1)
    qT = qT_ref[...]
    row = lax.broadcasted_iota(jnp.int32, qT.shape, 0)
    zero = jnp.zeros_like(qT)
    qm = (jnp.where(row < HEAD_DIM, qT, zero), jnp.where(row >= HEAD_DIM, qT, zero))

    def prev_bias(m, s):
        corner = s[tq - 128:, :128] + bp_ref[0, m]
        bottom = jnp.concatenate([corner, s[tq - 128:, 128:]], axis=1)
        return jnp.concatenate([s[:tq - 128], bottom], axis=0)

    def diag_bias(m, s):
        return s + bd_ref[0, m]

    def prev_diag_bias(m, s):
        return jnp.concatenate([prev_bias(m, s[:tq]), diag_bias(m, s[tq:])], axis=0)

    n_far = jnp.maximum(qi - 1, 0)

    def walk(tile_fn):
        def far_body(j, carry):
            tile_fn(j, None)
            return carry

        lax.fori_loop(0, n_far, far_body, 0)

        @pl.when(qi >= 1)
        def _():
            tile_fn(qi - 1, prev_bias)

        tile_fn(qi, diag_bias)

    span = []
    for m in range(2):
        qf = qm[m].astype(F32)
        qn = jnp.sqrt(jnp.sum(qf * qf, axis=0, keepdims=True))
        reach = qn * (jnp.sqrt(kn_ref[m:m + 1, :]) * 1.001)
        m_ref[m] = reach + bs_ref[m:m + 1, 0:1]
        span.append(jnp.max(2.0 * reach + bs_ref[m:m + 1, 1:2]))
    bound_is_tight = jnp.maximum(span[0], span[1]) <= SOFTMAX_BOUND_LIMIT

    @pl.when(jnp.logical_not(bound_is_tight))
    def _():
        m_ref[...] = jnp.full(m_ref.shape, NEG, F32)

        def max_tile(j, bias_fn):
            k = k_ref[j]
            for m in range(2):
                s = jnp.dot(k, qm[m], preferred_element_type=F32)
                if bias_fn is not None:
                    s = bias_fn(m, s)
                m_ref[m] = jnp.maximum(m_ref[m], jnp.max(s, axis=0, keepdims=True))

        walk(max_tile)

    l_ref[...] = jnp.zeros(l_ref.shape, F32)
    acc_ref[...] = jnp.zeros(acc_ref.shape, F32)

    def acc_tiles(js, bias_fn):
        k = jnp.concatenate([k_ref[j] for j in js], axis=0)
        vT = jnp.concatenate([vT_ref[j] for j in js], axis=1)
        nk = len(js) * tq
        for m in range(2):
            s = jnp.dot(k, qm[m], preferred_element_type=F32)
            if bias_fn is not None:
                s = bias_fn(m, s)
            p = jnp.exp2(s - m_ref[m])
            l_ref[m] += jnp.sum(p.reshape(nk // 8, 8, tq), axis=0)
            acc_ref[m] += jnp.dot(vT, p.astype(BF16), preferred_element_type=F32)

    def far_group(jj, carry):
        acc_tiles([DIFF_FAR_GROUP * jj + u for u in range(DIFF_FAR_GROUP)], None)
        return carry

    n_groups = n_far // DIFF_FAR_GROUP
    lax.fori_loop(0, n_groups, far_group, 0)
    start = n_groups * DIFF_FAR_GROUP
    left = n_far - start
    size = DIFF_FAR_GROUP // 2
    while size >= 1:
        @pl.when((left & size) != 0)
        def _(start=start, size=size):
            acc_tiles([start + u for u in range(size)], None)

        start = start + (left & size)
        size //= 2

    @pl.when(qi >= 1)
    def _():
        acc_tiles([qi - 1, qi], prev_diag_bias)

    @pl.when(qi == 0)
    def _():
        acc_tiles([qi], diag_bias)

    lam = _diff_lambda(lam_ref, lam_init)
    o0 = acc_ref[0] * (1.0 / jnp.sum(l_ref[0], axis=0, keepdims=True))
    o1 = acc_ref[1] * (1.0 / jnp.sum(l_ref[1], axis=0, keepdims=True))
    o = o0 - lam * o1
    ms = jnp.mean(o * o, axis=0, keepdims=True)
    o = o * lax.rsqrt(ms + EPS) * sub_ref[...] * (1.0 - lam_init)
    o_ref[...] = o.T.astype(BF16)


def _diff_prompt(qT, k16, vT, bias_d, bias_p, knorm2, bstats, lam_p, subln, lam_init):
    n_tok = k16.shape[0]
    tq = min(DIFF_TILE, n_tok)
    nb = n_tok // tq
    k3 = k16.reshape(nb, tq, D_MODEL)
    return pl.pallas_call(
        functools.partial(_diff_prompt_kernel, tq=tq, lam_init=lam_init),
        grid=(DIFF_HEADS, nb),
        in_specs=[
            pl.BlockSpec((2 * HEAD_DIM, tq), lambda h, i: (h, i)),
            pl.BlockSpec((nb, tq, 2 * HEAD_DIM), lambda h, i: (0, 0, h)),
            pl.BlockSpec((nb, 2 * HEAD_DIM, tq), lambda h, i: (0, h, 0)),
            pl.BlockSpec((1, 2, tq, tq), lambda h, i: (h, 0, 0, 0)),
            pl.BlockSpec((1, 2, 128, 128), lambda h, i: (h, 0, 0, 0)),
            pl.BlockSpec((None, 2, 1), lambda h, i: (h, 0, 0)),
            pl.BlockSpec((None, 2, 2), lambda h, i: (h, 0, 0)),
            pl.BlockSpec((4, HEAD_DIM), lambda h, i: (0, 0)),
            pl.BlockSpec((2 * HEAD_DIM, 1), lambda h, i: (0, 0)),
        ],
        out_specs=pl.BlockSpec((tq, 2 * HEAD_DIM), lambda h, i: (i, h)),
        out_shape=jax.ShapeDtypeStruct((n_tok, D_MODEL), BF16),
        scratch_shapes=[
            pltpu.VMEM((2, 1, tq), F32),
            pltpu.VMEM((2, 8, tq), F32),
            pltpu.VMEM((2, 2 * HEAD_DIM, tq), F32),
        ],
        compiler_params=_cparams(2),
        name="diff_prompt",
    )(qT, k3, vT, bias_d, bias_p, knorm2, bstats, lam_p, subln)


def _diff_sample_kernel(q_ref, ck_ref, cv_ref, kn_ref, vn_ref, b_ref, lam_ref, sub_ref, o_ref, *, lam_init, past):
    t = q_ref.shape[2]
    lane = lax.broadcasted_iota(jnp.int32, (t, 2 * HEAD_DIM), 1)
    lam = _diff_lambda(lam_ref, lam_init)
    for h in range(DIFF_HEADS):
        q = q_ref[0, h]
        zero = jnp.zeros_like(q)
        qq = jnp.concatenate([jnp.where(lane < HEAD_DIM, q, zero), jnp.where(lane >= HEAD_DIM, q, zero)], axis=0)
        ck = ck_ref[pl.ds(h, past, stride=DIFF_HEADS), :].astype(BF16)
        cv = cv_ref[pl.ds(h, past, stride=DIFF_HEADS), :].astype(BF16)
        kn = kn_ref[0, h]
        vn = vn_ref[0, h].astype(BF16)
        s_c = lax.dot_general(qq, ck, NT_DIMS, preferred_element_type=F32)
        s_n = lax.dot_general(qq, kn, NT_DIMS, preferred_element_type=F32)
        bias = b_ref[h].reshape(2 * t, 128 + t)
        s_far = s_c[:, :past - 128]
        s_near = s_c[:, past - 128:] + bias[:, :128]
        s_n = s_n + bias[:, 128:]
        m = jnp.maximum(jnp.maximum(jnp.max(s_far, axis=1, keepdims=True), jnp.max(s_near, axis=1, keepdims=True)),
                        jnp.max(s_n, axis=1, keepdims=True))
        p_far = jnp.exp2(s_far - m)
        p_near = jnp.exp2(s_near - m)
        p_n = jnp.exp2(s_n - m)
        l = (jnp.sum(p_far, axis=1, keepdims=True) + jnp.sum(p_near, axis=1, keepdims=True)
             + jnp.sum(p_n, axis=1, keepdims=True))
        o = (jnp.dot(p_far.astype(BF16), cv[:past - 128], preferred_element_type=F32)
             + jnp.dot(p_near.astype(BF16), cv[past - 128:], preferred_element_type=F32)
             + jnp.dot(p_n.astype(BF16), vn, preferred_element_type=F32))
        o = o * (1.0 / l)
        od = o[:t] - lam * o[t:]
        ms = jnp.mean(od * od, axis=-1, keepdims=True)
        od = od * lax.rsqrt(ms + EPS) * sub_ref[...] * (1.0 - lam_init)
        o_ref[0, h] = od.astype(BF16)


def _diff_sample(q_s, cache_k, cache_v, layer, kn, vn, bias_s, lam_p, subln_row, lam_init):
    nb, _, t, _ = q_s.shape
    past = cache_k.shape[2]
    rows = past * DIFF_HEADS
    ck = cache_k.reshape(cache_k.shape[0], nb, rows, 2 * HEAD_DIM)
    cv = cache_v.reshape(cache_v.shape[0], nb, rows, 2 * HEAD_DIM)
    cache_spec = pl.BlockSpec((None, None, rows, 2 * HEAD_DIM), lambda b: (layer, b, 0, 0))
    head_spec = pl.BlockSpec((1, DIFF_HEADS, t, 2 * HEAD_DIM), lambda b: (b, 0, 0, 0))
    return pl.pallas_call(
        functools.partial(_diff_sample_kernel, lam_init=lam_init, past=past),
        grid=(nb,),
        in_specs=[
            head_spec, cache_spec, cache_spec, head_spec, head_spec,
            pl.BlockSpec((DIFF_HEADS, 2, t, 128 + t), lambda b: (0, 0, 0, 0)),
            pl.BlockSpec((4, HEAD_DIM), lambda b: (0, 0)),
            pl.BlockSpec((1, 2 * HEAD_DIM), lambda b: (0, 0)),
        ],
        out_specs=head_spec,
        out_shape=jax.ShapeDtypeStruct((nb, DIFF_HEADS, t, 2 * HEAD_DIM), BF16),
        compiler_params=_cparams(1),
        name="diff_sample",
    )(q_s, ck, cv, kn, vn, bias_s, lam_p, subln_row)


def _swa_prompt_kernel(qT_ref, kc_ref, kp_ref, vc_ref, vp_ref, b_ref, sink_ref, o_ref):
    i = pl.program_id(0)
    qT = qT_ref[...]
    kp = kp_ref[...]
    kc = kc_ref[...]
    vp = vp_ref[0]
    vc = vc_ref[0]
    tq = qT.shape[1]
    key_row = lax.broadcasted_iota(jnp.int32, (2 * tq, SWA_GROUP * tq), 0)
    has_prev = key_row >= jnp.where(i > 0, 0, tq)
    zeros = jnp.zeros((HEAD_DIM, tq), BF16)
    outs = []
    for kv in range(SWA_KV_HEADS):
        pair = kv // 2
        k_band = jnp.concatenate([kp[:, pair * 128:(pair + 1) * 128], kc[:, pair * 128:(pair + 1) * 128]], axis=0)
        v_band = jnp.concatenate([vp[kv * HEAD_DIM:(kv + 1) * HEAD_DIM], vc[kv * HEAD_DIM:(kv + 1) * HEAD_DIM]],
                                 axis=1)
        qg = []
        for g in range(SWA_GROUP):
            h = kv * SWA_GROUP + g
            qh = qT[h * HEAD_DIM:(h + 1) * HEAD_DIM]
            qg.append(jnp.concatenate([qh, zeros] if kv % 2 == 0 else [zeros, qh], axis=0))
        q4 = jnp.concatenate(qg, axis=1)
        s = jnp.dot(k_band, q4, preferred_element_type=F32) + b_ref[kv]
        s = jnp.where(has_prev, s, NEG)
        sink = sink_ref[kv]
        m = jnp.maximum(jnp.max(s, axis=0, keepdims=True), sink)
        p = jnp.exp2(s - m)
        den = jnp.sum(p, axis=0, keepdims=True) + jnp.exp2(sink - m)
        oT4 = jnp.dot(v_band, p.astype(BF16), preferred_element_type=F32) * (1.0 / den)
        for g in range(SWA_GROUP):
            outs.append(oT4[:, g * tq:(g + 1) * tq])
    o_ref[...] = jnp.concatenate(outs, axis=0).T.astype(BF16)


def _swa_prompt(qT, k16, vT, bias, sinks):
    n_tok = k16.shape[0]
    tq = SWA_TILE
    nkv = SWA_KV_HEADS * HEAD_DIM
    return pl.pallas_call(
        _swa_prompt_kernel,
        grid=(n_tok // tq,),
        in_specs=[
            pl.BlockSpec((D_MODEL, tq), lambda i: (0, i)),
            pl.BlockSpec((tq, nkv), lambda i: (i, 0)),
            pl.BlockSpec((tq, nkv), lambda i: (jnp.maximum(i - 1, 0), 0)),
            pl.BlockSpec((1, nkv, tq), lambda i: (i, 0, 0)),
            pl.BlockSpec((1, nkv, tq), lambda i: (jnp.maximum(i - 1, 0), 0, 0)),
            _resident((SWA_KV_HEADS, 2 * tq, SWA_GROUP * tq), lambda i: (0, 0, 0)),
            _resident((SWA_KV_HEADS, 1, SWA_GROUP * tq), lambda i: (0, 0, 0)),
        ],
        out_specs=pl.BlockSpec((tq, D_MODEL), lambda i: (i, 0)),
        out_shape=jax.ShapeDtypeStruct((n_tok, D_MODEL), BF16),
        compiler_params=_cparams(1),
        name="swa_prompt",
    )(qT, k16, k16, vT, vT, bias, sinks)


def _swa_sample_kernel(q_ref, kc_ref, vc_ref, kn_ref, vn_ref, b_ref, sink_ref, o_ref, ko_ref, vo_ref):
    t = kn_ref.shape[2]
    buf = kc_ref.shape[2]
    for kv in range(SWA_KV_HEADS):
        kc32 = kc_ref[0, kv]
        vc32 = vc_ref[0, kv]
        kn32 = kn_ref[0, kv]
        vn32 = vn_ref[0, kv]
        ko_ref[0, kv, 0:buf - t, :] = kc32[t:]
        ko_ref[0, kv, buf - t:buf, :] = kn32
        vo_ref[0, kv, 0:buf - t, :] = vc32[t:]
        vo_ref[0, kv, buf - t:buf, :] = vn32
        kc = kc32.astype(BF16)
        vc = vc32.astype(BF16)
        kn = kn32.astype(BF16)
        vn = vn32.astype(BF16)
        heads = range(kv * SWA_GROUP, (kv + 1) * SWA_GROUP)
        q = jnp.concatenate([q_ref[0, h] for h in heads], axis=0)
        b = jnp.concatenate([b_ref[h] for h in heads], axis=0)
        sink = jnp.concatenate([jnp.broadcast_to(sink_ref[h], (t, 1)) for h in heads], axis=0)
        s_c = lax.dot_general(q, kc, NT_DIMS, preferred_element_type=F32) + b[:, :buf]
        s_n = lax.dot_general(q, kn, NT_DIMS, preferred_element_type=F32) + b[:, buf:]
        m = jnp.maximum(jnp.maximum(jnp.max(s_c, axis=1, keepdims=True), jnp.max(s_n, axis=1, keepdims=True)),
                        sink)
        p_c = jnp.exp2(s_c - m)
        p_n = jnp.exp2(s_n - m)
        den = (jnp.sum(p_c, axis=1, keepdims=True) + jnp.sum(p_n, axis=1, keepdims=True)
               + jnp.exp2(sink - m))
        o = (jnp.dot(p_c.astype(BF16), vc, preferred_element_type=F32)
             + jnp.dot(p_n.astype(BF16), vn, preferred_element_type=F32)) * (1.0 / den)
        for g, h in enumerate(heads):
            o_ref[0, h] = o[g * t:(g + 1) * t].astype(BF16)


def _swa_sample(q_s, kc, vc, kn, vn, bias, sinks):
    nb, _, t, _ = q_s.shape
    buf = kc.shape[2]
    kvspec = pl.BlockSpec((1, SWA_KV_HEADS, buf, HEAD_DIM), lambda b: (b, 0, 0, 0))
    nspec = pl.BlockSpec((1, SWA_KV_HEADS, t, HEAD_DIM), lambda b: (b, 0, 0, 0))
    return pl.pallas_call(
        _swa_sample_kernel,
        grid=(nb,),
        in_specs=[
            pl.BlockSpec((1, SWA_Q_HEADS, t, HEAD_DIM), lambda b: (b, 0, 0, 0)),
            kvspec, kvspec, nspec, nspec,
            pl.BlockSpec((SWA_Q_HEADS, t, buf + t), lambda b: (0, 0, 0)),
            pl.BlockSpec((SWA_Q_HEADS, 1, 1), lambda b: (0, 0, 0)),
        ],
        out_specs=[
            pl.BlockSpec((1, SWA_Q_HEADS, t, HEAD_DIM), lambda b: (b, 0, 0, 0)),
            kvspec, kvspec,
        ],
        out_shape=[
            jax.ShapeDtypeStruct((nb, SWA_Q_HEADS, t, HEAD_DIM), BF16),
            jax.ShapeDtypeStruct((nb, SWA_KV_HEADS, buf, HEAD_DIM), F32),
            jax.ShapeDtypeStruct((nb, SWA_KV_HEADS, buf, HEAD_DIM), F32),
        ],
        compiler_params=_cparams(1),
        name="swa_sample",
    )(q_s, kc, vc, kn, vn, bias, sinks)


def _softplus(z):
    return jnp.maximum(z, 0.0) + jnp.log2(1.0 + jnp.exp2(-jnp.abs(z)))


def _split_bf16(x):
    hi = x.astype(BF16)
    lo = (x - hi.astype(F32)).astype(BF16)
    return hi, lo


def _sb_prompt_kernel(qT_ref, k_ref, vT_ref, kmax_ref, o_ref, carry_ref, acc_ref, *, tq, pairs):
    qi = pl.program_id(1)
    width = 2 * pairs * tq
    row = lax.broadcasted_iota(jnp.int32, (2 * HEAD_DIM, tq), 0)
    qpairs, zbs = [], []
    for pr in range(pairs):
        qT = qT_ref[2 * HEAD_DIM * pr:2 * HEAD_DIM * (pr + 1), :]
        zero = jnp.zeros_like(qT)
        qp = jnp.concatenate([jnp.where(row < HEAD_DIM, qT, zero), jnp.where(row >= HEAD_DIM, qT, zero)], axis=1)
        qpairs.append(qp)
        kmax = kmax_ref[2 * HEAD_DIM * pr:2 * HEAD_DIM * (pr + 1), :]
        zbs.append(jnp.sum(jnp.abs(qp.astype(F32)) * kmax, axis=0, keepdims=True))
    zb = jnp.concatenate(zbs, axis=1)
    kr = lax.broadcasted_iota(jnp.int32, (tq, tq), 0)
    kc = lax.broadcasted_iota(jnp.int32, (tq, tq), 1)
    tri = jnp.where(kc >= kr, 1.0, 0.0).astype(BF16)
    krw = lax.broadcasted_iota(jnp.int32, (tq, width), 0)
    kcw = lax.broadcasted_iota(jnp.int32, (tq, width), 1)
    causal = krw < lax.rem(kcw, tq)
    carry_ref[...] = jnp.zeros(carry_ref.shape, F32)
    acc_ref[...] = jnp.zeros(acc_ref.shape, F32)

    def tiles(js, first_is_diag):
        zs, cs = [], []
        for n, j in enumerate(js):
            k = k_ref[j]
            z = jnp.concatenate([jnp.dot(k[:, 2 * HEAD_DIM * pr:2 * HEAD_DIM * (pr + 1)], qpairs[pr],
                                         preferred_element_type=F32) for pr in range(pairs)], axis=1)
            lsp = _softplus(z)
            if first_is_diag and n == 0:
                lsp = jnp.where(causal, lsp, 0.0)
            hi, lo = _split_bf16(lsp)
            zs.append(z)
            cs.append(jnp.dot(tri, hi, preferred_element_type=F32) + jnp.dot(tri, lo, preferred_element_type=F32))
        carry = carry_ref[...]
        weights = []
        for n in range(len(js)):
            a = jnp.exp2(zs[n] - cs[n] - carry)
            if first_is_diag and n == 0:
                a = jnp.where(causal, a, 0.0)
            weights.append(a.astype(BF16))
            carry = carry + cs[n][0:1]
        carry_ref[...] = carry
        for pr in range(pairs):
            vT = jnp.concatenate([vT_ref[j, 2 * HEAD_DIM * pr:2 * HEAD_DIM * (pr + 1), :] for j in js],
                                 axis=1)
            w = jnp.concatenate([a[:, 2 * tq * pr:2 * tq * (pr + 1)] for a in weights], axis=0)
            acc_ref[pr] += jnp.dot(vT, w, preferred_element_type=F32)

    def still_live():
        return (jnp.min(carry_ref[...] - zb) < SB_SKIP_MARGIN).astype(jnp.int32)

    @pl.when(qi == 0)
    def _():
        tiles([qi], True)

    @pl.when(qi == 1)
    def _():
        tiles([qi, qi - 1], True)

    @pl.when(qi >= 2)
    def _():
        tiles([qi, qi - 1, qi - 2], True)

    def cond(state):
        j, live = state
        return jnp.logical_and(j >= 0, live > 0)

    def body(state):
        j, _ = state
        tiles([j], False)
        return j - 1, still_live()

    lax.while_loop(cond, body, (qi - 3, still_live()))
    outs = []
    for pr in range(pairs):
        outs += [acc_ref[pr, 0:HEAD_DIM, 0:tq], acc_ref[pr, HEAD_DIM:2 * HEAD_DIM, tq:2 * tq]]
    o_ref[...] = jnp.concatenate(outs, axis=0).T.astype(BF16)


def _sb_prompt(qT, k16, vT, kmax):
    n_tok = k16.shape[0]
    tq = min(SB_TILE, n_tok)
    nb = n_tok // tq
    k3 = k16.reshape(nb, tq, D_MODEL)
    pairs = SB_PAIRS_PER_STEP
    wide = 2 * HEAD_DIM * pairs
    return pl.pallas_call(
        functools.partial(_sb_prompt_kernel, tq=tq, pairs=pairs),
        grid=(SB_HEADS // (2 * pairs), nb),
        in_specs=[
            pl.BlockSpec((wide, tq), lambda p, i: (p, i)),
            pl.BlockSpec((nb, tq, wide), lambda p, i: (0, 0, p)),
            pl.BlockSpec((nb, wide, tq), lambda p, i: (0, p, 0)),
            pl.BlockSpec((wide, 1), lambda p, i: (p, 0)),
        ],
        out_specs=pl.BlockSpec((tq, wide), lambda p, i: (i, p)),
        out_shape=jax.ShapeDtypeStruct((n_tok, D_MODEL), BF16),
        scratch_shapes=[
            pltpu.VMEM((1, 2 * pairs * tq), F32),
            pltpu.VMEM((pairs, 2 * HEAD_DIM, 2 * tq), F32),
        ],
        compiler_params=_cparams(2),
        name="sb_prompt",
    )(qT, k3, vT, kmax)


def _tri_lanes(n):
    r = lax.broadcasted_iota(jnp.int32, (n, n), 0)
    c = lax.broadcasted_iota(jnp.int32, (n, n), 1)
    return jnp.where(r >= c, 1.0, 0.0).astype(BF16)


def _sb_sample_kernel(q_ref, ck_ref, cv_ref, kn_ref, vn_ref, o_ref, carry_ref, acc_ref, *, ch, kb):
    c = pl.program_id(1)
    nh = q_ref.shape[1]
    t = q_ref.shape[2]

    @pl.when(c == 0)
    def _():
        z = jnp.concatenate([lax.dot_general(q_ref[0, h], kn_ref[0, h], NT_DIMS, preferred_element_type=F32)
                             for h in range(nh)], axis=0)
        r = lax.broadcasted_iota(jnp.int32, (nh * t, t), 0)
        kc = lax.broadcasted_iota(jnp.int32, (nh * t, t), 1)
        causal = kc < lax.rem(r, t)
        hi, lo = _split_bf16(jnp.where(causal, _softplus(z), 0.0))
        tri = _tri_lanes(t)
        cs = jnp.dot(hi, tri, preferred_element_type=F32) + jnp.dot(lo, tri, preferred_element_type=F32)
        a = jnp.where(causal, jnp.exp2(z - cs), 0.0).astype(BF16)
        carry_ref[...] = cs[:, 0:1]
        for h in range(nh):
            acc_ref[h * t:(h + 1) * t, :] = jnp.dot(a[h * t:(h + 1) * t], vn_ref[0, h].astype(BF16),
                                                    preferred_element_type=F32)

    z = jnp.concatenate([jnp.dot(q_ref[0, h], ck_ref[h].astype(BF16), preferred_element_type=F32)
                         for h in range(nh)], axis=0)
    nblk = ch // kb
    rows = nh * t
    zst = jnp.concatenate([z[:, b * kb:(b + 1) * kb] for b in range(nblk)], axis=0)
    hi, lo = _split_bf16(_softplus(zst))
    tri = _tri_lanes(kb)
    cs = jnp.dot(hi, tri, preferred_element_type=F32) + jnp.dot(lo, tri, preferred_element_type=F32)
    run = carry_ref[...]
    carries = [None] * nblk
    for b in reversed(range(nblk)):
        carries[b] = run
        run = run + cs[b * rows:(b + 1) * rows, 0:1]
    carry_ref[...] = run
    a = jnp.exp2(zst - cs - jnp.concatenate(carries, axis=0))
    a = jnp.concatenate([a[b * rows:(b + 1) * rows] for b in range(nblk)], axis=1).astype(BF16)
    for h in range(nh):
        acc_ref[h * t:(h + 1) * t, :] += lax.dot_general(a[h * t:(h + 1) * t], cv_ref[h].astype(BF16), NT_DIMS,
                                                         preferred_element_type=F32)

    @pl.when(c == pl.num_programs(1) - 1)
    def _():
        for h in range(nh):
            o_ref[0, h] = acc_ref[h * t:(h + 1) * t, :].astype(BF16)


def _sb_sample(q_s, cache_k, cache_v, layer, kn, vn):
    nb, nh, t, _ = q_s.shape
    past = cache_k.shape[2]
    ch = min(SB_SAMPLE_CHUNK, past)
    nch = past // ch
    ck = cache_k.transpose(0, 1, 3, 4, 2)
    cv = cache_v.transpose(0, 1, 3, 4, 2)
    cache_spec = pl.BlockSpec((None, None, nh, HEAD_DIM, ch), lambda b, c: (layer, b, 0, 0, nch - 1 - c))
    head_spec = pl.BlockSpec((1, nh, t, HEAD_DIM), lambda b, c: (b, 0, 0, 0))
    return pl.pallas_call(
        functools.partial(_sb_sample_kernel, ch=ch, kb=min(256, ch)),
        grid=(nb, nch),
        in_specs=[head_spec, cache_spec, cache_spec, head_spec, head_spec],
        out_specs=head_spec,
        out_shape=jax.ShapeDtypeStruct((nb, nh, t, HEAD_DIM), BF16),
        scratch_shapes=[
            pltpu.VMEM((nh * t, 1), F32),
            pltpu.VMEM((nh * t, HEAD_DIM), F32),
        ],
        compiler_params=_cparams(2),
        name="sb_sample",
    )(q_s, ck, cv, kn, vn)


def _tile_col(gain, n, scale):
    return (jnp.tile(gain.astype(F32), n // gain.shape[0]) * scale).reshape(n, 1)


def _heads_major(x, nb, t, nh, hd):
    return x.reshape(nb, t, nh, hd).transpose(0, 2, 1, 3)


def _diff_bias_tiles(rel_bias, tq, t):
    j = jnp.arange(tq)
    valid_d = (j // CHUNK)[:, None] <= (j // CHUNK)[None, :]
    bias_d = _bias_tile(rel_bias, tq, tq, 0, valid_d, True).reshape(DIFF_HEADS, 2, tq, tq)
    bias_p = _bias_tile(rel_bias, 128, 128, -128, None, True).reshape(DIFF_HEADS, 2, 128, 128)
    bstats = _bias_stats(rel_bias).reshape(DIFF_HEADS, 2, 2)
    bias_s = jnp.swapaxes(_bias_tile(rel_bias, 128 + t, t, -128, None, True), 1, 2)
    bias_s = bias_s.reshape(DIFF_HEADS, 2, t, 128 + t)
    return bias_d, bias_p, bstats, bias_s


def _diff_layer(xp, xs, cache_k, cache_v, layer, n_layers, earlier, bias_tiles, g, w_qkv, q_gain, k_gain, lam_p,
                subln, lam_init):
    bias_d, bias_p, bstats, bias_s = bias_tiles
    n_p = xp.shape[0]
    nb = cache_k.shape[1]
    t = xs.shape[0] // nb
    wt = w_qkv.T.astype(BF16)
    qg = _tile_col(q_gain, D_MODEL, HEAD_DIM ** -0.5 * LOG2E)
    kg = _tile_col(k_gain, D_MODEL, 1.0)
    g = g.reshape(1, D_MODEL)
    kw = dict(nq=D_MODEL, nk=D_MODEL, nv=D_MODEL, head_norm=True)
    tq = min(DIFF_TILE, n_p)
    qT, k32, k16, v32, vT, knorm2 = _project(xp, g, wt, qg, kg, vb=tq, kstat="normsq", slot=layer,
                                             n_slots=n_layers, earlier=earlier, **kw)
    qT_s, k32_s, k16_s, v32_s, _ = _project(xs, g, wt, qg, kg, vb=128, **kw)
    k32_s, v32_s = k32_s[0], v32_s[0]

    o_p = _diff_prompt(qT, k16, vT, bias_d, bias_p, knorm2.reshape(DIFF_HEADS, 2, 1), bstats, lam_p,
                       subln.reshape(2 * HEAD_DIM, 1), lam_init)
    hm = functools.partial(_heads_major, nb=nb, t=t, nh=DIFF_HEADS, hd=2 * HEAD_DIM)
    o_s = _diff_sample(hm(qT_s.T), cache_k, cache_v, layer, hm(k16_s), hm(v32_s), bias_s, lam_p,
                       subln.reshape(1, 2 * HEAD_DIM), lam_init)
    o_s = o_s.transpose(0, 2, 1, 3).reshape(nb * t, D_MODEL)
    return (o_p, o_s,
            k32, v32,
            k32_s.reshape(nb, t, DIFF_HEADS, 2 * HEAD_DIM), v32_s.reshape(nb, t, DIFF_HEADS, 2 * HEAD_DIM))


def _swa_layer(xp, xs, ck, cv, rel_bias, g, w_qkv, q_gain, k_gain, sinks):
    n_p = xp.shape[0]
    nb, buf = ck.shape[0], ck.shape[1]
    t = xs.shape[0] // nb
    nkv = SWA_KV_HEADS * HEAD_DIM
    wt = w_qkv.T.astype(BF16)
    qg = _tile_col(q_gain, D_MODEL, HEAD_DIM ** -0.5 * LOG2E)
    kg = _tile_col(k_gain, nkv, 1.0)
    g = g.reshape(1, D_MODEL)
    kw = dict(nq=D_MODEL, nk=nkv, nv=nkv, vb=SWA_TILE, head_norm=True)
    qT, k32, k16, v32, vT = _project(xp, g, wt, qg, kg, **kw)
    qT_s, k32_s, _, v32_s, _ = _project(xs, g, wt, qg, kg, **kw)
    k32, v32, k32_s, v32_s = k32[0], v32[0], k32_s[0], v32_s[0]
    sink_col = (sinks.astype(F32) * LOG2E).reshape(SWA_Q_HEADS, 1, 1)
    wchunks = WINDOW // CHUNK

    tq = SWA_TILE
    kchunk = jnp.floor_divide(jnp.arange(2 * tq) - tq, CHUNK)[:, None]
    qchunk = (jnp.arange(tq) // CHUNK)[None, :]
    valid = (kchunk <= qchunk) & (kchunk >= qchunk - wchunks)
    bias = _bias_tile(rel_bias, 2 * tq, tq, -tq, valid, False)
    bias_g = bias.reshape(SWA_KV_HEADS, SWA_GROUP, 2 * tq, tq).transpose(0, 2, 1, 3)
    bias_g = bias_g.reshape(SWA_KV_HEADS, 2 * tq, SWA_GROUP * tq)
    sink_g = jnp.repeat(sink_col.reshape(SWA_KV_HEADS, 1, SWA_GROUP), tq, axis=2)
    o_p = _swa_prompt(qT, k16, vT, bias_g, sink_g)

    sc = jnp.floor_divide(jnp.arange(buf + t) - buf, CHUNK)[:, None]
    tc = (jnp.arange(t) // CHUNK)[None, :]
    valid_s = (sc <= tc) & (sc >= tc - wchunks)
    bias_s = jnp.swapaxes(_bias_tile(rel_bias, buf + t, t, -buf, valid_s, False), 1, 2)
    q_s = _heads_major(qT_s.T, nb, t, SWA_Q_HEADS, HEAD_DIM)
    kn = _heads_major(k32_s, nb, t, SWA_KV_HEADS, HEAD_DIM)
    vn = _heads_major(v32_s, nb, t, SWA_KV_HEADS, HEAD_DIM)
    o_s, ko, vo = _swa_sample(q_s, ck.transpose(0, 2, 1, 3), cv.transpose(0, 2, 1, 3), kn, vn, bias_s, sink_col)
    o_s = o_s.transpose(0, 2, 1, 3).reshape(nb * t, D_MODEL)
    wbuf = min(WINDOW, n_p)
    return (o_p, o_s,
            k32[n_p - wbuf:].reshape(1, wbuf, SWA_KV_HEADS, HEAD_DIM),
            v32[n_p - wbuf:].reshape(1, wbuf, SWA_KV_HEADS, HEAD_DIM),
            ko.transpose(0, 2, 1, 3), vo.transpose(0, 2, 1, 3))


def _sb_layer(xp, xs, cache_k, cache_v, layer, n_layers, earlier, g, w_qkv):
    n_p = xp.shape[0]
    nb = cache_k.shape[1]
    t = xs.shape[0] // nb
    wt = w_qkv.T.astype(BF16)
    qg = jnp.full((D_MODEL, 1), HEAD_DIM ** -0.5 * LOG2E, F32)
    kg = jnp.ones((D_MODEL, 1), F32)
    g = g.reshape(1, D_MODEL)
    kw = dict(nq=D_MODEL, nk=D_MODEL, nv=D_MODEL, head_norm=False)
    tq = min(SB_TILE, n_p)
    qT, k32, k16, v32, vT, kmax = _project(xp, g, wt, qg, kg, vb=tq, kstat="absmax", slot=layer,
                                           n_slots=n_layers, earlier=earlier, **kw)
    qT_s, k32_s, k16_s, v32_s, _ = _project(xs, g, wt, qg, kg, vb=128, **kw)
    k32_s, v32_s = k32_s[0], v32_s[0]
    o_p = _sb_prompt(qT, k16, vT, kmax)
    hm = functools.partial(_heads_major, nb=nb, t=t, nh=SB_HEADS, hd=HEAD_DIM)
    o_s = _sb_sample(hm(qT_s.T), cache_k, cache_v, layer, hm(k16_s), hm(v32_s))
    o_s = o_s.transpose(0, 2, 1, 3).reshape(nb * t, D_MODEL)
    return (o_p, o_s,
            k32, v32,
            k32_s.reshape(nb, t, SB_HEADS, HEAD_DIM), v32_s.reshape(nb, t, SB_HEADS, HEAD_DIM))


def kernel(x_prompt, x_sample, cache_diff_k, cache_diff_v, cache_swa_k, cache_swa_v, cache_sb_k, cache_sb_v, rel_bias, norm_mix, norm_mlp, w_up, w_down, diff_w_qkv, diff_w_o, diff_q_norm, diff_k_norm, diff_lambda, diff_subln, swa_w_qkv, swa_w_o, swa_q_norm, swa_k_norm, swa_sinks, sb_w_qkv, sb_w_o):
    bp, n_p, _ = x_prompt.shape
    assert bp == 1
    nb, t, _ = x_sample.shape
    depth = norm_mix.shape[0]
    xp = x_prompt.reshape(n_p, D_MODEL)
    xs = x_sample.reshape(nb * t, D_MODEL)
    n_diff, n_sb = cache_diff_k.shape[0], cache_sb_k.shape[0]
    outs = {name: [] for name in ("pwk", "pwv", "sdk", "sdv", "swk", "swv", "sbk", "sbv")}
    diff_kv = sb_kv = None
    diff_bias = _diff_bias_tiles(rel_bias, min(DIFF_TILE, n_p), t)
    wup16 = w_up.astype(BF16)
    wdn16 = w_down.astype(BF16)
    for i in range(depth):
        j = i // N_MIXERS
        if i % N_MIXERS == 0:
            lam_init = 0.8 - 0.6 * math.exp(-0.3 * i)
            o_p, o_s, kp, vp, kn, vn = _diff_layer(
                xp, xs, cache_diff_k, cache_diff_v, j, n_diff, diff_kv, diff_bias, norm_mix[i], diff_w_qkv[j],
                diff_q_norm[j], diff_k_norm[j], diff_lambda[j], diff_subln[j], lam_init)
            diff_kv = (kp, vp)
            w_o = diff_w_o[j]
            names = ("sdk", "sdv")
        elif i % N_MIXERS == 1:
            o_p, o_s, kp, vp, kn, vn = _swa_layer(
                xp, xs, cache_swa_k[j], cache_swa_v[j], rel_bias, norm_mix[i], swa_w_qkv[j],
                swa_q_norm[j], swa_k_norm[j], swa_sinks[j])
            outs["pwk"].append(kp)
            outs["pwv"].append(vp)
            w_o = swa_w_o[j]
            names = ("swk", "swv")
        else:
            o_p, o_s, kp, vp, kn, vn = _sb_layer(xp, xs, cache_sb_k, cache_sb_v, j, n_sb, sb_kv, norm_mix[i],
                                                 sb_w_qkv[j])
            sb_kv = (kp, vp)
            w_o = sb_w_o[j]
            names = ("sbk", "sbv")
        for name, val in zip(names, (kn, vn)):
            outs[name].append(val)
        wo16 = w_o.astype(BF16)
        g_mlp = norm_mlp[i].reshape(1, D_MODEL)
        xp = _outproj_mlp(xp, o_p, wo16, g_mlp, wup16, wdn16, i)
        xs = _outproj_mlp(xs, o_s, wo16, g_mlp, wup16, wdn16, i)
    st = {name: jnp.stack(v) for name, v in outs.items()}
    diff_shape = (n_diff, 1, n_p, DIFF_HEADS, 2 * HEAD_DIM)
    sb_shape = (n_sb, 1, n_p, SB_HEADS, HEAD_DIM)
    return (xp.reshape(1, n_p, D_MODEL), xs.reshape(nb, t, D_MODEL),
            diff_kv[0].reshape(diff_shape), diff_kv[1].reshape(diff_shape), st["pwk"], st["pwv"],
            sb_kv[0].reshape(sb_shape), sb_kv[1].reshape(sb_shape),
            st["sdk"], st["sdv"], st["swk"], st["swv"], st["sbk"], st["sbv"])
```

```python
import functools
import math

import jax
import jax.numpy as jnp
from jax import lax
from jax.experimental import pallas as pl
from jax.experimental.pallas import tpu as pltpu

F32 = jnp.float32
BF16 = jnp.bfloat16

D_MODEL = 1024
HEAD_DIM = 64
CHUNK = 64
N_MIXERS = 3
DIFF_HEADS = 8
SWA_Q_HEADS = 16
SWA_KV_HEADS = 4
SWA_GROUP = 4
WINDOW = 128
SB_HEADS = 16
N_BUCKETS = 32
MAX_DISTANCE = 128
D_FF = 4 * D_MODEL
EPS = 1e-6
LOG2E = 1.4426950408889634
NEG = -1e30
LANES = 128
NEAR = MAX_DISTANCE
FAR_BUCKET = 15
SB_SKIP_MARGIN = 145.0
SOFTMAX_BOUND_LIMIT = 100.0

V7X_VMEM_BYTES = 64 * 1024 * 1024
VMEM_LIMIT = V7X_VMEM_BYTES * 7 // 8
TOKEN_TILE = 512
PROJ_CHUNK = 512
DIFF_TILE = 512
DIFF_FAR_GROUP = 8
SB_TILE = 256
SB_PAIRS_PER_STEP = 2
SB_SAMPLE_CHUNK = 512
SWA_TILE = 128

NT_DIMS = (((1,), (1,)), ((), ()))


def _cparams(n_axes):
    return pltpu.CompilerParams(dimension_semantics=("arbitrary",) * n_axes,
                                vmem_limit_bytes=VMEM_LIMIT)


def _resident(block_shape, index_map):
    return pl.BlockSpec(block_shape, index_map, pipeline_mode=pl.Buffered(1))


def _t5_bucket(rel):
    half = N_BUCKETS // 2
    exact = half // 2
    ret = jnp.where(rel > 0, half, 0).astype(jnp.int32)
    n = jnp.abs(rel)
    nf = jnp.maximum(n, 1).astype(F32)
    large = exact + (jnp.log(nf / exact) / math.log(MAX_DISTANCE / exact) * (half - exact)).astype(jnp.int32)
    large = jnp.minimum(large, half - 1)
    return ret + jnp.where(n < exact, n, large)


def _bias_tile(rel_bias, n_j, n_i, r0, valid, shift_far):
    tab = rel_bias.astype(F32)
    period = -(-(n_i + n_j - 1) // LANES) * LANES
    u = jnp.arange(period)
    rel = jnp.where(u < n_i, r0 - u, r0 + (period - u))
    vals = jnp.moveaxis(tab[_t5_bucket(rel)], -1, 0)
    if shift_far:
        vals = vals - tab[FAR_BUCKET][:, None]
    vals = (vals * LOG2E).reshape(vals.shape[0], 1, period)

    def expand(y_ref, o_ref):
        rows = jnp.broadcast_to(y_ref[0], (n_j, period))
        o_ref[0] = pltpu.roll(rows, 0, 1, stride=1, stride_axis=0)[:, :n_i]

    b = pl.pallas_call(
        expand,
        grid=(vals.shape[0],),
        in_specs=[pl.BlockSpec((1, 1, period), lambda c: (c, 0, 0))],
        out_specs=pl.BlockSpec((1, n_j, n_i), lambda c: (c, 0, 0)),
        out_shape=jax.ShapeDtypeStruct((vals.shape[0], n_j, n_i), F32),
        compiler_params=_cparams(1),
        name="bias_toeplitz",
    )(vals)
    if valid is not None:
        b = jnp.where(valid[None], b, NEG)
    return b


def _bias_stats(rel_bias):
    tab = rel_bias.astype(F32)
    sh = (tab - tab[FAR_BUCKET][None, :]) * LOG2E
    bmax = jnp.max(sh, axis=0)
    return jnp.stack([bmax, bmax - jnp.min(sh, axis=0)], axis=-1)


def _proj_kernel(x_ref, g_ref, wt_ref, qg_ref, kg_ref, *refs, nq, nk, nv, vb, head_norm, kstat, n_alias,
                 slot, all_slots):
    out_refs = refs[n_alias:]
    qT_ref, k32_ref, k16_ref, v32_ref, vT_ref = out_refs[:5]
    if all_slots > 1:
        for s in range(all_slots):
            if s != slot:
                k32_ref[s] = jnp.zeros(k32_ref.shape[1:], F32)
                v32_ref[s] = jnp.zeros(v32_ref.shape[1:], F32)
        k32_ref = k32_ref.at[slot]
        v32_ref = v32_ref.at[slot]
    x = x_ref[...]
    ms = jnp.mean(x * x, axis=-1, keepdims=True)
    h = (x * lax.rsqrt(ms + EPS) * g_ref[...]).astype(BF16)
    tm = x.shape[0]

    def rows(r0, n):
        return lax.dot_general(wt_ref[r0:r0 + n, :], h, NT_DIMS, preferred_element_type=F32)

    def headnorm(t, gcol):
        n = t.shape[0]
        t3 = t.reshape(n // HEAD_DIM, HEAD_DIM, tm)
        r = lax.rsqrt(jnp.mean(t3 * t3, axis=1, keepdims=True) + EPS)
        return (t3 * r).reshape(n, tm) * gcol

    for r0 in range(0, nq, PROJ_CHUNK):
        n = min(PROJ_CHUNK, nq - r0)
        qt = rows(r0, n)
        qt = headnorm(qt, qg_ref[r0:r0 + n, :]) if head_norm else qt * qg_ref[r0:r0 + n, :]
        qT_ref[r0:r0 + n, :] = qt.astype(BF16)

    for r0 in range(0, nk, PROJ_CHUNK):
        n = min(PROJ_CHUNK, nk - r0)
        kt = rows(nq + r0, n)
        if head_norm:
            kt = headnorm(kt, kg_ref[r0:r0 + n, :])
        k = kt.T
        k32_ref[:, r0:r0 + n] = k
        k16_ref[:, r0:r0 + n] = k.astype(BF16)
        if kstat is not None:
            stat_ref = out_refs[5]
            kr = kt.astype(BF16).astype(F32)
            if kstat == "absmax":
                stat = jnp.max(jnp.abs(kr), axis=1, keepdims=True)
                srows = slice(r0, r0 + n)
            else:
                k3 = kr.reshape(n // HEAD_DIM, HEAD_DIM, tm)
                stat = jnp.max(jnp.sum(k3 * k3, axis=1), axis=1, keepdims=True)
                srows = slice(r0 // HEAD_DIM, (r0 + n) // HEAD_DIM)

            @pl.when(pl.program_id(0) == 0)
            def _(stat=stat, srows=srows):
                stat_ref[srows, :] = stat

            @pl.when(pl.program_id(0) > 0)
            def _(stat=stat, srows=srows):
                stat_ref[srows, :] = jnp.maximum(stat_ref[srows, :], stat)

    for r0 in range(0, nv, PROJ_CHUNK):
        n = min(PROJ_CHUNK, nv - r0)
        vt = rows(nq + nk + r0, n)
        v32_ref[:, r0:r0 + n] = vt.T
        vt16 = vt.astype(BF16)
        for b in range(tm // vb):
            vT_ref[b, r0:r0 + n, :] = vt16[:, b * vb:(b + 1) * vb]


def _project(x, g, wt, qg, kg, *, nq, nk, nv, vb, head_norm, kstat=None, slot=0, n_slots=1, earlier=None):
    n_tok = x.shape[0]
    tm = min(TOKEN_TILE, n_tok)
    vb = min(vb, tm)
    grid = (n_tok // tm,)
    ntot = nq + nk + nv
    out_shape = [
        jax.ShapeDtypeStruct((nq, n_tok), BF16),
        jax.ShapeDtypeStruct((n_slots, n_tok, nk), F32),
        jax.ShapeDtypeStruct((n_tok, nk), BF16),
        jax.ShapeDtypeStruct((n_slots, n_tok, nv), F32),
        jax.ShapeDtypeStruct((n_tok // vb, nv, vb), BF16),
    ]
    all_slots = n_slots if (earlier is None and n_slots > 1) else 1

    def kv_spec(width):
        if all_slots > 1:
            return pl.BlockSpec((n_slots, tm, width), lambda i: (0, i, 0))
        return pl.BlockSpec((None, tm, width), lambda i: (slot, i, 0))

    out_specs = [
        pl.BlockSpec((nq, tm), lambda i: (0, i)),
        kv_spec(nk),
        pl.BlockSpec((tm, nk), lambda i: (i, 0)),
        kv_spec(nv),
        pl.BlockSpec((tm // vb, nv, vb), lambda i: (i, 0, 0)),
    ]
    if kstat is not None:
        n_stat = nk if kstat == "absmax" else nk // HEAD_DIM
        out_shape.append(jax.ShapeDtypeStruct((n_stat, 1), F32))
        out_specs.append(pl.BlockSpec((n_stat, 1), lambda i: (0, 0)))
    in_specs = [
        pl.BlockSpec((tm, D_MODEL), lambda i: (i, 0)),
        _resident((1, D_MODEL), lambda i: (0, 0)),
        _resident((ntot, D_MODEL), lambda i: (0, 0)),
        _resident((nq, 1), lambda i: (0, 0)),
        _resident((nk, 1), lambda i: (0, 0)),
    ]
    args = [x, g, wt, qg, kg]
    aliases = {}
    if earlier is not None:
        aliases = {len(args): 1, len(args) + 1: 3}
        in_specs += [pl.BlockSpec(memory_space=pl.ANY)] * 2
        args += list(earlier)
    return pl.pallas_call(
        functools.partial(_proj_kernel, nq=nq, nk=nk, nv=nv, vb=vb, head_norm=head_norm, kstat=kstat,
                          n_alias=len(aliases), slot=slot, all_slots=all_slots),
        grid=grid,
        in_specs=in_specs,
        out_specs=out_specs,
        out_shape=out_shape,
        input_output_aliases=aliases,
        compiler_params=_cparams(1),
        name="proj",
    )(*args)


def _mlp_kernel(x_ref, o_ref, wo_ref, g_ref, wup_ref, wdn_ref, y_ref, *, f_chunk):
    x1 = x_ref[...] + jnp.dot(o_ref[...], wo_ref[...], preferred_element_type=F32)
    ms = jnp.mean(x1 * x1, axis=-1, keepdims=True)
    hn = (x1 * lax.rsqrt(ms + EPS) * g_ref[...]).astype(BF16)
    acc = x1
    for f in range(D_FF // f_chunk):
        u = jnp.dot(hn, wup_ref[:, f * f_chunk:(f + 1) * f_chunk], preferred_element_type=F32)
        a = jnp.square(jnp.maximum(u, 0.0)).astype(BF16)
        acc = acc + jnp.dot(a, wdn_ref[f * f_chunk:(f + 1) * f_chunk, :], preferred_element_type=F32)
    y_ref[...] = acc


def _outproj_mlp(x, o, wo, g, wup, wdn, layer):
    n_tok = x.shape[0]
    tm = min(TOKEN_TILE, n_tok)
    return pl.pallas_call(
        functools.partial(_mlp_kernel, f_chunk=1024),
        grid=(n_tok // tm,),
        in_specs=[
            pl.BlockSpec((tm, D_MODEL), lambda i: (i, 0)),
            pl.BlockSpec((tm, D_MODEL), lambda i: (i, 0)),
            _resident((D_MODEL, D_MODEL), lambda i: (0, 0)),
            _resident((1, D_MODEL), lambda i: (0, 0)),
            _resident((None, D_MODEL, D_FF), lambda i: (layer, 0, 0)),
            _resident((None, D_FF, D_MODEL), lambda i: (layer, 0, 0)),
        ],
        out_specs=pl.BlockSpec((tm, D_MODEL), lambda i: (i, 0)),
        out_shape=jax.ShapeDtypeStruct((n_tok, D_MODEL), F32),
        compiler_params=_cparams(1),
        name="outproj_mlp",
    )(x, o, wo, g, wup, wdn)


def _diff_lambda(lam_ref, lam_init):
    lp = lam_ref[...]
    a = jnp.sum(lp[0:1] * lp[1:2], axis=-1, keepdims=True)
    b = jnp.sum(lp[2:3] * lp[3:4], axis=-1, keepdims=True)
    return jnp.exp(a) - jnp.exp(b) + lam_init


def _diff_prompt_kernel(qT_ref, k_ref, vT_ref, bd_ref, bp_ref, kn_ref, bs_ref, lam_ref, sub_ref, o_ref,
                        m_ref, l_ref, acc_ref, *, tq, lam_init):
    qi = pl.program_id(1)
    qT = qT_ref[...]
    row = lax.broadcasted_iota(jnp.int32, qT.shape, 0)
    zero = jnp.zeros_like(qT)
    qm = (jnp.where(row < HEAD_DIM, qT, zero), jnp.where(row >= HEAD_DIM, qT, zero))

    def prev_bias(m, s):
        corner = s[tq - NEAR:, :NEAR] + bp_ref[0, m]
        bottom = jnp.concatenate([corner, s[tq - NEAR:, NEAR:]], axis=1)
        return jnp.concatenate([s[:tq - NEAR], bottom], axis=0)

    def diag_bias(m, s):
        return s + bd_ref[0, m]

    def prev_diag_bias(m, s):
        return jnp.concatenate([prev_bias(m, s[:tq]), diag_bias(m, s[tq:])], axis=0)

    n_far = jnp.maximum(qi - 1, 0)

    def walk(tile_fn):
        def far_body(j, carry):
            tile_fn(j, None)
            return carry

        lax.fori_loop(0, n_far, far_body, 0)

        @pl.when(qi >= 1)
        def _():
            tile_fn(qi - 1, prev_bias)

        tile_fn(qi, diag_bias)

    span = []
    for m in range(2):
        qf = qm[m].astype(F32)
        qn = jnp.sqrt(jnp.sum(qf * qf, axis=0, keepdims=True))
        reach = qn * (jnp.sqrt(kn_ref[m:m + 1, :]) * 1.001)
        m_ref[m] = reach + bs_ref[m:m + 1, 0:1]
        span.append(jnp.max(2.0 * reach + bs_ref[m:m + 1, 1:2]))
    bound_is_tight = jnp.maximum(span[0], span[1]) <= SOFTMAX_BOUND_LIMIT

    @pl.when(jnp.logical_not(bound_is_tight))
    def _():
        m_ref[...] = jnp.full(m_ref.shape, NEG, F32)

        def max_tile(j, bias_fn):
            k = k_ref[j]
            for m in range(2):
                s = jnp.dot(k, qm[m], preferred_element_type=F32)
                if bias_fn is not None:
                    s = bias_fn(m, s)
                m_ref[m] = jnp.maximum(m_ref[m], jnp.max(s, axis=0, keepdims=True))

        walk(max_tile)

    l_ref[...] = jnp.zeros(l_ref.shape, F32)
    acc_ref[...] = jnp.zeros(acc_ref.shape, F32)

    def acc_tiles(js, bias_fn):
        k = jnp.concatenate([k_ref[j] for j in js], axis=0)
        vT = jnp.concatenate([vT_ref[j] for j in js], axis=1)
        nk = len(js) * tq
        for m in range(2):
            s = jnp.dot(k, qm[m], preferred_element_type=F32)
            if bias_fn is not None:
                s = bias_fn(m, s)
            p = jnp.exp2(s - m_ref[m])
            l_ref[m] += jnp.sum(p.reshape(nk // 8, 8, tq), axis=0)
            acc_ref[m] += jnp.dot(vT, p.astype(BF16), preferred_element_type=F32)

    def far_group(jj, carry):
        acc_tiles([DIFF_FAR_GROUP * jj + u for u in range(DIFF_FAR_GROUP)], None)
        return carry

    n_groups = n_far // DIFF_FAR_GROUP
    lax.fori_loop(0, n_groups, far_group, 0)
    start = n_groups * DIFF_FAR_GROUP
    left = n_far - start
    size = DIFF_FAR_GROUP // 2
    while size >= 2:
        @pl.when((left & size) != 0)
        def _(start=start, size=size):
            acc_tiles([start + u for u in range(size)], None)

        start = start + (left & size)
        size //= 2
    odd = (left & 1) != 0

    def plain_prev_diag_bias(m, s):
        return jnp.concatenate([s[:tq], prev_diag_bias(m, s[tq:])], axis=0)

    @pl.when(odd)
    def _():
        acc_tiles([start, qi - 1, qi], plain_prev_diag_bias)

    @pl.when(jnp.logical_and(qi >= 1, jnp.logical_not(odd)))
    def _():
        acc_tiles([qi - 1, qi], prev_diag_bias)

    @pl.when(qi == 0)
    def _():
        acc_tiles([qi], diag_bias)

    lam = _diff_lambda(lam_ref, lam_init)
    o0 = acc_ref[0] * (1.0 / jnp.sum(l_ref[0], axis=0, keepdims=True))
    o1 = acc_ref[1] * (1.0 / jnp.sum(l_ref[1], axis=0, keepdims=True))
    o = o0 - lam * o1
    ms = jnp.mean(o * o, axis=0, keepdims=True)
    o = o * lax.rsqrt(ms + EPS) * sub_ref[...] * (1.0 - lam_init)
    o_ref[...] = o.T.astype(BF16)


def _diff_prompt(qT, k16, vT, bias_d, bias_p, knorm2, bstats, lam_p, subln, lam_init):
    n_tok = k16.shape[0]
    tq = min(DIFF_TILE, n_tok)
    nb = n_tok // tq
    k3 = k16.reshape(nb, tq, D_MODEL)
    return pl.pallas_call(
        functools.partial(_diff_prompt_kernel, tq=tq, lam_init=lam_init),
        grid=(DIFF_HEADS, nb),
        in_specs=[
            pl.BlockSpec((2 * HEAD_DIM, tq), lambda h, i: (h, i)),
            pl.BlockSpec((nb, tq, 2 * HEAD_DIM), lambda h, i: (0, 0, h)),
            pl.BlockSpec((nb, 2 * HEAD_DIM, tq), lambda h, i: (0, h, 0)),
            pl.BlockSpec((1, 2, tq, tq), lambda h, i: (h, 0, 0, 0)),
            pl.BlockSpec((1, 2, NEAR, NEAR), lambda h, i: (h, 0, 0, 0)),
            pl.BlockSpec((None, 2, 1), lambda h, i: (h, 0, 0)),
            pl.BlockSpec((None, 2, 2), lambda h, i: (h, 0, 0)),
            pl.BlockSpec((4, HEAD_DIM), lambda h, i: (0, 0)),
            pl.BlockSpec((2 * HEAD_DIM, 1), lambda h, i: (0, 0)),
        ],
        out_specs=pl.BlockSpec((tq, 2 * HEAD_DIM), lambda h, i: (i, h)),
        out_shape=jax.ShapeDtypeStruct((n_tok, D_MODEL), BF16),
        scratch_shapes=[
            pltpu.VMEM((2, 1, tq), F32),
            pltpu.VMEM((2, 8, tq), F32),
            pltpu.VMEM((2, 2 * HEAD_DIM, tq), F32),
        ],
        compiler_params=_cparams(2),
        name="diff_prompt",
    )(qT, k3, vT, bias_d, bias_p, knorm2, bstats, lam_p, subln)


def _diff_sample_kernel(q_ref, ck_ref, cv_ref, kn_ref, vn_ref, b_ref, lam_ref, sub_ref, o_ref, *, lam_init, past):
    t = q_ref.shape[2]
    lane = lax.broadcasted_iota(jnp.int32, (t, 2 * HEAD_DIM), 1)
    lam = _diff_lambda(lam_ref, lam_init)
    for h in range(DIFF_HEADS):
        q = q_ref[0, h]
        zero = jnp.zeros_like(q)
        qq = jnp.concatenate([jnp.where(lane < HEAD_DIM, q, zero), jnp.where(lane >= HEAD_DIM, q, zero)], axis=0)
        ck = ck_ref[pl.ds(h, past, stride=DIFF_HEADS), :].astype(BF16)
        cv = cv_ref[pl.ds(h, past, stride=DIFF_HEADS), :].astype(BF16)
        kn = kn_ref[0, h]
        vn = vn_ref[0, h].astype(BF16)
        s_c = lax.dot_general(qq, ck, NT_DIMS, preferred_element_type=F32)
        s_n = lax.dot_general(qq, kn, NT_DIMS, preferred_element_type=F32)
        bias = b_ref[h].reshape(2 * t, NEAR + t)
        s_far = s_c[:, :past - NEAR]
        s_near = s_c[:, past - NEAR:] + bias[:, :NEAR]
        s_n = s_n + bias[:, NEAR:]
        m = jnp.maximum(jnp.maximum(jnp.max(s_far, axis=1, keepdims=True), jnp.max(s_near, axis=1, keepdims=True)),
                        jnp.max(s_n, axis=1, keepdims=True))
        p_far = jnp.exp2(s_far - m)
        p_near = jnp.exp2(s_near - m)
        p_n = jnp.exp2(s_n - m)
        l = (jnp.sum(p_far, axis=1, keepdims=True) + jnp.sum(p_near, axis=1, keepdims=True)
             + jnp.sum(p_n, axis=1, keepdims=True))
        o = (jnp.dot(p_far.astype(BF16), cv[:past - NEAR], preferred_element_type=F32)
             + jnp.dot(p_near.astype(BF16), cv[past - NEAR:], preferred_element_type=F32)
             + jnp.dot(p_n.astype(BF16), vn, preferred_element_type=F32))
        o = o * (1.0 / l)
        od = o[:t] - lam * o[t:]
        ms = jnp.mean(od * od, axis=-1, keepdims=True)
        od = od * lax.rsqrt(ms + EPS) * sub_ref[...] * (1.0 - lam_init)
        o_ref[0, h] = od.astype(BF16)


def _diff_sample(q_s, cache_k, cache_v, layer, kn, vn, bias_s, lam_p, subln_row, lam_init):
    nb, _, t, _ = q_s.shape
    past = cache_k.shape[2]
    rows = past * DIFF_HEADS
    ck = cache_k.reshape(cache_k.shape[0], nb, rows, 2 * HEAD_DIM)
    cv = cache_v.reshape(cache_v.shape[0], nb, rows, 2 * HEAD_DIM)
    cache_spec = pl.BlockSpec((None, None, rows, 2 * HEAD_DIM), lambda b: (layer, b, 0, 0))
    head_spec = pl.BlockSpec((1, DIFF_HEADS, t, 2 * HEAD_DIM), lambda b: (b, 0, 0, 0))
    return pl.pallas_call(
        functools.partial(_diff_sample_kernel, lam_init=lam_init, past=past),
        grid=(nb,),
        in_specs=[
            head_spec, cache_spec, cache_spec, head_spec, head_spec,
            pl.BlockSpec((DIFF_HEADS, 2, t, NEAR + t), lambda b: (0, 0, 0, 0)),
            pl.BlockSpec((4, HEAD_DIM), lambda b: (0, 0)),
            pl.BlockSpec((1, 2 * HEAD_DIM), lambda b: (0, 0)),
        ],
        out_specs=head_spec,
        out_shape=jax.ShapeDtypeStruct((nb, DIFF_HEADS, t, 2 * HEAD_DIM), BF16),
        compiler_params=_cparams(1),
        name="diff_sample",
    )(q_s, ck, cv, kn, vn, bias_s, lam_p, subln_row)


def _swa_prompt_kernel(qT_ref, kc_ref, kp_ref, vc_ref, vp_ref, b_ref, sink_ref, o_ref):
    i = pl.program_id(0)
    qT = qT_ref[...]
    kp = kp_ref[...]
    kc = kc_ref[...]
    vp = vp_ref[0]
    vc = vc_ref[0]
    tq = qT.shape[1]
    key_row = lax.broadcasted_iota(jnp.int32, (2 * tq, SWA_GROUP * tq), 0)
    has_prev = key_row >= jnp.where(i > 0, 0, tq)
    zeros = jnp.zeros((HEAD_DIM, tq), BF16)
    outs = []
    for kv in range(SWA_KV_HEADS):
        pair = kv // 2
        lanes = slice(pair * 2 * HEAD_DIM, (pair + 1) * 2 * HEAD_DIM)
        k_band = jnp.concatenate([kp[:, lanes], kc[:, lanes]], axis=0)
        v_band = jnp.concatenate([vp[kv * HEAD_DIM:(kv + 1) * HEAD_DIM], vc[kv * HEAD_DIM:(kv + 1) * HEAD_DIM]],
                                 axis=1)
        qg = []
        for g in range(SWA_GROUP):
            h = kv * SWA_GROUP + g
            qh = qT[h * HEAD_DIM:(h + 1) * HEAD_DIM]
            qg.append(jnp.concatenate([qh, zeros] if kv % 2 == 0 else [zeros, qh], axis=0))
        q4 = jnp.concatenate(qg, axis=1)
        s = jnp.dot(k_band, q4, preferred_element_type=F32) + b_ref[kv]
        s = jnp.where(has_prev, s, NEG)
        sink = sink_ref[kv]
        m = jnp.maximum(jnp.max(s, axis=0, keepdims=True), sink)
        p = jnp.exp2(s - m)
        den = jnp.sum(p, axis=0, keepdims=True) + jnp.exp2(sink - m)
        oT4 = jnp.dot(v_band, p.astype(BF16), preferred_element_type=F32) * (1.0 / den)
        for g in range(SWA_GROUP):
            outs.append(oT4[:, g * tq:(g + 1) * tq])
    o_ref[...] = jnp.concatenate(outs, axis=0).T.astype(BF16)


def _swa_prompt(qT, k16, vT, bias, sinks):
    n_tok = k16.shape[0]
    tq = SWA_TILE
    nkv = SWA_KV_HEADS * HEAD_DIM
    return pl.pallas_call(
        _swa_prompt_kernel,
        grid=(n_tok // tq,),
        in_specs=[
            pl.BlockSpec((D_MODEL, tq), lambda i: (0, i)),
            pl.BlockSpec((tq, nkv), lambda i: (i, 0)),
            pl.BlockSpec((tq, nkv), lambda i: (jnp.maximum(i - 1, 0), 0)),
            pl.BlockSpec((1, nkv, tq), lambda i: (i, 0, 0)),
            pl.BlockSpec((1, nkv, tq), lambda i: (jnp.maximum(i - 1, 0), 0, 0)),
            _resident((SWA_KV_HEADS, 2 * tq, SWA_GROUP * tq), lambda i: (0, 0, 0)),
            _resident((SWA_KV_HEADS, 1, SWA_GROUP * tq), lambda i: (0, 0, 0)),
        ],
        out_specs=pl.BlockSpec((tq, D_MODEL), lambda i: (i, 0)),
        out_shape=jax.ShapeDtypeStruct((n_tok, D_MODEL), BF16),
        compiler_params=_cparams(1),
        name="swa_prompt",
    )(qT, k16, k16, vT, vT, bias, sinks)


def _swa_sample_kernel(q_ref, kc_ref, vc_ref, kn_ref, vn_ref, b_ref, sink_ref, o_ref, ko_ref, vo_ref):
    t = kn_ref.shape[2]
    buf = kc_ref.shape[2]
    for kv in range(SWA_KV_HEADS):
        kc32 = kc_ref[0, kv]
        vc32 = vc_ref[0, kv]
        kn32 = kn_ref[0, kv]
        vn32 = vn_ref[0, kv]
        ko_ref[0, kv, 0:buf - t, :] = kc32[t:]
        ko_ref[0, kv, buf - t:buf, :] = kn32
        vo_ref[0, kv, 0:buf - t, :] = vc32[t:]
        vo_ref[0, kv, buf - t:buf, :] = vn32
        kc = kc32.astype(BF16)
        vc = vc32.astype(BF16)
        kn = kn32.astype(BF16)
        vn = vn32.astype(BF16)
        heads = range(kv * SWA_GROUP, (kv + 1) * SWA_GROUP)
        q = jnp.concatenate([q_ref[0, h] for h in heads], axis=0)
        b = jnp.concatenate([b_ref[h] for h in heads], axis=0)
        sink = jnp.concatenate([jnp.broadcast_to(sink_ref[h], (t, 1)) for h in heads], axis=0)
        s_c = lax.dot_general(q, kc, NT_DIMS, preferred_element_type=F32) + b[:, :buf]
        s_n = lax.dot_general(q, kn, NT_DIMS, preferred_element_type=F32) + b[:, buf:]
        m = jnp.maximum(jnp.maximum(jnp.max(s_c, axis=1, keepdims=True), jnp.max(s_n, axis=1, keepdims=True)),
                        sink)
        p_c = jnp.exp2(s_c - m)
        p_n = jnp.exp2(s_n - m)
        den = (jnp.sum(p_c, axis=1, keepdims=True) + jnp.sum(p_n, axis=1, keepdims=True)
               + jnp.exp2(sink - m))
        o = (jnp.dot(p_c.astype(BF16), vc, preferred_element_type=F32)
             + jnp.dot(p_n.astype(BF16), vn, preferred_element_type=F32)) * (1.0 / den)
        for g, h in enumerate(heads):
            o_ref[0, h] = o[g * t:(g + 1) * t].astype(BF16)


def _swa_sample(q_s, kc, vc, kn, vn, bias, sinks):
    nb, _, t, _ = q_s.shape
    buf = kc.shape[2]
    kvspec = pl.BlockSpec((1, SWA_KV_HEADS, buf, HEAD_DIM), lambda b: (b, 0, 0, 0))
    nspec = pl.BlockSpec((1, SWA_KV_HEADS, t, HEAD_DIM), lambda b: (b, 0, 0, 0))
    return pl.pallas_call(
        _swa_sample_kernel,
        grid=(nb,),
        in_specs=[
            pl.BlockSpec((1, SWA_Q_HEADS, t, HEAD_DIM), lambda b: (b, 0, 0, 0)),
            kvspec, kvspec, nspec, nspec,
            pl.BlockSpec((SWA_Q_HEADS, t, buf + t), lambda b: (0, 0, 0)),
            pl.BlockSpec((SWA_Q_HEADS, 1, 1), lambda b: (0, 0, 0)),
        ],
        out_specs=[
            pl.BlockSpec((1, SWA_Q_HEADS, t, HEAD_DIM), lambda b: (b, 0, 0, 0)),
            kvspec, kvspec,
        ],
        out_shape=[
            jax.ShapeDtypeStruct((nb, SWA_Q_HEADS, t, HEAD_DIM), BF16),
            jax.ShapeDtypeStruct((nb, SWA_KV_HEADS, buf, HEAD_DIM), F32),
            jax.ShapeDtypeStruct((nb, SWA_KV_HEADS, buf, HEAD_DIM), F32),
        ],
        compiler_params=_cparams(1),
        name="swa_sample",
    )(q_s, kc, vc, kn, vn, bias, sinks)


def _softplus(z):
    return jnp.maximum(z, 0.0) + jnp.log2(1.0 + jnp.exp2(-jnp.abs(z)))


def _split_bf16(x):
    hi = x.astype(BF16)
    lo = (x - hi.astype(F32)).astype(BF16)
    return hi, lo


def _sb_prompt_kernel(qT_ref, k_ref, vT_ref, kmax_ref, o_ref, carry_ref, acc_ref, *, tq, pairs):
    qi = pl.program_id(1)
    width = 2 * pairs * tq
    row = lax.broadcasted_iota(jnp.int32, (2 * HEAD_DIM, tq), 0)
    qpairs, zbs = [], []
    for pr in range(pairs):
        qT = qT_ref[2 * HEAD_DIM * pr:2 * HEAD_DIM * (pr + 1), :]
        zero = jnp.zeros_like(qT)
        qp = jnp.concatenate([jnp.where(row < HEAD_DIM, qT, zero), jnp.where(row >= HEAD_DIM, qT, zero)], axis=1)
        qpairs.append(qp)
        kmax = kmax_ref[2 * HEAD_DIM * pr:2 * HEAD_DIM * (pr + 1), :]
        zbs.append(jnp.sum(jnp.abs(qp.astype(F32)) * kmax, axis=0, keepdims=True))
    zb = jnp.concatenate(zbs, axis=1)
    kr = lax.broadcasted_iota(jnp.int32, (tq, tq), 0)
    kc = lax.broadcasted_iota(jnp.int32, (tq, tq), 1)
    tri = jnp.where(kc >= kr, 1.0, 0.0).astype(BF16)
    tri2 = jnp.concatenate([tri, tri], axis=1)
    krw = lax.broadcasted_iota(jnp.int32, (tq, width), 0)
    kcw = lax.broadcasted_iota(jnp.int32, (tq, width), 1)
    causal = krw < lax.rem(kcw, tq)
    carry_ref[...] = jnp.zeros(carry_ref.shape, F32)
    acc_ref[...] = jnp.zeros(acc_ref.shape, F32)

    def tiles(js, first_is_diag):
        zs, cs = [], []
        for n, j in enumerate(js):
            k = k_ref[j]
            z = jnp.concatenate([jnp.dot(k[:, 2 * HEAD_DIM * pr:2 * HEAD_DIM * (pr + 1)], qpairs[pr],
                                         preferred_element_type=F32) for pr in range(pairs)], axis=1)
            lsp = _softplus(z)
            if first_is_diag and n == 0:
                lsp = jnp.where(causal, lsp, 0.0)
            hi, lo = _split_bf16(lsp)
            zs.append(z)
            cs.append(jnp.dot(tri2, jnp.concatenate([hi, lo], axis=0), preferred_element_type=F32))
        carry = carry_ref[...]
        weights = []
        for n in range(len(js)):
            a = jnp.exp2(zs[n] - cs[n] - carry)
            if first_is_diag and n == 0:
                a = jnp.where(causal, a, 0.0)
            weights.append(a.astype(BF16))
            carry = carry + cs[n][0:1]
        carry_ref[...] = carry
        for pr in range(pairs):
            vT = jnp.concatenate([vT_ref[j, 2 * HEAD_DIM * pr:2 * HEAD_DIM * (pr + 1), :] for j in js],
                                 axis=1)
            w = jnp.concatenate([a[:, 2 * tq * pr:2 * tq * (pr + 1)] for a in weights], axis=0)
            acc_ref[pr] += jnp.dot(vT, w, preferred_element_type=F32)

    def still_live():
        return (jnp.min(carry_ref[...] - zb) < SB_SKIP_MARGIN).astype(jnp.int32)

    @pl.when(qi == 0)
    def _():
        tiles([qi], True)

    @pl.when(qi == 1)
    def _():
        tiles([qi, qi - 1], True)

    @pl.when(qi >= 2)
    def _():
        tiles([qi, qi - 1, qi - 2], True)

    def cond(state):
        j, live = state
        return jnp.logical_and(j >= 0, live > 0)

    def body(state):
        j, _ = state
        tiles([j], False)
        return j - 1, still_live()

    lax.while_loop(cond, body, (qi - 3, still_live()))
    outs = []
    for pr in range(pairs):
        outs += [acc_ref[pr, 0:HEAD_DIM, 0:tq], acc_ref[pr, HEAD_DIM:2 * HEAD_DIM, tq:2 * tq]]
    o_ref[...] = jnp.concatenate(outs, axis=0).T.astype(BF16)


def _sb_prompt(qT, k16, vT, kmax):
    n_tok = k16.shape[0]
    tq = min(SB_TILE, n_tok)
    nb = n_tok // tq
    k3 = k16.reshape(nb, tq, D_MODEL)
    pairs = SB_PAIRS_PER_STEP
    wide = 2 * HEAD_DIM * pairs
    return pl.pallas_call(
        functools.partial(_sb_prompt_kernel, tq=tq, pairs=pairs),
        grid=(SB_HEADS // (2 * pairs), nb),
        in_specs=[
            pl.BlockSpec((wide, tq), lambda p, i: (p, i)),
            pl.BlockSpec((nb, tq, wide), lambda p, i: (0, 0, p)),
            pl.BlockSpec((nb, wide, tq), lambda p, i: (0, p, 0)),
            pl.BlockSpec((wide, 1), lambda p, i: (p, 0)),
        ],
        out_specs=pl.BlockSpec((tq, wide), lambda p, i: (i, p)),
        out_shape=jax.ShapeDtypeStruct((n_tok, D_MODEL), BF16),
        scratch_shapes=[
            pltpu.VMEM((1, 2 * pairs * tq), F32),
            pltpu.VMEM((pairs, 2 * HEAD_DIM, 2 * tq), F32),
        ],
        compiler_params=_cparams(2),
        name="sb_prompt",
    )(qT, k3, vT, kmax)


def _tri_lanes(n):
    r = lax.broadcasted_iota(jnp.int32, (n, n), 0)
    c = lax.broadcasted_iota(jnp.int32, (n, n), 1)
    return jnp.where(r >= c, 1.0, 0.0).astype(BF16)


def _sb_sample_kernel(q_ref, ck_ref, cv_ref, kn_ref, vn_ref, o_ref, carry_ref, acc_ref, *, ch, kb):
    c = pl.program_id(1)
    nh = q_ref.shape[1]
    t = q_ref.shape[2]

    @pl.when(c == 0)
    def _():
        z = jnp.concatenate([lax.dot_general(q_ref[0, h], kn_ref[0, h], NT_DIMS, preferred_element_type=F32)
                             for h in range(nh)], axis=0)
        r = lax.broadcasted_iota(jnp.int32, (nh * t, t), 0)
        kc = lax.broadcasted_iota(jnp.int32, (nh * t, t), 1)
        causal = kc < lax.rem(r, t)
        hi, lo = _split_bf16(jnp.where(causal, _softplus(z), 0.0))
        tri = _tri_lanes(t)
        cs = jnp.dot(hi, tri, preferred_element_type=F32) + jnp.dot(lo, tri, preferred_element_type=F32)
        a = jnp.where(causal, jnp.exp2(z - cs), 0.0).astype(BF16)
        carry_ref[...] = cs[:, 0:1]
        for h in range(nh):
            acc_ref[h * t:(h + 1) * t, :] = jnp.dot(a[h * t:(h + 1) * t], vn_ref[0, h].astype(BF16),
                                                    preferred_element_type=F32)

    z = jnp.concatenate([jnp.dot(q_ref[0, h], ck_ref[h].astype(BF16), preferred_element_type=F32)
                         for h in range(nh)], axis=0)
    nblk = ch // kb
    rows = nh * t
    zst = jnp.concatenate([z[:, b * kb:(b + 1) * kb] for b in range(nblk)], axis=0)
    hi, lo = _split_bf16(_softplus(zst))
    tri = _tri_lanes(kb)
    cs = jnp.dot(hi, tri, preferred_element_type=F32) + jnp.dot(lo, tri, preferred_element_type=F32)
    run = carry_ref[...]
    carries = [None] * nblk
    for b in reversed(range(nblk)):
        carries[b] = run
        run = run + cs[b * rows:(b + 1) * rows, 0:1]
    carry_ref[...] = run
    a = jnp.exp2(zst - cs - jnp.concatenate(carries, axis=0))
    a = jnp.concatenate([a[b * rows:(b + 1) * rows] for b in range(nblk)], axis=1).astype(BF16)
    for h in range(nh):
        acc_ref[h * t:(h + 1) * t, :] += lax.dot_general(a[h * t:(h + 1) * t], cv_ref[h].astype(BF16), NT_DIMS,
                                                         preferred_element_type=F32)

    @pl.when(c == pl.num_programs(1) - 1)
    def _():
        for h in range(nh):
            o_ref[0, h] = acc_ref[h * t:(h + 1) * t, :].astype(BF16)


def _sb_sample(q_s, cache_k, cache_v, layer, kn, vn):
    nb, nh, t, _ = q_s.shape
    past = cache_k.shape[2]
    ch = min(SB_SAMPLE_CHUNK, past)
    nch = past // ch
    ck = cache_k.transpose(0, 1, 3, 4, 2)
    cv = cache_v.transpose(0, 1, 3, 4, 2)
    cache_spec = pl.BlockSpec((None, None, nh, HEAD_DIM, ch), lambda b, c: (layer, b, 0, 0, nch - 1 - c))
    head_spec = pl.BlockSpec((1, nh, t, HEAD_DIM), lambda b, c: (b, 0, 0, 0))
    return pl.pallas_call(
        functools.partial(_sb_sample_kernel, ch=ch, kb=min(SB_TILE, ch)),
        grid=(nb, nch),
        in_specs=[head_spec, cache_spec, cache_spec, head_spec, head_spec],
        out_specs=head_spec,
        out_shape=jax.ShapeDtypeStruct((nb, nh, t, HEAD_DIM), BF16),
        scratch_shapes=[
            pltpu.VMEM((nh * t, 1), F32),
            pltpu.VMEM((nh * t, HEAD_DIM), F32),
        ],
        compiler_params=_cparams(2),
        name="sb_sample",
    )(q_s, ck, cv, kn, vn)


def _tile_col(gain, n, scale):
    return (jnp.tile(gain.astype(F32), n // gain.shape[0]) * scale).reshape(n, 1)


def _heads_major(x, nb, t, nh, hd):
    return x.reshape(nb, t, nh, hd).transpose(0, 2, 1, 3)


def _diff_bias_tiles(rel_bias, tq, t):
    j = jnp.arange(tq)
    valid_d = (j // CHUNK)[:, None] <= (j // CHUNK)[None, :]
    bias_d = _bias_tile(rel_bias, tq, tq, 0, valid_d, True).reshape(DIFF_HEADS, 2, tq, tq)
    bias_p = _bias_tile(rel_bias, NEAR, NEAR, -NEAR, None, True).reshape(DIFF_HEADS, 2, NEAR, NEAR)
    bstats = _bias_stats(rel_bias).reshape(DIFF_HEADS, 2, 2)
    bias_s = jnp.swapaxes(_bias_tile(rel_bias, NEAR + t, t, -NEAR, None, True), 1, 2)
    bias_s = bias_s.reshape(DIFF_HEADS, 2, t, NEAR + t)
    return bias_d, bias_p, bstats, bias_s


def _diff_layer(xp, xs, cache_k, cache_v, layer, n_layers, earlier, bias_tiles, g, w_qkv, q_gain, k_gain, lam_p,
                subln, lam_init):
    bias_d, bias_p, bstats, bias_s = bias_tiles
    n_p = xp.shape[0]
    nb = cache_k.shape[1]
    t = xs.shape[0] // nb
    wt = w_qkv.T.astype(BF16)
    qg = _tile_col(q_gain, D_MODEL, HEAD_DIM ** -0.5 * LOG2E)
    kg = _tile_col(k_gain, D_MODEL, 1.0)
    g = g.reshape(1, D_MODEL)
    kw = dict(nq=D_MODEL, nk=D_MODEL, nv=D_MODEL, head_norm=True)
    tq = min(DIFF_TILE, n_p)
    qT, k32, k16, v32, vT, knorm2 = _project(xp, g, wt, qg, kg, vb=tq, kstat="normsq", slot=layer,
                                             n_slots=n_layers, earlier=earlier, **kw)
    qT_s, k32_s, k16_s, v32_s, _ = _project(xs, g, wt, qg, kg, vb=LANES, **kw)
    k32_s, v32_s = k32_s[0], v32_s[0]

    o_p = _diff_prompt(qT, k16, vT, bias_d, bias_p, knorm2.reshape(DIFF_HEADS, 2, 1), bstats, lam_p,
                       subln.reshape(2 * HEAD_DIM, 1), lam_init)
    hm = functools.partial(_heads_major, nb=nb, t=t, nh=DIFF_HEADS, hd=2 * HEAD_DIM)
    o_s = _diff_sample(hm(qT_s.T), cache_k, cache_v, layer, hm(k16_s), hm(v32_s), bias_s, lam_p,
                       subln.reshape(1, 2 * HEAD_DIM), lam_init)
    o_s = o_s.transpose(0, 2, 1, 3).reshape(nb * t, D_MODEL)
    return (o_p, o_s,
            k32, v32,
            k32_s.reshape(nb, t, DIFF_HEADS, 2 * HEAD_DIM), v32_s.reshape(nb, t, DIFF_HEADS, 2 * HEAD_DIM))


def _swa_layer(xp, xs, ck, cv, rel_bias, g, w_qkv, q_gain, k_gain, sinks):
    n_p = xp.shape[0]
    nb, buf = ck.shape[0], ck.shape[1]
    t = xs.shape[0] // nb
    nkv = SWA_KV_HEADS * HEAD_DIM
    wt = w_qkv.T.astype(BF16)
    qg = _tile_col(q_gain, D_MODEL, HEAD_DIM ** -0.5 * LOG2E)
    kg = _tile_col(k_gain, nkv, 1.0)
    g = g.reshape(1, D_MODEL)
    kw = dict(nq=D_MODEL, nk=nkv, nv=nkv, vb=SWA_TILE, head_norm=True)
    qT, k32, k16, v32, vT = _project(xp, g, wt, qg, kg, **kw)
    qT_s, k32_s, _, v32_s, _ = _project(xs, g, wt, qg, kg, **kw)
    k32, v32, k32_s, v32_s = k32[0], v32[0], k32_s[0], v32_s[0]
    sink_col = (sinks.astype(F32) * LOG2E).reshape(SWA_Q_HEADS, 1, 1)
    wchunks = WINDOW // CHUNK

    tq = SWA_TILE
    kchunk = jnp.floor_divide(jnp.arange(2 * tq) - tq, CHUNK)[:, None]
    qchunk = (jnp.arange(tq) // CHUNK)[None, :]
    valid = (kchunk <= qchunk) & (kchunk >= qchunk - wchunks)
    bias = _bias_tile(rel_bias, 2 * tq, tq, -tq, valid, False)
    bias_g = bias.reshape(SWA_KV_HEADS, SWA_GROUP, 2 * tq, tq).transpose(0, 2, 1, 3)
    bias_g = bias_g.reshape(SWA_KV_HEADS, 2 * tq, SWA_GROUP * tq)
    sink_g = jnp.repeat(sink_col.reshape(SWA_KV_HEADS, 1, SWA_GROUP), tq, axis=2)
    o_p = _swa_prompt(qT, k16, vT, bias_g, sink_g)

    sc = jnp.floor_divide(jnp.arange(buf + t) - buf, CHUNK)[:, None]
    tc = (jnp.arange(t) // CHUNK)[None, :]
    valid_s = (sc <= tc) & (sc >= tc - wchunks)
    bias_s = jnp.swapaxes(_bias_tile(rel_bias, buf + t, t, -buf, valid_s, False), 1, 2)
    q_s = _heads_major(qT_s.T, nb, t, SWA_Q_HEADS, HEAD_DIM)
    kn = _heads_major(k32_s, nb, t, SWA_KV_HEADS, HEAD_DIM)
    vn = _heads_major(v32_s, nb, t, SWA_KV_HEADS, HEAD_DIM)
    o_s, ko, vo = _swa_sample(q_s, ck.transpose(0, 2, 1, 3), cv.transpose(0, 2, 1, 3), kn, vn, bias_s, sink_col)
    o_s = o_s.transpose(0, 2, 1, 3).reshape(nb * t, D_MODEL)
    wbuf = min(WINDOW, n_p)
    return (o_p, o_s,
            k32[n_p - wbuf:].reshape(1, wbuf, SWA_KV_HEADS, HEAD_DIM),
            v32[n_p - wbuf:].reshape(1, wbuf, SWA_KV_HEADS, HEAD_DIM),
            ko.transpose(0, 2, 1, 3), vo.transpose(0, 2, 1, 3))


def _sb_layer(xp, xs, cache_k, cache_v, layer, n_layers, earlier, g, w_qkv):
    n_p = xp.shape[0]
    nb = cache_k.shape[1]
    t = xs.shape[0] // nb
    wt = w_qkv.T.astype(BF16)
    qg = jnp.full((D_MODEL, 1), HEAD_DIM ** -0.5 * LOG2E, F32)
    kg = jnp.ones((D_MODEL, 1), F32)
    g = g.reshape(1, D_MODEL)
    kw = dict(nq=D_MODEL, nk=D_MODEL, nv=D_MODEL, head_norm=False)
    tq = min(SB_TILE, n_p)
    qT, k32, k16, v32, vT, kmax = _project(xp, g, wt, qg, kg, vb=tq, kstat="absmax", slot=layer,
                                           n_slots=n_layers, earlier=earlier, **kw)
    qT_s, k32_s, k16_s, v32_s, _ = _project(xs, g, wt, qg, kg, vb=LANES, **kw)
    k32_s, v32_s = k32_s[0], v32_s[0]
    o_p = _sb_prompt(qT, k16, vT, kmax)
    hm = functools.partial(_heads_major, nb=nb, t=t, nh=SB_HEADS, hd=HEAD_DIM)
    o_s = _sb_sample(hm(qT_s.T), cache_k, cache_v, layer, hm(k16_s), hm(v32_s))
    o_s = o_s.transpose(0, 2, 1, 3).reshape(nb * t, D_MODEL)
    return (o_p, o_s,
            k32, v32,
            k32_s.reshape(nb, t, SB_HEADS, HEAD_DIM), v32_s.reshape(nb, t, SB_HEADS, HEAD_DIM))


def kernel(x_prompt, x_sample, cache_diff_k, cache_diff_v, cache_swa_k, cache_swa_v, cache_sb_k, cache_sb_v, rel_bias, norm_mix, norm_mlp, w_up, w_down, diff_w_qkv, diff_w_o, diff_q_norm, diff_k_norm, diff_lambda, diff_subln, swa_w_qkv, swa_w_o, swa_q_norm, swa_k_norm, swa_sinks, sb_w_qkv, sb_w_o):
    bp, n_p, _ = x_prompt.shape
    assert bp == 1
    nb, t, _ = x_sample.shape
    depth = norm_mix.shape[0]
    xp = x_prompt.reshape(n_p, D_MODEL)
    xs = x_sample.reshape(nb * t, D_MODEL)
    n_diff, n_sb = cache_diff_k.shape[0], cache_sb_k.shape[0]
    outs = {name: [] for name in ("pwk", "pwv", "sdk", "sdv", "swk", "swv", "sbk", "sbv")}
    diff_kv = sb_kv = None
    diff_bias = _diff_bias_tiles(rel_bias, min(DIFF_TILE, n_p), t)
    wup16 = w_up.astype(BF16)
    wdn16 = w_down.astype(BF16)
    for i in range(depth):
        j = i // N_MIXERS
        if i % N_MIXERS == 0:
            lam_init = 0.8 - 0.6 * math.exp(-0.3 * i)
            o_p, o_s, kp, vp, kn, vn = _diff_layer(
                xp, xs, cache_diff_k, cache_diff_v, j, n_diff, diff_kv, diff_bias, norm_mix[i], diff_w_qkv[j],
                diff_q_norm[j], diff_k_norm[j], diff_lambda[j], diff_subln[j], lam_init)
            diff_kv = (kp, vp)
            w_o = diff_w_o[j]
            names = ("sdk", "sdv")
        elif i % N_MIXERS == 1:
            o_p, o_s, kp, vp, kn, vn = _swa_layer(
                xp, xs, cache_swa_k[j], cache_swa_v[j], rel_bias, norm_mix[i], swa_w_qkv[j],
                swa_q_norm[j], swa_k_norm[j], swa_sinks[j])
            outs["pwk"].append(kp)
            outs["pwv"].append(vp)
            w_o = swa_w_o[j]
            names = ("swk", "swv")
        else:
            o_p, o_s, kp, vp, kn, vn = _sb_layer(xp, xs, cache_sb_k, cache_sb_v, j, n_sb, sb_kv, norm_mix[i],
                                                 sb_w_qkv[j])
            sb_kv = (kp, vp)
            w_o = sb_w_o[j]
            names = ("sbk", "sbv")
        for name, val in zip(names, (kn, vn)):
            outs[name].append(val)
        wo16 = w_o.astype(BF16)
        g_mlp = norm_mlp[i].reshape(1, D_MODEL)
        xp = _outproj_mlp(xp, o_p, wo16, g_mlp, wup16, wdn16, i)
        xs = _outproj_mlp(xs, o_s, wo16, g_mlp, wup16, wdn16, i)
    st = {name: jnp.stack(v) for name, v in outs.items()}
    diff_shape = (n_diff, 1, n_p, DIFF_HEADS, 2 * HEAD_DIM)
    sb_shape = (n_sb, 1, n_p, SB_HEADS, HEAD_DIM)
    return (xp.reshape(1, n_p, D_MODEL), xs.reshape(nb, t, D_MODEL),
            diff_kv[0].reshape(diff_shape), diff_kv[1].reshape(diff_shape), st["pwk"], st["pwv"],
            sb_kv[0].reshape(sb_shape), sb_kv[1].reshape(sb_shape),
            st["sdk"], st["sdv"], st["swk"], st["swv"], st["sbk"], st["sbv"])
```

```python
import functools
import math

import jax
import jax.numpy as jnp
from jax import lax
from jax.experimental import pallas as pl
from jax.experimental.pallas import tpu as pltpu

F32 = jnp.float32
BF16 = jnp.bfloat16

D_MODEL = 1024
HEAD_DIM = 64
CHUNK = 64
N_MIXERS = 3
DIFF_HEADS = 8
SWA_Q_HEADS = 16
SWA_KV_HEADS = 4
SWA_GROUP = 4
WINDOW = 128
SB_HEADS = 16
N_BUCKETS = 32
MAX_DISTANCE = 128
D_FF = 4 * D_MODEL
EPS = 1e-6
LOG2E = 1.4426950408889634
NEG = -1e30
LANES = 128
NEAR = MAX_DISTANCE
FAR_BUCKET = 15
SB_SKIP_MARGIN = 145.0
SOFTMAX_BOUND_LIMIT = 100.0

V7X_VMEM_BYTES = 64 * 1024 * 1024
VMEM_LIMIT = V7X_VMEM_BYTES * 7 // 8
TOKEN_TILE = 512
PROJ_CHUNK = 512
DIFF_TILE = 512
DIFF_FAR_GROUP = 8
SB_TILE = 256
SB_PAIRS_PER_STEP = 2
SB_SAMPLE_CHUNK = 512
SWA_TILE = 128

NT_DIMS = (((1,), (1,)), ((), ()))


def _cparams(n_axes):
    return pltpu.CompilerParams(dimension_semantics=("arbitrary",) * n_axes,
                                vmem_limit_bytes=VMEM_LIMIT)


def _resident(block_shape, index_map):
    return pl.BlockSpec(block_shape, index_map, pipeline_mode=pl.Buffered(1))


def _t5_bucket(rel):
    half = N_BUCKETS // 2
    exact = half // 2
    ret = jnp.where(rel > 0, half, 0).astype(jnp.int32)
    n = jnp.abs(rel)
    nf = jnp.maximum(n, 1).astype(F32)
    large = exact + (jnp.log(nf / exact) / math.log(MAX_DISTANCE / exact) * (half - exact)).astype(jnp.int32)
    large = jnp.minimum(large, half - 1)
    return ret + jnp.where(n < exact, n, large)


def _bias_tile(rel_bias, n_j, n_i, r0, valid, shift_far):
    tab = rel_bias.astype(F32)
    period = -(-(n_i + n_j - 1) // LANES) * LANES
    u = jnp.arange(period)
    rel = jnp.where(u < n_i, r0 - u, r0 + (period - u))
    vals = jnp.moveaxis(tab[_t5_bucket(rel)], -1, 0)
    if shift_far:
        vals = vals - tab[FAR_BUCKET][:, None]
    vals = (vals * LOG2E).reshape(vals.shape[0], 1, period)

    def expand(y_ref, o_ref):
        rows = jnp.broadcast_to(y_ref[0], (n_j, period))
        o_ref[0] = pltpu.roll(rows, 0, 1, stride=1, stride_axis=0)[:, :n_i]

    b = pl.pallas_call(
        expand,
        grid=(vals.shape[0],),
        in_specs=[pl.BlockSpec((1, 1, period), lambda c: (c, 0, 0))],
        out_specs=pl.BlockSpec((1, n_j, n_i), lambda c: (c, 0, 0)),
        out_shape=jax.ShapeDtypeStruct((vals.shape[0], n_j, n_i), F32),
        compiler_params=_cparams(1),
        name="bias_toeplitz",
    )(vals)
    if valid is not None:
        b = jnp.where(valid[None], b, NEG)
    return b


def _bias_stats(rel_bias):
    tab = rel_bias.astype(F32)
    sh = (tab - tab[FAR_BUCKET][None, :]) * LOG2E
    bmax = jnp.max(sh, axis=0)
    return jnp.stack([bmax, bmax - jnp.min(sh, axis=0)], axis=-1)


def _proj_kernel(x_ref, g_ref, wt_ref, qg_ref, kg_ref, *refs, nq, nk, nv, vb, head_norm, kstat, n_alias,
                 slot, all_slots, kv_feature_major):
    out_refs = refs[n_alias:]
    qT_ref, k32_ref, k16_ref, v32_ref, vT_ref = out_refs[:5]
    if all_slots > 1:
        for s in range(all_slots):
            if s != slot:
                k32_ref[s] = jnp.zeros(k32_ref.shape[1:], F32)
                v32_ref[s] = jnp.zeros(v32_ref.shape[1:], F32)
        k32_ref = k32_ref.at[slot]
        v32_ref = v32_ref.at[slot]
    x = x_ref[...]
    ms = jnp.mean(x * x, axis=-1, keepdims=True)
    h = (x * lax.rsqrt(ms + EPS) * g_ref[...]).astype(BF16)
    tm = x.shape[0]

    def rows(r0, n):
        return lax.dot_general(wt_ref[r0:r0 + n, :], h, NT_DIMS, preferred_element_type=F32)

    def headnorm(t, gcol):
        n = t.shape[0]
        t3 = t.reshape(n // HEAD_DIM, HEAD_DIM, tm)
        r = lax.rsqrt(jnp.mean(t3 * t3, axis=1, keepdims=True) + EPS)
        return (t3 * r).reshape(n, tm) * gcol

    for r0 in range(0, nq, PROJ_CHUNK):
        n = min(PROJ_CHUNK, nq - r0)
        qt = rows(r0, n)
        qt = headnorm(qt, qg_ref[r0:r0 + n, :]) if head_norm else qt * qg_ref[r0:r0 + n, :]
        qT_ref[r0:r0 + n, :] = qt.astype(BF16)

    for r0 in range(0, nk, PROJ_CHUNK):
        n = min(PROJ_CHUNK, nk - r0)
        kt = rows(nq + r0, n)
        if head_norm:
            kt = headnorm(kt, kg_ref[r0:r0 + n, :])
        k = kt.T
        if kv_feature_major:
            k32_ref[r0:r0 + n, :] = kt
        else:
            k32_ref[:, r0:r0 + n] = k
        k16_ref[:, r0:r0 + n] = k.astype(BF16)
        if kstat is not None:
            stat_ref = out_refs[5]
            kr = kt.astype(BF16).astype(F32)
            if kstat == "absmax":
                stat = jnp.max(jnp.abs(kr), axis=1, keepdims=True)
                srows = slice(r0, r0 + n)
            else:
                k3 = kr.reshape(n // HEAD_DIM, HEAD_DIM, tm)
                stat = jnp.max(jnp.sum(k3 * k3, axis=1), axis=1, keepdims=True)
                srows = slice(r0 // HEAD_DIM, (r0 + n) // HEAD_DIM)

            @pl.when(pl.program_id(0) == 0)
            def _(stat=stat, srows=srows):
                stat_ref[srows, :] = stat

            @pl.when(pl.program_id(0) > 0)
            def _(stat=stat, srows=srows):
                stat_ref[srows, :] = jnp.maximum(stat_ref[srows, :], stat)

    for r0 in range(0, nv, PROJ_CHUNK):
        n = min(PROJ_CHUNK, nv - r0)
        vt = rows(nq + nk + r0, n)
        if kv_feature_major:
            v32_ref[r0:r0 + n, :] = vt
        else:
            v32_ref[:, r0:r0 + n] = vt.T
        vt16 = vt.astype(BF16)
        for b in range(tm // vb):
            vT_ref[b, r0:r0 + n, :] = vt16[:, b * vb:(b + 1) * vb]


def _project(x, g, wt, qg, kg, *, nq, nk, nv, vb, head_norm, kstat=None, slot=0, n_slots=1, earlier=None,
             kv_feature_major=False):
    n_tok = x.shape[0]
    tm = min(TOKEN_TILE, n_tok)
    vb = min(vb, tm)
    grid = (n_tok // tm,)
    ntot = nq + nk + nv

    def kv_shape(width):
        return (n_slots, width, n_tok) if kv_feature_major else (n_slots, n_tok, width)

    out_shape = [
        jax.ShapeDtypeStruct((nq, n_tok), BF16),
        jax.ShapeDtypeStruct(kv_shape(nk), F32),
        jax.ShapeDtypeStruct((n_tok, nk), BF16),
        jax.ShapeDtypeStruct(kv_shape(nv), F32),
        jax.ShapeDtypeStruct((n_tok // vb, nv, vb), BF16),
    ]
    all_slots = n_slots if (earlier is None and n_slots > 1) else 1

    def kv_spec(width):
        lead = n_slots if all_slots > 1 else None
        first = 0 if all_slots > 1 else slot
        if kv_feature_major:
            return pl.BlockSpec((lead, width, tm), lambda i: (first, 0, i))
        return pl.BlockSpec((lead, tm, width), lambda i: (first, i, 0))

    out_specs = [
        pl.BlockSpec((nq, tm), lambda i: (0, i)),
        kv_spec(nk),
        pl.BlockSpec((tm, nk), lambda i: (i, 0)),
        kv_spec(nv),
        pl.BlockSpec((tm // vb, nv, vb), lambda i: (i, 0, 0)),
    ]
    if kstat is not None:
        n_stat = nk if kstat == "absmax" else nk // HEAD_DIM
        out_shape.append(jax.ShapeDtypeStruct((n_stat, 1), F32))
        out_specs.append(pl.BlockSpec((n_stat, 1), lambda i: (0, 0)))
    in_specs = [
        pl.BlockSpec((tm, D_MODEL), lambda i: (i, 0)),
        _resident((1, D_MODEL), lambda i: (0, 0)),
        _resident((ntot, D_MODEL), lambda i: (0, 0)),
        _resident((nq, 1), lambda i: (0, 0)),
        _resident((nk, 1), lambda i: (0, 0)),
    ]
    args = [x, g, wt, qg, kg]
    aliases = {}
    if earlier is not None:
        aliases = {len(args): 1, len(args) + 1: 3}
        in_specs += [pl.BlockSpec(memory_space=pl.ANY)] * 2
        args += list(earlier)
    return pl.pallas_call(
        functools.partial(_proj_kernel, nq=nq, nk=nk, nv=nv, vb=vb, head_norm=head_norm, kstat=kstat,
                          n_alias=len(aliases), slot=slot, all_slots=all_slots,
                          kv_feature_major=kv_feature_major),
        grid=grid,
        in_specs=in_specs,
        out_specs=out_specs,
        out_shape=out_shape,
        input_output_aliases=aliases,
        compiler_params=_cparams(1),
        name="proj",
    )(*args)


def _mlp_kernel(x_ref, o_ref, wo_ref, g_ref, wup_ref, wdn_ref, y_ref, *, f_chunk):
    x1 = x_ref[...] + jnp.dot(o_ref[...], wo_ref[...], preferred_element_type=F32)
    ms = jnp.mean(x1 * x1, axis=-1, keepdims=True)
    hn = (x1 * lax.rsqrt(ms + EPS) * g_ref[...]).astype(BF16)
    acc = x1
    for f in range(D_FF // f_chunk):
        u = jnp.dot(hn, wup_ref[:, f * f_chunk:(f + 1) * f_chunk], preferred_element_type=F32)
        a = jnp.square(jnp.maximum(u, 0.0)).astype(BF16)
        acc = acc + jnp.dot(a, wdn_ref[f * f_chunk:(f + 1) * f_chunk, :], preferred_element_type=F32)
    y_ref[...] = acc


def _outproj_mlp(x, o, wo, g, wup, wdn, layer):
    n_tok = x.shape[0]
    tm = min(TOKEN_TILE, n_tok)
    return pl.pallas_call(
        functools.partial(_mlp_kernel, f_chunk=1024),
        grid=(n_tok // tm,),
        in_specs=[
            pl.BlockSpec((tm, D_MODEL), lambda i: (i, 0)),
            pl.BlockSpec((tm, D_MODEL), lambda i: (i, 0)),
            _resident((D_MODEL, D_MODEL), lambda i: (0, 0)),
            _resident((1, D_MODEL), lambda i: (0, 0)),
            _resident((None, D_MODEL, D_FF), lambda i: (layer, 0, 0)),
            _resident((None, D_FF, D_MODEL), lambda i: (layer, 0, 0)),
        ],
        out_specs=pl.BlockSpec((tm, D_MODEL), lambda i: (i, 0)),
        out_shape=jax.ShapeDtypeStruct((n_tok, D_MODEL), F32),
        compiler_params=_cparams(1),
        name="outproj_mlp",
    )(x, o, wo, g, wup, wdn)


def _diff_lambda(lam_ref, lam_init):
    lp = lam_ref[...]
    a = jnp.sum(lp[0:1] * lp[1:2], axis=-1, keepdims=True)
    b = jnp.sum(lp[2:3] * lp[3:4], axis=-1, keepdims=True)
    return jnp.exp(a) - jnp.exp(b) + lam_init


def _diff_prompt_kernel(qT_ref, k_ref, vT_ref, bd_ref, bp_ref, kn_ref, bs_ref, lam_ref, sub_ref, o_ref,
                        m_ref, l_ref, acc_ref, *, tq, lam_init):
    qi = pl.program_id(1)
    qT = qT_ref[...]
    row = lax.broadcasted_iota(jnp.int32, qT.shape, 0)
    zero = jnp.zeros_like(qT)
    qm = (jnp.where(row < HEAD_DIM, qT, zero), jnp.where(row >= HEAD_DIM, qT, zero))

    def prev_bias(m, s):
        corner = s[tq - NEAR:, :NEAR] + bp_ref[0, m]
        bottom = jnp.concatenate([corner, s[tq - NEAR:, NEAR:]], axis=1)
        return jnp.concatenate([s[:tq - NEAR], bottom], axis=0)

    def diag_bias(m, s):
        return s + bd_ref[0, m]

    def prev_diag_bias(m, s):
        return jnp.concatenate([prev_bias(m, s[:tq]), diag_bias(m, s[tq:])], axis=0)

    n_far = jnp.maximum(qi - 1, 0)

    def walk(tile_fn):
        def far_body(j, carry):
            tile_fn(j, None)
            return carry

        lax.fori_loop(0, n_far, far_body, 0)

        @pl.when(qi >= 1)
        def _():
            tile_fn(qi - 1, prev_bias)

        tile_fn(qi, diag_bias)

    span = []
    for m in range(2):
        qf = qm[m].astype(F32)
        qn = jnp.sqrt(jnp.sum(qf * qf, axis=0, keepdims=True))
        reach = qn * (jnp.sqrt(kn_ref[m:m + 1, :]) * 1.001)
        m_ref[m] = reach + bs_ref[m:m + 1, 0:1]
        span.append(jnp.max(2.0 * reach + bs_ref[m:m + 1, 1:2]))
    bound_is_tight = jnp.maximum(span[0], span[1]) <= SOFTMAX_BOUND_LIMIT

    @pl.when(jnp.logical_not(bound_is_tight))
    def _():
        m_ref[...] = jnp.full(m_ref.shape, NEG, F32)

        def max_tile(j, bias_fn):
            k = k_ref[j]
            for m in range(2):
                s = jnp.dot(k, qm[m], preferred_element_type=F32)
                if bias_fn is not None:
                    s = bias_fn(m, s)
                m_ref[m] = jnp.maximum(m_ref[m], jnp.max(s, axis=0, keepdims=True))

        walk(max_tile)

    l_ref[...] = jnp.zeros(l_ref.shape, F32)
    acc_ref[...] = jnp.zeros(acc_ref.shape, F32)

    def acc_tiles(js, bias_fn):
        k = jnp.concatenate([k_ref[j] for j in js], axis=0)
        vT = jnp.concatenate([vT_ref[j] for j in js], axis=1)
        nk = len(js) * tq
        for m in range(2):
            s = jnp.dot(k, qm[m], preferred_element_type=F32)
            if bias_fn is not None:
                s = bias_fn(m, s)
            p = jnp.exp2(s - m_ref[m])
            l_ref[m] += jnp.sum(p.reshape(nk // 8, 8, tq), axis=0)
            acc_ref[m] += jnp.dot(vT, p.astype(BF16), preferred_element_type=F32)

    def far_group(jj, carry):
        acc_tiles([DIFF_FAR_GROUP * jj + u for u in range(DIFF_FAR_GROUP)], None)
        return carry

    n_groups = n_far // DIFF_FAR_GROUP
    lax.fori_loop(0, n_groups, far_group, 0)
    start = n_groups * DIFF_FAR_GROUP
    left = n_far - start
    size = DIFF_FAR_GROUP // 2
    while size >= 2:
        @pl.when((left & size) != 0)
        def _(start=start, size=size):
            acc_tiles([start + u for u in range(size)], None)

        start = start + (left & size)
        size //= 2
    odd = (left & 1) != 0

    def plain_prev_diag_bias(m, s):
        return jnp.concatenate([s[:tq], prev_diag_bias(m, s[tq:])], axis=0)

    @pl.when(odd)
    def _():
        acc_tiles([start, qi - 1, qi], plain_prev_diag_bias)

    @pl.when(jnp.logical_and(qi >= 1, jnp.logical_not(odd)))
    def _():
        acc_tiles([qi - 1, qi], prev_diag_bias)

    @pl.when(qi == 0)
    def _():
        acc_tiles([qi], diag_bias)

    lam = _diff_lambda(lam_ref, lam_init)
    o0 = acc_ref[0] * (1.0 / jnp.sum(l_ref[0], axis=0, keepdims=True))
    o1 = acc_ref[1] * (1.0 / jnp.sum(l_ref[1], axis=0, keepdims=True))
    o = o0 - lam * o1
    ms = jnp.mean(o * o, axis=0, keepdims=True)
    o = o * lax.rsqrt(ms + EPS) * sub_ref[...] * (1.0 - lam_init)
    o_ref[...] = o.T.astype(BF16)


def _diff_prompt(qT, k16, vT, bias_d, bias_p, knorm2, bstats, lam_p, subln, lam_init):
    n_tok = k16.shape[0]
    tq = min(DIFF_TILE, n_tok)
    nb = n_tok // tq
    k3 = k16.reshape(nb, tq, D_MODEL)
    return pl.pallas_call(
        functools.partial(_diff_prompt_kernel, tq=tq, lam_init=lam_init),
        grid=(DIFF_HEADS, nb),
        in_specs=[
            pl.BlockSpec((2 * HEAD_DIM, tq), lambda h, i: (h, i)),
            pl.BlockSpec((nb, tq, 2 * HEAD_DIM), lambda h, i: (0, 0, h)),
            pl.BlockSpec((nb, 2 * HEAD_DIM, tq), lambda h, i: (0, h, 0)),
            pl.BlockSpec((1, 2, tq, tq), lambda h, i: (h, 0, 0, 0)),
            pl.BlockSpec((1, 2, NEAR, NEAR), lambda h, i: (h, 0, 0, 0)),
            pl.BlockSpec((None, 2, 1), lambda h, i: (h, 0, 0)),
            pl.BlockSpec((None, 2, 2), lambda h, i: (h, 0, 0)),
            pl.BlockSpec((4, HEAD_DIM), lambda h, i: (0, 0)),
            pl.BlockSpec((2 * HEAD_DIM, 1), lambda h, i: (0, 0)),
        ],
        out_specs=pl.BlockSpec((tq, 2 * HEAD_DIM), lambda h, i: (i, h)),
        out_shape=jax.ShapeDtypeStruct((n_tok, D_MODEL), BF16),
        scratch_shapes=[
            pltpu.VMEM((2, 1, tq), F32),
            pltpu.VMEM((2, 8, tq), F32),
            pltpu.VMEM((2, 2 * HEAD_DIM, tq), F32),
        ],
        compiler_params=_cparams(2),
        name="diff_prompt",
    )(qT, k3, vT, bias_d, bias_p, knorm2, bstats, lam_p, subln)


def _diff_sample_kernel(q_ref, ck_ref, cv_ref, kn_ref, vn_ref, b_ref, lam_ref, sub_ref, o_ref, *, lam_init, past):
    t = q_ref.shape[2]
    lane = lax.broadcasted_iota(jnp.int32, (t, 2 * HEAD_DIM), 1)
    lam = _diff_lambda(lam_ref, lam_init)
    for h in range(DIFF_HEADS):
        q = q_ref[0, h]
        zero = jnp.zeros_like(q)
        qq = jnp.concatenate([jnp.where(lane < HEAD_DIM, q, zero), jnp.where(lane >= HEAD_DIM, q, zero)], axis=0)
        ck = ck_ref[pl.ds(h, past, stride=DIFF_HEADS), :].astype(BF16)
        cv = cv_ref[pl.ds(h, past, stride=DIFF_HEADS), :].astype(BF16)
        kn = kn_ref[0, h]
        vn = vn_ref[0, h].astype(BF16)
        s_c = lax.dot_general(qq, ck, NT_DIMS, preferred_element_type=F32)
        s_n = lax.dot_general(qq, kn, NT_DIMS, preferred_element_type=F32)
        bias = b_ref[h].reshape(2 * t, NEAR + t)
        s_far = s_c[:, :past - NEAR]
        s_near = s_c[:, past - NEAR:] + bias[:, :NEAR]
        s_n = s_n + bias[:, NEAR:]
        m = jnp.maximum(jnp.maximum(jnp.max(s_far, axis=1, keepdims=True), jnp.max(s_near, axis=1, keepdims=True)),
                        jnp.max(s_n, axis=1, keepdims=True))
        p_far = jnp.exp2(s_far - m)
        p_near = jnp.exp2(s_near - m)
        p_n = jnp.exp2(s_n - m)
        l = (jnp.sum(p_far, axis=1, keepdims=True) + jnp.sum(p_near, axis=1, keepdims=True)
             + jnp.sum(p_n, axis=1, keepdims=True))
        o = (jnp.dot(p_far.astype(BF16), cv[:past - NEAR], preferred_element_type=F32)
             + jnp.dot(p_near.astype(BF16), cv[past - NEAR:], preferred_element_type=F32)
             + jnp.dot(p_n.astype(BF16), vn, preferred_element_type=F32))
        o = o * (1.0 / l)
        od = o[:t] - lam * o[t:]
        ms = jnp.mean(od * od, axis=-1, keepdims=True)
        od = od * lax.rsqrt(ms + EPS) * sub_ref[...] * (1.0 - lam_init)
        o_ref[0, h] = od.astype(BF16)


def _diff_sample(q_s, cache_k, cache_v, layer, kn, vn, bias_s, lam_p, subln_row, lam_init):
    nb, _, t, _ = q_s.shape
    past = cache_k.shape[2]
    rows = past * DIFF_HEADS
    ck = cache_k.reshape(cache_k.shape[0], nb, rows, 2 * HEAD_DIM)
    cv = cache_v.reshape(cache_v.shape[0], nb, rows, 2 * HEAD_DIM)
    cache_spec = pl.BlockSpec((None, None, rows, 2 * HEAD_DIM), lambda b: (layer, b, 0, 0))
    head_spec = pl.BlockSpec((1, DIFF_HEADS, t, 2 * HEAD_DIM), lambda b: (b, 0, 0, 0))
    return pl.pallas_call(
        functools.partial(_diff_sample_kernel, lam_init=lam_init, past=past),
        grid=(nb,),
        in_specs=[
            head_spec, cache_spec, cache_spec, head_spec, head_spec,
            pl.BlockSpec((DIFF_HEADS, 2, t, NEAR + t), lambda b: (0, 0, 0, 0)),
            pl.BlockSpec((4, HEAD_DIM), lambda b: (0, 0)),
            pl.BlockSpec((1, 2 * HEAD_DIM), lambda b: (0, 0)),
        ],
        out_specs=head_spec,
        out_shape=jax.ShapeDtypeStruct((nb, DIFF_HEADS, t, 2 * HEAD_DIM), BF16),
        compiler_params=_cparams(1),
        name="diff_sample",
    )(q_s, ck, cv, kn, vn, bias_s, lam_p, subln_row)


def _swa_prompt_kernel(qT_ref, kc_ref, kp_ref, vc_ref, vp_ref, b_ref, sink_ref, o_ref):
    i = pl.program_id(0)
    qT = qT_ref[...]
    kp = kp_ref[...]
    kc = kc_ref[...]
    vp = vp_ref[0]
    vc = vc_ref[0]
    tq = qT.shape[1]
    key_row = lax.broadcasted_iota(jnp.int32, (2 * tq, SWA_GROUP * tq), 0)
    has_prev = key_row >= jnp.where(i > 0, 0, tq)
    zeros = jnp.zeros((HEAD_DIM, tq), BF16)
    outs = []
    for kv in range(SWA_KV_HEADS):
        pair = kv // 2
        lanes = slice(pair * 2 * HEAD_DIM, (pair + 1) * 2 * HEAD_DIM)
        k_band = jnp.concatenate([kp[:, lanes], kc[:, lanes]], axis=0)
        v_band = jnp.concatenate([vp[kv * HEAD_DIM:(kv + 1) * HEAD_DIM], vc[kv * HEAD_DIM:(kv + 1) * HEAD_DIM]],
                                 axis=1)
        qg = []
        for g in range(SWA_GROUP):
            h = kv * SWA_GROUP + g
            qh = qT[h * HEAD_DIM:(h + 1) * HEAD_DIM]
            qg.append(jnp.concatenate([qh, zeros] if kv % 2 == 0 else [zeros, qh], axis=0))
        q4 = jnp.concatenate(qg, axis=1)
        s = jnp.dot(k_band, q4, preferred_element_type=F32) + b_ref[kv]
        s = jnp.where(has_prev, s, NEG)
        sink = sink_ref[kv]
        m = jnp.maximum(jnp.max(s, axis=0, keepdims=True), sink)
        p = jnp.exp2(s - m)
        den = jnp.sum(p, axis=0, keepdims=True) + jnp.exp2(sink - m)
        oT4 = jnp.dot(v_band, p.astype(BF16), preferred_element_type=F32) * (1.0 / den)
        for g in range(SWA_GROUP):
            outs.append(oT4[:, g * tq:(g + 1) * tq])
    o_ref[...] = jnp.concatenate(outs, axis=0).T.astype(BF16)


def _swa_prompt(qT, k16, vT, bias, sinks):
    n_tok = k16.shape[0]
    tq = SWA_TILE
    nkv = SWA_KV_HEADS * HEAD_DIM
    return pl.pallas_call(
        _swa_prompt_kernel,
        grid=(n_tok // tq,),
        in_specs=[
            pl.BlockSpec((D_MODEL, tq), lambda i: (0, i)),
            pl.BlockSpec((tq, nkv), lambda i: (i, 0)),
            pl.BlockSpec((tq, nkv), lambda i: (jnp.maximum(i - 1, 0), 0)),
            pl.BlockSpec((1, nkv, tq), lambda i: (i, 0, 0)),
            pl.BlockSpec((1, nkv, tq), lambda i: (jnp.maximum(i - 1, 0), 0, 0)),
            _resident((SWA_KV_HEADS, 2 * tq, SWA_GROUP * tq), lambda i: (0, 0, 0)),
            _resident((SWA_KV_HEADS, 1, SWA_GROUP * tq), lambda i: (0, 0, 0)),
        ],
        out_specs=pl.BlockSpec((tq, D_MODEL), lambda i: (i, 0)),
        out_shape=jax.ShapeDtypeStruct((n_tok, D_MODEL), BF16),
        compiler_params=_cparams(1),
        name="swa_prompt",
    )(qT, k16, k16, vT, vT, bias, sinks)


def _swa_sample_kernel(q_ref, kc_ref, vc_ref, kn_ref, vn_ref, b_ref, sink_ref, o_ref, ko_ref, vo_ref):
    t = kn_ref.shape[2]
    buf = kc_ref.shape[2]
    for kv in range(SWA_KV_HEADS):
        kc32 = kc_ref[0, kv]
        vc32 = vc_ref[0, kv]
        kn32 = kn_ref[0, kv]
        vn32 = vn_ref[0, kv]
        ko_ref[0, kv, 0:buf - t, :] = kc32[t:]
        ko_ref[0, kv, buf - t:buf, :] = kn32
        vo_ref[0, kv, 0:buf - t, :] = vc32[t:]
        vo_ref[0, kv, buf - t:buf, :] = vn32
        kc = kc32.astype(BF16)
        vc = vc32.astype(BF16)
        kn = kn32.astype(BF16)
        vn = vn32.astype(BF16)
        heads = range(kv * SWA_GROUP, (kv + 1) * SWA_GROUP)
        q = jnp.concatenate([q_ref[0, h] for h in heads], axis=0)
        b = jnp.concatenate([b_ref[h] for h in heads], axis=0)
        sink = jnp.concatenate([jnp.broadcast_to(sink_ref[h], (t, 1)) for h in heads], axis=0)
        s_c = lax.dot_general(q, kc, NT_DIMS, preferred_element_type=F32) + b[:, :buf]
        s_n = lax.dot_general(q, kn, NT_DIMS, preferred_element_type=F32) + b[:, buf:]
        m = jnp.maximum(jnp.maximum(jnp.max(s_c, axis=1, keepdims=True), jnp.max(s_n, axis=1, keepdims=True)),
                        sink)
        p_c = jnp.exp2(s_c - m)
        p_n = jnp.exp2(s_n - m)
        den = (jnp.sum(p_c, axis=1, keepdims=True) + jnp.sum(p_n, axis=1, keepdims=True)
               + jnp.exp2(sink - m))
        o = (jnp.dot(p_c.astype(BF16), vc, preferred_element_type=F32)
             + jnp.dot(p_n.astype(BF16), vn, preferred_element_type=F32)) * (1.0 / den)
        for g, h in enumerate(heads):
            o_ref[0, h] = o[g * t:(g + 1) * t].astype(BF16)


def _swa_sample(q_s, kc, vc, kn, vn, bias, sinks):
    nb, _, t, _ = q_s.shape
    buf = kc.shape[2]
    kvspec = pl.BlockSpec((1, SWA_KV_HEADS, buf, HEAD_DIM), lambda b: (b, 0, 0, 0))
    nspec = pl.BlockSpec((1, SWA_KV_HEADS, t, HEAD_DIM), lambda b: (b, 0, 0, 0))
    return pl.pallas_call(
        _swa_sample_kernel,
        grid=(nb,),
        in_specs=[
            pl.BlockSpec((1, SWA_Q_HEADS, t, HEAD_DIM), lambda b: (b, 0, 0, 0)),
            kvspec, kvspec, nspec, nspec,
            pl.BlockSpec((SWA_Q_HEADS, t, buf + t), lambda b: (0, 0, 0)),
            pl.BlockSpec((SWA_Q_HEADS, 1, 1), lambda b: (0, 0, 0)),
        ],
        out_specs=[
            pl.BlockSpec((1, SWA_Q_HEADS, t, HEAD_DIM), lambda b: (b, 0, 0, 0)),
            kvspec, kvspec,
        ],
        out_shape=[
            jax.ShapeDtypeStruct((nb, SWA_Q_HEADS, t, HEAD_DIM), BF16),
            jax.ShapeDtypeStruct((nb, SWA_KV_HEADS, buf, HEAD_DIM), F32),
            jax.ShapeDtypeStruct((nb, SWA_KV_HEADS, buf, HEAD_DIM), F32),
        ],
        compiler_params=_cparams(1),
        name="swa_sample",
    )(q_s, kc, vc, kn, vn, bias, sinks)


def _softplus(z):
    return jnp.maximum(z, 0.0) + jnp.log2(1.0 + jnp.exp2(-jnp.abs(z)))


def _split_bf16(x):
    hi = x.astype(BF16)
    lo = (x - hi.astype(F32)).astype(BF16)
    return hi, lo


def _sb_prompt_kernel(qT_ref, k_ref, vT_ref, kmax_ref, o_ref, carry_ref, acc_ref, *, tq, pairs):
    qi = pl.program_id(1)
    width = 2 * pairs * tq
    row = lax.broadcasted_iota(jnp.int32, (2 * HEAD_DIM, tq), 0)
    qpairs, zbs = [], []
    for pr in range(pairs):
        qT = qT_ref[2 * HEAD_DIM * pr:2 * HEAD_DIM * (pr + 1), :]
        zero = jnp.zeros_like(qT)
        qp = jnp.concatenate([jnp.where(row < HEAD_DIM, qT, zero), jnp.where(row >= HEAD_DIM, qT, zero)], axis=1)
        qpairs.append(qp)
        kmax = kmax_ref[2 * HEAD_DIM * pr:2 * HEAD_DIM * (pr + 1), :]
        zbs.append(jnp.sum(jnp.abs(qp.astype(F32)) * kmax, axis=0, keepdims=True))
    zb = jnp.concatenate(zbs, axis=1)
    kr = lax.broadcasted_iota(jnp.int32, (tq, tq), 0)
    kc = lax.broadcasted_iota(jnp.int32, (tq, tq), 1)
    tri = jnp.where(kc >= kr, 1.0, 0.0).astype(BF16)
    tri2 = jnp.concatenate([tri, tri], axis=1)
    krw = lax.broadcasted_iota(jnp.int32, (tq, width), 0)
    kcw = lax.broadcasted_iota(jnp.int32, (tq, width), 1)
    causal = krw < lax.rem(kcw, tq)
    carry_ref[...] = jnp.zeros(carry_ref.shape, F32)
    acc_ref[...] = jnp.zeros(acc_ref.shape, F32)

    def tiles(js, first_is_diag):
        zs, cs = [], []
        for n, j in enumerate(js):
            k = k_ref[j]
            z = jnp.concatenate([jnp.dot(k[:, 2 * HEAD_DIM * pr:2 * HEAD_DIM * (pr + 1)], qpairs[pr],
                                         preferred_element_type=F32) for pr in range(pairs)], axis=1)
            lsp = _softplus(z)
            if first_is_diag and n == 0:
                lsp = jnp.where(causal, lsp, 0.0)
            hi, lo = _split_bf16(lsp)
            zs.append(z)
            cs.append(jnp.dot(tri2, jnp.concatenate([hi, lo], axis=0), preferred_element_type=F32))
        carry = carry_ref[...]
        weights = []
        for n in range(len(js)):
            a = jnp.exp2(zs[n] - cs[n] - carry)
            if first_is_diag and n == 0:
                a = jnp.where(causal, a, 0.0)
            weights.append(a.astype(BF16))
            carry = carry + cs[n][0:1]
        carry_ref[...] = carry
        for pr in range(pairs):
            vT = jnp.concatenate([vT_ref[j, 2 * HEAD_DIM * pr:2 * HEAD_DIM * (pr + 1), :] for j in js],
                                 axis=1)
            w = jnp.concatenate([a[:, 2 * tq * pr:2 * tq * (pr + 1)] for a in weights], axis=0)
            acc_ref[pr] += jnp.dot(vT, w, preferred_element_type=F32)

    def still_live():
        return (jnp.min(carry_ref[...] - zb) < SB_SKIP_MARGIN).astype(jnp.int32)

    @pl.when(qi == 0)
    def _():
        tiles([qi], True)

    @pl.when(qi == 1)
    def _():
        tiles([qi, qi - 1], True)

    @pl.when(qi >= 2)
    def _():
        tiles([qi, qi - 1, qi - 2], True)

    def cond(state):
        j, live = state
        return jnp.logical_and(j >= 0, live > 0)

    def body(state):
        j, _ = state
        tiles([j], False)
        return j - 1, still_live()

    lax.while_loop(cond, body, (qi - 3, still_live()))
    outs = []
    for pr in range(pairs):
        outs += [acc_ref[pr, 0:HEAD_DIM, 0:tq], acc_ref[pr, HEAD_DIM:2 * HEAD_DIM, tq:2 * tq]]
    o_ref[...] = jnp.concatenate(outs, axis=0).T.astype(BF16)


def _sb_prompt(qT, k16, vT, kmax):
    n_tok = k16.shape[0]
    tq = min(SB_TILE, n_tok)
    nb = n_tok // tq
    k3 = k16.reshape(nb, tq, D_MODEL)
    pairs = SB_PAIRS_PER_STEP
    wide = 2 * HEAD_DIM * pairs
    return pl.pallas_call(
        functools.partial(_sb_prompt_kernel, tq=tq, pairs=pairs),
        grid=(SB_HEADS // (2 * pairs), nb),
        in_specs=[
            pl.BlockSpec((wide, tq), lambda p, i: (p, i)),
            pl.BlockSpec((nb, tq, wide), lambda p, i: (0, 0, p)),
            pl.BlockSpec((nb, wide, tq), lambda p, i: (0, p, 0)),
            pl.BlockSpec((wide, 1), lambda p, i: (p, 0)),
        ],
        out_specs=pl.BlockSpec((tq, wide), lambda p, i: (i, p)),
        out_shape=jax.ShapeDtypeStruct((n_tok, D_MODEL), BF16),
        scratch_shapes=[
            pltpu.VMEM((1, 2 * pairs * tq), F32),
            pltpu.VMEM((pairs, 2 * HEAD_DIM, 2 * tq), F32),
        ],
        compiler_params=_cparams(2),
        name="sb_prompt",
    )(qT, k3, vT, kmax)


def _tri_lanes(n):
    r = lax.broadcasted_iota(jnp.int32, (n, n), 0)
    c = lax.broadcasted_iota(jnp.int32, (n, n), 1)
    return jnp.where(r >= c, 1.0, 0.0).astype(BF16)


def _sb_sample_kernel(q_ref, ck_ref, cv_ref, kn_ref, vn_ref, o_ref, carry_ref, acc_ref, *, ch, kb):
    c = pl.program_id(1)
    nh = q_ref.shape[1]
    t = q_ref.shape[2]

    @pl.when(c == 0)
    def _():
        z = jnp.concatenate([lax.dot_general(q_ref[0, h], kn_ref[0, h], NT_DIMS, preferred_element_type=F32)
                             for h in range(nh)], axis=0)
        r = lax.broadcasted_iota(jnp.int32, (nh * t, t), 0)
        kc = lax.broadcasted_iota(jnp.int32, (nh * t, t), 1)
        causal = kc < lax.rem(r, t)
        hi, lo = _split_bf16(jnp.where(causal, _softplus(z), 0.0))
        tri = _tri_lanes(t)
        cs = jnp.dot(hi, tri, preferred_element_type=F32) + jnp.dot(lo, tri, preferred_element_type=F32)
        a = jnp.where(causal, jnp.exp2(z - cs), 0.0).astype(BF16)
        carry_ref[...] = cs[:, 0:1]
        for h in range(nh):
            acc_ref[h * t:(h + 1) * t, :] = jnp.dot(a[h * t:(h + 1) * t], vn_ref[0, h].astype(BF16),
                                                    preferred_element_type=F32)

    z = jnp.concatenate([jnp.dot(q_ref[0, h], ck_ref[h].astype(BF16), preferred_element_type=F32)
                         for h in range(nh)], axis=0)
    nblk = ch // kb
    rows = nh * t
    zst = jnp.concatenate([z[:, b * kb:(b + 1) * kb] for b in range(nblk)], axis=0)
    hi, lo = _split_bf16(_softplus(zst))
    tri = _tri_lanes(kb)
    cs = jnp.dot(hi, tri, preferred_element_type=F32) + jnp.dot(lo, tri, preferred_element_type=F32)
    run = carry_ref[...]
    carries = [None] * nblk
    for b in reversed(range(nblk)):
        carries[b] = run
        run = run + cs[b * rows:(b + 1) * rows, 0:1]
    carry_ref[...] = run
    a = jnp.exp2(zst - cs - jnp.concatenate(carries, axis=0))
    a = jnp.concatenate([a[b * rows:(b + 1) * rows] for b in range(nblk)], axis=1).astype(BF16)
    for h in range(nh):
        acc_ref[h * t:(h + 1) * t, :] += lax.dot_general(a[h * t:(h + 1) * t], cv_ref[h].astype(BF16), NT_DIMS,
                                                         preferred_element_type=F32)

    @pl.when(c == pl.num_programs(1) - 1)
    def _():
        for h in range(nh):
            o_ref[0, h] = acc_ref[h * t:(h + 1) * t, :].astype(BF16)


def _sb_sample(q_s, cache_k, cache_v, layer, kn, vn):
    nb, nh, t, _ = q_s.shape
    past = cache_k.shape[2]
    ch = min(SB_SAMPLE_CHUNK, past)
    nch = past // ch
    ck = cache_k.transpose(0, 1, 3, 4, 2)
    cv = cache_v.transpose(0, 1, 3, 4, 2)
    cache_spec = pl.BlockSpec((None, None, nh, HEAD_DIM, ch), lambda b, c: (layer, b, 0, 0, nch - 1 - c))
    head_spec = pl.BlockSpec((1, nh, t, HEAD_DIM), lambda b, c: (b, 0, 0, 0))
    return pl.pallas_call(
        functools.partial(_sb_sample_kernel, ch=ch, kb=min(SB_TILE, ch)),
        grid=(nb, nch),
        in_specs=[head_spec, cache_spec, cache_spec, head_spec, head_spec],
        out_specs=head_spec,
        out_shape=jax.ShapeDtypeStruct((nb, nh, t, HEAD_DIM), BF16),
        scratch_shapes=[
            pltpu.VMEM((nh * t, 1), F32),
            pltpu.VMEM((nh * t, HEAD_DIM), F32),
        ],
        compiler_params=_cparams(2),
        name="sb_sample",
    )(q_s, ck, cv, kn, vn)


def _tile_col(gain, n, scale):
    return (jnp.tile(gain.astype(F32), n // gain.shape[0]) * scale).reshape(n, 1)


def _heads_major(x, nb, t, nh, hd):
    return x.reshape(nb, t, nh, hd).transpose(0, 2, 1, 3)


def _diff_bias_tiles(rel_bias, tq, t):
    j = jnp.arange(tq)
    valid_d = (j // CHUNK)[:, None] <= (j // CHUNK)[None, :]
    bias_d = _bias_tile(rel_bias, tq, tq, 0, valid_d, True).reshape(DIFF_HEADS, 2, tq, tq)
    bias_p = _bias_tile(rel_bias, NEAR, NEAR, -NEAR, None, True).reshape(DIFF_HEADS, 2, NEAR, NEAR)
    bstats = _bias_stats(rel_bias).reshape(DIFF_HEADS, 2, 2)
    bias_s = jnp.swapaxes(_bias_tile(rel_bias, NEAR + t, t, -NEAR, None, True), 1, 2)
    bias_s = bias_s.reshape(DIFF_HEADS, 2, t, NEAR + t)
    return bias_d, bias_p, bstats, bias_s


def _diff_layer(xp, xs, cache_k, cache_v, layer, n_layers, earlier, bias_tiles, g, w_qkv, q_gain, k_gain, lam_p,
                subln, lam_init):
    bias_d, bias_p, bstats, bias_s = bias_tiles
    n_p = xp.shape[0]
    nb = cache_k.shape[1]
    t = xs.shape[0] // nb
    wt = w_qkv.T.astype(BF16)
    qg = _tile_col(q_gain, D_MODEL, HEAD_DIM ** -0.5 * LOG2E)
    kg = _tile_col(k_gain, D_MODEL, 1.0)
    g = g.reshape(1, D_MODEL)
    kw = dict(nq=D_MODEL, nk=D_MODEL, nv=D_MODEL, head_norm=True)
    tq = min(DIFF_TILE, n_p)
    qT, k32, k16, v32, vT, knorm2 = _project(xp, g, wt, qg, kg, vb=tq, kstat="normsq", slot=layer,
                                             n_slots=n_layers, earlier=earlier, **kw)
    qT_s, k32_s, k16_s, v32_s, _ = _project(xs, g, wt, qg, kg, vb=LANES, **kw)
    k32_s, v32_s = k32_s[0], v32_s[0]

    o_p = _diff_prompt(qT, k16, vT, bias_d, bias_p, knorm2.reshape(DIFF_HEADS, 2, 1), bstats, lam_p,
                       subln.reshape(2 * HEAD_DIM, 1), lam_init)
    hm = functools.partial(_heads_major, nb=nb, t=t, nh=DIFF_HEADS, hd=2 * HEAD_DIM)
    o_s = _diff_sample(hm(qT_s.T), cache_k, cache_v, layer, hm(k16_s), hm(v32_s), bias_s, lam_p,
                       subln.reshape(1, 2 * HEAD_DIM), lam_init)
    o_s = o_s.transpose(0, 2, 1, 3).reshape(nb * t, D_MODEL)
    return (o_p, o_s,
            k32, v32,
            k32_s.reshape(nb, t, DIFF_HEADS, 2 * HEAD_DIM), v32_s.reshape(nb, t, DIFF_HEADS, 2 * HEAD_DIM))


def _swa_layer(xp, xs, ck, cv, rel_bias, g, w_qkv, q_gain, k_gain, sinks):
    n_p = xp.shape[0]
    nb, buf = ck.shape[0], ck.shape[1]
    t = xs.shape[0] // nb
    nkv = SWA_KV_HEADS * HEAD_DIM
    wt = w_qkv.T.astype(BF16)
    qg = _tile_col(q_gain, D_MODEL, HEAD_DIM ** -0.5 * LOG2E)
    kg = _tile_col(k_gain, nkv, 1.0)
    g = g.reshape(1, D_MODEL)
    kw = dict(nq=D_MODEL, nk=nkv, nv=nkv, vb=SWA_TILE, head_norm=True)
    qT, k32, k16, v32, vT = _project(xp, g, wt, qg, kg, **kw)
    qT_s, k32_s, _, v32_s, _ = _project(xs, g, wt, qg, kg, **kw)
    k32, v32, k32_s, v32_s = k32[0], v32[0], k32_s[0], v32_s[0]
    sink_col = (sinks.astype(F32) * LOG2E).reshape(SWA_Q_HEADS, 1, 1)
    wchunks = WINDOW // CHUNK

    tq = SWA_TILE
    kchunk = jnp.floor_divide(jnp.arange(2 * tq) - tq, CHUNK)[:, None]
    qchunk = (jnp.arange(tq) // CHUNK)[None, :]
    valid = (kchunk <= qchunk) & (kchunk >= qchunk - wchunks)
    bias = _bias_tile(rel_bias, 2 * tq, tq, -tq, valid, False)
    bias_g = bias.reshape(SWA_KV_HEADS, SWA_GROUP, 2 * tq, tq).transpose(0, 2, 1, 3)
    bias_g = bias_g.reshape(SWA_KV_HEADS, 2 * tq, SWA_GROUP * tq)
    sink_g = jnp.repeat(sink_col.reshape(SWA_KV_HEADS, 1, SWA_GROUP), tq, axis=2)
    o_p = _swa_prompt(qT, k16, vT, bias_g, sink_g)

    sc = jnp.floor_divide(jnp.arange(buf + t) - buf, CHUNK)[:, None]
    tc = (jnp.arange(t) // CHUNK)[None, :]
    valid_s = (sc <= tc) & (sc >= tc - wchunks)
    bias_s = jnp.swapaxes(_bias_tile(rel_bias, buf + t, t, -buf, valid_s, False), 1, 2)
    q_s = _heads_major(qT_s.T, nb, t, SWA_Q_HEADS, HEAD_DIM)
    kn = _heads_major(k32_s, nb, t, SWA_KV_HEADS, HEAD_DIM)
    vn = _heads_major(v32_s, nb, t, SWA_KV_HEADS, HEAD_DIM)
    o_s, ko, vo = _swa_sample(q_s, ck.transpose(0, 2, 1, 3), cv.transpose(0, 2, 1, 3), kn, vn, bias_s, sink_col)
    o_s = o_s.transpose(0, 2, 1, 3).reshape(nb * t, D_MODEL)
    wbuf = min(WINDOW, n_p)
    return (o_p, o_s,
            k32[n_p - wbuf:].reshape(1, wbuf, SWA_KV_HEADS, HEAD_DIM),
            v32[n_p - wbuf:].reshape(1, wbuf, SWA_KV_HEADS, HEAD_DIM),
            ko.transpose(0, 2, 1, 3), vo.transpose(0, 2, 1, 3))


def _sb_layer(xp, xs, cache_k, cache_v, layer, n_layers, earlier, g, w_qkv):
    n_p = xp.shape[0]
    nb = cache_k.shape[1]
    t = xs.shape[0] // nb
    wt = w_qkv.T.astype(BF16)
    qg = jnp.full((D_MODEL, 1), HEAD_DIM ** -0.5 * LOG2E, F32)
    kg = jnp.ones((D_MODEL, 1), F32)
    g = g.reshape(1, D_MODEL)
    kw = dict(nq=D_MODEL, nk=D_MODEL, nv=D_MODEL, head_norm=False)
    tq = min(SB_TILE, n_p)
    qT, k32, k16, v32, vT, kmax = _project(xp, g, wt, qg, kg, vb=tq, kstat="absmax", slot=layer,
                                           n_slots=n_layers, earlier=earlier, kv_feature_major=True, **kw)
    qT_s, k32_s, k16_s, v32_s, _ = _project(xs, g, wt, qg, kg, vb=LANES, **kw)
    k32_s, v32_s = k32_s[0], v32_s[0]
    o_p = _sb_prompt(qT, k16, vT, kmax)
    hm = functools.partial(_heads_major, nb=nb, t=t, nh=SB_HEADS, hd=HEAD_DIM)
    o_s = _sb_sample(hm(qT_s.T), cache_k, cache_v, layer, hm(k16_s), hm(v32_s))
    o_s = o_s.transpose(0, 2, 1, 3).reshape(nb * t, D_MODEL)
    return (o_p, o_s,
            k32, v32,
            k32_s.reshape(nb, t, SB_HEADS, HEAD_DIM), v32_s.reshape(nb, t, SB_HEADS, HEAD_DIM))


def kernel(x_prompt, x_sample, cache_diff_k, cache_diff_v, cache_swa_k, cache_swa_v, cache_sb_k, cache_sb_v, rel_bias, norm_mix, norm_mlp, w_up, w_down, diff_w_qkv, diff_w_o, diff_q_norm, diff_k_norm, diff_lambda, diff_subln, swa_w_qkv, swa_w_o, swa_q_norm, swa_k_norm, swa_sinks, sb_w_qkv, sb_w_o):
    bp, n_p, _ = x_prompt.shape
    assert bp == 1
    nb, t, _ = x_sample.shape
    depth = norm_mix.shape[0]
    xp = x_prompt.reshape(n_p, D_MODEL)
    xs = x_sample.reshape(nb * t, D_MODEL)
    n_diff, n_sb = cache_diff_k.shape[0], cache_sb_k.shape[0]
    outs = {name: [] for name in ("pwk", "pwv", "sdk", "sdv", "swk", "swv", "sbk", "sbv")}
    diff_kv = sb_kv = None
    diff_bias = _diff_bias_tiles(rel_bias, min(DIFF_TILE, n_p), t)
    wup16 = w_up.astype(BF16)
    wdn16 = w_down.astype(BF16)
    for i in range(depth):
        j = i // N_MIXERS
        if i % N_MIXERS == 0:
            lam_init = 0.8 - 0.6 * math.exp(-0.3 * i)
            o_p, o_s, kp, vp, kn, vn = _diff_layer(
                xp, xs, cache_diff_k, cache_diff_v, j, n_diff, diff_kv, diff_bias, norm_mix[i], diff_w_qkv[j],
                diff_q_norm[j], diff_k_norm[j], diff_lambda[j], diff_subln[j], lam_init)
            diff_kv = (kp, vp)
            w_o = diff_w_o[j]
            names = ("sdk", "sdv")
        elif i % N_MIXERS == 1:
            o_p, o_s, kp, vp, kn, vn = _swa_layer(
                xp, xs, cache_swa_k[j], cache_swa_v[j], rel_bias, norm_mix[i], swa_w_qkv[j],
                swa_q_norm[j], swa_k_norm[j], swa_sinks[j])
            outs["pwk"].append(kp)
            outs["pwv"].append(vp)
            w_o = swa_w_o[j]
            names = ("swk", "swv")
        else:
            o_p, o_s, kp, vp, kn, vn = _sb_layer(xp, xs, cache_sb_k, cache_sb_v, j, n_sb, sb_kv, norm_mix[i],
                                                 sb_w_qkv[j])
            sb_kv = (kp, vp)
            w_o = sb_w_o[j]
            names = ("sbk", "sbv")
        for name, val in zip(names, (kn, vn)):
            outs[name].append(val)
        wo16 = w_o.astype(BF16)
        g_mlp = norm_mlp[i].reshape(1, D_MODEL)
        xp = _outproj_mlp(xp, o_p, wo16, g_mlp, wup16, wdn16, i)
        xs = _outproj_mlp(xs, o_s, wo16, g_mlp, wup16, wdn16, i)
    st = {name: jnp.stack(v) for name, v in outs.items()}
    diff_shape = (n_diff, 1, n_p, DIFF_HEADS, 2 * HEAD_DIM)

    def sb_result(a):
        return a.reshape(n_sb, 1, SB_HEADS, HEAD_DIM, n_p).transpose(0, 1, 4, 2, 3)

    return (xp.reshape(1, n_p, D_MODEL), xs.reshape(nb, t, D_MODEL),
            diff_kv[0].reshape(diff_shape), diff_kv[1].reshape(diff_shape), st["pwk"], st["pwv"],
            sb_result(sb_kv[0]), sb_result(sb_kv[1]),
            st["sdk"], st["sdv"], st["swk"], st["swv"], st["sbk"], st["sbv"])
```

```python
import functools
import math

import jax
import jax.numpy as jnp
from jax import lax
from jax.experimental import pallas as pl
from jax.experimental.pallas import tpu as pltpu

F32 = jnp.float32
BF16 = jnp.bfloat16

D_MODEL = 1024
HEAD_DIM = 64
CHUNK = 64
N_MIXERS = 3
DIFF_HEADS = 8
SWA_Q_HEADS = 16
SWA_KV_HEADS = 4
SWA_GROUP = 4
WINDOW = 128
SB_HEADS = 16
N_BUCKETS = 32
MAX_DISTANCE = 128
D_FF = 4 * D_MODEL
EPS = 1e-6
LOG2E = 1.4426950408889634
NEG = -1e30
LANES = 128
NEAR = MAX_DISTANCE
FAR_BUCKET = 15
SB_SKIP_MARGIN = 145.0
SOFTMAX_BOUND_LIMIT = 100.0

V7X_VMEM_BYTES = 64 * 1024 * 1024
VMEM_LIMIT = V7X_VMEM_BYTES * 7 // 8
TOKEN_TILE = 512
PROJ_CHUNK = 512
DIFF_TILE = 512
DIFF_FAR_GROUP = 8
SB_TILE = 256
SB_PAIRS_PER_STEP = 2
SB_SAMPLE_CHUNK = 512
SWA_TILE = 128

NT_DIMS = (((1,), (1,)), ((), ()))


def _cparams(n_axes):
    return pltpu.CompilerParams(dimension_semantics=("arbitrary",) * n_axes,
                                vmem_limit_bytes=VMEM_LIMIT)


def _resident(block_shape, index_map):
    return pl.BlockSpec(block_shape, index_map, pipeline_mode=pl.Buffered(1))


def _t5_bucket(rel):
    half = N_BUCKETS // 2
    exact = half // 2
    ret = jnp.where(rel > 0, half, 0).astype(jnp.int32)
    n = jnp.abs(rel)
    nf = jnp.maximum(n, 1).astype(F32)
    large = exact + (jnp.log(nf / exact) / math.log(MAX_DISTANCE / exact) * (half - exact)).astype(jnp.int32)
    large = jnp.minimum(large, half - 1)
    return ret + jnp.where(n < exact, n, large)


def _bias_tile(rel_bias, n_j, n_i, r0, valid, shift_far):
    tab = rel_bias.astype(F32)
    period = -(-(n_i + n_j - 1) // LANES) * LANES
    u = jnp.arange(period)
    rel = jnp.where(u < n_i, r0 - u, r0 + (period - u))
    vals = jnp.moveaxis(tab[_t5_bucket(rel)], -1, 0)
    if shift_far:
        vals = vals - tab[FAR_BUCKET][:, None]
    vals = (vals * LOG2E).reshape(vals.shape[0], 1, period)

    def expand(y_ref, o_ref):
        rows = jnp.broadcast_to(y_ref[0], (n_j, period))
        o_ref[0] = pltpu.roll(rows, 0, 1, stride=1, stride_axis=0)[:, :n_i]

    b = pl.pallas_call(
        expand,
        grid=(vals.shape[0],),
        in_specs=[pl.BlockSpec((1, 1, period), lambda c: (c, 0, 0))],
        out_specs=pl.BlockSpec((1, n_j, n_i), lambda c: (c, 0, 0)),
        out_shape=jax.ShapeDtypeStruct((vals.shape[0], n_j, n_i), F32),
        compiler_params=_cparams(1),
        name="bias_toeplitz",
    )(vals)
    if valid is not None:
        b = jnp.where(valid[None], b, NEG)
    return b


def _bias_stats(rel_bias):
    tab = rel_bias.astype(F32)
    sh = (tab - tab[FAR_BUCKET][None, :]) * LOG2E
    bmax = jnp.max(sh, axis=0)
    return jnp.stack([bmax, bmax - jnp.min(sh, axis=0)], axis=-1)


def _proj_kernel(x_ref, g_ref, wt_ref, qg_ref, kg_ref, *refs, nq, nk, nv, vb, head_norm, kstat, n_alias,
                 slot, all_slots, kv_layout):
    out_refs = refs[n_alias:]
    qT_ref, k32_ref, k16_ref, v32_ref, vT_ref = out_refs[:5]
    if all_slots > 1:
        for s in range(all_slots):
            if s != slot:
                k32_ref[s] = jnp.zeros(k32_ref.shape[1:], F32)
                v32_ref[s] = jnp.zeros(v32_ref.shape[1:], F32)
        k32_ref = k32_ref.at[slot]
        v32_ref = v32_ref.at[slot]
    x = x_ref[...]
    ms = jnp.mean(x * x, axis=-1, keepdims=True)
    h = (x * lax.rsqrt(ms + EPS) * g_ref[...]).astype(BF16)
    tm = x.shape[0]

    def rows(r0, n):
        return lax.dot_general(wt_ref[r0:r0 + n, :], h, NT_DIMS, preferred_element_type=F32)

    def headnorm(t, gcol):
        n = t.shape[0]
        t3 = t.reshape(n // HEAD_DIM, HEAD_DIM, tm)
        r = lax.rsqrt(jnp.mean(t3 * t3, axis=1, keepdims=True) + EPS)
        return (t3 * r).reshape(n, tm) * gcol

    def store_kv(ref, r0, xt, x):
        n = xt.shape[0]
        if kv_layout == "feature":
            ref[r0:r0 + n, :] = xt
            return
        x = xt.T if x is None else x
        if kv_layout == "token":
            ref[:, r0:r0 + n] = x
            return
        heads = ref.shape[0] // tm
        for c in range(n // LANES):
            head = r0 // LANES + c
            ref[pl.ds(head, tm, stride=heads), :] = x[:, c * LANES:(c + 1) * LANES]

    for r0 in range(0, nq, PROJ_CHUNK):
        n = min(PROJ_CHUNK, nq - r0)
        qt = rows(r0, n)
        qt = headnorm(qt, qg_ref[r0:r0 + n, :]) if head_norm else qt * qg_ref[r0:r0 + n, :]
        qT_ref[r0:r0 + n, :] = qt.astype(BF16)

    for r0 in range(0, nk, PROJ_CHUNK):
        n = min(PROJ_CHUNK, nk - r0)
        kt = rows(nq + r0, n)
        if head_norm:
            kt = headnorm(kt, kg_ref[r0:r0 + n, :])
        k = kt.T
        store_kv(k32_ref, r0, kt, k)
        k16_ref[:, r0:r0 + n] = k.astype(BF16)
        if kstat is not None:
            stat_ref = out_refs[5]
            kr = kt.astype(BF16).astype(F32)
            if kstat == "absmax":
                stat = jnp.max(jnp.abs(kr), axis=1, keepdims=True)
                srows = slice(r0, r0 + n)
            else:
                k3 = kr.reshape(n // HEAD_DIM, HEAD_DIM, tm)
                stat = jnp.max(jnp.sum(k3 * k3, axis=1), axis=1, keepdims=True)
                srows = slice(r0 // HEAD_DIM, (r0 + n) // HEAD_DIM)

            @pl.when(pl.program_id(0) == 0)
            def _(stat=stat, srows=srows):
                stat_ref[srows, :] = stat

            @pl.when(pl.program_id(0) > 0)
            def _(stat=stat, srows=srows):
                stat_ref[srows, :] = jnp.maximum(stat_ref[srows, :], stat)

    for r0 in range(0, nv, PROJ_CHUNK):
        n = min(PROJ_CHUNK, nv - r0)
        vt = rows(nq + nk + r0, n)
        store_kv(v32_ref, r0, vt, None)
        vt16 = vt.astype(BF16)
        for b in range(tm // vb):
            vT_ref[b, r0:r0 + n, :] = vt16[:, b * vb:(b + 1) * vb]


def _project(x, g, wt, qg, kg, *, nq, nk, nv, vb, head_norm, kstat=None, slot=0, n_slots=1, earlier=None,
             kv_layout="token"):
    n_tok = x.shape[0]
    tm = min(TOKEN_TILE, n_tok)
    vb = min(vb, tm)
    grid = (n_tok // tm,)
    ntot = nq + nk + nv

    def kv_shape(width):
        if kv_layout == "feature":
            return (n_slots, width, n_tok)
        if kv_layout == "head_rows":
            return (n_slots, n_tok * (width // LANES), LANES)
        return (n_slots, n_tok, width)

    out_shape = [
        jax.ShapeDtypeStruct((nq, n_tok), BF16),
        jax.ShapeDtypeStruct(kv_shape(nk), F32),
        jax.ShapeDtypeStruct((n_tok, nk), BF16),
        jax.ShapeDtypeStruct(kv_shape(nv), F32),
        jax.ShapeDtypeStruct((n_tok // vb, nv, vb), BF16),
    ]
    all_slots = n_slots if (earlier is None and n_slots > 1) else 1

    def kv_spec(width):
        lead = n_slots if all_slots > 1 else None
        first = 0 if all_slots > 1 else slot
        if kv_layout == "feature":
            return pl.BlockSpec((lead, width, tm), lambda i: (first, 0, i))
        if kv_layout == "head_rows":
            return pl.BlockSpec((lead, tm * (width // LANES), LANES), lambda i: (first, i, 0))
        return pl.BlockSpec((lead, tm, width), lambda i: (first, i, 0))

    out_specs = [
        pl.BlockSpec((nq, tm), lambda i: (0, i)),
        kv_spec(nk),
        pl.BlockSpec((tm, nk), lambda i: (i, 0)),
        kv_spec(nv),
        pl.BlockSpec((tm // vb, nv, vb), lambda i: (i, 0, 0)),
    ]
    if kstat is not None:
        n_stat = nk if kstat == "absmax" else nk // HEAD_DIM
        out_shape.append(jax.ShapeDtypeStruct((n_stat, 1), F32))
        out_specs.append(pl.BlockSpec((n_stat, 1), lambda i: (0, 0)))
    in_specs = [
        pl.BlockSpec((tm, D_MODEL), lambda i: (i, 0)),
        _resident((1, D_MODEL), lambda i: (0, 0)),
        _resident((ntot, D_MODEL), lambda i: (0, 0)),
        _resident((nq, 1), lambda i: (0, 0)),
        _resident((nk, 1), lambda i: (0, 0)),
    ]
    args = [x, g, wt, qg, kg]
    aliases = {}
    if earlier is not None:
        aliases = {len(args): 1, len(args) + 1: 3}
        in_specs += [pl.BlockSpec(memory_space=pl.ANY)] * 2
        args += list(earlier)
    return pl.pallas_call(
        functools.partial(_proj_kernel, nq=nq, nk=nk, nv=nv, vb=vb, head_norm=head_norm, kstat=kstat,
                          n_alias=len(aliases), slot=slot, all_slots=all_slots,
                          kv_layout=kv_layout),
        grid=grid,
        in_specs=in_specs,
        out_specs=out_specs,
        out_shape=out_shape,
        input_output_aliases=aliases,
        compiler_params=_cparams(1),
        name="proj",
    )(*args)


def _mlp_kernel(x_ref, o_ref, wo_ref, g_ref, wup_ref, wdn_ref, y_ref, *, f_chunk):
    x1 = x_ref[...] + jnp.dot(o_ref[...], wo_ref[...], preferred_element_type=F32)
    ms = jnp.mean(x1 * x1, axis=-1, keepdims=True)
    hn = (x1 * lax.rsqrt(ms + EPS) * g_ref[...]).astype(BF16)
    acc = x1
    for f in range(D_FF // f_chunk):
        u = jnp.dot(hn, wup_ref[:, f * f_chunk:(f + 1) * f_chunk], preferred_element_type=F32)
        a = jnp.square(jnp.maximum(u, 0.0)).astype(BF16)
        acc = acc + jnp.dot(a, wdn_ref[f * f_chunk:(f + 1) * f_chunk, :], preferred_element_type=F32)
    y_ref[...] = acc


def _outproj_mlp(x, o, wo, g, wup, wdn, layer):
    n_tok = x.shape[0]
    tm = min(TOKEN_TILE, n_tok)
    return pl.pallas_call(
        functools.partial(_mlp_kernel, f_chunk=1024),
        grid=(n_tok // tm,),
        in_specs=[
            pl.BlockSpec((tm, D_MODEL), lambda i: (i, 0)),
            pl.BlockSpec((tm, D_MODEL), lambda i: (i, 0)),
            _resident((D_MODEL, D_MODEL), lambda i: (0, 0)),
            _resident((1, D_MODEL), lambda i: (0, 0)),
            _resident((None, D_MODEL, D_FF), lambda i: (layer, 0, 0)),
            _resident((None, D_FF, D_MODEL), lambda i: (layer, 0, 0)),
        ],
        out_specs=pl.BlockSpec((tm, D_MODEL), lambda i: (i, 0)),
        out_shape=jax.ShapeDtypeStruct((n_tok, D_MODEL), F32),
        compiler_params=_cparams(1),
        name="outproj_mlp",
    )(x, o, wo, g, wup, wdn)


def _diff_lambda(lam_ref, lam_init):
    lp = lam_ref[...]
    a = jnp.sum(lp[0:1] * lp[1:2], axis=-1, keepdims=True)
    b = jnp.sum(lp[2:3] * lp[3:4], axis=-1, keepdims=True)
    return jnp.exp(a) - jnp.exp(b) + lam_init


def _diff_prompt_kernel(qT_ref, k_ref, vT_ref, bd_ref, bp_ref, kn_ref, bs_ref, lam_ref, sub_ref, o_ref,
                        m_ref, l_ref, acc_ref, *, tq, lam_init):
    qi = pl.program_id(1)
    qT = qT_ref[...]
    row = lax.broadcasted_iota(jnp.int32, qT.shape, 0)
    zero = jnp.zeros_like(qT)
    qm = (jnp.where(row < HEAD_DIM, qT, zero), jnp.where(row >= HEAD_DIM, qT, zero))

    def prev_bias(m, s):
        corner = s[tq - NEAR:, :NEAR] + bp_ref[0, m]
        bottom = jnp.concatenate([corner, s[tq - NEAR:, NEAR:]], axis=1)
        return jnp.concatenate([s[:tq - NEAR], bottom], axis=0)

    def diag_bias(m, s):
        return s + bd_ref[0, m]

    def prev_diag_bias(m, s):
        return jnp.concatenate([prev_bias(m, s[:tq]), diag_bias(m, s[tq:])], axis=0)

    n_far = jnp.maximum(qi - 1, 0)

    def walk(tile_fn):
        def far_body(j, carry):
            tile_fn(j, None)
            return carry

        lax.fori_loop(0, n_far, far_body, 0)

        @pl.when(qi >= 1)
        def _():
            tile_fn(qi - 1, prev_bias)

        tile_fn(qi, diag_bias)

    span = []
    for m in range(2):
        qf = qm[m].astype(F32)
        qn = jnp.sqrt(jnp.sum(qf * qf, axis=0, keepdims=True))
        reach = qn * (jnp.sqrt(kn_ref[m:m + 1, :]) * 1.001)
        m_ref[m] = reach + bs_ref[m:m + 1, 0:1]
        span.append(jnp.max(2.0 * reach + bs_ref[m:m + 1, 1:2]))
    bound_is_tight = jnp.maximum(span[0], span[1]) <= SOFTMAX_BOUND_LIMIT

    @pl.when(jnp.logical_not(bound_is_tight))
    def _():
        m_ref[...] = jnp.full(m_ref.shape, NEG, F32)

        def max_tile(j, bias_fn):
            k = k_ref[j]
            for m in range(2):
                s = jnp.dot(k, qm[m], preferred_element_type=F32)
                if bias_fn is not None:
                    s = bias_fn(m, s)
                m_ref[m] = jnp.maximum(m_ref[m], jnp.max(s, axis=0, keepdims=True))

        walk(max_tile)

    l_ref[...] = jnp.zeros(l_ref.shape, F32)
    acc_ref[...] = jnp.zeros(acc_ref.shape, F32)

    def acc_tiles(js, bias_fn):
        k = jnp.concatenate([k_ref[j] for j in js], axis=0)
        vT = jnp.concatenate([vT_ref[j] for j in js], axis=1)
        nk = len(js) * tq
        for m in range(2):
            s = jnp.dot(k, qm[m], preferred_element_type=F32)
            if bias_fn is not None:
                s = bias_fn(m, s)
            p = jnp.exp2(s - m_ref[m])
            l_ref[m] += jnp.sum(p.reshape(nk // 8, 8, tq), axis=0)
            acc_ref[m] += jnp.dot(vT, p.astype(BF16), preferred_element_type=F32)

    def far_group(jj, carry):
        acc_tiles([DIFF_FAR_GROUP * jj + u for u in range(DIFF_FAR_GROUP)], None)
        return carry

    n_groups = n_far // DIFF_FAR_GROUP
    lax.fori_loop(0, n_groups, far_group, 0)
    start = n_groups * DIFF_FAR_GROUP
    left = n_far - start
    size = DIFF_FAR_GROUP // 2
    while size >= 2:
        @pl.when((left & size) != 0)
        def _(start=start, size=size):
            acc_tiles([start + u for u in range(size)], None)

        start = start + (left & size)
        size //= 2
    odd = (left & 1) != 0

    def plain_prev_diag_bias(m, s):
        return jnp.concatenate([s[:tq], prev_diag_bias(m, s[tq:])], axis=0)

    @pl.when(odd)
    def _():
        acc_tiles([start, qi - 1, qi], plain_prev_diag_bias)

    @pl.when(jnp.logical_and(qi >= 1, jnp.logical_not(odd)))
    def _():
        acc_tiles([qi - 1, qi], prev_diag_bias)

    @pl.when(qi == 0)
    def _():
        acc_tiles([qi], diag_bias)

    lam = _diff_lambda(lam_ref, lam_init)
    o0 = acc_ref[0] * (1.0 / jnp.sum(l_ref[0], axis=0, keepdims=True))
    o1 = acc_ref[1] * (1.0 / jnp.sum(l_ref[1], axis=0, keepdims=True))
    o = o0 - lam * o1
    ms = jnp.mean(o * o, axis=0, keepdims=True)
    o = o * lax.rsqrt(ms + EPS) * sub_ref[...] * (1.0 - lam_init)
    o_ref[...] = o.T.astype(BF16)


def _diff_prompt(qT, k16, vT, bias_d, bias_p, knorm2, bstats, lam_p, subln, lam_init):
    n_tok = k16.shape[0]
    tq = min(DIFF_TILE, n_tok)
    nb = n_tok // tq
    k3 = k16.reshape(nb, tq, D_MODEL)
    return pl.pallas_call(
        functools.partial(_diff_prompt_kernel, tq=tq, lam_init=lam_init),
        grid=(DIFF_HEADS, nb),
        in_specs=[
            pl.BlockSpec((2 * HEAD_DIM, tq), lambda h, i: (h, i)),
            pl.BlockSpec((nb, tq, 2 * HEAD_DIM), lambda h, i: (0, 0, h)),
            pl.BlockSpec((nb, 2 * HEAD_DIM, tq), lambda h, i: (0, h, 0)),
            pl.BlockSpec((1, 2, tq, tq), lambda h, i: (h, 0, 0, 0)),
            pl.BlockSpec((1, 2, NEAR, NEAR), lambda h, i: (h, 0, 0, 0)),
            pl.BlockSpec((None, 2, 1), lambda h, i: (h, 0, 0)),
            pl.BlockSpec((None, 2, 2), lambda h, i: (h, 0, 0)),
            pl.BlockSpec((4, HEAD_DIM), lambda h, i: (0, 0)),
            pl.BlockSpec((2 * HEAD_DIM, 1), lambda h, i: (0, 0)),
        ],
        out_specs=pl.BlockSpec((tq, 2 * HEAD_DIM), lambda h, i: (i, h)),
        out_shape=jax.ShapeDtypeStruct((n_tok, D_MODEL), BF16),
        scratch_shapes=[
            pltpu.VMEM((2, 1, tq), F32),
            pltpu.VMEM((2, 8, tq), F32),
            pltpu.VMEM((2, 2 * HEAD_DIM, tq), F32),
        ],
        compiler_params=_cparams(2),
        name="diff_prompt",
    )(qT, k3, vT, bias_d, bias_p, knorm2, bstats, lam_p, subln)


def _diff_sample_kernel(q_ref, ck_ref, cv_ref, kn_ref, vn_ref, b_ref, lam_ref, sub_ref, o_ref, *, lam_init, past):
    t = q_ref.shape[2]
    lane = lax.broadcasted_iota(jnp.int32, (t, 2 * HEAD_DIM), 1)
    lam = _diff_lambda(lam_ref, lam_init)
    for h in range(DIFF_HEADS):
        q = q_ref[0, h]
        zero = jnp.zeros_like(q)
        qq = jnp.concatenate([jnp.where(lane < HEAD_DIM, q, zero), jnp.where(lane >= HEAD_DIM, q, zero)], axis=0)
        ck = ck_ref[pl.ds(h, past, stride=DIFF_HEADS), :].astype(BF16)
        cv = cv_ref[pl.ds(h, past, stride=DIFF_HEADS), :].astype(BF16)
        kn = kn_ref[0, h]
        vn = vn_ref[0, h].astype(BF16)
        s_c = lax.dot_general(qq, ck, NT_DIMS, preferred_element_type=F32)
        s_n = lax.dot_general(qq, kn, NT_DIMS, preferred_element_type=F32)
        bias = b_ref[h].reshape(2 * t, NEAR + t)
        s_far = s_c[:, :past - NEAR]
        s_near = s_c[:, past - NEAR:] + bias[:, :NEAR]
        s_n = s_n + bias[:, NEAR:]
        m = jnp.maximum(jnp.maximum(jnp.max(s_far, axis=1, keepdims=True), jnp.max(s_near, axis=1, keepdims=True)),
                        jnp.max(s_n, axis=1, keepdims=True))
        p_far = jnp.exp2(s_far - m)
        p_near = jnp.exp2(s_near - m)
        p_n = jnp.exp2(s_n - m)
        l = (jnp.sum(p_far, axis=1, keepdims=True) + jnp.sum(p_near, axis=1, keepdims=True)
             + jnp.sum(p_n, axis=1, keepdims=True))
        o = (jnp.dot(p_far.astype(BF16), cv[:past - NEAR], preferred_element_type=F32)
             + jnp.dot(p_near.astype(BF16), cv[past - NEAR:], preferred_element_type=F32)
             + jnp.dot(p_n.astype(BF16), vn, preferred_element_type=F32))
        o = o * (1.0 / l)
        od = o[:t] - lam * o[t:]
        ms = jnp.mean(od * od, axis=-1, keepdims=True)
        od = od * lax.rsqrt(ms + EPS) * sub_ref[...] * (1.0 - lam_init)
        o_ref[0, h] = od.astype(BF16)


def _diff_sample(q_s, cache_k, cache_v, layer, kn, vn, bias_s, lam_p, subln_row, lam_init):
    nb, _, t, _ = q_s.shape
    past = cache_k.shape[2]
    rows = past * DIFF_HEADS
    ck = cache_k.reshape(cache_k.shape[0], nb, rows, 2 * HEAD_DIM)
    cv = cache_v.reshape(cache_v.shape[0], nb, rows, 2 * HEAD_DIM)
    cache_spec = pl.BlockSpec((None, None, rows, 2 * HEAD_DIM), lambda b: (layer, b, 0, 0))
    head_spec = pl.BlockSpec((1, DIFF_HEADS, t, 2 * HEAD_DIM), lambda b: (b, 0, 0, 0))
    return pl.pallas_call(
        functools.partial(_diff_sample_kernel, lam_init=lam_init, past=past),
        grid=(nb,),
        in_specs=[
            head_spec, cache_spec, cache_spec, head_spec, head_spec,
            pl.BlockSpec((DIFF_HEADS, 2, t, NEAR + t), lambda b: (0, 0, 0, 0)),
            pl.BlockSpec((4, HEAD_DIM), lambda b: (0, 0)),
            pl.BlockSpec((1, 2 * HEAD_DIM), lambda b: (0, 0)),
        ],
        out_specs=head_spec,
        out_shape=jax.ShapeDtypeStruct((nb, DIFF_HEADS, t, 2 * HEAD_DIM), BF16),
        compiler_params=_cparams(1),
        name="diff_sample",
    )(q_s, ck, cv, kn, vn, bias_s, lam_p, subln_row)


def _swa_prompt_kernel(qT_ref, kc_ref, kp_ref, vc_ref, vp_ref, b_ref, sink_ref, o_ref):
    i = pl.program_id(0)
    qT = qT_ref[...]
    kp = kp_ref[...]
    kc = kc_ref[...]
    vp = vp_ref[0]
    vc = vc_ref[0]
    tq = qT.shape[1]
    key_row = lax.broadcasted_iota(jnp.int32, (2 * tq, SWA_GROUP * tq), 0)
    has_prev = key_row >= jnp.where(i > 0, 0, tq)
    zeros = jnp.zeros((HEAD_DIM, tq), BF16)
    outs = []
    for kv in range(SWA_KV_HEADS):
        pair = kv // 2
        lanes = slice(pair * 2 * HEAD_DIM, (pair + 1) * 2 * HEAD_DIM)
        k_band = jnp.concatenate([kp[:, lanes], kc[:, lanes]], axis=0)
        v_band = jnp.concatenate([vp[kv * HEAD_DIM:(kv + 1) * HEAD_DIM], vc[kv * HEAD_DIM:(kv + 1) * HEAD_DIM]],
                                 axis=1)
        qg = []
        for g in range(SWA_GROUP):
            h = kv * SWA_GROUP + g
            qh = qT[h * HEAD_DIM:(h + 1) * HEAD_DIM]
            qg.append(jnp.concatenate([qh, zeros] if kv % 2 == 0 else [zeros, qh], axis=0))
        q4 = jnp.concatenate(qg, axis=1)
        s = jnp.dot(k_band, q4, preferred_element_type=F32) + b_ref[kv]
        s = jnp.where(has_prev, s, NEG)
        sink = sink_ref[kv]
        m = jnp.maximum(jnp.max(s, axis=0, keepdims=True), sink)
        p = jnp.exp2(s - m)
        den = jnp.sum(p, axis=0, keepdims=True) + jnp.exp2(sink - m)
        oT4 = jnp.dot(v_band, p.astype(BF16), preferred_element_type=F32) * (1.0 / den)
        for g in range(SWA_GROUP):
            outs.append(oT4[:, g * tq:(g + 1) * tq])
    o_ref[...] = jnp.concatenate(outs, axis=0).T.astype(BF16)


def _swa_prompt(qT, k16, vT, bias, sinks):
    n_tok = k16.shape[0]
    tq = SWA_TILE
    nkv = SWA_KV_HEADS * HEAD_DIM
    return pl.pallas_call(
        _swa_prompt_kernel,
        grid=(n_tok // tq,),
        in_specs=[
            pl.BlockSpec((D_MODEL, tq), lambda i: (0, i)),
            pl.BlockSpec((tq, nkv), lambda i: (i, 0)),
            pl.BlockSpec((tq, nkv), lambda i: (jnp.maximum(i - 1, 0), 0)),
            pl.BlockSpec((1, nkv, tq), lambda i: (i, 0, 0)),
            pl.BlockSpec((1, nkv, tq), lambda i: (jnp.maximum(i - 1, 0), 0, 0)),
            _resident((SWA_KV_HEADS, 2 * tq, SWA_GROUP * tq), lambda i: (0, 0, 0)),
            _resident((SWA_KV_HEADS, 1, SWA_GROUP * tq), lambda i: (0, 0, 0)),
        ],
        out_specs=pl.BlockSpec((tq, D_MODEL), lambda i: (i, 0)),
        out_shape=jax.ShapeDtypeStruct((n_tok, D_MODEL), BF16),
        compiler_params=_cparams(1),
        name="swa_prompt",
    )(qT, k16, k16, vT, vT, bias, sinks)


def _swa_sample_kernel(q_ref, kc_ref, vc_ref, kn_ref, vn_ref, b_ref, sink_ref, o_ref, ko_ref, vo_ref):
    t = kn_ref.shape[2]
    buf = kc_ref.shape[2]
    for kv in range(SWA_KV_HEADS):
        kc32 = kc_ref[0, kv]
        vc32 = vc_ref[0, kv]
        kn32 = kn_ref[0, kv]
        vn32 = vn_ref[0, kv]
        ko_ref[0, kv, 0:buf - t, :] = kc32[t:]
        ko_ref[0, kv, buf - t:buf, :] = kn32
        vo_ref[0, kv, 0:buf - t, :] = vc32[t:]
        vo_ref[0, kv, buf - t:buf, :] = vn32
        kc = kc32.astype(BF16)
        vc = vc32.astype(BF16)
        kn = kn32.astype(BF16)
        vn = vn32.astype(BF16)
        heads = range(kv * SWA_GROUP, (kv + 1) * SWA_GROUP)
        q = jnp.concatenate([q_ref[0, h] for h in heads], axis=0)
        b = jnp.concatenate([b_ref[h] for h in heads], axis=0)
        sink = jnp.concatenate([jnp.broadcast_to(sink_ref[h], (t, 1)) for h in heads], axis=0)
        s_c = lax.dot_general(q, kc, NT_DIMS, preferred_element_type=F32) + b[:, :buf]
        s_n = lax.dot_general(q, kn, NT_DIMS, preferred_element_type=F32) + b[:, buf:]
        m = jnp.maximum(jnp.maximum(jnp.max(s_c, axis=1, keepdims=True), jnp.max(s_n, axis=1, keepdims=True)),
                        sink)
        p_c = jnp.exp2(s_c - m)
        p_n = jnp.exp2(s_n - m)
        den = (jnp.sum(p_c, axis=1, keepdims=True) + jnp.sum(p_n, axis=1, keepdims=True)
               + jnp.exp2(sink - m))
        o = (jnp.dot(p_c.astype(BF16), vc, preferred_element_type=F32)
             + jnp.dot(p_n.astype(BF16), vn, preferred_element_type=F32)) * (1.0 / den)
        for g, h in enumerate(heads):
            o_ref[0, h] = o[g * t:(g + 1) * t].astype(BF16)


def _swa_sample(q_s, kc, vc, kn, vn, bias, sinks):
    nb, _, t, _ = q_s.shape
    buf = kc.shape[2]
    kvspec = pl.BlockSpec((1, SWA_KV_HEADS, buf, HEAD_DIM), lambda b: (b, 0, 0, 0))
    nspec = pl.BlockSpec((1, SWA_KV_HEADS, t, HEAD_DIM), lambda b: (b, 0, 0, 0))
    return pl.pallas_call(
        _swa_sample_kernel,
        grid=(nb,),
        in_specs=[
            pl.BlockSpec((1, SWA_Q_HEADS, t, HEAD_DIM), lambda b: (b, 0, 0, 0)),
            kvspec, kvspec, nspec, nspec,
            pl.BlockSpec((SWA_Q_HEADS, t, buf + t), lambda b: (0, 0, 0)),
            pl.BlockSpec((SWA_Q_HEADS, 1, 1), lambda b: (0, 0, 0)),
        ],
        out_specs=[
            pl.BlockSpec((1, SWA_Q_HEADS, t, HEAD_DIM), lambda b: (b, 0, 0, 0)),
            kvspec, kvspec,
        ],
        out_shape=[
            jax.ShapeDtypeStruct((nb, SWA_Q_HEADS, t, HEAD_DIM), BF16),
            jax.ShapeDtypeStruct((nb, SWA_KV_HEADS, buf, HEAD_DIM), F32),
            jax.ShapeDtypeStruct((nb, SWA_KV_HEADS, buf, HEAD_DIM), F32),
        ],
        compiler_params=_cparams(1),
        name="swa_sample",
    )(q_s, kc, vc, kn, vn, bias, sinks)


def _softplus(z):
    return jnp.maximum(z, 0.0) + jnp.log2(1.0 + jnp.exp2(-jnp.abs(z)))


def _split_bf16(x):
    hi = x.astype(BF16)
    lo = (x - hi.astype(F32)).astype(BF16)
    return hi, lo


def _sb_prompt_kernel(qT_ref, k_ref, vT_ref, kmax_ref, o_ref, carry_ref, acc_ref, *, tq, pairs):
    qi = pl.program_id(1)
    width = 2 * pairs * tq
    row = lax.broadcasted_iota(jnp.int32, (2 * HEAD_DIM, tq), 0)
    qpairs, zbs = [], []
    for pr in range(pairs):
        qT = qT_ref[2 * HEAD_DIM * pr:2 * HEAD_DIM * (pr + 1), :]
        zero = jnp.zeros_like(qT)
        qp = jnp.concatenate([jnp.where(row < HEAD_DIM, qT, zero), jnp.where(row >= HEAD_DIM, qT, zero)], axis=1)
        qpairs.append(qp)
        kmax = kmax_ref[2 * HEAD_DIM * pr:2 * HEAD_DIM * (pr + 1), :]
        zbs.append(jnp.sum(jnp.abs(qp.astype(F32)) * kmax, axis=0, keepdims=True))
    zb = jnp.concatenate(zbs, axis=1)
    kr = lax.broadcasted_iota(jnp.int32, (tq, tq), 0)
    kc = lax.broadcasted_iota(jnp.int32, (tq, tq), 1)
    tri = jnp.where(kc >= kr, 1.0, 0.0).astype(BF16)
    tri2 = jnp.concatenate([tri, tri], axis=1)
    krw = lax.broadcasted_iota(jnp.int32, (tq, width), 0)
    kcw = lax.broadcasted_iota(jnp.int32, (tq, width), 1)
    causal = krw < lax.rem(kcw, tq)
    carry_ref[...] = jnp.zeros(carry_ref.shape, F32)
    acc_ref[...] = jnp.zeros(acc_ref.shape, F32)

    def tiles(js, first_is_diag):
        zs, cs = [], []
        for n, j in enumerate(js):
            k = k_ref[j]
            z = jnp.concatenate([jnp.dot(k[:, 2 * HEAD_DIM * pr:2 * HEAD_DIM * (pr + 1)], qpairs[pr],
                                         preferred_element_type=F32) for pr in range(pairs)], axis=1)
            lsp = _softplus(z)
            if first_is_diag and n == 0:
                lsp = jnp.where(causal, lsp, 0.0)
            hi, lo = _split_bf16(lsp)
            zs.append(z)
            cs.append(jnp.dot(tri2, jnp.concatenate([hi, lo], axis=0), preferred_element_type=F32))
        carry = carry_ref[...]
        weights = []
        for n in range(len(js)):
            a = jnp.exp2(zs[n] - cs[n] - carry)
            if first_is_diag and n == 0:
                a = jnp.where(causal, a, 0.0)
            weights.append(a.astype(BF16))
            carry = carry + cs[n][0:1]
        carry_ref[...] = carry
        for pr in range(pairs):
            vT = jnp.concatenate([vT_ref[j, 2 * HEAD_DIM * pr:2 * HEAD_DIM * (pr + 1), :] for j in js],
                                 axis=1)
            w = jnp.concatenate([a[:, 2 * tq * pr:2 * tq * (pr + 1)] for a in weights], axis=0)
            acc_ref[pr] += jnp.dot(vT, w, preferred_element_type=F32)

    def still_live():
        return (jnp.min(carry_ref[...] - zb) < SB_SKIP_MARGIN).astype(jnp.int32)

    @pl.when(qi == 0)
    def _():
        tiles([qi], True)

    @pl.when(qi == 1)
    def _():
        tiles([qi, qi - 1], True)

    @pl.when(qi >= 2)
    def _():
        tiles([qi, qi - 1, qi - 2], True)

    def cond(state):
        j, live = state
        return jnp.logical_and(j >= 0, live > 0)

    def body(state):
        j, _ = state
        tiles([j], False)
        return j - 1, still_live()

    lax.while_loop(cond, body, (qi - 3, still_live()))
    outs = []
    for pr in range(pairs):
        outs += [acc_ref[pr, 0:HEAD_DIM, 0:tq], acc_ref[pr, HEAD_DIM:2 * HEAD_DIM, tq:2 * tq]]
    o_ref[...] = jnp.concatenate(outs, axis=0).T.astype(BF16)


def _sb_prompt(qT, k16, vT, kmax):
    n_tok = k16.shape[0]
    tq = min(SB_TILE, n_tok)
    nb = n_tok // tq
    k3 = k16.reshape(nb, tq, D_MODEL)
    pairs = SB_PAIRS_PER_STEP
    wide = 2 * HEAD_DIM * pairs
    return pl.pallas_call(
        functools.partial(_sb_prompt_kernel, tq=tq, pairs=pairs),
        grid=(SB_HEADS // (2 * pairs), nb),
        in_specs=[
            pl.BlockSpec((wide, tq), lambda p, i: (p, i)),
            pl.BlockSpec((nb, tq, wide), lambda p, i: (0, 0, p)),
            pl.BlockSpec((nb, wide, tq), lambda p, i: (0, p, 0)),
            pl.BlockSpec((wide, 1), lambda p, i: (p, 0)),
        ],
        out_specs=pl.BlockSpec((tq, wide), lambda p, i: (i, p)),
        out_shape=jax.ShapeDtypeStruct((n_tok, D_MODEL), BF16),
        scratch_shapes=[
            pltpu.VMEM((1, 2 * pairs * tq), F32),
            pltpu.VMEM((pairs, 2 * HEAD_DIM, 2 * tq), F32),
        ],
        compiler_params=_cparams(2),
        name="sb_prompt",
    )(qT, k3, vT, kmax)


def _tri_lanes(n):
    r = lax.broadcasted_iota(jnp.int32, (n, n), 0)
    c = lax.broadcasted_iota(jnp.int32, (n, n), 1)
    return jnp.where(r >= c, 1.0, 0.0).astype(BF16)


def _sb_sample_kernel(q_ref, ck_ref, cv_ref, kn_ref, vn_ref, o_ref, carry_ref, acc_ref, *, ch, kb):
    c = pl.program_id(1)
    nh = q_ref.shape[1]
    t = q_ref.shape[2]

    @pl.when(c == 0)
    def _():
        z = jnp.concatenate([lax.dot_general(q_ref[0, h], kn_ref[0, h], NT_DIMS, preferred_element_type=F32)
                             for h in range(nh)], axis=0)
        r = lax.broadcasted_iota(jnp.int32, (nh * t, t), 0)
        kc = lax.broadcasted_iota(jnp.int32, (nh * t, t), 1)
        causal = kc < lax.rem(r, t)
        hi, lo = _split_bf16(jnp.where(causal, _softplus(z), 0.0))
        tri = _tri_lanes(t)
        cs = jnp.dot(hi, tri, preferred_element_type=F32) + jnp.dot(lo, tri, preferred_element_type=F32)
        a = jnp.where(causal, jnp.exp2(z - cs), 0.0).astype(BF16)
        carry_ref[...] = cs[:, 0:1]
        for h in range(nh):
            acc_ref[h * t:(h + 1) * t, :] = jnp.dot(a[h * t:(h + 1) * t], vn_ref[0, h].astype(BF16),
                                                    preferred_element_type=F32)

    z = jnp.concatenate([jnp.dot(q_ref[0, h], ck_ref[h].astype(BF16), preferred_element_type=F32)
                         for h in range(nh)], axis=0)
    nblk = ch // kb
    rows = nh * t
    zst = jnp.concatenate([z[:, b * kb:(b + 1) * kb] for b in range(nblk)], axis=0)
    hi, lo = _split_bf16(_softplus(zst))
    tri = _tri_lanes(kb)
    cs = jnp.dot(hi, tri, preferred_element_type=F32) + jnp.dot(lo, tri, preferred_element_type=F32)
    run = carry_ref[...]
    carries = [None] * nblk
    for b in reversed(range(nblk)):
        carries[b] = run
        run = run + cs[b * rows:(b + 1) * rows, 0:1]
    carry_ref[...] = run
    a = jnp.exp2(zst - cs - jnp.concatenate(carries, axis=0))
    a = jnp.concatenate([a[b * rows:(b + 1) * rows] for b in range(nblk)], axis=1).astype(BF16)
    for h in range(nh):
        acc_ref[h * t:(h + 1) * t, :] += lax.dot_general(a[h * t:(h + 1) * t], cv_ref[h].astype(BF16), NT_DIMS,
                                                         preferred_element_type=F32)

    @pl.when(c == pl.num_programs(1) - 1)
    def _():
        for h in range(nh):
            o_ref[0, h] = acc_ref[h * t:(h + 1) * t, :].astype(BF16)


def _sb_sample(q_s, cache_k, cache_v, layer, kn, vn):
    nb, nh, t, _ = q_s.shape
    past = cache_k.shape[2]
    ch = min(SB_SAMPLE_CHUNK, past)
    nch = past // ch
    ck = cache_k.transpose(0, 1, 3, 4, 2)
    cv = cache_v.transpose(0, 1, 3, 4, 2)
    cache_spec = pl.BlockSpec((None, None, nh, HEAD_DIM, ch), lambda b, c: (layer, b, 0, 0, nch - 1 - c))
    head_spec = pl.BlockSpec((1, nh, t, HEAD_DIM), lambda b, c: (b, 0, 0, 0))
    return pl.pallas_call(
        functools.partial(_sb_sample_kernel, ch=ch, kb=min(SB_TILE, ch)),
        grid=(nb, nch),
        in_specs=[head_spec, cache_spec, cache_spec, head_spec, head_spec],
        out_specs=head_spec,
        out_shape=jax.ShapeDtypeStruct((nb, nh, t, HEAD_DIM), BF16),
        scratch_shapes=[
            pltpu.VMEM((nh * t, 1), F32),
            pltpu.VMEM((nh * t, HEAD_DIM), F32),
        ],
        compiler_params=_cparams(2),
        name="sb_sample",
    )(q_s, ck, cv, kn, vn)


def _tile_col(gain, n, scale):
    return (jnp.tile(gain.astype(F32), n // gain.shape[0]) * scale).reshape(n, 1)


def _heads_major(x, nb, t, nh, hd):
    return x.reshape(nb, t, nh, hd).transpose(0, 2, 1, 3)


def _diff_bias_tiles(rel_bias, tq, t):
    j = jnp.arange(tq)
    valid_d = (j // CHUNK)[:, None] <= (j // CHUNK)[None, :]
    bias_d = _bias_tile(rel_bias, tq, tq, 0, valid_d, True).reshape(DIFF_HEADS, 2, tq, tq)
    bias_p = _bias_tile(rel_bias, NEAR, NEAR, -NEAR, None, True).reshape(DIFF_HEADS, 2, NEAR, NEAR)
    bstats = _bias_stats(rel_bias).reshape(DIFF_HEADS, 2, 2)
    bias_s = jnp.swapaxes(_bias_tile(rel_bias, NEAR + t, t, -NEAR, None, True), 1, 2)
    bias_s = bias_s.reshape(DIFF_HEADS, 2, t, NEAR + t)
    return bias_d, bias_p, bstats, bias_s


def _diff_layer(xp, xs, cache_k, cache_v, layer, n_layers, earlier, bias_tiles, g, w_qkv, q_gain, k_gain, lam_p,
                subln, lam_init):
    bias_d, bias_p, bstats, bias_s = bias_tiles
    n_p = xp.shape[0]
    nb = cache_k.shape[1]
    t = xs.shape[0] // nb
    wt = w_qkv.T.astype(BF16)
    qg = _tile_col(q_gain, D_MODEL, HEAD_DIM ** -0.5 * LOG2E)
    kg = _tile_col(k_gain, D_MODEL, 1.0)
    g = g.reshape(1, D_MODEL)
    kw = dict(nq=D_MODEL, nk=D_MODEL, nv=D_MODEL, head_norm=True)
    tq = min(DIFF_TILE, n_p)
    qT, k32, k16, v32, vT, knorm2 = _project(xp, g, wt, qg, kg, vb=tq, kstat="normsq", slot=layer,
                                             n_slots=n_layers, earlier=earlier, kv_layout="head_rows", **kw)
    qT_s, k32_s, k16_s, v32_s, _ = _project(xs, g, wt, qg, kg, vb=LANES, **kw)
    k32_s, v32_s = k32_s[0], v32_s[0]

    o_p = _diff_prompt(qT, k16, vT, bias_d, bias_p, knorm2.reshape(DIFF_HEADS, 2, 1), bstats, lam_p,
                       subln.reshape(2 * HEAD_DIM, 1), lam_init)
    hm = functools.partial(_heads_major, nb=nb, t=t, nh=DIFF_HEADS, hd=2 * HEAD_DIM)
    o_s = _diff_sample(hm(qT_s.T), cache_k, cache_v, layer, hm(k16_s), hm(v32_s), bias_s, lam_p,
                       subln.reshape(1, 2 * HEAD_DIM), lam_init)
    o_s = o_s.transpose(0, 2, 1, 3).reshape(nb * t, D_MODEL)
    return (o_p, o_s,
            k32, v32,
            k32_s.reshape(nb, t, DIFF_HEADS, 2 * HEAD_DIM), v32_s.reshape(nb, t, DIFF_HEADS, 2 * HEAD_DIM))


def _swa_layer(xp, xs, ck, cv, rel_bias, g, w_qkv, q_gain, k_gain, sinks):
    n_p = xp.shape[0]
    nb, buf = ck.shape[0], ck.shape[1]
    t = xs.shape[0] // nb
    nkv = SWA_KV_HEADS * HEAD_DIM
    wt = w_qkv.T.astype(BF16)
    qg = _tile_col(q_gain, D_MODEL, HEAD_DIM ** -0.5 * LOG2E)
    kg = _tile_col(k_gain, nkv, 1.0)
    g = g.reshape(1, D_MODEL)
    kw = dict(nq=D_MODEL, nk=nkv, nv=nkv, vb=SWA_TILE, head_norm=True)
    qT, k32, k16, v32, vT = _project(xp, g, wt, qg, kg, **kw)
    qT_s, k32_s, _, v32_s, _ = _project(xs, g, wt, qg, kg, **kw)
    k32, v32, k32_s, v32_s = k32[0], v32[0], k32_s[0], v32_s[0]
    sink_col = (sinks.astype(F32) * LOG2E).reshape(SWA_Q_HEADS, 1, 1)
    wchunks = WINDOW // CHUNK

    tq = SWA_TILE
    kchunk = jnp.floor_divide(jnp.arange(2 * tq) - tq, CHUNK)[:, None]
    qchunk = (jnp.arange(tq) // CHUNK)[None, :]
    valid = (kchunk <= qchunk) & (kchunk >= qchunk - wchunks)
    bias = _bias_tile(rel_bias, 2 * tq, tq, -tq, valid, False)
    bias_g = bias.reshape(SWA_KV_HEADS, SWA_GROUP, 2 * tq, tq).transpose(0, 2, 1, 3)
    bias_g = bias_g.reshape(SWA_KV_HEADS, 2 * tq, SWA_GROUP * tq)
    sink_g = jnp.repeat(sink_col.reshape(SWA_KV_HEADS, 1, SWA_GROUP), tq, axis=2)
    o_p = _swa_prompt(qT, k16, vT, bias_g, sink_g)

    sc = jnp.floor_divide(jnp.arange(buf + t) - buf, CHUNK)[:, None]
    tc = (jnp.arange(t) // CHUNK)[None, :]
    valid_s = (sc <= tc) & (sc >= tc - wchunks)
    bias_s = jnp.swapaxes(_bias_tile(rel_bias, buf + t, t, -buf, valid_s, False), 1, 2)
    q_s = _heads_major(qT_s.T, nb, t, SWA_Q_HEADS, HEAD_DIM)
    kn = _heads_major(k32_s, nb, t, SWA_KV_HEADS, HEAD_DIM)
    vn = _heads_major(v32_s, nb, t, SWA_KV_HEADS, HEAD_DIM)
    o_s, ko, vo = _swa_sample(q_s, ck.transpose(0, 2, 1, 3), cv.transpose(0, 2, 1, 3), kn, vn, bias_s, sink_col)
    o_s = o_s.transpose(0, 2, 1, 3).reshape(nb * t, D_MODEL)
    wbuf = min(WINDOW, n_p)
    return (o_p, o_s,
            k32[n_p - wbuf:].reshape(1, wbuf, SWA_KV_HEADS, HEAD_DIM),
            v32[n_p - wbuf:].reshape(1, wbuf, SWA_KV_HEADS, HEAD_DIM),
            ko.transpose(0, 2, 1, 3), vo.transpose(0, 2, 1, 3))


def _sb_layer(xp, xs, cache_k, cache_v, layer, n_layers, earlier, g, w_qkv):
    n_p = xp.shape[0]
    nb = cache_k.shape[1]
    t = xs.shape[0] // nb
    wt = w_qkv.T.astype(BF16)
    qg = jnp.full((D_MODEL, 1), HEAD_DIM ** -0.5 * LOG2E, F32)
    kg = jnp.ones((D_MODEL, 1), F32)
    g = g.reshape(1, D_MODEL)
    kw = dict(nq=D_MODEL, nk=D_MODEL, nv=D_MODEL, head_norm=False)
    tq = min(SB_TILE, n_p)
    qT, k32, k16, v32, vT, kmax = _project(xp, g, wt, qg, kg, vb=tq, kstat="absmax", slot=layer,
                                           n_slots=n_layers, earlier=earlier, kv_layout="feature", **kw)
    qT_s, k32_s, k16_s, v32_s, _ = _project(xs, g, wt, qg, kg, vb=LANES, **kw)
    k32_s, v32_s = k32_s[0], v32_s[0]
    o_p = _sb_prompt(qT, k16, vT, kmax)
    hm = functools.partial(_heads_major, nb=nb, t=t, nh=SB_HEADS, hd=HEAD_DIM)
    o_s = _sb_sample(hm(qT_s.T), cache_k, cache_v, layer, hm(k16_s), hm(v32_s))
    o_s = o_s.transpose(0, 2, 1, 3).reshape(nb * t, D_MODEL)
    return (o_p, o_s,
            k32, v32,
            k32_s.reshape(nb, t, SB_HEADS, HEAD_DIM), v32_s.reshape(nb, t, SB_HEADS, HEAD_DIM))


def kernel(x_prompt, x_sample, cache_diff_k, cache_diff_v, cache_swa_k, cache_swa_v, cache_sb_k, cache_sb_v, rel_bias, norm_mix, norm_mlp, w_up, w_down, diff_w_qkv, diff_w_o, diff_q_norm, diff_k_norm, diff_lambda, diff_subln, swa_w_qkv, swa_w_o, swa_q_norm, swa_k_norm, swa_sinks, sb_w_qkv, sb_w_o):
    bp, n_p, _ = x_prompt.shape
    assert bp == 1
    nb, t, _ = x_sample.shape
    depth = norm_mix.shape[0]
    xp = x_prompt.reshape(n_p, D_MODEL)
    xs = x_sample.reshape(nb * t, D_MODEL)
    n_diff, n_sb = cache_diff_k.shape[0], cache_sb_k.shape[0]
    outs = {name: [] for name in ("pwk", "pwv", "sdk", "sdv", "swk", "swv", "sbk", "sbv")}
    diff_kv = sb_kv = None
    diff_bias = _diff_bias_tiles(rel_bias, min(DIFF_TILE, n_p), t)
    wup16 = w_up.astype(BF16)
    wdn16 = w_down.astype(BF16)
    for i in range(depth):
        j = i // N_MIXERS
        if i % N_MIXERS == 0:
            lam_init = 0.8 - 0.6 * math.exp(-0.3 * i)
            o_p, o_s, kp, vp, kn, vn = _diff_layer(
                xp, xs, cache_diff_k, cache_diff_v, j, n_diff, diff_kv, diff_bias, norm_mix[i], diff_w_qkv[j],
                diff_q_norm[j], diff_k_norm[j], diff_lambda[j], diff_subln[j], lam_init)
            diff_kv = (kp, vp)
            w_o = diff_w_o[j]
            names = ("sdk", "sdv")
        elif i % N_MIXERS == 1:
            o_p, o_s, kp, vp, kn, vn = _swa_layer(
                xp, xs, cache_swa_k[j], cache_swa_v[j], rel_bias, norm_mix[i], swa_w_qkv[j],
                swa_q_norm[j], swa_k_norm[j], swa_sinks[j])
            outs["pwk"].append(kp)
            outs["pwv"].append(vp)
            w_o = swa_w_o[j]
            names = ("swk", "swv")
        else:
            o_p, o_s, kp, vp, kn, vn = _sb_layer(xp, xs, cache_sb_k, cache_sb_v, j, n_sb, sb_kv, norm_mix[i],
                                                 sb_w_qkv[j])
            sb_kv = (kp, vp)
            w_o = sb_w_o[j]
            names = ("sbk", "sbv")
        for name, val in zip(names, (kn, vn)):
            outs[name].append(val)
        wo16 = w_o.astype(BF16)
        g_mlp = norm_mlp[i].reshape(1, D_MODEL)
        xp = _outproj_mlp(xp, o_p, wo16, g_mlp, wup16, wdn16, i)
        xs = _outproj_mlp(xs, o_s, wo16, g_mlp, wup16, wdn16, i)
    st = {name: jnp.stack(v) for name, v in outs.items()}
    diff_shape = (n_diff, 1, n_p, DIFF_HEADS, 2 * HEAD_DIM)

    def sb_result(a):
        return a.reshape(n_sb, 1, SB_HEADS, HEAD_DIM, n_p).transpose(0, 1, 4, 2, 3)

    return (xp.reshape(1, n_p, D_MODEL), xs.reshape(nb, t, D_MODEL),
            diff_kv[0].reshape(diff_shape), diff_kv[1].reshape(diff_shape), st["pwk"], st["pwv"],
            sb_result(sb_kv[0]), sb_result(sb_kv[1]),
            st["sdk"], st["sdv"], st["swk"], st["swv"], st["sbk"], st["sbv"])
```

```python
import functools
import math

import jax
import jax.numpy as jnp
from jax import lax
from jax.experimental import pallas as pl
from jax.experimental.pallas import tpu as pltpu

F32 = jnp.float32
BF16 = jnp.bfloat16

D_MODEL = 1024
HEAD_DIM = 64
CHUNK = 64
N_MIXERS = 3
DIFF_HEADS = 8
SWA_Q_HEADS = 16
SWA_KV_HEADS = 4
SWA_GROUP = 4
WINDOW = 128
SB_HEADS = 16
N_BUCKETS = 32
MAX_DISTANCE = 128
D_FF = 4 * D_MODEL
EPS = 1e-6
LOG2E = 1.4426950408889634
NEG = -1e30
LANES = 128
NEAR = MAX_DISTANCE
FAR_BUCKET = 15
SB_SKIP_MARGIN = 145.0
SOFTMAX_BOUND_LIMIT = 100.0

V7X_VMEM_BYTES = 64 * 1024 * 1024
VMEM_LIMIT = V7X_VMEM_BYTES * 7 // 8
TOKEN_TILE = 512
PROJ_CHUNK = 512
DIFF_TILE = 512
DIFF_FAR_GROUP = 8
SB_TILE = 256
SB_KEY_TILE = 128
SB_PREV_TILES = 3
SB_PAIRS_PER_STEP = 2
SB_SAMPLE_CHUNK = 512
SWA_TILE = 128

NT_DIMS = (((1,), (1,)), ((), ()))


def _cparams(n_axes):
    return pltpu.CompilerParams(dimension_semantics=("arbitrary",) * n_axes,
                                vmem_limit_bytes=VMEM_LIMIT)


def _resident(block_shape, index_map):
    return pl.BlockSpec(block_shape, index_map, pipeline_mode=pl.Buffered(1))


def _t5_bucket(rel):
    half = N_BUCKETS // 2
    exact = half // 2
    ret = jnp.where(rel > 0, half, 0).astype(jnp.int32)
    n = jnp.abs(rel)
    nf = jnp.maximum(n, 1).astype(F32)
    large = exact + (jnp.log(nf / exact) / math.log(MAX_DISTANCE / exact) * (half - exact)).astype(jnp.int32)
    large = jnp.minimum(large, half - 1)
    return ret + jnp.where(n < exact, n, large)


def _bias_tile(rel_bias, n_j, n_i, r0, valid, shift_far):
    tab = rel_bias.astype(F32)
    period = -(-(n_i + n_j - 1) // LANES) * LANES
    u = jnp.arange(period)
    rel = jnp.where(u < n_i, r0 - u, r0 + (period - u))
    vals = jnp.moveaxis(tab[_t5_bucket(rel)], -1, 0)
    if shift_far:
        vals = vals - tab[FAR_BUCKET][:, None]
    vals = (vals * LOG2E).reshape(vals.shape[0], 1, period)

    def expand(y_ref, o_ref):
        rows = jnp.broadcast_to(y_ref[0], (n_j, period))
        o_ref[0] = pltpu.roll(rows, 0, 1, stride=1, stride_axis=0)[:, :n_i]

    b = pl.pallas_call(
        expand,
        grid=(vals.shape[0],),
        in_specs=[pl.BlockSpec((1, 1, period), lambda c: (c, 0, 0))],
        out_specs=pl.BlockSpec((1, n_j, n_i), lambda c: (c, 0, 0)),
        out_shape=jax.ShapeDtypeStruct((vals.shape[0], n_j, n_i), F32),
        compiler_params=_cparams(1),
        name="bias_toeplitz",
    )(vals)
    if valid is not None:
        b = jnp.where(valid[None], b, NEG)
    return b


def _bias_stats(rel_bias):
    tab = rel_bias.astype(F32)
    sh = (tab - tab[FAR_BUCKET][None, :]) * LOG2E
    bmax = jnp.max(sh, axis=0)
    return jnp.stack([bmax, bmax - jnp.min(sh, axis=0)], axis=-1)


def _proj_kernel(x_ref, g_ref, wt_ref, qg_ref, kg_ref, *refs, nq, nk, nv, vb, head_norm, kstat, n_alias,
                 slot, all_slots, kv_layout):
    out_refs = refs[n_alias:]
    qT_ref, k32_ref, k16_ref, v32_ref, vT_ref = out_refs[:5]
    if all_slots > 1:
        for s in range(all_slots):
            if s != slot:
                k32_ref[s] = jnp.zeros(k32_ref.shape[1:], F32)
                v32_ref[s] = jnp.zeros(v32_ref.shape[1:], F32)
        k32_ref = k32_ref.at[slot]
        v32_ref = v32_ref.at[slot]
    x = x_ref[...]
    ms = jnp.mean(x * x, axis=-1, keepdims=True)
    h = (x * lax.rsqrt(ms + EPS) * g_ref[...]).astype(BF16)
    tm = x.shape[0]

    def rows(r0, n):
        return lax.dot_general(wt_ref[r0:r0 + n, :], h, NT_DIMS, preferred_element_type=F32)

    def headnorm(t, gcol):
        n = t.shape[0]
        t3 = t.reshape(n // HEAD_DIM, HEAD_DIM, tm)
        r = lax.rsqrt(jnp.mean(t3 * t3, axis=1, keepdims=True) + EPS)
        return (t3 * r).reshape(n, tm) * gcol

    def store_kv(ref, r0, xt, x):
        n = xt.shape[0]
        if kv_layout == "feature":
            ref[r0:r0 + n, :] = xt
            return
        x = xt.T if x is None else x
        if kv_layout == "token":
            ref[:, r0:r0 + n] = x
            return
        heads = ref.shape[0] // tm
        for c in range(n // LANES):
            head = r0 // LANES + c
            ref[pl.ds(head, tm, stride=heads), :] = x[:, c * LANES:(c + 1) * LANES]

    for r0 in range(0, nq, PROJ_CHUNK):
        n = min(PROJ_CHUNK, nq - r0)
        qt = rows(r0, n)
        qt = headnorm(qt, qg_ref[r0:r0 + n, :]) if head_norm else qt * qg_ref[r0:r0 + n, :]
        qT_ref[r0:r0 + n, :] = qt.astype(BF16)

    for r0 in range(0, nk, PROJ_CHUNK):
        n = min(PROJ_CHUNK, nk - r0)
        kt = rows(nq + r0, n)
        if head_norm:
            kt = headnorm(kt, kg_ref[r0:r0 + n, :])
        k = kt.T
        store_kv(k32_ref, r0, kt, k)
        k16_ref[:, r0:r0 + n] = k.astype(BF16)
        if kstat is not None:
            stat_ref = out_refs[5]
            kr = kt.astype(BF16).astype(F32)
            if kstat == "absmax":
                stat = jnp.max(jnp.abs(kr), axis=1, keepdims=True)
                srows = slice(r0, r0 + n)
            else:
                k3 = kr.reshape(n // HEAD_DIM, HEAD_DIM, tm)
                stat = jnp.max(jnp.sum(k3 * k3, axis=1), axis=1, keepdims=True)
                srows = slice(r0 // HEAD_DIM, (r0 + n) // HEAD_DIM)

            @pl.when(pl.program_id(0) == 0)
            def _(stat=stat, srows=srows):
                stat_ref[srows, :] = stat

            @pl.when(pl.program_id(0) > 0)
            def _(stat=stat, srows=srows):
                stat_ref[srows, :] = jnp.maximum(stat_ref[srows, :], stat)

    for r0 in range(0, nv, PROJ_CHUNK):
        n = min(PROJ_CHUNK, nv - r0)
        vt = rows(nq + nk + r0, n)
        store_kv(v32_ref, r0, vt, None)
        vt16 = vt.astype(BF16)
        for b in range(tm // vb):
            vT_ref[b, r0:r0 + n, :] = vt16[:, b * vb:(b + 1) * vb]


def _project(x, g, wt, qg, kg, *, nq, nk, nv, vb, head_norm, kstat=None, slot=0, n_slots=1, earlier=None,
             kv_layout="token"):
    n_tok = x.shape[0]
    tm = min(TOKEN_TILE, n_tok)
    vb = min(vb, tm)
    grid = (n_tok // tm,)
    ntot = nq + nk + nv

    def kv_shape(width):
        if kv_layout == "feature":
            return (n_slots, width, n_tok)
        if kv_layout == "head_rows":
            return (n_slots, n_tok * (width // LANES), LANES)
        return (n_slots, n_tok, width)

    out_shape = [
        jax.ShapeDtypeStruct((nq, n_tok), BF16),
        jax.ShapeDtypeStruct(kv_shape(nk), F32),
        jax.ShapeDtypeStruct((n_tok, nk), BF16),
        jax.ShapeDtypeStruct(kv_shape(nv), F32),
        jax.ShapeDtypeStruct((n_tok // vb, nv, vb), BF16),
    ]
    all_slots = n_slots if (earlier is None and n_slots > 1) else 1

    def kv_spec(width):
        lead = n_slots if all_slots > 1 else None
        first = 0 if all_slots > 1 else slot
        if kv_layout == "feature":
            return pl.BlockSpec((lead, width, tm), lambda i: (first, 0, i))
        if kv_layout == "head_rows":
            return pl.BlockSpec((lead, tm * (width // LANES), LANES), lambda i: (first, i, 0))
        return pl.BlockSpec((lead, tm, width), lambda i: (first, i, 0))

    out_specs = [
        pl.BlockSpec((nq, tm), lambda i: (0, i)),
        kv_spec(nk),
        pl.BlockSpec((tm, nk), lambda i: (i, 0)),
        kv_spec(nv),
        pl.BlockSpec((tm // vb, nv, vb), lambda i: (i, 0, 0)),
    ]
    if kstat is not None:
        n_stat = nk if kstat == "absmax" else nk // HEAD_DIM
        out_shape.append(jax.ShapeDtypeStruct((n_stat, 1), F32))
        out_specs.append(pl.BlockSpec((n_stat, 1), lambda i: (0, 0)))
    in_specs = [
        pl.BlockSpec((tm, D_MODEL), lambda i: (i, 0)),
        _resident((1, D_MODEL), lambda i: (0, 0)),
        _resident((ntot, D_MODEL), lambda i: (0, 0)),
        _resident((nq, 1), lambda i: (0, 0)),
        _resident((nk, 1), lambda i: (0, 0)),
    ]
    args = [x, g, wt, qg, kg]
    aliases = {}
    if earlier is not None:
        aliases = {len(args): 1, len(args) + 1: 3}
        in_specs += [pl.BlockSpec(memory_space=pl.ANY)] * 2
        args += list(earlier)
    return pl.pallas_call(
        functools.partial(_proj_kernel, nq=nq, nk=nk, nv=nv, vb=vb, head_norm=head_norm, kstat=kstat,
                          n_alias=len(aliases), slot=slot, all_slots=all_slots,
                          kv_layout=kv_layout),
        grid=grid,
        in_specs=in_specs,
        out_specs=out_specs,
        out_shape=out_shape,
        input_output_aliases=aliases,
        compiler_params=_cparams(1),
        name="proj",
    )(*args)


def _mlp_kernel(x_ref, o_ref, wo_ref, g_ref, wup_ref, wdn_ref, y_ref, *, f_chunk):
    x1 = x_ref[...] + jnp.dot(o_ref[...], wo_ref[...], preferred_element_type=F32)
    ms = jnp.mean(x1 * x1, axis=-1, keepdims=True)
    hn = (x1 * lax.rsqrt(ms + EPS) * g_ref[...]).astype(BF16)
    acc = x1
    for f in range(D_FF // f_chunk):
        u = jnp.dot(hn, wup_ref[:, f * f_chunk:(f + 1) * f_chunk], preferred_element_type=F32)
        a = jnp.square(jnp.maximum(u, 0.0)).astype(BF16)
        acc = acc + jnp.dot(a, wdn_ref[f * f_chunk:(f + 1) * f_chunk, :], preferred_element_type=F32)
    y_ref[...] = acc


def _outproj_mlp(x, o, wo, g, wup, wdn, layer):
    n_tok = x.shape[0]
    tm = min(TOKEN_TILE, n_tok)
    return pl.pallas_call(
        functools.partial(_mlp_kernel, f_chunk=1024),
        grid=(n_tok // tm,),
        in_specs=[
            pl.BlockSpec((tm, D_MODEL), lambda i: (i, 0)),
            pl.BlockSpec((tm, D_MODEL), lambda i: (i, 0)),
            _resident((D_MODEL, D_MODEL), lambda i: (0, 0)),
            _resident((1, D_MODEL), lambda i: (0, 0)),
            _resident((None, D_MODEL, D_FF), lambda i: (layer, 0, 0)),
            _resident((None, D_FF, D_MODEL), lambda i: (layer, 0, 0)),
        ],
        out_specs=pl.BlockSpec((tm, D_MODEL), lambda i: (i, 0)),
        out_shape=jax.ShapeDtypeStruct((n_tok, D_MODEL), F32),
        compiler_params=_cparams(1),
        name="outproj_mlp",
    )(x, o, wo, g, wup, wdn)


def _diff_lambda(lam_ref, lam_init):
    lp = lam_ref[...]
    a = jnp.sum(lp[0:1] * lp[1:2], axis=-1, keepdims=True)
    b = jnp.sum(lp[2:3] * lp[3:4], axis=-1, keepdims=True)
    return jnp.exp(a) - jnp.exp(b) + lam_init


def _diff_prompt_kernel(qT_ref, k_ref, vT_ref, bd_ref, bp_ref, kn_ref, bs_ref, lam_ref, sub_ref, o_ref,
                        m_ref, l_ref, acc_ref, *, tq, lam_init):
    qi = pl.program_id(1)
    qT = qT_ref[...]
    row = lax.broadcasted_iota(jnp.int32, qT.shape, 0)
    zero = jnp.zeros_like(qT)
    qm = (jnp.where(row < HEAD_DIM, qT, zero), jnp.where(row >= HEAD_DIM, qT, zero))

    def prev_bias(m, s):
        corner = s[tq - NEAR:, :NEAR] + bp_ref[0, m]
        bottom = jnp.concatenate([corner, s[tq - NEAR:, NEAR:]], axis=1)
        return jnp.concatenate([s[:tq - NEAR], bottom], axis=0)

    def diag_bias(m, s):
        return s + bd_ref[0, m]

    def prev_diag_bias(m, s):
        return jnp.concatenate([prev_bias(m, s[:tq]), diag_bias(m, s[tq:])], axis=0)

    n_far = jnp.maximum(qi - 1, 0)

    def walk(tile_fn):
        def far_body(j, carry):
            tile_fn(j, None)
            return carry

        lax.fori_loop(0, n_far, far_body, 0)

        @pl.when(qi >= 1)
        def _():
            tile_fn(qi - 1, prev_bias)

        tile_fn(qi, diag_bias)

    span = []
    for m in range(2):
        qf = qm[m].astype(F32)
        qn = jnp.sqrt(jnp.sum(qf * qf, axis=0, keepdims=True))
        reach = qn * (jnp.sqrt(kn_ref[m:m + 1, :]) * 1.001)
        m_ref[m] = reach + bs_ref[m:m + 1, 0:1]
        span.append(jnp.max(2.0 * reach + bs_ref[m:m + 1, 1:2]))
    bound_is_tight = jnp.maximum(span[0], span[1]) <= SOFTMAX_BOUND_LIMIT

    @pl.when(jnp.logical_not(bound_is_tight))
    def _():
        m_ref[...] = jnp.full(m_ref.shape, NEG, F32)

        def max_tile(j, bias_fn):
            k = k_ref[j]
            for m in range(2):
                s = jnp.dot(k, qm[m], preferred_element_type=F32)
                if bias_fn is not None:
                    s = bias_fn(m, s)
                m_ref[m] = jnp.maximum(m_ref[m], jnp.max(s, axis=0, keepdims=True))

        walk(max_tile)

    l_ref[...] = jnp.zeros(l_ref.shape, F32)
    acc_ref[...] = jnp.zeros(acc_ref.shape, F32)

    def acc_tiles(js, bias_fn):
        k = jnp.concatenate([k_ref[j] for j in js], axis=0)
        vT = jnp.concatenate([vT_ref[j] for j in js], axis=1)
        nk = len(js) * tq
        for m in range(2):
            s = jnp.dot(k, qm[m], preferred_element_type=F32)
            if bias_fn is not None:
                s = bias_fn(m, s)
            p = jnp.exp2(s - m_ref[m])
            l_ref[m] += jnp.sum(p.reshape(nk // 8, 8, tq), axis=0)
            acc_ref[m] += jnp.dot(vT, p.astype(BF16), preferred_element_type=F32)

    def far_group(jj, carry):
        acc_tiles([DIFF_FAR_GROUP * jj + u for u in range(DIFF_FAR_GROUP)], None)
        return carry

    n_groups = n_far // DIFF_FAR_GROUP
    lax.fori_loop(0, n_groups, far_group, 0)
    start = n_groups * DIFF_FAR_GROUP
    left = n_far - start
    size = DIFF_FAR_GROUP // 2
    while size >= 2:
        @pl.when((left & size) != 0)
        def _(start=start, size=size):
            acc_tiles([start + u for u in range(size)], None)

        start = start + (left & size)
        size //= 2
    odd = (left & 1) != 0

    def plain_prev_diag_bias(m, s):
        return jnp.concatenate([s[:tq], prev_diag_bias(m, s[tq:])], axis=0)

    @pl.when(odd)
    def _():
        acc_tiles([start, qi - 1, qi], plain_prev_diag_bias)

    @pl.when(jnp.logical_and(qi >= 1, jnp.logical_not(odd)))
    def _():
        acc_tiles([qi - 1, qi], prev_diag_bias)

    @pl.when(qi == 0)
    def _():
        acc_tiles([qi], diag_bias)

    lam = _diff_lambda(lam_ref, lam_init)
    o0 = acc_ref[0] * (1.0 / jnp.sum(l_ref[0], axis=0, keepdims=True))
    o1 = acc_ref[1] * (1.0 / jnp.sum(l_ref[1], axis=0, keepdims=True))
    o = o0 - lam * o1
    ms = jnp.mean(o * o, axis=0, keepdims=True)
    o = o * lax.rsqrt(ms + EPS) * sub_ref[...] * (1.0 - lam_init)
    o_ref[...] = o.T.astype(BF16)


def _diff_prompt(qT, k16, vT, bias_d, bias_p, knorm2, bstats, lam_p, subln, lam_init):
    n_tok = k16.shape[0]
    tq = min(DIFF_TILE, n_tok)
    nb = n_tok // tq
    k3 = k16.reshape(nb, tq, D_MODEL)
    return pl.pallas_call(
        functools.partial(_diff_prompt_kernel, tq=tq, lam_init=lam_init),
        grid=(DIFF_HEADS, nb),
        in_specs=[
            pl.BlockSpec((2 * HEAD_DIM, tq), lambda h, i: (h, i)),
            pl.BlockSpec((nb, tq, 2 * HEAD_DIM), lambda h, i: (0, 0, h)),
            pl.BlockSpec((nb, 2 * HEAD_DIM, tq), lambda h, i: (0, h, 0)),
            pl.BlockSpec((1, 2, tq, tq), lambda h, i: (h, 0, 0, 0)),
            pl.BlockSpec((1, 2, NEAR, NEAR), lambda h, i: (h, 0, 0, 0)),
            pl.BlockSpec((None, 2, 1), lambda h, i: (h, 0, 0)),
            pl.BlockSpec((None, 2, 2), lambda h, i: (h, 0, 0)),
            pl.BlockSpec((4, HEAD_DIM), lambda h, i: (0, 0)),
            pl.BlockSpec((2 * HEAD_DIM, 1), lambda h, i: (0, 0)),
        ],
        out_specs=pl.BlockSpec((tq, 2 * HEAD_DIM), lambda h, i: (i, h)),
        out_shape=jax.ShapeDtypeStruct((n_tok, D_MODEL), BF16),
        scratch_shapes=[
            pltpu.VMEM((2, 1, tq), F32),
            pltpu.VMEM((2, 8, tq), F32),
            pltpu.VMEM((2, 2 * HEAD_DIM, tq), F32),
        ],
        compiler_params=_cparams(2),
        name="diff_prompt",
    )(qT, k3, vT, bias_d, bias_p, knorm2, bstats, lam_p, subln)


def _diff_sample_kernel(q_ref, ck_ref, cv_ref, kn_ref, vn_ref, b_ref, lam_ref, sub_ref, o_ref, *, lam_init, past):
    t = q_ref.shape[2]
    lane = lax.broadcasted_iota(jnp.int32, (t, 2 * HEAD_DIM), 1)
    lam = _diff_lambda(lam_ref, lam_init)
    for h in range(DIFF_HEADS):
        q = q_ref[0, h]
        zero = jnp.zeros_like(q)
        qq = jnp.concatenate([jnp.where(lane < HEAD_DIM, q, zero), jnp.where(lane >= HEAD_DIM, q, zero)], axis=0)
        ck = ck_ref[pl.ds(h, past, stride=DIFF_HEADS), :].astype(BF16)
        cv = cv_ref[pl.ds(h, past, stride=DIFF_HEADS), :].astype(BF16)
        kn = kn_ref[0, h]
        vn = vn_ref[0, h].astype(BF16)
        s_c = lax.dot_general(qq, ck, NT_DIMS, preferred_element_type=F32)
        s_n = lax.dot_general(qq, kn, NT_DIMS, preferred_element_type=F32)
        bias = b_ref[h].reshape(2 * t, NEAR + t)
        s_far = s_c[:, :past - NEAR]
        s_near = s_c[:, past - NEAR:] + bias[:, :NEAR]
        s_n = s_n + bias[:, NEAR:]
        m = jnp.maximum(jnp.maximum(jnp.max(s_far, axis=1, keepdims=True), jnp.max(s_near, axis=1, keepdims=True)),
                        jnp.max(s_n, axis=1, keepdims=True))
        p_far = jnp.exp2(s_far - m)
        p_near = jnp.exp2(s_near - m)
        p_n = jnp.exp2(s_n - m)
        l = (jnp.sum(p_far, axis=1, keepdims=True) + jnp.sum(p_near, axis=1, keepdims=True)
             + jnp.sum(p_n, axis=1, keepdims=True))
        o = (jnp.dot(p_far.astype(BF16), cv[:past - NEAR], preferred_element_type=F32)
             + jnp.dot(p_near.astype(BF16), cv[past - NEAR:], preferred_element_type=F32)
             + jnp.dot(p_n.astype(BF16), vn, preferred_element_type=F32))
        o = o * (1.0 / l)
        od = o[:t] - lam * o[t:]
        ms = jnp.mean(od * od, axis=-1, keepdims=True)
        od = od * lax.rsqrt(ms + EPS) * sub_ref[...] * (1.0 - lam_init)
        o_ref[0, h] = od.astype(BF16)


def _diff_sample(q_s, cache_k, cache_v, layer, kn, vn, bias_s, lam_p, subln_row, lam_init):
    nb, _, t, _ = q_s.shape
    past = cache_k.shape[2]
    rows = past * DIFF_HEADS
    ck = cache_k.reshape(cache_k.shape[0], nb, rows, 2 * HEAD_DIM)
    cv = cache_v.reshape(cache_v.shape[0], nb, rows, 2 * HEAD_DIM)
    cache_spec = pl.BlockSpec((None, None, rows, 2 * HEAD_DIM), lambda b: (layer, b, 0, 0))
    head_spec = pl.BlockSpec((1, DIFF_HEADS, t, 2 * HEAD_DIM), lambda b: (b, 0, 0, 0))
    return pl.pallas_call(
        functools.partial(_diff_sample_kernel, lam_init=lam_init, past=past),
        grid=(nb,),
        in_specs=[
            head_spec, cache_spec, cache_spec, head_spec, head_spec,
            pl.BlockSpec((DIFF_HEADS, 2, t, NEAR + t), lambda b: (0, 0, 0, 0)),
            pl.BlockSpec((4, HEAD_DIM), lambda b: (0, 0)),
            pl.BlockSpec((1, 2 * HEAD_DIM), lambda b: (0, 0)),
        ],
        out_specs=head_spec,
        out_shape=jax.ShapeDtypeStruct((nb, DIFF_HEADS, t, 2 * HEAD_DIM), BF16),
        compiler_params=_cparams(1),
        name="diff_sample",
    )(q_s, ck, cv, kn, vn, bias_s, lam_p, subln_row)


def _swa_prompt_kernel(qT_ref, kc_ref, kp_ref, vc_ref, vp_ref, b_ref, sink_ref, o_ref):
    i = pl.program_id(0)
    qT = qT_ref[...]
    kp = kp_ref[...]
    kc = kc_ref[...]
    vp = vp_ref[0]
    vc = vc_ref[0]
    tq = qT.shape[1]
    key_row = lax.broadcasted_iota(jnp.int32, (2 * tq, SWA_GROUP * tq), 0)
    has_prev = key_row >= jnp.where(i > 0, 0, tq)
    zeros = jnp.zeros((HEAD_DIM, tq), BF16)
    outs = []
    for kv in range(SWA_KV_HEADS):
        pair = kv // 2
        lanes = slice(pair * 2 * HEAD_DIM, (pair + 1) * 2 * HEAD_DIM)
        k_band = jnp.concatenate([kp[:, lanes], kc[:, lanes]], axis=0)
        v_band = jnp.concatenate([vp[kv * HEAD_DIM:(kv + 1) * HEAD_DIM], vc[kv * HEAD_DIM:(kv + 1) * HEAD_DIM]],
                                 axis=1)
        qg = []
        for g in range(SWA_GROUP):
            h = kv * SWA_GROUP + g
            qh = qT[h * HEAD_DIM:(h + 1) * HEAD_DIM]
            qg.append(jnp.concatenate([qh, zeros] if kv % 2 == 0 else [zeros, qh], axis=0))
        q4 = jnp.concatenate(qg, axis=1)
        s = jnp.dot(k_band, q4, preferred_element_type=F32) + b_ref[kv]
        s = jnp.where(has_prev, s, NEG)
        sink = sink_ref[kv]
        m = jnp.maximum(jnp.max(s, axis=0, keepdims=True), sink)
        p = jnp.exp2(s - m)
        den = jnp.sum(p, axis=0, keepdims=True) + jnp.exp2(sink - m)
        oT4 = jnp.dot(v_band, p.astype(BF16), preferred_element_type=F32) * (1.0 / den)
        for g in range(SWA_GROUP):
            outs.append(oT4[:, g * tq:(g + 1) * tq])
    o_ref[...] = jnp.concatenate(outs, axis=0).T.astype(BF16)


def _swa_prompt(qT, k16, vT, bias, sinks):
    n_tok = k16.shape[0]
    tq = SWA_TILE
    nkv = SWA_KV_HEADS * HEAD_DIM
    return pl.pallas_call(
        _swa_prompt_kernel,
        grid=(n_tok // tq,),
        in_specs=[
            pl.BlockSpec((D_MODEL, tq), lambda i: (0, i)),
            pl.BlockSpec((tq, nkv), lambda i: (i, 0)),
            pl.BlockSpec((tq, nkv), lambda i: (jnp.maximum(i - 1, 0), 0)),
            pl.BlockSpec((1, nkv, tq), lambda i: (i, 0, 0)),
            pl.BlockSpec((1, nkv, tq), lambda i: (jnp.maximum(i - 1, 0), 0, 0)),
            _resident((SWA_KV_HEADS, 2 * tq, SWA_GROUP * tq), lambda i: (0, 0, 0)),
            _resident((SWA_KV_HEADS, 1, SWA_GROUP * tq), lambda i: (0, 0, 0)),
        ],
        out_specs=pl.BlockSpec((tq, D_MODEL), lambda i: (i, 0)),
        out_shape=jax.ShapeDtypeStruct((n_tok, D_MODEL), BF16),
        compiler_params=_cparams(1),
        name="swa_prompt",
    )(qT, k16, k16, vT, vT, bias, sinks)


def _swa_sample_kernel(q_ref, kc_ref, vc_ref, kn_ref, vn_ref, b_ref, sink_ref, o_ref, ko_ref, vo_ref):
    t = kn_ref.shape[2]
    buf = kc_ref.shape[2]
    for kv in range(SWA_KV_HEADS):
        kc32 = kc_ref[0, kv]
        vc32 = vc_ref[0, kv]
        kn32 = kn_ref[0, kv]
        vn32 = vn_ref[0, kv]
        ko_ref[0, kv, 0:buf - t, :] = kc32[t:]
        ko_ref[0, kv, buf - t:buf, :] = kn32
        vo_ref[0, kv, 0:buf - t, :] = vc32[t:]
        vo_ref[0, kv, buf - t:buf, :] = vn32
        kc = kc32.astype(BF16)
        vc = vc32.astype(BF16)
        kn = kn32.astype(BF16)
        vn = vn32.astype(BF16)
        heads = range(kv * SWA_GROUP, (kv + 1) * SWA_GROUP)
        q = jnp.concatenate([q_ref[0, h] for h in heads], axis=0)
        b = jnp.concatenate([b_ref[h] for h in heads], axis=0)
        sink = jnp.concatenate([jnp.broadcast_to(sink_ref[h], (t, 1)) for h in heads], axis=0)
        s_c = lax.dot_general(q, kc, NT_DIMS, preferred_element_type=F32) + b[:, :buf]
        s_n = lax.dot_general(q, kn, NT_DIMS, preferred_element_type=F32) + b[:, buf:]
        m = jnp.maximum(jnp.maximum(jnp.max(s_c, axis=1, keepdims=True), jnp.max(s_n, axis=1, keepdims=True)),
                        sink)
        p_c = jnp.exp2(s_c - m)
        p_n = jnp.exp2(s_n - m)
        den = (jnp.sum(p_c, axis=1, keepdims=True) + jnp.sum(p_n, axis=1, keepdims=True)
               + jnp.exp2(sink - m))
        o = (jnp.dot(p_c.astype(BF16), vc, preferred_element_type=F32)
             + jnp.dot(p_n.astype(BF16), vn, preferred_element_type=F32)) * (1.0 / den)
        for g, h in enumerate(heads):
            o_ref[0, h] = o[g * t:(g + 1) * t].astype(BF16)


def _swa_sample(q_s, kc, vc, kn, vn, bias, sinks):
    nb, _, t, _ = q_s.shape
    buf = kc.shape[2]
    kvspec = pl.BlockSpec((1, SWA_KV_HEADS, buf, HEAD_DIM), lambda b: (b, 0, 0, 0))
    nspec = pl.BlockSpec((1, SWA_KV_HEADS, t, HEAD_DIM), lambda b: (b, 0, 0, 0))
    return pl.pallas_call(
        _swa_sample_kernel,
        grid=(nb,),
        in_specs=[
            pl.BlockSpec((1, SWA_Q_HEADS, t, HEAD_DIM), lambda b: (b, 0, 0, 0)),
            kvspec, kvspec, nspec, nspec,
            pl.BlockSpec((SWA_Q_HEADS, t, buf + t), lambda b: (0, 0, 0)),
            pl.BlockSpec((SWA_Q_HEADS, 1, 1), lambda b: (0, 0, 0)),
        ],
        out_specs=[
            pl.BlockSpec((1, SWA_Q_HEADS, t, HEAD_DIM), lambda b: (b, 0, 0, 0)),
            kvspec, kvspec,
        ],
        out_shape=[
            jax.ShapeDtypeStruct((nb, SWA_Q_HEADS, t, HEAD_DIM), BF16),
            jax.ShapeDtypeStruct((nb, SWA_KV_HEADS, buf, HEAD_DIM), F32),
            jax.ShapeDtypeStruct((nb, SWA_KV_HEADS, buf, HEAD_DIM), F32),
        ],
        compiler_params=_cparams(1),
        name="swa_sample",
    )(q_s, kc, vc, kn, vn, bias, sinks)


def _softplus(z):
    return jnp.maximum(z, 0.0) + jnp.log2(1.0 + jnp.exp2(-jnp.abs(z)))


def _split_bf16(x):
    hi = x.astype(BF16)
    lo = (x - hi.astype(F32)).astype(BF16)
    return hi, lo


def _sb_prompt_kernel(qT_ref, k_ref, vT_ref, kmax_ref, o_ref, carry_ref, acc_ref, *, tq, tk, pairs):
    qi = pl.program_id(1)
    ratio = tq // tk
    width = 2 * pairs * tq
    row = lax.broadcasted_iota(jnp.int32, (2 * HEAD_DIM, tq), 0)
    qpairs, zbs = [], []
    for pr in range(pairs):
        qT = qT_ref[2 * HEAD_DIM * pr:2 * HEAD_DIM * (pr + 1), :]
        zero = jnp.zeros_like(qT)
        qp = jnp.concatenate([jnp.where(row < HEAD_DIM, qT, zero), jnp.where(row >= HEAD_DIM, qT, zero)], axis=1)
        qpairs.append(qp)
        kmax = kmax_ref[2 * HEAD_DIM * pr:2 * HEAD_DIM * (pr + 1), :]
        zbs.append(jnp.sum(jnp.abs(qp.astype(F32)) * kmax, axis=0, keepdims=True))
    zb = jnp.concatenate(zbs, axis=1)
    kr = lax.broadcasted_iota(jnp.int32, (tk, tk), 0)
    kc = lax.broadcasted_iota(jnp.int32, (tk, tk), 1)
    tri = jnp.where(kc >= kr, 1.0, 0.0).astype(BF16)
    tri2 = jnp.concatenate([tri, tri], axis=1)
    krw = lax.broadcasted_iota(jnp.int32, (tk, width), 0)
    qcol = lax.rem(lax.broadcasted_iota(jnp.int32, (tk, width), 1), tq)
    causal = [krw + u * tk < qcol for u in range(ratio)]
    carry_ref[...] = jnp.zeros(carry_ref.shape, F32)
    acc_ref[...] = jnp.zeros(acc_ref.shape, F32)

    def tiles(js, with_diag):
        masks = [causal[ratio - 1 - n] if (with_diag and n < ratio) else None for n in range(len(js))]
        zs, cs = [], []
        for n, j in enumerate(js):
            k = k_ref[j]
            z = jnp.concatenate([jnp.dot(k[:, 2 * HEAD_DIM * pr:2 * HEAD_DIM * (pr + 1)], qpairs[pr],
                                         preferred_element_type=F32) for pr in range(pairs)], axis=1)
            lsp = _softplus(z)
            if masks[n] is not None:
                lsp = jnp.where(masks[n], lsp, 0.0)
            hi, lo = _split_bf16(lsp)
            zs.append(z)
            cs.append(jnp.dot(tri2, jnp.concatenate([hi, lo], axis=0), preferred_element_type=F32))
        carry = carry_ref[...]
        weights = []
        for n in range(len(js)):
            a = jnp.exp2(zs[n] - cs[n] - carry)
            if masks[n] is not None:
                a = jnp.where(masks[n], a, 0.0)
            weights.append(a.astype(BF16))
            carry = carry + cs[n][0:1]
        carry_ref[...] = carry
        for pr in range(pairs):
            vT = jnp.concatenate([vT_ref[j, 2 * HEAD_DIM * pr:2 * HEAD_DIM * (pr + 1), :] for j in js],
                                 axis=1)
            w = jnp.concatenate([a[:, 2 * tq * pr:2 * tq * (pr + 1)] for a in weights], axis=0)
            acc_ref[pr] += jnp.dot(vT, w, preferred_element_type=F32)

    def still_live():
        return (jnp.min(carry_ref[...] - zb) < SB_SKIP_MARGIN).astype(jnp.int32)

    base = ratio * qi
    first_blocks = []
    q = 0
    while ratio * q < SB_PREV_TILES:
        first_blocks.append((qi == q, ratio * q))
        q += 1
    first_blocks.append((qi >= q, SB_PREV_TILES))
    for when, n_prev in first_blocks:
        @pl.when(when)
        def _(n_prev=n_prev):
            tiles([base + ratio - 1 - u for u in range(ratio)] + [base - 1 - u for u in range(n_prev)], True)

    def cond(state):
        j, live = state
        return jnp.logical_and(j >= 0, live > 0)

    def body(state):
        j, _ = state
        tiles([j], False)
        return j - 1, still_live()

    lax.while_loop(cond, body, (jnp.maximum(base - 1 - SB_PREV_TILES, -1), still_live()))
    outs = []
    for pr in range(pairs):
        outs += [acc_ref[pr, 0:HEAD_DIM, 0:tq], acc_ref[pr, HEAD_DIM:2 * HEAD_DIM, tq:2 * tq]]
    o_ref[...] = jnp.concatenate(outs, axis=0).T.astype(BF16)


def _sb_prompt(qT, k16, vT, kmax):
    n_tok = k16.shape[0]
    tq = min(SB_TILE, n_tok)
    tk = min(SB_KEY_TILE, tq)
    nb = n_tok // tq
    nbk = n_tok // tk
    k3 = k16.reshape(nbk, tk, D_MODEL)
    pairs = SB_PAIRS_PER_STEP
    wide = 2 * HEAD_DIM * pairs
    return pl.pallas_call(
        functools.partial(_sb_prompt_kernel, tq=tq, tk=tk, pairs=pairs),
        grid=(SB_HEADS // (2 * pairs), nb),
        in_specs=[
            pl.BlockSpec((wide, tq), lambda p, i: (p, i)),
            pl.BlockSpec((nbk, tk, wide), lambda p, i: (0, 0, p)),
            pl.BlockSpec((nbk, wide, tk), lambda p, i: (0, p, 0)),
            pl.BlockSpec((wide, 1), lambda p, i: (p, 0)),
        ],
        out_specs=pl.BlockSpec((tq, wide), lambda p, i: (i, p)),
        out_shape=jax.ShapeDtypeStruct((n_tok, D_MODEL), BF16),
        scratch_shapes=[
            pltpu.VMEM((1, 2 * pairs * tq), F32),
            pltpu.VMEM((pairs, 2 * HEAD_DIM, 2 * tq), F32),
        ],
        compiler_params=_cparams(2),
        name="sb_prompt",
    )(qT, k3, vT, kmax)


def _tri_lanes(n):
    r = lax.broadcasted_iota(jnp.int32, (n, n), 0)
    c = lax.broadcasted_iota(jnp.int32, (n, n), 1)
    return jnp.where(r >= c, 1.0, 0.0).astype(BF16)


def _sb_sample_kernel(q_ref, ck_ref, cv_ref, kn_ref, vn_ref, o_ref, carry_ref, acc_ref, *, ch, kb):
    c = pl.program_id(1)
    nh = q_ref.shape[1]
    t = q_ref.shape[2]

    @pl.when(c == 0)
    def _():
        z = jnp.concatenate([lax.dot_general(q_ref[0, h], kn_ref[0, h], NT_DIMS, preferred_element_type=F32)
                             for h in range(nh)], axis=0)
        r = lax.broadcasted_iota(jnp.int32, (nh * t, t), 0)
        kc = lax.broadcasted_iota(jnp.int32, (nh * t, t), 1)
        causal = kc < lax.rem(r, t)
        hi, lo = _split_bf16(jnp.where(causal, _softplus(z), 0.0))
        tri = _tri_lanes(t)
        cs = jnp.dot(hi, tri, preferred_element_type=F32) + jnp.dot(lo, tri, preferred_element_type=F32)
        a = jnp.where(causal, jnp.exp2(z - cs), 0.0).astype(BF16)
        carry_ref[...] = cs[:, 0:1]
        for h in range(nh):
            acc_ref[h * t:(h + 1) * t, :] = jnp.dot(a[h * t:(h + 1) * t], vn_ref[0, h].astype(BF16),
                                                    preferred_element_type=F32)

    z = jnp.concatenate([jnp.dot(q_ref[0, h], ck_ref[h].astype(BF16), preferred_element_type=F32)
                         for h in range(nh)], axis=0)
    nblk = ch // kb
    rows = nh * t
    zst = jnp.concatenate([z[:, b * kb:(b + 1) * kb] for b in range(nblk)], axis=0)
    hi, lo = _split_bf16(_softplus(zst))
    tri = _tri_lanes(kb)
    cs = jnp.dot(hi, tri, preferred_element_type=F32) + jnp.dot(lo, tri, preferred_element_type=F32)
    run = carry_ref[...]
    carries = [None] * nblk
    for b in reversed(range(nblk)):
        carries[b] = run
        run = run + cs[b * rows:(b + 1) * rows, 0:1]
    carry_ref[...] = run
    a = jnp.exp2(zst - cs - jnp.concatenate(carries, axis=0))
    a = jnp.concatenate([a[b * rows:(b + 1) * rows] for b in range(nblk)], axis=1).astype(BF16)
    for h in range(nh):
        acc_ref[h * t:(h + 1) * t, :] += lax.dot_general(a[h * t:(h + 1) * t], cv_ref[h].astype(BF16), NT_DIMS,
                                                         preferred_element_type=F32)

    @pl.when(c == pl.num_programs(1) - 1)
    def _():
        for h in range(nh):
            o_ref[0, h] = acc_ref[h * t:(h + 1) * t, :].astype(BF16)


def _sb_sample(q_s, cache_k, cache_v, layer, kn, vn):
    nb, nh, t, _ = q_s.shape
    past = cache_k.shape[2]
    ch = min(SB_SAMPLE_CHUNK, past)
    nch = past // ch
    ck = cache_k.transpose(0, 1, 3, 4, 2)
    cv = cache_v.transpose(0, 1, 3, 4, 2)
    cache_spec = pl.BlockSpec((None, None, nh, HEAD_DIM, ch), lambda b, c: (layer, b, 0, 0, nch - 1 - c))
    head_spec = pl.BlockSpec((1, nh, t, HEAD_DIM), lambda b, c: (b, 0, 0, 0))
    return pl.pallas_call(
        functools.partial(_sb_sample_kernel, ch=ch, kb=min(SB_TILE, ch)),
        grid=(nb, nch),
        in_specs=[head_spec, cache_spec, cache_spec, head_spec, head_spec],
        out_specs=head_spec,
        out_shape=jax.ShapeDtypeStruct((nb, nh, t, HEAD_DIM), BF16),
        scratch_shapes=[
            pltpu.VMEM((nh * t, 1), F32),
            pltpu.VMEM((nh * t, HEAD_DIM), F32),
        ],
        compiler_params=_cparams(2),
        name="sb_sample",
    )(q_s, ck, cv, kn, vn)


def _tile_col(gain, n, scale):
    return (jnp.tile(gain.astype(F32), n // gain.shape[0]) * scale).reshape(n, 1)


def _heads_major(x, nb, t, nh, hd):
    return x.reshape(nb, t, nh, hd).transpose(0, 2, 1, 3)


def _diff_bias_tiles(rel_bias, tq, t):
    j = jnp.arange(tq)
    valid_d = (j // CHUNK)[:, None] <= (j // CHUNK)[None, :]
    bias_d = _bias_tile(rel_bias, tq, tq, 0, valid_d, True).reshape(DIFF_HEADS, 2, tq, tq)
    bias_p = _bias_tile(rel_bias, NEAR, NEAR, -NEAR, None, True).reshape(DIFF_HEADS, 2, NEAR, NEAR)
    bstats = _bias_stats(rel_bias).reshape(DIFF_HEADS, 2, 2)
    bias_s = jnp.swapaxes(_bias_tile(rel_bias, NEAR + t, t, -NEAR, None, True), 1, 2)
    bias_s = bias_s.reshape(DIFF_HEADS, 2, t, NEAR + t)
    return bias_d, bias_p, bstats, bias_s


def _diff_layer(xp, xs, cache_k, cache_v, layer, n_layers, earlier, bias_tiles, g, w_qkv, q_gain, k_gain, lam_p,
                subln, lam_init):
    bias_d, bias_p, bstats, bias_s = bias_tiles
    n_p = xp.shape[0]
    nb = cache_k.shape[1]
    t = xs.shape[0] // nb
    wt = w_qkv.T.astype(BF16)
    qg = _tile_col(q_gain, D_MODEL, HEAD_DIM ** -0.5 * LOG2E)
    kg = _tile_col(k_gain, D_MODEL, 1.0)
    g = g.reshape(1, D_MODEL)
    kw = dict(nq=D_MODEL, nk=D_MODEL, nv=D_MODEL, head_norm=True)
    tq = min(DIFF_TILE, n_p)
    qT, k32, k16, v32, vT, knorm2 = _project(xp, g, wt, qg, kg, vb=tq, kstat="normsq", slot=layer,
                                             n_slots=n_layers, earlier=earlier, kv_layout="head_rows", **kw)
    qT_s, k32_s, k16_s, v32_s, _ = _project(xs, g, wt, qg, kg, vb=LANES, **kw)
    k32_s, v32_s = k32_s[0], v32_s[0]

    o_p = _diff_prompt(qT, k16, vT, bias_d, bias_p, knorm2.reshape(DIFF_HEADS, 2, 1), bstats, lam_p,
                       subln.reshape(2 * HEAD_DIM, 1), lam_init)
    hm = functools.partial(_heads_major, nb=nb, t=t, nh=DIFF_HEADS, hd=2 * HEAD_DIM)
    o_s = _diff_sample(hm(qT_s.T), cache_k, cache_v, layer, hm(k16_s), hm(v32_s), bias_s, lam_p,
                       subln.reshape(1, 2 * HEAD_DIM), lam_init)
    o_s = o_s.transpose(0, 2, 1, 3).reshape(nb * t, D_MODEL)
    return (o_p, o_s,
            k32, v32,
            k32_s.reshape(nb, t, DIFF_HEADS, 2 * HEAD_DIM), v32_s.reshape(nb, t, DIFF_HEADS, 2 * HEAD_DIM))


def _swa_layer(xp, xs, ck, cv, rel_bias, g, w_qkv, q_gain, k_gain, sinks):
    n_p = xp.shape[0]
    nb, buf = ck.shape[0], ck.shape[1]
    t = xs.shape[0] // nb
    nkv = SWA_KV_HEADS * HEAD_DIM
    wt = w_qkv.T.astype(BF16)
    qg = _tile_col(q_gain, D_MODEL, HEAD_DIM ** -0.5 * LOG2E)
    kg = _tile_col(k_gain, nkv, 1.0)
    g = g.reshape(1, D_MODEL)
    kw = dict(nq=D_MODEL, nk=nkv, nv=nkv, vb=SWA_TILE, head_norm=True)
    qT, k32, k16, v32, vT = _project(xp, g, wt, qg, kg, **kw)
    qT_s, k32_s, _, v32_s, _ = _project(xs, g, wt, qg, kg, **kw)
    k32, v32, k32_s, v32_s = k32[0], v32[0], k32_s[0], v32_s[0]
    sink_col = (sinks.astype(F32) * LOG2E).reshape(SWA_Q_HEADS, 1, 1)
    wchunks = WINDOW // CHUNK

    tq = SWA_TILE
    kchunk = jnp.floor_divide(jnp.arange(2 * tq) - tq, CHUNK)[:, None]
    qchunk = (jnp.arange(tq) // CHUNK)[None, :]
    valid = (kchunk <= qchunk) & (kchunk >= qchunk - wchunks)
    bias = _bias_tile(rel_bias, 2 * tq, tq, -tq, valid, False)
    bias_g = bias.reshape(SWA_KV_HEADS, SWA_GROUP, 2 * tq, tq).transpose(0, 2, 1, 3)
    bias_g = bias_g.reshape(SWA_KV_HEADS, 2 * tq, SWA_GROUP * tq)
    sink_g = jnp.repeat(sink_col.reshape(SWA_KV_HEADS, 1, SWA_GROUP), tq, axis=2)
    o_p = _swa_prompt(qT, k16, vT, bias_g, sink_g)

    sc = jnp.floor_divide(jnp.arange(buf + t) - buf, CHUNK)[:, None]
    tc = (jnp.arange(t) // CHUNK)[None, :]
    valid_s = (sc <= tc) & (sc >= tc - wchunks)
    bias_s = jnp.swapaxes(_bias_tile(rel_bias, buf + t, t, -buf, valid_s, False), 1, 2)
    q_s = _heads_major(qT_s.T, nb, t, SWA_Q_HEADS, HEAD_DIM)
    kn = _heads_major(k32_s, nb, t, SWA_KV_HEADS, HEAD_DIM)
    vn = _heads_major(v32_s, nb, t, SWA_KV_HEADS, HEAD_DIM)
    o_s, ko, vo = _swa_sample(q_s, ck.transpose(0, 2, 1, 3), cv.transpose(0, 2, 1, 3), kn, vn, bias_s, sink_col)
    o_s = o_s.transpose(0, 2, 1, 3).reshape(nb * t, D_MODEL)
    wbuf = min(WINDOW, n_p)
    return (o_p, o_s,
            k32[n_p - wbuf:].reshape(1, wbuf, SWA_KV_HEADS, HEAD_DIM),
            v32[n_p - wbuf:].reshape(1, wbuf, SWA_KV_HEADS, HEAD_DIM),
            ko.transpose(0, 2, 1, 3), vo.transpose(0, 2, 1, 3))


def _sb_layer(xp, xs, cache_k, cache_v, layer, n_layers, earlier, g, w_qkv):
    n_p = xp.shape[0]
    nb = cache_k.shape[1]
    t = xs.shape[0] // nb
    wt = w_qkv.T.astype(BF16)
    qg = jnp.full((D_MODEL, 1), HEAD_DIM ** -0.5 * LOG2E, F32)
    kg = jnp.ones((D_MODEL, 1), F32)
    g = g.reshape(1, D_MODEL)
    kw = dict(nq=D_MODEL, nk=D_MODEL, nv=D_MODEL, head_norm=False)
    tk = min(SB_KEY_TILE, SB_TILE, n_p)
    qT, k32, k16, v32, vT, kmax = _project(xp, g, wt, qg, kg, vb=tk, kstat="absmax", slot=layer,
                                           n_slots=n_layers, earlier=earlier, kv_layout="feature", **kw)
    qT_s, k32_s, k16_s, v32_s, _ = _project(xs, g, wt, qg, kg, vb=LANES, **kw)
    k32_s, v32_s = k32_s[0], v32_s[0]
    o_p = _sb_prompt(qT, k16, vT, kmax)
    hm = functools.partial(_heads_major, nb=nb, t=t, nh=SB_HEADS, hd=HEAD_DIM)
    o_s = _sb_sample(hm(qT_s.T), cache_k, cache_v, layer, hm(k16_s), hm(v32_s))
    o_s = o_s.transpose(0, 2, 1, 3).reshape(nb * t, D_MODEL)
    return (o_p, o_s,
            k32, v32,
            k32_s.reshape(nb, t, SB_HEADS, HEAD_DIM), v32_s.reshape(nb, t, SB_HEADS, HEAD_DIM))


def kernel(x_prompt, x_sample, cache_diff_k, cache_diff_v, cache_swa_k, cache_swa_v, cache_sb_k, cache_sb_v, rel_bias, norm_mix, norm_mlp, w_up, w_down, diff_w_qkv, diff_w_o, diff_q_norm, diff_k_norm, diff_lambda, diff_subln, swa_w_qkv, swa_w_o, swa_q_norm, swa_k_norm, swa_sinks, sb_w_qkv, sb_w_o):
    bp, n_p, _ = x_prompt.shape
    assert bp == 1
    nb, t, _ = x_sample.shape
    depth = norm_mix.shape[0]
    xp = x_prompt.reshape(n_p, D_MODEL)
    xs = x_sample.reshape(nb * t, D_MODEL)
    n_diff, n_sb = cache_diff_k.shape[0], cache_sb_k.shape[0]
    outs = {name: [] for name in ("pwk", "pwv", "sdk", "sdv", "swk", "swv", "sbk", "sbv")}
    diff_kv = sb_kv = None
    diff_bias = _diff_bias_tiles(rel_bias, min(DIFF_TILE, n_p), t)
    wup16 = w_up.astype(BF16)
    wdn16 = w_down.astype(BF16)
    for i in range(depth):
        j = i // N_MIXERS
        if i % N_MIXERS == 0:
            lam_init = 0.8 - 0.6 * math.exp(-0.3 * i)
            o_p, o_s, kp, vp, kn, vn = _diff_layer(
                xp, xs, cache_diff_k, cache_diff_v, j, n_diff, diff_kv, diff_bias, norm_mix[i], diff_w_qkv[j],
                diff_q_norm[j], diff_k_norm[j], diff_lambda[j], diff_subln[j], lam_init)
            diff_kv = (kp, vp)
            w_o = diff_w_o[j]
            names = ("sdk", "sdv")
        elif i % N_MIXERS == 1:
            o_p, o_s, kp, vp, kn, vn = _swa_layer(
                xp, xs, cache_swa_k[j], cache_swa_v[j], rel_bias, norm_mix[i], swa_w_qkv[j],
                swa_q_norm[j], swa_k_norm[j], swa_sinks[j])
            outs["pwk"].append(kp)
            outs["pwv"].append(vp)
            w_o = swa_w_o[j]
            names = ("swk", "swv")
        else:
            o_p, o_s, kp, vp, kn, vn = _sb_layer(xp, xs, cache_sb_k, cache_sb_v, j, n_sb, sb_kv, norm_mix[i],
                                                 sb_w_qkv[j])
            sb_kv = (kp, vp)
            w_o = sb_w_o[j]
            names = ("sbk", "sbv")
        for name, val in zip(names, (kn, vn)):
            outs[name].append(val)
        wo16 = w_o.astype(BF16)
        g_mlp = norm_mlp[i].reshape(1, D_MODEL)
        xp = _outproj_mlp(xp, o_p, wo16, g_mlp, wup16, wdn16, i)
        xs = _outproj_mlp(xs, o_s, wo16, g_mlp, wup16, wdn16, i)
    st = {name: jnp.stack(v) for name, v in outs.items()}
    diff_shape = (n_diff, 1, n_p, DIFF_HEADS, 2 * HEAD_DIM)

    def sb_result(a):
        return a.reshape(n_sb, 1, SB_HEADS, HEAD_DIM, n_p).transpose(0, 1, 4, 2, 3)

    return (xp.reshape(1, n_p, D_MODEL), xs.reshape(nb, t, D_MODEL),
            diff_kv[0].reshape(diff_shape), diff_kv[1].reshape(diff_shape), st["pwk"], st["pwv"],
            sb_result(sb_kv[0]), sb_result(sb_kv[1]),
            st["sdk"], st["sdv"], st["swk"], st["swv"], st["sbk"], st["sbv"])
```

```python
import functools
import math

import jax
import jax.numpy as jnp
from jax import lax
from jax.experimental import pallas as pl
from jax.experimental.pallas import tpu as pltpu

F32 = jnp.float32
BF16 = jnp.bfloat16

D_MODEL = 1024
HEAD_DIM = 64
CHUNK = 64
N_MIXERS = 3
DIFF_HEADS = 8
SWA_Q_HEADS = 16
SWA_KV_HEADS = 4
SWA_GROUP = 4
WINDOW = 128
SB_HEADS = 16
N_BUCKETS = 32
MAX_DISTANCE = 128
D_FF = 4 * D_MODEL
EPS = 1e-6
LOG2E = 1.4426950408889634
NEG = -1e30
LANES = 128
NEAR = MAX_DISTANCE
FAR_BUCKET = 15
SB_SKIP_MARGIN = 145.0
SOFTMAX_BOUND_LIMIT = 100.0

V7X_VMEM_BYTES = 64 * 1024 * 1024
VMEM_LIMIT = V7X_VMEM_BYTES * 7 // 8
TOKEN_TILE = 512
PROJ_CHUNK = 512
DIFF_TILE = 512
DIFF_FAR_GROUP = 8
SB_TILE = 256
SB_PAIRS_PER_STEP = 2
SB_SAMPLE_CHUNK = 512
SWA_TILE = 128

NT_DIMS = (((1,), (1,)), ((), ()))


def _cparams(n_axes):
    return pltpu.CompilerParams(dimension_semantics=("arbitrary",) * n_axes,
                                vmem_limit_bytes=VMEM_LIMIT)


def _resident(block_shape, index_map):
    return pl.BlockSpec(block_shape, index_map, pipeline_mode=pl.Buffered(1))


def _t5_bucket(rel):
    half = N_BUCKETS // 2
    exact = half // 2
    ret = jnp.where(rel > 0, half, 0).astype(jnp.int32)
    n = jnp.abs(rel)
    nf = jnp.maximum(n, 1).astype(F32)
    large = exact + (jnp.log(nf / exact) / math.log(MAX_DISTANCE / exact) * (half - exact)).astype(jnp.int32)
    large = jnp.minimum(large, half - 1)
    return ret + jnp.where(n < exact, n, large)


def _bias_tile(rel_bias, n_j, n_i, r0, valid, shift_far):
    tab = rel_bias.astype(F32)
    period = -(-(n_i + n_j - 1) // LANES) * LANES
    u = jnp.arange(period)
    rel = jnp.where(u < n_i, r0 - u, r0 + (period - u))
    vals = jnp.moveaxis(tab[_t5_bucket(rel)], -1, 0)
    if shift_far:
        vals = vals - tab[FAR_BUCKET][:, None]
    vals = (vals * LOG2E).reshape(vals.shape[0], 1, period)

    def expand(y_ref, o_ref):
        rows = jnp.broadcast_to(y_ref[0], (n_j, period))
        o_ref[0] = pltpu.roll(rows, 0, 1, stride=1, stride_axis=0)[:, :n_i]

    b = pl.pallas_call(
        expand,
        grid=(vals.shape[0],),
        in_specs=[pl.BlockSpec((1, 1, period), lambda c: (c, 0, 0))],
        out_specs=pl.BlockSpec((1, n_j, n_i), lambda c: (c, 0, 0)),
        out_shape=jax.ShapeDtypeStruct((vals.shape[0], n_j, n_i), F32),
        compiler_params=_cparams(1),
        name="bias_toeplitz",
    )(vals)
    if valid is not None:
        b = jnp.where(valid[None], b, NEG)
    return b


def _bias_stats(rel_bias):
    tab = rel_bias.astype(F32)
    sh = (tab - tab[FAR_BUCKET][None, :]) * LOG2E
    bmax = jnp.max(sh, axis=0)
    return jnp.stack([bmax, bmax - jnp.min(sh, axis=0)], axis=-1)


def _proj_kernel(x_ref, g_ref, wt_ref, qg_ref, kg_ref, *refs, nq, nk, nv, vb, head_norm, kstat, n_alias,
                 slot, all_slots, kv_layout):
    out_refs = refs[n_alias:]
    qT_ref, k32_ref, k16_ref, v32_ref, vT_ref = out_refs[:5]
    if all_slots > 1:
        for s in range(all_slots):
            if s != slot:
                k32_ref[s] = jnp.zeros(k32_ref.shape[1:], F32)
                v32_ref[s] = jnp.zeros(v32_ref.shape[1:], F32)
        k32_ref = k32_ref.at[slot]
        v32_ref = v32_ref.at[slot]
    x = x_ref[...]
    ms = jnp.mean(x * x, axis=-1, keepdims=True)
    h = (x * lax.rsqrt(ms + EPS) * g_ref[...]).astype(BF16)
    tm = x.shape[0]

    def rows(r0, n):
        return lax.dot_general(wt_ref[r0:r0 + n, :], h, NT_DIMS, preferred_element_type=F32)

    def headnorm(t, gcol):
        n = t.shape[0]
        t3 = t.reshape(n // HEAD_DIM, HEAD_DIM, tm)
        r = lax.rsqrt(jnp.mean(t3 * t3, axis=1, keepdims=True) + EPS)
        return (t3 * r).reshape(n, tm) * gcol

    def store_kv(ref, r0, xt, x):
        n = xt.shape[0]
        if kv_layout == "feature":
            ref[r0:r0 + n, :] = xt
            return
        x = xt.T if x is None else x
        if kv_layout == "token":
            ref[:, r0:r0 + n] = x
            return
        heads = ref.shape[0] // tm
        for c in range(n // LANES):
            head = r0 // LANES + c
            ref[pl.ds(head, tm, stride=heads), :] = x[:, c * LANES:(c + 1) * LANES]

    for r0 in range(0, nq, PROJ_CHUNK):
        n = min(PROJ_CHUNK, nq - r0)
        qt = rows(r0, n)
        qt = headnorm(qt, qg_ref[r0:r0 + n, :]) if head_norm else qt * qg_ref[r0:r0 + n, :]
        qT_ref[r0:r0 + n, :] = qt.astype(BF16)

    for r0 in range(0, nk, PROJ_CHUNK):
        n = min(PROJ_CHUNK, nk - r0)
        kt = rows(nq + r0, n)
        if head_norm:
            kt = headnorm(kt, kg_ref[r0:r0 + n, :])
        k = kt.T
        store_kv(k32_ref, r0, kt, k)
        k16_ref[:, r0:r0 + n] = k.astype(BF16)
        if kstat is not None:
            stat_ref = out_refs[5]
            kr = kt.astype(BF16).astype(F32)
            if kstat == "absmax":
                stat = jnp.max(jnp.abs(kr), axis=1, keepdims=True)
                srows = slice(r0, r0 + n)
            else:
                k3 = kr.reshape(n // HEAD_DIM, HEAD_DIM, tm)
                stat = jnp.max(jnp.sum(k3 * k3, axis=1), axis=1, keepdims=True)
                srows = slice(r0 // HEAD_DIM, (r0 + n) // HEAD_DIM)

            @pl.when(pl.program_id(0) == 0)
            def _(stat=stat, srows=srows):
                stat_ref[srows, :] = stat

            @pl.when(pl.program_id(0) > 0)
            def _(stat=stat, srows=srows):
                stat_ref[srows, :] = jnp.maximum(stat_ref[srows, :], stat)

    for r0 in range(0, nv, PROJ_CHUNK):
        n = min(PROJ_CHUNK, nv - r0)
        vt = rows(nq + nk + r0, n)
        store_kv(v32_ref, r0, vt, None)
        vt16 = vt.astype(BF16)
        for b in range(tm // vb):
            vT_ref[b, r0:r0 + n, :] = vt16[:, b * vb:(b + 1) * vb]


def _project(x, g, wt, qg, kg, *, nq, nk, nv, vb, head_norm, kstat=None, slot=0, n_slots=1, earlier=None,
             kv_layout="token"):
    n_tok = x.shape[0]
    tm = min(TOKEN_TILE, n_tok)
    vb = min(vb, tm)
    grid = (n_tok // tm,)
    ntot = nq + nk + nv

    def kv_shape(width):
        if kv_layout == "feature":
            return (n_slots, width, n_tok)
        if kv_layout == "head_rows":
            return (n_slots, n_tok * (width // LANES), LANES)
        return (n_slots, n_tok, width)

    out_shape = [
        jax.ShapeDtypeStruct((nq, n_tok), BF16),
        jax.ShapeDtypeStruct(kv_shape(nk), F32),
        jax.ShapeDtypeStruct((n_tok, nk), BF16),
        jax.ShapeDtypeStruct(kv_shape(nv), F32),
        jax.ShapeDtypeStruct((n_tok // vb, nv, vb), BF16),
    ]
    all_slots = n_slots if (earlier is None and n_slots > 1) else 1

    def kv_spec(width):
        lead = n_slots if all_slots > 1 else None
        first = 0 if all_slots > 1 else slot
        if kv_layout == "feature":
            return pl.BlockSpec((lead, width, tm), lambda i: (first, 0, i))
        if kv_layout == "head_rows":
            return pl.BlockSpec((lead, tm * (width // LANES), LANES), lambda i: (first, i, 0))
        return pl.BlockSpec((lead, tm, width), lambda i: (first, i, 0))

    out_specs = [
        pl.BlockSpec((nq, tm), lambda i: (0, i)),
        kv_spec(nk),
        pl.BlockSpec((tm, nk), lambda i: (i, 0)),
        kv_spec(nv),
        pl.BlockSpec((tm // vb, nv, vb), lambda i: (i, 0, 0)),
    ]
    if kstat is not None:
        n_stat = nk if kstat == "absmax" else nk // HEAD_DIM
        out_shape.append(jax.ShapeDtypeStruct((n_stat, 1), F32))
        out_specs.append(pl.BlockSpec((n_stat, 1), lambda i: (0, 0)))
    in_specs = [
        pl.BlockSpec((tm, D_MODEL), lambda i: (i, 0)),
        _resident((1, D_MODEL), lambda i: (0, 0)),
        _resident((ntot, D_MODEL), lambda i: (0, 0)),
        _resident((nq, 1), lambda i: (0, 0)),
        _resident((nk, 1), lambda i: (0, 0)),
    ]
    args = [x, g, wt, qg, kg]
    aliases = {}
    if earlier is not None:
        aliases = {len(args): 1, len(args) + 1: 3}
        in_specs += [pl.BlockSpec(memory_space=pl.ANY)] * 2
        args += list(earlier)
    return pl.pallas_call(
        functools.partial(_proj_kernel, nq=nq, nk=nk, nv=nv, vb=vb, head_norm=head_norm, kstat=kstat,
                          n_alias=len(aliases), slot=slot, all_slots=all_slots,
                          kv_layout=kv_layout),
        grid=grid,
        in_specs=in_specs,
        out_specs=out_specs,
        out_shape=out_shape,
        input_output_aliases=aliases,
        compiler_params=_cparams(1),
        name="proj",
    )(*args)


def _mlp_kernel(xp_ref, op_ref, xs_ref, os_ref, wo_ref, g_ref, wup_ref, wdn_ref, yp_ref, ys_ref, *, f_chunk,
                n_prompt_tiles):
    def token_tile(x_ref, o_ref, y_ref):
        x1 = x_ref[...] + jnp.dot(o_ref[...], wo_ref[...], preferred_element_type=F32)
        ms = jnp.mean(x1 * x1, axis=-1, keepdims=True)
        hn = (x1 * lax.rsqrt(ms + EPS) * g_ref[...]).astype(BF16)
        acc = x1
        for f in range(D_FF // f_chunk):
            u = jnp.dot(hn, wup_ref[:, f * f_chunk:(f + 1) * f_chunk], preferred_element_type=F32)
            a = jnp.square(jnp.maximum(u, 0.0)).astype(BF16)
            acc = acc + jnp.dot(a, wdn_ref[f * f_chunk:(f + 1) * f_chunk, :], preferred_element_type=F32)
        y_ref[...] = acc

    @pl.when(pl.program_id(0) < n_prompt_tiles)
    def _():
        token_tile(xp_ref, op_ref, yp_ref)

    @pl.when(pl.program_id(0) == n_prompt_tiles)
    def _():
        token_tile(xs_ref, os_ref, ys_ref)


def _outproj_mlp(xp, o_p, xs, o_s, wo, g, wup, wdn, layer):
    n_p, n_s = xp.shape[0], xs.shape[0]
    tm = min(TOKEN_TILE, n_p)
    n_tiles = n_p // tm
    prompt_spec = pl.BlockSpec((tm, D_MODEL), lambda i: (jnp.minimum(i, n_tiles - 1), 0))
    sample_spec = pl.BlockSpec((n_s, D_MODEL), lambda i: (0, 0))
    return pl.pallas_call(
        functools.partial(_mlp_kernel, f_chunk=1024, n_prompt_tiles=n_tiles),
        grid=(n_tiles + 1,),
        in_specs=[
            prompt_spec, prompt_spec, sample_spec, sample_spec,
            _resident((D_MODEL, D_MODEL), lambda i: (0, 0)),
            _resident((1, D_MODEL), lambda i: (0, 0)),
            _resident((None, D_MODEL, D_FF), lambda i: (layer, 0, 0)),
            _resident((None, D_FF, D_MODEL), lambda i: (layer, 0, 0)),
        ],
        out_specs=[prompt_spec, sample_spec],
        out_shape=[jax.ShapeDtypeStruct((n_p, D_MODEL), F32), jax.ShapeDtypeStruct((n_s, D_MODEL), F32)],
        compiler_params=_cparams(1),
        name="outproj_mlp",
    )(xp, o_p, xs, o_s, wo, g, wup, wdn)


def _diff_lambda(lam_ref, lam_init):
    lp = lam_ref[...]
    a = jnp.sum(lp[0:1] * lp[1:2], axis=-1, keepdims=True)
    b = jnp.sum(lp[2:3] * lp[3:4], axis=-1, keepdims=True)
    return jnp.exp(a) - jnp.exp(b) + lam_init


def _diff_prompt_kernel(qT_ref, k_ref, vT_ref, bd_ref, bp_ref, kn_ref, bs_ref, lam_ref, sub_ref, o_ref,
                        m_ref, l_ref, acc_ref, *, tq, lam_init):
    qi = pl.program_id(1)
    qT = qT_ref[...]
    row = lax.broadcasted_iota(jnp.int32, qT.shape, 0)
    zero = jnp.zeros_like(qT)
    qm = (jnp.where(row < HEAD_DIM, qT, zero), jnp.where(row >= HEAD_DIM, qT, zero))

    def prev_bias(m, s):
        corner = s[tq - NEAR:, :NEAR] + bp_ref[0, m]
        bottom = jnp.concatenate([corner, s[tq - NEAR:, NEAR:]], axis=1)
        return jnp.concatenate([s[:tq - NEAR], bottom], axis=0)

    def diag_bias(m, s):
        return s + bd_ref[0, m]

    def prev_diag_bias(m, s):
        return jnp.concatenate([prev_bias(m, s[:tq]), diag_bias(m, s[tq:])], axis=0)

    n_far = jnp.maximum(qi - 1, 0)

    def walk(tile_fn):
        def far_body(j, carry):
            tile_fn(j, None)
            return carry

        lax.fori_loop(0, n_far, far_body, 0)

        @pl.when(qi >= 1)
        def _():
            tile_fn(qi - 1, prev_bias)

        tile_fn(qi, diag_bias)

    span = []
    for m in range(2):
        qf = qm[m].astype(F32)
        qn = jnp.sqrt(jnp.sum(qf * qf, axis=0, keepdims=True))
        reach = qn * (jnp.sqrt(kn_ref[m:m + 1, :]) * 1.001)
        m_ref[m] = reach + bs_ref[m:m + 1, 0:1]
        span.append(jnp.max(2.0 * reach + bs_ref[m:m + 1, 1:2]))
    bound_is_tight = jnp.maximum(span[0], span[1]) <= SOFTMAX_BOUND_LIMIT

    @pl.when(jnp.logical_not(bound_is_tight))
    def _():
        m_ref[...] = jnp.full(m_ref.shape, NEG, F32)

        def max_tile(j, bias_fn):
            k = k_ref[j]
            for m in range(2):
                s = jnp.dot(k, qm[m], preferred_element_type=F32)
                if bias_fn is not None:
                    s = bias_fn(m, s)
                m_ref[m] = jnp.maximum(m_ref[m], jnp.max(s, axis=0, keepdims=True))

        walk(max_tile)

    l_ref[...] = jnp.zeros(l_ref.shape, F32)
    acc_ref[...] = jnp.zeros(acc_ref.shape, F32)

    def acc_tiles(js, bias_fn):
        k = jnp.concatenate([k_ref[j] for j in js], axis=0)
        vT = jnp.concatenate([vT_ref[j] for j in js], axis=1)
        nk = len(js) * tq
        for m in range(2):
            s = jnp.dot(k, qm[m], preferred_element_type=F32)
            if bias_fn is not None:
                s = bias_fn(m, s)
            p = jnp.exp2(s - m_ref[m])
            l_ref[m] += jnp.sum(p.reshape(nk // 8, 8, tq), axis=0)
            acc_ref[m] += jnp.dot(vT, p.astype(BF16), preferred_element_type=F32)

    def far_group(jj, carry):
        acc_tiles([DIFF_FAR_GROUP * jj + u for u in range(DIFF_FAR_GROUP)], None)
        return carry

    n_groups = n_far // DIFF_FAR_GROUP
    lax.fori_loop(0, n_groups, far_group, 0)
    start = n_groups * DIFF_FAR_GROUP
    left = n_far - start
    size = DIFF_FAR_GROUP // 2
    while size >= 2:
        @pl.when((left & size) != 0)
        def _(start=start, size=size):
            acc_tiles([start + u for u in range(size)], None)

        start = start + (left & size)
        size //= 2
    odd = (left & 1) != 0

    def plain_prev_diag_bias(m, s):
        return jnp.concatenate([s[:tq], prev_diag_bias(m, s[tq:])], axis=0)

    @pl.when(odd)
    def _():
        acc_tiles([start, qi - 1, qi], plain_prev_diag_bias)

    @pl.when(jnp.logical_and(qi >= 1, jnp.logical_not(odd)))
    def _():
        acc_tiles([qi - 1, qi], prev_diag_bias)

    @pl.when(qi == 0)
    def _():
        acc_tiles([qi], diag_bias)

    lam = _diff_lambda(lam_ref, lam_init)
    o0 = acc_ref[0] * (1.0 / jnp.sum(l_ref[0], axis=0, keepdims=True))
    o1 = acc_ref[1] * (1.0 / jnp.sum(l_ref[1], axis=0, keepdims=True))
    o = o0 - lam * o1
    ms = jnp.mean(o * o, axis=0, keepdims=True)
    o = o * lax.rsqrt(ms + EPS) * sub_ref[...] * (1.0 - lam_init)
    o_ref[...] = o.T.astype(BF16)


def _diff_prompt(qT, k16, vT, bias_d, bias_p, knorm2, bstats, lam_p, subln, lam_init):
    n_tok = k16.shape[0]
    tq = min(DIFF_TILE, n_tok)
    nb = n_tok // tq
    k3 = k16.reshape(nb, tq, D_MODEL)
    return pl.pallas_call(
        functools.partial(_diff_prompt_kernel, tq=tq, lam_init=lam_init),
        grid=(DIFF_HEADS, nb),
        in_specs=[
            pl.BlockSpec((2 * HEAD_DIM, tq), lambda h, i: (h, i)),
            pl.BlockSpec((nb, tq, 2 * HEAD_DIM), lambda h, i: (0, 0, h)),
            pl.BlockSpec((nb, 2 * HEAD_DIM, tq), lambda h, i: (0, h, 0)),
            pl.BlockSpec((1, 2, tq, tq), lambda h, i: (h, 0, 0, 0)),
            pl.BlockSpec((1, 2, NEAR, NEAR), lambda h, i: (h, 0, 0, 0)),
            pl.BlockSpec((None, 2, 1), lambda h, i: (h, 0, 0)),
            pl.BlockSpec((None, 2, 2), lambda h, i: (h, 0, 0)),
            pl.BlockSpec((4, HEAD_DIM), lambda h, i: (0, 0)),
            pl.BlockSpec((2 * HEAD_DIM, 1), lambda h, i: (0, 0)),
        ],
        out_specs=pl.BlockSpec((tq, 2 * HEAD_DIM), lambda h, i: (i, h)),
        out_shape=jax.ShapeDtypeStruct((n_tok, D_MODEL), BF16),
        scratch_shapes=[
            pltpu.VMEM((2, 1, tq), F32),
            pltpu.VMEM((2, 8, tq), F32),
            pltpu.VMEM((2, 2 * HEAD_DIM, tq), F32),
        ],
        compiler_params=_cparams(2),
        name="diff_prompt",
    )(qT, k3, vT, bias_d, bias_p, knorm2, bstats, lam_p, subln)


def _diff_sample_kernel(q_ref, ck_ref, cv_ref, kn_ref, vn_ref, b_ref, lam_ref, sub_ref, o_ref, *, lam_init, past):
    t = q_ref.shape[2]
    lane = lax.broadcasted_iota(jnp.int32, (t, 2 * HEAD_DIM), 1)
    lam = _diff_lambda(lam_ref, lam_init)
    for h in range(DIFF_HEADS):
        q = q_ref[0, h]
        zero = jnp.zeros_like(q)
        qq = jnp.concatenate([jnp.where(lane < HEAD_DIM, q, zero), jnp.where(lane >= HEAD_DIM, q, zero)], axis=0)
        ck = ck_ref[pl.ds(h, past, stride=DIFF_HEADS), :].astype(BF16)
        cv = cv_ref[pl.ds(h, past, stride=DIFF_HEADS), :].astype(BF16)
        kn = kn_ref[0, h]
        vn = vn_ref[0, h].astype(BF16)
        s_c = lax.dot_general(qq, ck, NT_DIMS, preferred_element_type=F32)
        s_n = lax.dot_general(qq, kn, NT_DIMS, preferred_element_type=F32)
        bias = b_ref[h].reshape(2 * t, NEAR + t)
        s_far = s_c[:, :past - NEAR]
        s_near = s_c[:, past - NEAR:] + bias[:, :NEAR]
        s_n = s_n + bias[:, NEAR:]
        m = jnp.maximum(jnp.maximum(jnp.max(s_far, axis=1, keepdims=True), jnp.max(s_near, axis=1, keepdims=True)),
                        jnp.max(s_n, axis=1, keepdims=True))
        p_far = jnp.exp2(s_far - m)
        p_near = jnp.exp2(s_near - m)
        p_n = jnp.exp2(s_n - m)
        l = (jnp.sum(p_far, axis=1, keepdims=True) + jnp.sum(p_near, axis=1, keepdims=True)
             + jnp.sum(p_n, axis=1, keepdims=True))
        o = (jnp.dot(p_far.astype(BF16), cv[:past - NEAR], preferred_element_type=F32)
             + jnp.dot(p_near.astype(BF16), cv[past - NEAR:], preferred_element_type=F32)
             + jnp.dot(p_n.astype(BF16), vn, preferred_element_type=F32))
        o = o * (1.0 / l)
        od = o[:t] - lam * o[t:]
        ms = jnp.mean(od * od, axis=-1, keepdims=True)
        od = od * lax.rsqrt(ms + EPS) * sub_ref[...] * (1.0 - lam_init)
        o_ref[0, h] = od.astype(BF16)


def _diff_sample(q_s, cache_k, cache_v, layer, kn, vn, bias_s, lam_p, subln_row, lam_init):
    nb, _, t, _ = q_s.shape
    past = cache_k.shape[2]
    rows = past * DIFF_HEADS
    ck = cache_k.reshape(cache_k.shape[0], nb, rows, 2 * HEAD_DIM)
    cv = cache_v.reshape(cache_v.shape[0], nb, rows, 2 * HEAD_DIM)
    cache_spec = pl.BlockSpec((None, None, rows, 2 * HEAD_DIM), lambda b: (layer, b, 0, 0))
    head_spec = pl.BlockSpec((1, DIFF_HEADS, t, 2 * HEAD_DIM), lambda b: (b, 0, 0, 0))
    return pl.pallas_call(
        functools.partial(_diff_sample_kernel, lam_init=lam_init, past=past),
        grid=(nb,),
        in_specs=[
            head_spec, cache_spec, cache_spec, head_spec, head_spec,
            pl.BlockSpec((DIFF_HEADS, 2, t, NEAR + t), lambda b: (0, 0, 0, 0)),
            pl.BlockSpec((4, HEAD_DIM), lambda b: (0, 0)),
            pl.BlockSpec((1, 2 * HEAD_DIM), lambda b: (0, 0)),
        ],
        out_specs=head_spec,
        out_shape=jax.ShapeDtypeStruct((nb, DIFF_HEADS, t, 2 * HEAD_DIM), BF16),
        compiler_params=_cparams(1),
        name="diff_sample",
    )(q_s, ck, cv, kn, vn, bias_s, lam_p, subln_row)


def _swa_prompt_kernel(qT_ref, kc_ref, kp_ref, vc_ref, vp_ref, b_ref, sink_ref, o_ref):
    i = pl.program_id(0)
    qT = qT_ref[...]
    kp = kp_ref[...]
    kc = kc_ref[...]
    vp = vp_ref[0]
    vc = vc_ref[0]
    tq = qT.shape[1]
    key_row = lax.broadcasted_iota(jnp.int32, (2 * tq, SWA_GROUP * tq), 0)
    has_prev = key_row >= jnp.where(i > 0, 0, tq)
    zeros = jnp.zeros((HEAD_DIM, tq), BF16)
    outs = []
    for kv in range(SWA_KV_HEADS):
        pair = kv // 2
        lanes = slice(pair * 2 * HEAD_DIM, (pair + 1) * 2 * HEAD_DIM)
        k_band = jnp.concatenate([kp[:, lanes], kc[:, lanes]], axis=0)
        v_band = jnp.concatenate([vp[kv * HEAD_DIM:(kv + 1) * HEAD_DIM], vc[kv * HEAD_DIM:(kv + 1) * HEAD_DIM]],
                                 axis=1)
        qg = []
        for g in range(SWA_GROUP):
            h = kv * SWA_GROUP + g
            qh = qT[h * HEAD_DIM:(h + 1) * HEAD_DIM]
            qg.append(jnp.concatenate([qh, zeros] if kv % 2 == 0 else [zeros, qh], axis=0))
        q4 = jnp.concatenate(qg, axis=1)
        s = jnp.dot(k_band, q4, preferred_element_type=F32) + b_ref[kv]
        s = jnp.where(has_prev, s, NEG)
        sink = sink_ref[kv]
        m = jnp.maximum(jnp.max(s, axis=0, keepdims=True), sink)
        p = jnp.exp2(s - m)
        den = jnp.sum(p, axis=0, keepdims=True) + jnp.exp2(sink - m)
        oT4 = jnp.dot(v_band, p.astype(BF16), preferred_element_type=F32) * (1.0 / den)
        for g in range(SWA_GROUP):
            outs.append(oT4[:, g * tq:(g + 1) * tq])
    o_ref[...] = jnp.concatenate(outs, axis=0).T.astype(BF16)


def _swa_prompt(qT, k16, vT, bias, sinks):
    n_tok = k16.shape[0]
    tq = SWA_TILE
    nkv = SWA_KV_HEADS * HEAD_DIM
    return pl.pallas_call(
        _swa_prompt_kernel,
        grid=(n_tok // tq,),
        in_specs=[
            pl.BlockSpec((D_MODEL, tq), lambda i: (0, i)),
            pl.BlockSpec((tq, nkv), lambda i: (i, 0)),
            pl.BlockSpec((tq, nkv), lambda i: (jnp.maximum(i - 1, 0), 0)),
            pl.BlockSpec((1, nkv, tq), lambda i: (i, 0, 0)),
            pl.BlockSpec((1, nkv, tq), lambda i: (jnp.maximum(i - 1, 0), 0, 0)),
            _resident((SWA_KV_HEADS, 2 * tq, SWA_GROUP * tq), lambda i: (0, 0, 0)),
            _resident((SWA_KV_HEADS, 1, SWA_GROUP * tq), lambda i: (0, 0, 0)),
        ],
        out_specs=pl.BlockSpec((tq, D_MODEL), lambda i: (i, 0)),
        out_shape=jax.ShapeDtypeStruct((n_tok, D_MODEL), BF16),
        compiler_params=_cparams(1),
        name="swa_prompt",
    )(qT, k16, k16, vT, vT, bias, sinks)


def _swa_sample_kernel(q_ref, kc_ref, vc_ref, kn_ref, vn_ref, b_ref, sink_ref, o_ref, ko_ref, vo_ref):
    t = kn_ref.shape[2]
    buf = kc_ref.shape[2]
    for kv in range(SWA_KV_HEADS):
        kc32 = kc_ref[0, kv]
        vc32 = vc_ref[0, kv]
        kn32 = kn_ref[0, kv]
        vn32 = vn_ref[0, kv]
        ko_ref[0, kv, 0:buf - t, :] = kc32[t:]
        ko_ref[0, kv, buf - t:buf, :] = kn32
        vo_ref[0, kv, 0:buf - t, :] = vc32[t:]
        vo_ref[0, kv, buf - t:buf, :] = vn32
        kc = kc32.astype(BF16)
        vc = vc32.astype(BF16)
        kn = kn32.astype(BF16)
        vn = vn32.astype(BF16)
        heads = range(kv * SWA_GROUP, (kv + 1) * SWA_GROUP)
        q = jnp.concatenate([q_ref[0, h] for h in heads], axis=0)
        b = jnp.concatenate([b_ref[h] for h in heads], axis=0)
        sink = jnp.concatenate([jnp.broadcast_to(sink_ref[h], (t, 1)) for h in heads], axis=0)
        s_c = lax.dot_general(q, kc, NT_DIMS, preferred_element_type=F32) + b[:, :buf]
        s_n = lax.dot_general(q, kn, NT_DIMS, preferred_element_type=F32) + b[:, buf:]
        m = jnp.maximum(jnp.maximum(jnp.max(s_c, axis=1, keepdims=True), jnp.max(s_n, axis=1, keepdims=True)),
                        sink)
        p_c = jnp.exp2(s_c - m)
        p_n = jnp.exp2(s_n - m)
        den = (jnp.sum(p_c, axis=1, keepdims=True) + jnp.sum(p_n, axis=1, keepdims=True)
               + jnp.exp2(sink - m))
        o = (jnp.dot(p_c.astype(BF16), vc, preferred_element_type=F32)
             + jnp.dot(p_n.astype(BF16), vn, preferred_element_type=F32)) * (1.0 / den)
        for g, h in enumerate(heads):
            o_ref[0, h] = o[g * t:(g + 1) * t].astype(BF16)


def _swa_sample(q_s, kc, vc, kn, vn, bias, sinks):
    nb, _, t, _ = q_s.shape
    buf = kc.shape[2]
    kvspec = pl.BlockSpec((1, SWA_KV_HEADS, buf, HEAD_DIM), lambda b: (b, 0, 0, 0))
    nspec = pl.BlockSpec((1, SWA_KV_HEADS, t, HEAD_DIM), lambda b: (b, 0, 0, 0))
    return pl.pallas_call(
        _swa_sample_kernel,
        grid=(nb,),
        in_specs=[
            pl.BlockSpec((1, SWA_Q_HEADS, t, HEAD_DIM), lambda b: (b, 0, 0, 0)),
            kvspec, kvspec, nspec, nspec,
            pl.BlockSpec((SWA_Q_HEADS, t, buf + t), lambda b: (0, 0, 0)),
            pl.BlockSpec((SWA_Q_HEADS, 1, 1), lambda b: (0, 0, 0)),
        ],
        out_specs=[
            pl.BlockSpec((1, SWA_Q_HEADS, t, HEAD_DIM), lambda b: (b, 0, 0, 0)),
            kvspec, kvspec,
        ],
        out_shape=[
            jax.ShapeDtypeStruct((nb, SWA_Q_HEADS, t, HEAD_DIM), BF16),
            jax.ShapeDtypeStruct((nb, SWA_KV_HEADS, buf, HEAD_DIM), F32),
            jax.ShapeDtypeStruct((nb, SWA_KV_HEADS, buf, HEAD_DIM), F32),
        ],
        compiler_params=_cparams(1),
        name="swa_sample",
    )(q_s, kc, vc, kn, vn, bias, sinks)


def _softplus(z):
    return jnp.maximum(z, 0.0) + jnp.log2(1.0 + jnp.exp2(-jnp.abs(z)))


def _split_bf16(x):
    hi = x.astype(BF16)
    lo = (x - hi.astype(F32)).astype(BF16)
    return hi, lo


def _sb_prompt_kernel(qT_ref, k_ref, vT_ref, kmax_ref, o_ref, carry_ref, acc_ref, *, tq, pairs):
    qi = pl.program_id(1)
    width = 2 * pairs * tq
    row = lax.broadcasted_iota(jnp.int32, (2 * HEAD_DIM, tq), 0)
    qpairs, zbs = [], []
    for pr in range(pairs):
        qT = qT_ref[2 * HEAD_DIM * pr:2 * HEAD_DIM * (pr + 1), :]
        zero = jnp.zeros_like(qT)
        qp = jnp.concatenate([jnp.where(row < HEAD_DIM, qT, zero), jnp.where(row >= HEAD_DIM, qT, zero)], axis=1)
        qpairs.append(qp)
        kmax = kmax_ref[2 * HEAD_DIM * pr:2 * HEAD_DIM * (pr + 1), :]
        zbs.append(jnp.sum(jnp.abs(qp.astype(F32)) * kmax, axis=0, keepdims=True))
    zb = jnp.concatenate(zbs, axis=1)
    kr = lax.broadcasted_iota(jnp.int32, (tq, tq), 0)
    kc = lax.broadcasted_iota(jnp.int32, (tq, tq), 1)
    tri = jnp.where(kc >= kr, 1.0, 0.0).astype(BF16)
    tri2 = jnp.concatenate([tri, tri], axis=1)
    krw = lax.broadcasted_iota(jnp.int32, (tq, width), 0)
    kcw = lax.broadcasted_iota(jnp.int32, (tq, width), 1)
    causal = krw < lax.rem(kcw, tq)
    carry_ref[...] = jnp.zeros(carry_ref.shape, F32)
    acc_ref[...] = jnp.zeros(acc_ref.shape, F32)

    def tiles(js, first_is_diag):
        zs, cs = [], []
        for n, j in enumerate(js):
            k = k_ref[j]
            z = jnp.concatenate([jnp.dot(k[:, 2 * HEAD_DIM * pr:2 * HEAD_DIM * (pr + 1)], qpairs[pr],
                                         preferred_element_type=F32) for pr in range(pairs)], axis=1)
            lsp = _softplus(z)
            if first_is_diag and n == 0:
                lsp = jnp.where(causal, lsp, 0.0)
            hi, lo = _split_bf16(lsp)
            zs.append(z)
            cs.append(jnp.dot(tri2, jnp.concatenate([hi, lo], axis=0), preferred_element_type=F32))
        carry = carry_ref[...]
        weights = []
        for n in range(len(js)):
            a = jnp.exp2(zs[n] - cs[n] - carry)
            if first_is_diag and n == 0:
                a = jnp.where(causal, a, 0.0)
            weights.append(a.astype(BF16))
            carry = carry + cs[n][0:1]
        carry_ref[...] = carry
        for pr in range(pairs):
            vT = jnp.concatenate([vT_ref[j, 2 * HEAD_DIM * pr:2 * HEAD_DIM * (pr + 1), :] for j in js],
                                 axis=1)
            w = jnp.concatenate([a[:, 2 * tq * pr:2 * tq * (pr + 1)] for a in weights], axis=0)
            acc_ref[pr] += jnp.dot(vT, w, preferred_element_type=F32)

    def still_live():
        return (jnp.min(carry_ref[...] - zb) < SB_SKIP_MARGIN).astype(jnp.int32)

    @pl.when(qi == 0)
    def _():
        tiles([qi], True)

    @pl.when(qi == 1)
    def _():
        tiles([qi, qi - 1], True)

    @pl.when(qi >= 2)
    def _():
        tiles([qi, qi - 1, qi - 2], True)

    def cond(state):
        j, live = state
        return jnp.logical_and(j >= 0, live > 0)

    def body(state):
        j, _ = state
        tiles([j], False)
        return j - 1, still_live()

    lax.while_loop(cond, body, (qi - 3, still_live()))
    outs = []
    for pr in range(pairs):
        outs += [acc_ref[pr, 0:HEAD_DIM, 0:tq], acc_ref[pr, HEAD_DIM:2 * HEAD_DIM, tq:2 * tq]]
    o_ref[...] = jnp.concatenate(outs, axis=0).T.astype(BF16)


def _sb_prompt(qT, k16, vT, kmax):
    n_tok = k16.shape[0]
    tq = min(SB_TILE, n_tok)
    nb = n_tok // tq
    k3 = k16.reshape(nb, tq, D_MODEL)
    pairs = SB_PAIRS_PER_STEP
    wide = 2 * HEAD_DIM * pairs
    return pl.pallas_call(
        functools.partial(_sb_prompt_kernel, tq=tq, pairs=pairs),
        grid=(SB_HEADS // (2 * pairs), nb),
        in_specs=[
            pl.BlockSpec((wide, tq), lambda p, i: (p, i)),
            pl.BlockSpec((nb, tq, wide), lambda p, i: (0, 0, p)),
            pl.BlockSpec((nb, wide, tq), lambda p, i: (0, p, 0)),
            pl.BlockSpec((wide, 1), lambda p, i: (p, 0)),
        ],
        out_specs=pl.BlockSpec((tq, wide), lambda p, i: (i, p)),
        out_shape=jax.ShapeDtypeStruct((n_tok, D_MODEL), BF16),
        scratch_shapes=[
            pltpu.VMEM((1, 2 * pairs * tq), F32),
            pltpu.VMEM((pairs, 2 * HEAD_DIM, 2 * tq), F32),
        ],
        compiler_params=_cparams(2),
        name="sb_prompt",
    )(qT, k3, vT, kmax)


def _tri_lanes(n):
    r = lax.broadcasted_iota(jnp.int32, (n, n), 0)
    c = lax.broadcasted_iota(jnp.int32, (n, n), 1)
    return jnp.where(r >= c, 1.0, 0.0).astype(BF16)


def _sb_sample_kernel(q_ref, ck_ref, cv_ref, kn_ref, vn_ref, o_ref, carry_ref, acc_ref, *, ch, kb):
    c = pl.program_id(1)
    nh = q_ref.shape[1]
    t = q_ref.shape[2]

    @pl.when(c == 0)
    def _():
        z = jnp.concatenate([lax.dot_general(q_ref[0, h], kn_ref[0, h], NT_DIMS, preferred_element_type=F32)
                             for h in range(nh)], axis=0)
        r = lax.broadcasted_iota(jnp.int32, (nh * t, t), 0)
        kc = lax.broadcasted_iota(jnp.int32, (nh * t, t), 1)
        causal = kc < lax.rem(r, t)
        hi, lo = _split_bf16(jnp.where(causal, _softplus(z), 0.0))
        tri = _tri_lanes(t)
        cs = jnp.dot(hi, tri, preferred_element_type=F32) + jnp.dot(lo, tri, preferred_element_type=F32)
        a = jnp.where(causal, jnp.exp2(z - cs), 0.0).astype(BF16)
        carry_ref[...] = cs[:, 0:1]
        for h in range(nh):
            acc_ref[h * t:(h + 1) * t, :] = jnp.dot(a[h * t:(h + 1) * t], vn_ref[0, h].astype(BF16),
                                                    preferred_element_type=F32)

    z = jnp.concatenate([jnp.dot(q_ref[0, h], ck_ref[h].astype(BF16), preferred_element_type=F32)
                         for h in range(nh)], axis=0)
    nblk = ch // kb
    rows = nh * t
    zst = jnp.concatenate([z[:, b * kb:(b + 1) * kb] for b in range(nblk)], axis=0)
    hi, lo = _split_bf16(_softplus(zst))
    tri = _tri_lanes(kb)
    cs = jnp.dot(hi, tri, preferred_element_type=F32) + jnp.dot(lo, tri, preferred_element_type=F32)
    run = carry_ref[...]
    carries = [None] * nblk
    for b in reversed(range(nblk)):
        carries[b] = run
        run = run + cs[b * rows:(b + 1) * rows, 0:1]
    carry_ref[...] = run
    a = jnp.exp2(zst - cs - jnp.concatenate(carries, axis=0))
    a = jnp.concatenate([a[b * rows:(b + 1) * rows] for b in range(nblk)], axis=1).astype(BF16)
    for h in range(nh):
        acc_ref[h * t:(h + 1) * t, :] += lax.dot_general(a[h * t:(h + 1) * t], cv_ref[h].astype(BF16), NT_DIMS,
                                                         preferred_element_type=F32)

    @pl.when(c == pl.num_programs(1) - 1)
    def _():
        for h in range(nh):
            o_ref[0, h] = acc_ref[h * t:(h + 1) * t, :].astype(BF16)


def _sb_sample(q_s, cache_k, cache_v, layer, kn, vn):
    nb, nh, t, _ = q_s.shape
    past = cache_k.shape[2]
    ch = min(SB_SAMPLE_CHUNK, past)
    nch = past // ch
    ck = cache_k.transpose(0, 1, 3, 4, 2)
    cv = cache_v.transpose(0, 1, 3, 4, 2)
    cache_spec = pl.BlockSpec((None, None, nh, HEAD_DIM, ch), lambda b, c: (layer, b, 0, 0, nch - 1 - c))
    head_spec = pl.BlockSpec((1, nh, t, HEAD_DIM), lambda b, c: (b, 0, 0, 0))
    return pl.pallas_call(
        functools.partial(_sb_sample_kernel, ch=ch, kb=min(SB_TILE, ch)),
        grid=(nb, nch),
        in_specs=[head_spec, cache_spec, cache_spec, head_spec, head_spec],
        out_specs=head_spec,
        out_shape=jax.ShapeDtypeStruct((nb, nh, t, HEAD_DIM), BF16),
        scratch_shapes=[
            pltpu.VMEM((nh * t, 1), F32),
            pltpu.VMEM((nh * t, HEAD_DIM), F32),
        ],
        compiler_params=_cparams(2),
        name="sb_sample",
    )(q_s, ck, cv, kn, vn)


def _tile_col(gain, n, scale):
    return (jnp.tile(gain.astype(F32), n // gain.shape[0]) * scale).reshape(n, 1)


def _heads_major(x, nb, t, nh, hd):
    return x.reshape(nb, t, nh, hd).transpose(0, 2, 1, 3)


def _diff_bias_tiles(rel_bias, tq, t):
    j = jnp.arange(tq)
    valid_d = (j // CHUNK)[:, None] <= (j // CHUNK)[None, :]
    bias_d = _bias_tile(rel_bias, tq, tq, 0, valid_d, True).reshape(DIFF_HEADS, 2, tq, tq)
    bias_p = _bias_tile(rel_bias, NEAR, NEAR, -NEAR, None, True).reshape(DIFF_HEADS, 2, NEAR, NEAR)
    bstats = _bias_stats(rel_bias).reshape(DIFF_HEADS, 2, 2)
    bias_s = jnp.swapaxes(_bias_tile(rel_bias, NEAR + t, t, -NEAR, None, True), 1, 2)
    bias_s = bias_s.reshape(DIFF_HEADS, 2, t, NEAR + t)
    return bias_d, bias_p, bstats, bias_s


def _diff_layer(xp, xs, cache_k, cache_v, layer, n_layers, earlier, bias_tiles, g, w_qkv, q_gain, k_gain, lam_p,
                subln, lam_init):
    bias_d, bias_p, bstats, bias_s = bias_tiles
    n_p = xp.shape[0]
    nb = cache_k.shape[1]
    t = xs.shape[0] // nb
    wt = w_qkv.T.astype(BF16)
    qg = _tile_col(q_gain, D_MODEL, HEAD_DIM ** -0.5 * LOG2E)
    kg = _tile_col(k_gain, D_MODEL, 1.0)
    g = g.reshape(1, D_MODEL)
    kw = dict(nq=D_MODEL, nk=D_MODEL, nv=D_MODEL, head_norm=True)
    tq = min(DIFF_TILE, n_p)
    qT, k32, k16, v32, vT, knorm2 = _project(xp, g, wt, qg, kg, vb=tq, kstat="normsq", slot=layer,
                                             n_slots=n_layers, earlier=earlier, kv_layout="head_rows", **kw)
    qT_s, k32_s, k16_s, v32_s, _ = _project(xs, g, wt, qg, kg, vb=LANES, **kw)
    k32_s, v32_s = k32_s[0], v32_s[0]

    o_p = _diff_prompt(qT, k16, vT, bias_d, bias_p, knorm2.reshape(DIFF_HEADS, 2, 1), bstats, lam_p,
                       subln.reshape(2 * HEAD_DIM, 1), lam_init)
    hm = functools.partial(_heads_major, nb=nb, t=t, nh=DIFF_HEADS, hd=2 * HEAD_DIM)
    o_s = _diff_sample(hm(qT_s.T), cache_k, cache_v, layer, hm(k16_s), hm(v32_s), bias_s, lam_p,
                       subln.reshape(1, 2 * HEAD_DIM), lam_init)
    o_s = o_s.transpose(0, 2, 1, 3).reshape(nb * t, D_MODEL)
    return (o_p, o_s,
            k32, v32,
            k32_s.reshape(nb, t, DIFF_HEADS, 2 * HEAD_DIM), v32_s.reshape(nb, t, DIFF_HEADS, 2 * HEAD_DIM))


def _swa_layer(xp, xs, ck, cv, rel_bias, g, w_qkv, q_gain, k_gain, sinks):
    n_p = xp.shape[0]
    nb, buf = ck.shape[0], ck.shape[1]
    t = xs.shape[0] // nb
    nkv = SWA_KV_HEADS * HEAD_DIM
    wt = w_qkv.T.astype(BF16)
    qg = _tile_col(q_gain, D_MODEL, HEAD_DIM ** -0.5 * LOG2E)
    kg = _tile_col(k_gain, nkv, 1.0)
    g = g.reshape(1, D_MODEL)
    kw = dict(nq=D_MODEL, nk=nkv, nv=nkv, vb=SWA_TILE, head_norm=True)
    qT, k32, k16, v32, vT = _project(xp, g, wt, qg, kg, **kw)
    qT_s, k32_s, _, v32_s, _ = _project(xs, g, wt, qg, kg, **kw)
    k32, v32, k32_s, v32_s = k32[0], v32[0], k32_s[0], v32_s[0]
    sink_col = (sinks.astype(F32) * LOG2E).reshape(SWA_Q_HEADS, 1, 1)
    wchunks = WINDOW // CHUNK

    tq = SWA_TILE
    kchunk = jnp.floor_divide(jnp.arange(2 * tq) - tq, CHUNK)[:, None]
    qchunk = (jnp.arange(tq) // CHUNK)[None, :]
    valid = (kchunk <= qchunk) & (kchunk >= qchunk - wchunks)
    bias = _bias_tile(rel_bias, 2 * tq, tq, -tq, valid, False)
    bias_g = bias.reshape(SWA_KV_HEADS, SWA_GROUP, 2 * tq, tq).transpose(0, 2, 1, 3)
    bias_g = bias_g.reshape(SWA_KV_HEADS, 2 * tq, SWA_GROUP * tq)
    sink_g = jnp.repeat(sink_col.reshape(SWA_KV_HEADS, 1, SWA_GROUP), tq, axis=2)
    o_p = _swa_prompt(qT, k16, vT, bias_g, sink_g)

    sc = jnp.floor_divide(jnp.arange(buf + t) - buf, CHUNK)[:, None]
    tc = (jnp.arange(t) // CHUNK)[None, :]
    valid_s = (sc <= tc) & (sc >= tc - wchunks)
    bias_s = jnp.swapaxes(_bias_tile(rel_bias, buf + t, t, -buf, valid_s, False), 1, 2)
    q_s = _heads_major(qT_s.T, nb, t, SWA_Q_HEADS, HEAD_DIM)
    kn = _heads_major(k32_s, nb, t, SWA_KV_HEADS, HEAD_DIM)
    vn = _heads_major(v32_s, nb, t, SWA_KV_HEADS, HEAD_DIM)
    o_s, ko, vo = _swa_sample(q_s, ck.transpose(0, 2, 1, 3), cv.transpose(0, 2, 1, 3), kn, vn, bias_s, sink_col)
    o_s = o_s.transpose(0, 2, 1, 3).reshape(nb * t, D_MODEL)
    wbuf = min(WINDOW, n_p)
    return (o_p, o_s,
            k32[n_p - wbuf:].reshape(1, wbuf, SWA_KV_HEADS, HEAD_DIM),
            v32[n_p - wbuf:].reshape(1, wbuf, SWA_KV_HEADS, HEAD_DIM),
            ko.transpose(0, 2, 1, 3), vo.transpose(0, 2, 1, 3))


def _sb_layer(xp, xs, cache_k, cache_v, layer, n_layers, earlier, g, w_qkv):
    n_p = xp.shape[0]
    nb = cache_k.shape[1]
    t = xs.shape[0] // nb
    wt = w_qkv.T.astype(BF16)
    qg = jnp.full((D_MODEL, 1), HEAD_DIM ** -0.5 * LOG2E, F32)
    kg = jnp.ones((D_MODEL, 1), F32)
    g = g.reshape(1, D_MODEL)
    kw = dict(nq=D_MODEL, nk=D_MODEL, nv=D_MODEL, head_norm=False)
    tq = min(SB_TILE, n_p)
    qT, k32, k16, v32, vT, kmax = _project(xp, g, wt, qg, kg, vb=tq, kstat="absmax", slot=layer,
                                           n_slots=n_layers, earlier=earlier, kv_layout="feature", **kw)
    qT_s, k32_s, k16_s, v32_s, _ = _project(xs, g, wt, qg, kg, vb=LANES, **kw)
    k32_s, v32_s = k32_s[0], v32_s[0]
    o_p = _sb_prompt(qT, k16, vT, kmax)
    hm = functools.partial(_heads_major, nb=nb, t=t, nh=SB_HEADS, hd=HEAD_DIM)
    o_s = _sb_sample(hm(qT_s.T), cache_k, cache_v, layer, hm(k16_s), hm(v32_s))
    o_s = o_s.transpose(0, 2, 1, 3).reshape(nb * t, D_MODEL)
    return (o_p, o_s,
            k32, v32,
            k32_s.reshape(nb, t, SB_HEADS, HEAD_DIM), v32_s.reshape(nb, t, SB_HEADS, HEAD_DIM))


def kernel(x_prompt, x_sample, cache_diff_k, cache_diff_v, cache_swa_k, cache_swa_v, cache_sb_k, cache_sb_v, rel_bias, norm_mix, norm_mlp, w_up, w_down, diff_w_qkv, diff_w_o, diff_q_norm, diff_k_norm, diff_lambda, diff_subln, swa_w_qkv, swa_w_o, swa_q_norm, swa_k_norm, swa_sinks, sb_w_qkv, sb_w_o):
    bp, n_p, _ = x_prompt.shape
    assert bp == 1
    nb, t, _ = x_sample.shape
    depth = norm_mix.shape[0]
    xp = x_prompt.reshape(n_p, D_MODEL)
    xs = x_sample.reshape(nb * t, D_MODEL)
    n_diff, n_sb = cache_diff_k.shape[0], cache_sb_k.shape[0]
    outs = {name: [] for name in ("pwk", "pwv", "sdk", "sdv", "swk", "swv", "sbk", "sbv")}
    diff_kv = sb_kv = None
    diff_bias = _diff_bias_tiles(rel_bias, min(DIFF_TILE, n_p), t)
    wup16 = w_up.astype(BF16)
    wdn16 = w_down.astype(BF16)
    for i in range(depth):
        j = i // N_MIXERS
        if i % N_MIXERS == 0:
            lam_init = 0.8 - 0.6 * math.exp(-0.3 * i)
            o_p, o_s, kp, vp, kn, vn = _diff_layer(
                xp, xs, cache_diff_k, cache_diff_v, j, n_diff, diff_kv, diff_bias, norm_mix[i], diff_w_qkv[j],
                diff_q_norm[j], diff_k_norm[j], diff_lambda[j], diff_subln[j], lam_init)
            diff_kv = (kp, vp)
            w_o = diff_w_o[j]
            names = ("sdk", "sdv")
        elif i % N_MIXERS == 1:
            o_p, o_s, kp, vp, kn, vn = _swa_layer(
                xp, xs, cache_swa_k[j], cache_swa_v[j], rel_bias, norm_mix[i], swa_w_qkv[j],
                swa_q_norm[j], swa_k_norm[j], swa_sinks[j])
            outs["pwk"].append(kp)
            outs["pwv"].append(vp)
            w_o = swa_w_o[j]
            names = ("swk", "swv")
        else:
            o_p, o_s, kp, vp, kn, vn = _sb_layer(xp, xs, cache_sb_k, cache_sb_v, j, n_sb, sb_kv, norm_mix[i],
                                                 sb_w_qkv[j])
            sb_kv = (kp, vp)
            w_o = sb_w_o[j]
            names = ("sbk", "sbv")
        for name, val in zip(names, (kn, vn)):
            outs[name].append(val)
        wo16 = w_o.astype(BF16)
        g_mlp = norm_mlp[i].reshape(1, D_MODEL)
        xp, xs = _outproj_mlp(xp, o_p, xs, o_s, wo16, g_mlp, wup16, wdn16, i)
    st = {name: jnp.stack(v) for name, v in outs.items()}
    diff_shape = (n_diff, 1, n_p, DIFF_HEADS, 2 * HEAD_DIM)

    def sb_result(a):
        return a.reshape(n_sb, 1, SB_HEADS, HEAD_DIM, n_p).transpose(0, 1, 4, 2, 3)

    return (xp.reshape(1, n_p, D_MODEL), xs.reshape(nb, t, D_MODEL),
            diff_kv[0].reshape(diff_shape), diff_kv[1].reshape(diff_shape), st["pwk"], st["pwv"],
            sb_result(sb_kv[0]), sb_result(sb_kv[1]),
            st["sdk"], st["sdv"], st["swk"], st["swv"], st["sbk"], st["sbv"])
```

```python
import functools
import math

import jax
import jax.numpy as jnp
from jax import lax
from jax.experimental import pallas as pl
from jax.experimental.pallas import tpu as pltpu

F32 = jnp.float32
BF16 = jnp.bfloat16

D_MODEL = 1024
HEAD_DIM = 64
CHUNK = 64
N_MIXERS = 3
DIFF_HEADS = 8
SWA_Q_HEADS = 16
SWA_KV_HEADS = 4
SWA_GROUP = 4
WINDOW = 128
SB_HEADS = 16
N_BUCKETS = 32
MAX_DISTANCE = 128
D_FF = 4 * D_MODEL
EPS = 1e-6
LOG2E = 1.4426950408889634
NEG = -1e30
LANES = 128
NEAR = MAX_DISTANCE
FAR_BUCKET = 15
SB_SKIP_MARGIN = 145.0
SOFTMAX_BOUND_LIMIT = 100.0

V7X_VMEM_BYTES = 64 * 1024 * 1024
VMEM_LIMIT = V7X_VMEM_BYTES * 7 // 8
TOKEN_TILE = 512
PROJ_CHUNK = 512
DIFF_TILE = 512
DIFF_FAR_GROUP = 8
SB_TILE = 256
SB_PAIRS_PER_STEP = 2
SB_SAMPLE_CHUNK = 512
SWA_TILE = 128

NT_DIMS = (((1,), (1,)), ((), ()))


def _cparams(n_axes):
    return pltpu.CompilerParams(dimension_semantics=("arbitrary",) * n_axes,
                                vmem_limit_bytes=VMEM_LIMIT)


def _resident(block_shape, index_map):
    return pl.BlockSpec(block_shape, index_map, pipeline_mode=pl.Buffered(1))


def _t5_bucket(rel):
    half = N_BUCKETS // 2
    exact = half // 2
    ret = jnp.where(rel > 0, half, 0).astype(jnp.int32)
    n = jnp.abs(rel)
    nf = jnp.maximum(n, 1).astype(F32)
    large = exact + (jnp.log(nf / exact) / math.log(MAX_DISTANCE / exact) * (half - exact)).astype(jnp.int32)
    large = jnp.minimum(large, half - 1)
    return ret + jnp.where(n < exact, n, large)


def _bias_tile(rel_bias, n_j, n_i, r0, valid, shift_far):
    tab = rel_bias.astype(F32)
    period = -(-(n_i + n_j - 1) // LANES) * LANES
    u = jnp.arange(period)
    rel = jnp.where(u < n_i, r0 - u, r0 + (period - u))
    vals = jnp.moveaxis(tab[_t5_bucket(rel)], -1, 0)
    if shift_far:
        vals = vals - tab[FAR_BUCKET][:, None]
    vals = (vals * LOG2E).reshape(vals.shape[0], 1, period)

    def expand(y_ref, o_ref):
        rows = jnp.broadcast_to(y_ref[0], (n_j, period))
        o_ref[0] = pltpu.roll(rows, 0, 1, stride=1, stride_axis=0)[:, :n_i]

    b = pl.pallas_call(
        expand,
        grid=(vals.shape[0],),
        in_specs=[pl.BlockSpec((1, 1, period), lambda c: (c, 0, 0))],
        out_specs=pl.BlockSpec((1, n_j, n_i), lambda c: (c, 0, 0)),
        out_shape=jax.ShapeDtypeStruct((vals.shape[0], n_j, n_i), F32),
        compiler_params=_cparams(1),
        name="bias_toeplitz",
    )(vals)
    if valid is not None:
        b = jnp.where(valid[None], b, NEG)
    return b


def _bias_stats(rel_bias):
    tab = rel_bias.astype(F32)
    sh = (tab - tab[FAR_BUCKET][None, :]) * LOG2E
    bmax = jnp.max(sh, axis=0)
    return jnp.stack([bmax, bmax - jnp.min(sh, axis=0)], axis=-1)


def _proj_kernel(x_ref, g_ref, wt_ref, qg_ref, kg_ref, *refs, nq, nk, nv, vb, head_norm, kstat, n_alias,
                 slot, all_slots, kv_layout):
    out_refs = refs[n_alias:]
    qT_ref, k32_ref, k16_ref, v32_ref, vT_ref = out_refs[:5]
    if all_slots > 1:
        for s in range(all_slots):
            if s != slot:
                k32_ref[s] = jnp.zeros(k32_ref.shape[1:], F32)
                v32_ref[s] = jnp.zeros(v32_ref.shape[1:], F32)
        k32_ref = k32_ref.at[slot]
        v32_ref = v32_ref.at[slot]
    x = x_ref[...]
    ms = jnp.mean(x * x, axis=-1, keepdims=True)
    h = (x * lax.rsqrt(ms + EPS) * g_ref[...]).astype(BF16)
    tm = x.shape[0]

    def rows(r0, n):
        return lax.dot_general(wt_ref[r0:r0 + n, :], h, NT_DIMS, preferred_element_type=F32)

    def headnorm(t, gcol):
        n = t.shape[0]
        t3 = t.reshape(n // HEAD_DIM, HEAD_DIM, tm)
        r = lax.rsqrt(jnp.mean(t3 * t3, axis=1, keepdims=True) + EPS)
        return (t3 * r).reshape(n, tm) * gcol

    def store_kv(ref, r0, xt, x):
        n = xt.shape[0]
        if kv_layout == "feature":
            ref[r0:r0 + n, :] = xt
            return
        x = xt.T if x is None else x
        if kv_layout == "token":
            ref[:, r0:r0 + n] = x
            return
        heads = ref.shape[0] // tm
        for c in range(n // LANES):
            head = r0 // LANES + c
            ref[pl.ds(head, tm, stride=heads), :] = x[:, c * LANES:(c + 1) * LANES]

    for r0 in range(0, nq, PROJ_CHUNK):
        n = min(PROJ_CHUNK, nq - r0)
        qt = rows(r0, n)
        qt = headnorm(qt, qg_ref[r0:r0 + n, :]) if head_norm else qt * qg_ref[r0:r0 + n, :]
        qT_ref[r0:r0 + n, :] = qt.astype(BF16)

    for r0 in range(0, nk, PROJ_CHUNK):
        n = min(PROJ_CHUNK, nk - r0)
        kt = rows(nq + r0, n)
        if head_norm:
            kt = headnorm(kt, kg_ref[r0:r0 + n, :])
        k = kt.T
        store_kv(k32_ref, r0, kt, k)
        k16_ref[:, r0:r0 + n] = k.astype(BF16)
        if kstat is not None:
            stat_ref = out_refs[5]
            kr = kt.astype(BF16).astype(F32)
            if kstat == "absmax":
                stat = jnp.max(jnp.abs(kr), axis=1, keepdims=True)
                srows = slice(r0, r0 + n)
            else:
                k3 = kr.reshape(n // HEAD_DIM, HEAD_DIM, tm)
                stat = jnp.max(jnp.sum(k3 * k3, axis=1), axis=1, keepdims=True)
                srows = slice(r0 // HEAD_DIM, (r0 + n) // HEAD_DIM)

            @pl.when(pl.program_id(0) == 0)
            def _(stat=stat, srows=srows):
                stat_ref[srows, :] = stat

            @pl.when(pl.program_id(0) > 0)
            def _(stat=stat, srows=srows):
                stat_ref[srows, :] = jnp.maximum(stat_ref[srows, :], stat)

    for r0 in range(0, nv, PROJ_CHUNK):
        n = min(PROJ_CHUNK, nv - r0)
        vt = rows(nq + nk + r0, n)
        store_kv(v32_ref, r0, vt, None)
        vt16 = vt.astype(BF16)
        for b in range(tm // vb):
            vT_ref[b, r0:r0 + n, :] = vt16[:, b * vb:(b + 1) * vb]


def _project(x, g, wt, qg, kg, *, nq, nk, nv, vb, head_norm, kstat=None, slot=0, n_slots=1, earlier=None,
             kv_layout="token"):
    n_tok = x.shape[0]
    tm = min(TOKEN_TILE, n_tok)
    vb = min(vb, tm)
    grid = (n_tok // tm,)
    ntot = nq + nk + nv

    def kv_shape(width):
        if kv_layout == "feature":
            return (n_slots, width, n_tok)
        if kv_layout == "head_rows":
            return (n_slots, n_tok * (width // LANES), LANES)
        return (n_slots, n_tok, width)

    out_shape = [
        jax.ShapeDtypeStruct((nq, n_tok), BF16),
        jax.ShapeDtypeStruct(kv_shape(nk), F32),
        jax.ShapeDtypeStruct((n_tok, nk), BF16),
        jax.ShapeDtypeStruct(kv_shape(nv), F32),
        jax.ShapeDtypeStruct((n_tok // vb, nv, vb), BF16),
    ]
    all_slots = n_slots if (earlier is None and n_slots > 1) else 1

    def kv_spec(width):
        lead = n_slots if all_slots > 1 else None
        first = 0 if all_slots > 1 else slot
        if kv_layout == "feature":
            return pl.BlockSpec((lead, width, tm), lambda i: (first, 0, i))
        if kv_layout == "head_rows":
            return pl.BlockSpec((lead, tm * (width // LANES), LANES), lambda i: (first, i, 0))
        return pl.BlockSpec((lead, tm, width), lambda i: (first, i, 0))

    out_specs = [
        pl.BlockSpec((nq, tm), lambda i: (0, i)),
        kv_spec(nk),
        pl.BlockSpec((tm, nk), lambda i: (i, 0)),
        kv_spec(nv),
        pl.BlockSpec((tm // vb, nv, vb), lambda i: (i, 0, 0)),
    ]
    if kstat is not None:
        n_stat = nk if kstat == "absmax" else nk // HEAD_DIM
        out_shape.append(jax.ShapeDtypeStruct((n_stat, 1), F32))
        out_specs.append(pl.BlockSpec((n_stat, 1), lambda i: (0, 0)))
    in_specs = [
        pl.BlockSpec((tm, D_MODEL), lambda i: (i, 0)),
        _resident((1, D_MODEL), lambda i: (0, 0)),
        _resident((ntot, D_MODEL), lambda i: (0, 0)),
        _resident((nq, 1), lambda i: (0, 0)),
        _resident((nk, 1), lambda i: (0, 0)),
    ]
    args = [x, g, wt, qg, kg]
    aliases = {}
    if earlier is not None:
        aliases = {len(args): 1, len(args) + 1: 3}
        in_specs += [pl.BlockSpec(memory_space=pl.ANY)] * 2
        args += list(earlier)
    return pl.pallas_call(
        functools.partial(_proj_kernel, nq=nq, nk=nk, nv=nv, vb=vb, head_norm=head_norm, kstat=kstat,
                          n_alias=len(aliases), slot=slot, all_slots=all_slots,
                          kv_layout=kv_layout),
        grid=grid,
        in_specs=in_specs,
        out_specs=out_specs,
        out_shape=out_shape,
        input_output_aliases=aliases,
        compiler_params=_cparams(1),
        name="proj",
    )(*args)


def _mlp_kernel(xp_ref, op_ref, xs_ref, os_ref, wo_ref, g_ref, wup_ref, wdn_ref, yp_ref, ys_ref, *, f_chunk,
                n_prompt_tiles):
    def token_tile(x_ref, o_ref, y_ref):
        x1 = x_ref[...] + jnp.dot(o_ref[...], wo_ref[...], preferred_element_type=F32)
        ms = jnp.mean(x1 * x1, axis=-1, keepdims=True)
        hn = (x1 * lax.rsqrt(ms + EPS) * g_ref[...]).astype(BF16)
        acc = x1
        for f in range(D_FF // f_chunk):
            u = jnp.dot(hn, wup_ref[:, f * f_chunk:(f + 1) * f_chunk], preferred_element_type=F32)
            a = jnp.square(jnp.maximum(u, 0.0)).astype(BF16)
            acc = acc + jnp.dot(a, wdn_ref[f * f_chunk:(f + 1) * f_chunk, :], preferred_element_type=F32)
        y_ref[...] = acc

    @pl.when(pl.program_id(0) < n_prompt_tiles)
    def _():
        token_tile(xp_ref, op_ref, yp_ref)

    @pl.when(pl.program_id(0) == n_prompt_tiles)
    def _():
        token_tile(xs_ref, os_ref, ys_ref)


def _outproj_mlp(xp, o_p, xs, o_s, wo, g, wup, wdn, layer):
    n_p, n_s = xp.shape[0], xs.shape[0]
    tm = min(TOKEN_TILE, n_p)
    n_tiles = n_p // tm
    prompt_spec = pl.BlockSpec((tm, D_MODEL), lambda i: (jnp.minimum(i, n_tiles - 1), 0))
    sample_spec = pl.BlockSpec((n_s, D_MODEL), lambda i: (0, 0))
    return pl.pallas_call(
        functools.partial(_mlp_kernel, f_chunk=1024, n_prompt_tiles=n_tiles),
        grid=(n_tiles + 1,),
        in_specs=[
            prompt_spec, prompt_spec, sample_spec, sample_spec,
            _resident((D_MODEL, D_MODEL), lambda i: (0, 0)),
            _resident((1, D_MODEL), lambda i: (0, 0)),
            _resident((None, D_MODEL, D_FF), lambda i: (layer, 0, 0)),
            _resident((None, D_FF, D_MODEL), lambda i: (layer, 0, 0)),
        ],
        out_specs=[prompt_spec, sample_spec],
        out_shape=[jax.ShapeDtypeStruct((n_p, D_MODEL), F32), jax.ShapeDtypeStruct((n_s, D_MODEL), F32)],
        compiler_params=_cparams(1),
        name="outproj_mlp",
    )(xp, o_p, xs, o_s, wo, g, wup, wdn)


def _diff_lambda(lam_ref, lam_init):
    lp = lam_ref[...]
    a = jnp.sum(lp[0:1] * lp[1:2], axis=-1, keepdims=True)
    b = jnp.sum(lp[2:3] * lp[3:4], axis=-1, keepdims=True)
    return jnp.exp(a) - jnp.exp(b) + lam_init


def _diff_prompt_kernel(qT_ref, k_ref, vT_ref, bd_ref, bp_ref, kn_ref, bs_ref, lam_ref, sub_ref, o_ref,
                        m_ref, l_ref, acc_ref, *, tq, lam_init):
    qi = pl.program_id(1)
    qT = qT_ref[...]
    row = lax.broadcasted_iota(jnp.int32, qT.shape, 0)
    zero = jnp.zeros_like(qT)
    qm = (jnp.where(row < HEAD_DIM, qT, zero), jnp.where(row >= HEAD_DIM, qT, zero))

    def prev_bias(m, s):
        corner = s[tq - NEAR:, :NEAR] + bp_ref[0, m]
        bottom = jnp.concatenate([corner, s[tq - NEAR:, NEAR:]], axis=1)
        return jnp.concatenate([s[:tq - NEAR], bottom], axis=0)

    def diag_bias(m, s):
        return s + bd_ref[0, m]

    def prev_diag_bias(m, s):
        return jnp.concatenate([prev_bias(m, s[:tq]), diag_bias(m, s[tq:])], axis=0)

    n_far = jnp.maximum(qi - 1, 0)

    def walk(tile_fn):
        def far_body(j, carry):
            tile_fn(j, None)
            return carry

        lax.fori_loop(0, n_far, far_body, 0)

        @pl.when(qi >= 1)
        def _():
            tile_fn(qi - 1, prev_bias)

        tile_fn(qi, diag_bias)

    span = []
    for m in range(2):
        qf = qm[m].astype(F32)
        qn = jnp.sqrt(jnp.sum(qf * qf, axis=0, keepdims=True))
        reach = qn * (jnp.sqrt(kn_ref[m:m + 1, :]) * 1.001)
        m_ref[m] = reach + bs_ref[m:m + 1, 0:1]
        span.append(jnp.max(2.0 * reach + bs_ref[m:m + 1, 1:2]))
    bound_is_tight = jnp.maximum(span[0], span[1]) <= SOFTMAX_BOUND_LIMIT

    @pl.when(jnp.logical_not(bound_is_tight))
    def _():
        m_ref[...] = jnp.full(m_ref.shape, NEG, F32)

        def max_tile(j, bias_fn):
            k = k_ref[j]
            for m in range(2):
                s = jnp.dot(k, qm[m], preferred_element_type=F32)
                if bias_fn is not None:
                    s = bias_fn(m, s)
                m_ref[m] = jnp.maximum(m_ref[m], jnp.max(s, axis=0, keepdims=True))

        walk(max_tile)

    l_ref[...] = jnp.zeros(l_ref.shape, F32)
    acc_ref[...] = jnp.zeros(acc_ref.shape, F32)

    def acc_tiles(js, bias_fn):
        k = jnp.concatenate([k_ref[j] for j in js], axis=0)
        vT = jnp.concatenate([vT_ref[j] for j in js], axis=1)
        nk = len(js) * tq
        for m in range(2):
            s = jnp.dot(k, qm[m], preferred_element_type=F32)
            if bias_fn is not None:
                s = bias_fn(m, s)
            p = jnp.exp2(s - m_ref[m])
            l_ref[m] += jnp.sum(p.reshape(nk // 8, 8, tq), axis=0)
            acc_ref[m] += jnp.dot(vT, p.astype(BF16), preferred_element_type=F32)

    def far_group(jj, carry):
        acc_tiles([DIFF_FAR_GROUP * jj + u for u in range(DIFF_FAR_GROUP)], None)
        return carry

    n_groups = n_far // DIFF_FAR_GROUP
    lax.fori_loop(0, n_groups, far_group, 0)
    start = n_groups * DIFF_FAR_GROUP
    left = n_far - start
    size = DIFF_FAR_GROUP // 2
    while size >= 2:
        @pl.when((left & size) != 0)
        def _(start=start, size=size):
            acc_tiles([start + u for u in range(size)], None)

        start = start + (left & size)
        size //= 2
    odd = (left & 1) != 0

    def plain_prev_diag_bias(m, s):
        return jnp.concatenate([s[:tq], prev_diag_bias(m, s[tq:])], axis=0)

    @pl.when(odd)
    def _():
        acc_tiles([start, qi - 1, qi], plain_prev_diag_bias)

    @pl.when(jnp.logical_and(qi >= 1, jnp.logical_not(odd)))
    def _():
        acc_tiles([qi - 1, qi], prev_diag_bias)

    @pl.when(qi == 0)
    def _():
        acc_tiles([qi], diag_bias)

    lam = _diff_lambda(lam_ref, lam_init)
    o0 = acc_ref[0] * (1.0 / jnp.sum(l_ref[0], axis=0, keepdims=True))
    o1 = acc_ref[1] * (1.0 / jnp.sum(l_ref[1], axis=0, keepdims=True))
    o = o0 - lam * o1
    ms = jnp.mean(o * o, axis=0, keepdims=True)
    o = o * lax.rsqrt(ms + EPS) * sub_ref[...] * (1.0 - lam_init)
    o_ref[...] = o.T.astype(BF16)


def _diff_prompt(qT, k16, vT, bias_d, bias_p, knorm2, bstats, lam_p, subln, lam_init):
    n_tok = k16.shape[0]
    tq = min(DIFF_TILE, n_tok)
    nb = n_tok // tq
    k3 = k16.reshape(nb, tq, D_MODEL)
    return pl.pallas_call(
        functools.partial(_diff_prompt_kernel, tq=tq, lam_init=lam_init),
        grid=(DIFF_HEADS, nb),
        in_specs=[
            pl.BlockSpec((2 * HEAD_DIM, tq), lambda h, i: (h, i)),
            pl.BlockSpec((nb, tq, 2 * HEAD_DIM), lambda h, i: (0, 0, h)),
            pl.BlockSpec((nb, 2 * HEAD_DIM, tq), lambda h, i: (0, h, 0)),
            pl.BlockSpec((1, 2, tq, tq), lambda h, i: (h, 0, 0, 0)),
            pl.BlockSpec((1, 2, NEAR, NEAR), lambda h, i: (h, 0, 0, 0)),
            pl.BlockSpec((None, 2, 1), lambda h, i: (h, 0, 0)),
            pl.BlockSpec((None, 2, 2), lambda h, i: (h, 0, 0)),
            pl.BlockSpec((4, HEAD_DIM), lambda h, i: (0, 0)),
            pl.BlockSpec((2 * HEAD_DIM, 1), lambda h, i: (0, 0)),
        ],
        out_specs=pl.BlockSpec((tq, 2 * HEAD_DIM), lambda h, i: (i, h)),
        out_shape=jax.ShapeDtypeStruct((n_tok, D_MODEL), BF16),
        scratch_shapes=[
            pltpu.VMEM((2, 1, tq), F32),
            pltpu.VMEM((2, 8, tq), F32),
            pltpu.VMEM((2, 2 * HEAD_DIM, tq), F32),
        ],
        compiler_params=_cparams(2),
        name="diff_prompt",
    )(qT, k3, vT, bias_d, bias_p, knorm2, bstats, lam_p, subln)


def _diff_sample_kernel(q_ref, ck_ref, cv_ref, kn_ref, vn_ref, b_ref, lam_ref, sub_ref, o_ref, *, lam_init, past):
    t = q_ref.shape[2]
    lane = lax.broadcasted_iota(jnp.int32, (t, 2 * HEAD_DIM), 1)
    lam = _diff_lambda(lam_ref, lam_init)
    for h in range(DIFF_HEADS):
        q = q_ref[0, h]
        zero = jnp.zeros_like(q)
        qq = jnp.concatenate([jnp.where(lane < HEAD_DIM, q, zero), jnp.where(lane >= HEAD_DIM, q, zero)], axis=0)
        ck = ck_ref[pl.ds(h, past, stride=DIFF_HEADS), :].astype(BF16)
        cv = cv_ref[pl.ds(h, past, stride=DIFF_HEADS), :].astype(BF16)
        kn = kn_ref[0, h]
        vn = vn_ref[0, h].astype(BF16)
        s_c = lax.dot_general(qq, ck, NT_DIMS, preferred_element_type=F32)
        s_n = lax.dot_general(qq, kn, NT_DIMS, preferred_element_type=F32)
        bias = b_ref[h].reshape(2 * t, NEAR + t)
        s_far = s_c[:, :past - NEAR]
        s_near = s_c[:, past - NEAR:] + bias[:, :NEAR]
        s_n = s_n + bias[:, NEAR:]
        m = jnp.maximum(jnp.maximum(jnp.max(s_far, axis=1, keepdims=True), jnp.max(s_near, axis=1, keepdims=True)),
                        jnp.max(s_n, axis=1, keepdims=True))
        p_far = jnp.exp2(s_far - m)
        p_near = jnp.exp2(s_near - m)
        p_n = jnp.exp2(s_n - m)
        l = (jnp.sum(p_far, axis=1, keepdims=True) + jnp.sum(p_near, axis=1, keepdims=True)
             + jnp.sum(p_n, axis=1, keepdims=True))
        o = (jnp.dot(p_far.astype(BF16), cv[:past - NEAR], preferred_element_type=F32)
             + jnp.dot(p_near.astype(BF16), cv[past - NEAR:], preferred_element_type=F32)
             + jnp.dot(p_n.astype(BF16), vn, preferred_element_type=F32))
        o = o * (1.0 / l)
        od = o[:t] - lam * o[t:]
        ms = jnp.mean(od * od, axis=-1, keepdims=True)
        od = od * lax.rsqrt(ms + EPS) * sub_ref[...] * (1.0 - lam_init)
        o_ref[0, h] = od.astype(BF16)


def _diff_sample(q_s, cache_k, cache_v, layer, kn, vn, bias_s, lam_p, subln_row, lam_init):
    nb, _, t, _ = q_s.shape
    past = cache_k.shape[2]
    rows = past * DIFF_HEADS
    ck = cache_k.reshape(cache_k.shape[0], nb, rows, 2 * HEAD_DIM)
    cv = cache_v.reshape(cache_v.shape[0], nb, rows, 2 * HEAD_DIM)
    cache_spec = pl.BlockSpec((None, None, rows, 2 * HEAD_DIM), lambda b: (layer, b, 0, 0))
    head_spec = pl.BlockSpec((1, DIFF_HEADS, t, 2 * HEAD_DIM), lambda b: (b, 0, 0, 0))
    return pl.pallas_call(
        functools.partial(_diff_sample_kernel, lam_init=lam_init, past=past),
        grid=(nb,),
        in_specs=[
            head_spec, cache_spec, cache_spec, head_spec, head_spec,
            pl.BlockSpec((DIFF_HEADS, 2, t, NEAR + t), lambda b: (0, 0, 0, 0)),
            pl.BlockSpec((4, HEAD_DIM), lambda b: (0, 0)),
            pl.BlockSpec((1, 2 * HEAD_DIM), lambda b: (0, 0)),
        ],
        out_specs=head_spec,
        out_shape=jax.ShapeDtypeStruct((nb, DIFF_HEADS, t, 2 * HEAD_DIM), BF16),
        compiler_params=_cparams(1),
        name="diff_sample",
    )(q_s, ck, cv, kn, vn, bias_s, lam_p, subln_row)


def _swa_prompt_kernel(qT_ref, kc_ref, kp_ref, vc_ref, vp_ref, b_ref, sink_ref, o_ref):
    i = pl.program_id(0)
    qT = qT_ref[...]
    kp = kp_ref[...]
    kc = kc_ref[...]
    vp = vp_ref[0]
    vc = vc_ref[0]
    tq = qT.shape[1]
    heads_per_pair = 2 * SWA_GROUP
    key_row = lax.broadcasted_iota(jnp.int32, (2 * tq, heads_per_pair * tq), 0)
    has_prev = key_row >= jnp.where(i > 0, 0, tq)
    zeros = jnp.zeros((HEAD_DIM, tq), BF16)
    outs = []
    for pair in range(SWA_KV_HEADS // 2):
        lanes = slice(pair * 2 * HEAD_DIM, (pair + 1) * 2 * HEAD_DIM)
        k_band = jnp.concatenate([kp[:, lanes], kc[:, lanes]], axis=0)
        qg = []
        for hh in range(heads_per_pair):
            h = pair * heads_per_pair + hh
            qh = qT[h * HEAD_DIM:(h + 1) * HEAD_DIM]
            qg.append(jnp.concatenate([qh, zeros] if hh < SWA_GROUP else [zeros, qh], axis=0))
        q8 = jnp.concatenate(qg, axis=1)
        bias = jnp.concatenate([b_ref[2 * pair], b_ref[2 * pair + 1]], axis=1)
        sink = jnp.concatenate([sink_ref[2 * pair], sink_ref[2 * pair + 1]], axis=1)
        s = jnp.dot(k_band, q8, preferred_element_type=F32) + bias
        s = jnp.where(has_prev, s, NEG)
        m = jnp.maximum(jnp.max(s, axis=0, keepdims=True), sink)
        p = jnp.exp2(s - m)
        den = jnp.sum(p, axis=0, keepdims=True) + jnp.exp2(sink - m)
        inv = 1.0 / den
        p = p.astype(BF16)
        for sub in range(2):
            kv = 2 * pair + sub
            cols = slice(sub * SWA_GROUP * tq, (sub + 1) * SWA_GROUP * tq)
            v_band = jnp.concatenate([vp[kv * HEAD_DIM:(kv + 1) * HEAD_DIM], vc[kv * HEAD_DIM:(kv + 1) * HEAD_DIM]],
                                     axis=1)
            oT4 = jnp.dot(v_band, p[:, cols], preferred_element_type=F32) * inv[:, cols]
            for g in range(SWA_GROUP):
                outs.append(oT4[:, g * tq:(g + 1) * tq])
    o_ref[...] = jnp.concatenate(outs, axis=0).T.astype(BF16)


def _swa_prompt(qT, k16, vT, bias, sinks):
    n_tok = k16.shape[0]
    tq = SWA_TILE
    nkv = SWA_KV_HEADS * HEAD_DIM
    return pl.pallas_call(
        _swa_prompt_kernel,
        grid=(n_tok // tq,),
        in_specs=[
            pl.BlockSpec((D_MODEL, tq), lambda i: (0, i)),
            pl.BlockSpec((tq, nkv), lambda i: (i, 0)),
            pl.BlockSpec((tq, nkv), lambda i: (jnp.maximum(i - 1, 0), 0)),
            pl.BlockSpec((1, nkv, tq), lambda i: (i, 0, 0)),
            pl.BlockSpec((1, nkv, tq), lambda i: (jnp.maximum(i - 1, 0), 0, 0)),
            _resident((SWA_KV_HEADS, 2 * tq, SWA_GROUP * tq), lambda i: (0, 0, 0)),
            _resident((SWA_KV_HEADS, 1, SWA_GROUP * tq), lambda i: (0, 0, 0)),
        ],
        out_specs=pl.BlockSpec((tq, D_MODEL), lambda i: (i, 0)),
        out_shape=jax.ShapeDtypeStruct((n_tok, D_MODEL), BF16),
        compiler_params=_cparams(1),
        name="swa_prompt",
    )(qT, k16, k16, vT, vT, bias, sinks)


def _swa_sample_kernel(q_ref, kc_ref, vc_ref, kn_ref, vn_ref, b_ref, sink_ref, o_ref, ko_ref, vo_ref):
    t = kn_ref.shape[2]
    buf = kc_ref.shape[2]
    for kv in range(SWA_KV_HEADS):
        kc32 = kc_ref[0, kv]
        vc32 = vc_ref[0, kv]
        kn32 = kn_ref[0, kv]
        vn32 = vn_ref[0, kv]
        ko_ref[0, kv, 0:buf - t, :] = kc32[t:]
        ko_ref[0, kv, buf - t:buf, :] = kn32
        vo_ref[0, kv, 0:buf - t, :] = vc32[t:]
        vo_ref[0, kv, buf - t:buf, :] = vn32
        kc = kc32.astype(BF16)
        vc = vc32.astype(BF16)
        kn = kn32.astype(BF16)
        vn = vn32.astype(BF16)
        heads = range(kv * SWA_GROUP, (kv + 1) * SWA_GROUP)
        q = jnp.concatenate([q_ref[0, h] for h in heads], axis=0)
        b = jnp.concatenate([b_ref[h] for h in heads], axis=0)
        sink = jnp.concatenate([jnp.broadcast_to(sink_ref[h], (t, 1)) for h in heads], axis=0)
        s_c = lax.dot_general(q, kc, NT_DIMS, preferred_element_type=F32) + b[:, :buf]
        s_n = lax.dot_general(q, kn, NT_DIMS, preferred_element_type=F32) + b[:, buf:]
        m = jnp.maximum(jnp.maximum(jnp.max(s_c, axis=1, keepdims=True), jnp.max(s_n, axis=1, keepdims=True)),
                        sink)
        p_c = jnp.exp2(s_c - m)
        p_n = jnp.exp2(s_n - m)
        den = (jnp.sum(p_c, axis=1, keepdims=True) + jnp.sum(p_n, axis=1, keepdims=True)
               + jnp.exp2(sink - m))
        o = (jnp.dot(p_c.astype(BF16), vc, preferred_element_type=F32)
             + jnp.dot(p_n.astype(BF16), vn, preferred_element_type=F32)) * (1.0 / den)
        for g, h in enumerate(heads):
            o_ref[0, h] = o[g * t:(g + 1) * t].astype(BF16)


def _swa_sample(q_s, kc, vc, kn, vn, bias, sinks):
    nb, _, t, _ = q_s.shape
    buf = kc.shape[2]
    kvspec = pl.BlockSpec((1, SWA_KV_HEADS, buf, HEAD_DIM), lambda b: (b, 0, 0, 0))
    nspec = pl.BlockSpec((1, SWA_KV_HEADS, t, HEAD_DIM), lambda b: (b, 0, 0, 0))
    return pl.pallas_call(
        _swa_sample_kernel,
        grid=(nb,),
        in_specs=[
            pl.BlockSpec((1, SWA_Q_HEADS, t, HEAD_DIM), lambda b: (b, 0, 0, 0)),
            kvspec, kvspec, nspec, nspec,
            pl.BlockSpec((SWA_Q_HEADS, t, buf + t), lambda b: (0, 0, 0)),
            pl.BlockSpec((SWA_Q_HEADS, 1, 1), lambda b: (0, 0, 0)),
        ],
        out_specs=[
            pl.BlockSpec((1, SWA_Q_HEADS, t, HEAD_DIM), lambda b: (b, 0, 0, 0)),
            kvspec, kvspec,
        ],
        out_shape=[
            jax.ShapeDtypeStruct((nb, SWA_Q_HEADS, t, HEAD_DIM), BF16),
            jax.ShapeDtypeStruct((nb, SWA_KV_HEADS, buf, HEAD_DIM), F32),
            jax.ShapeDtypeStruct((nb, SWA_KV_HEADS, buf, HEAD_DIM), F32),
        ],
        compiler_params=_cparams(1),
        name="swa_sample",
    )(q_s, kc, vc, kn, vn, bias, sinks)


def _softplus(z):
    return jnp.maximum(z, 0.0) + jnp.log2(1.0 + jnp.exp2(-jnp.abs(z)))


def _split_bf16(x):
    hi = x.astype(BF16)
    lo = (x - hi.astype(F32)).astype(BF16)
    return hi, lo


def _sb_prompt_kernel(qT_ref, k_ref, vT_ref, kmax_ref, o_ref, carry_ref, acc_ref, *, tq, pairs):
    qi = pl.program_id(1)
    width = 2 * pairs * tq
    row = lax.broadcasted_iota(jnp.int32, (2 * HEAD_DIM, tq), 0)
    qpairs, zbs = [], []
    for pr in range(pairs):
        qT = qT_ref[2 * HEAD_DIM * pr:2 * HEAD_DIM * (pr + 1), :]
        zero = jnp.zeros_like(qT)
        qp = jnp.concatenate([jnp.where(row < HEAD_DIM, qT, zero), jnp.where(row >= HEAD_DIM, qT, zero)], axis=1)
        qpairs.append(qp)
        kmax = kmax_ref[2 * HEAD_DIM * pr:2 * HEAD_DIM * (pr + 1), :]
        zbs.append(jnp.sum(jnp.abs(qp.astype(F32)) * kmax, axis=0, keepdims=True))
    zb = jnp.concatenate(zbs, axis=1)
    kr = lax.broadcasted_iota(jnp.int32, (tq, tq), 0)
    kc = lax.broadcasted_iota(jnp.int32, (tq, tq), 1)
    tri = jnp.where(kc >= kr, 1.0, 0.0).astype(BF16)
    tri2 = jnp.concatenate([tri, tri], axis=1)
    krw = lax.broadcasted_iota(jnp.int32, (tq, width), 0)
    kcw = lax.broadcasted_iota(jnp.int32, (tq, width), 1)
    causal = krw < lax.rem(kcw, tq)
    carry_ref[...] = jnp.zeros(carry_ref.shape, F32)
    acc_ref[...] = jnp.zeros(acc_ref.shape, F32)

    def tiles(js, first_is_diag):
        zs, cs = [], []
        for n, j in enumerate(js):
            k = k_ref[j]
            z = jnp.concatenate([jnp.dot(k[:, 2 * HEAD_DIM * pr:2 * HEAD_DIM * (pr + 1)], qpairs[pr],
                                         preferred_element_type=F32) for pr in range(pairs)], axis=1)
            lsp = _softplus(z)
            if first_is_diag and n == 0:
                lsp = jnp.where(causal, lsp, 0.0)
            hi, lo = _split_bf16(lsp)
            zs.append(z)
            cs.append(jnp.dot(tri2, jnp.concatenate([hi, lo], axis=0), preferred_element_type=F32))
        carry = carry_ref[...]
        weights = []
        for n in range(len(js)):
            a = jnp.exp2(zs[n] - cs[n] - carry)
            if first_is_diag and n == 0:
                a = jnp.where(causal, a, 0.0)
            weights.append(a.astype(BF16))
            carry = carry + cs[n][0:1]
        carry_ref[...] = carry
        for pr in range(pairs):
            vT = jnp.concatenate([vT_ref[j, 2 * HEAD_DIM * pr:2 * HEAD_DIM * (pr + 1), :] for j in js],
                                 axis=1)
            w = jnp.concatenate([a[:, 2 * tq * pr:2 * tq * (pr + 1)] for a in weights], axis=0)
            acc_ref[pr] += jnp.dot(vT, w, preferred_element_type=F32)

    def still_live():
        return (jnp.min(carry_ref[...] - zb) < SB_SKIP_MARGIN).astype(jnp.int32)

    @pl.when(qi == 0)
    def _():
        tiles([qi], True)

    @pl.when(qi == 1)
    def _():
        tiles([qi, qi - 1], True)

    @pl.when(qi >= 2)
    def _():
        tiles([qi, qi - 1, qi - 2], True)

    def cond(state):
        j, live = state
        return jnp.logical_and(j >= 0, live > 0)

    def body(state):
        j, _ = state
        tiles([j], False)
        return j - 1, still_live()

    lax.while_loop(cond, body, (qi - 3, still_live()))
    outs = []
    for pr in range(pairs):
        outs += [acc_ref[pr, 0:HEAD_DIM, 0:tq], acc_ref[pr, HEAD_DIM:2 * HEAD_DIM, tq:2 * tq]]
    o_ref[...] = jnp.concatenate(outs, axis=0).T.astype(BF16)


def _sb_prompt(qT, k16, vT, kmax):
    n_tok = k16.shape[0]
    tq = min(SB_TILE, n_tok)
    nb = n_tok // tq
    k3 = k16.reshape(nb, tq, D_MODEL)
    pairs = SB_PAIRS_PER_STEP
    wide = 2 * HEAD_DIM * pairs
    return pl.pallas_call(
        functools.partial(_sb_prompt_kernel, tq=tq, pairs=pairs),
        grid=(SB_HEADS // (2 * pairs), nb),
        in_specs=[
            pl.BlockSpec((wide, tq), lambda p, i: (p, i)),
            pl.BlockSpec((nb, tq, wide), lambda p, i: (0, 0, p)),
            pl.BlockSpec((nb, wide, tq), lambda p, i: (0, p, 0)),
            pl.BlockSpec((wide, 1), lambda p, i: (p, 0)),
        ],
        out_specs=pl.BlockSpec((tq, wide), lambda p, i: (i, p)),
        out_shape=jax.ShapeDtypeStruct((n_tok, D_MODEL), BF16),
        scratch_shapes=[
            pltpu.VMEM((1, 2 * pairs * tq), F32),
            pltpu.VMEM((pairs, 2 * HEAD_DIM, 2 * tq), F32),
        ],
        compiler_params=_cparams(2),
        name="sb_prompt",
    )(qT, k3, vT, kmax)


def _tri_lanes(n):
    r = lax.broadcasted_iota(jnp.int32, (n, n), 0)
    c = lax.broadcasted_iota(jnp.int32, (n, n), 1)
    return jnp.where(r >= c, 1.0, 0.0).astype(BF16)


def _sb_sample_kernel(q_ref, ck_ref, cv_ref, kn_ref, vn_ref, o_ref, carry_ref, acc_ref, *, ch, kb):
    c = pl.program_id(1)
    nh = q_ref.shape[1]
    t = q_ref.shape[2]

    @pl.when(c == 0)
    def _():
        z = jnp.concatenate([lax.dot_general(q_ref[0, h], kn_ref[0, h], NT_DIMS, preferred_element_type=F32)
                             for h in range(nh)], axis=0)
        r = lax.broadcasted_iota(jnp.int32, (nh * t, t), 0)
        kc = lax.broadcasted_iota(jnp.int32, (nh * t, t), 1)
        causal = kc < lax.rem(r, t)
        hi, lo = _split_bf16(jnp.where(causal, _softplus(z), 0.0))
        tri = _tri_lanes(t)
        cs = jnp.dot(hi, tri, preferred_element_type=F32) + jnp.dot(lo, tri, preferred_element_type=F32)
        a = jnp.where(causal, jnp.exp2(z - cs), 0.0).astype(BF16)
        carry_ref[...] = cs[:, 0:1]
        for h in range(nh):
            acc_ref[h * t:(h + 1) * t, :] = jnp.dot(a[h * t:(h + 1) * t], vn_ref[0, h].astype(BF16),
                                                    preferred_element_type=F32)

    z = jnp.concatenate([jnp.dot(q_ref[0, h], ck_ref[h].astype(BF16), preferred_element_type=F32)
                         for h in range(nh)], axis=0)
    nblk = ch // kb
    rows = nh * t
    zst = jnp.concatenate([z[:, b * kb:(b + 1) * kb] for b in range(nblk)], axis=0)
    hi, lo = _split_bf16(_softplus(zst))
    tri = _tri_lanes(kb)
    cs = jnp.dot(hi, tri, preferred_element_type=F32) + jnp.dot(lo, tri, preferred_element_type=F32)
    run = carry_ref[...]
    carries = [None] * nblk
    for b in reversed(range(nblk)):
        carries[b] = run
        run = run + cs[b * rows:(b + 1) * rows, 0:1]
    carry_ref[...] = run
    a = jnp.exp2(zst - cs - jnp.concatenate(carries, axis=0))
    a = jnp.concatenate([a[b * rows:(b + 1) * rows] for b in range(nblk)], axis=1).astype(BF16)
    for h in range(nh):
        acc_ref[h * t:(h + 1) * t, :] += lax.dot_general(a[h * t:(h + 1) * t], cv_ref[h].astype(BF16), NT_DIMS,
                                                         preferred_element_type=F32)

    @pl.when(c == pl.num_programs(1) - 1)
    def _():
        for h in range(nh):
            o_ref[0, h] = acc_ref[h * t:(h + 1) * t, :].astype(BF16)


def _sb_sample(q_s, cache_k, cache_v, layer, kn, vn):
    nb, nh, t, _ = q_s.shape
    past = cache_k.shape[2]
    ch = min(SB_SAMPLE_CHUNK, past)
    nch = past // ch
    ck = cache_k.transpose(0, 1, 3, 4, 2)
    cv = cache_v.transpose(0, 1, 3, 4, 2)
    cache_spec = pl.BlockSpec((None, None, nh, HEAD_DIM, ch), lambda b, c: (layer, b, 0, 0, nch - 1 - c))
    head_spec = pl.BlockSpec((1, nh, t, HEAD_DIM), lambda b, c: (b, 0, 0, 0))
    return pl.pallas_call(
        functools.partial(_sb_sample_kernel, ch=ch, kb=min(SB_TILE, ch)),
        grid=(nb, nch),
        in_specs=[head_spec, cache_spec, cache_spec, head_spec, head_spec],
        out_specs=head_spec,
        out_shape=jax.ShapeDtypeStruct((nb, nh, t, HEAD_DIM), BF16),
        scratch_shapes=[
            pltpu.VMEM((nh * t, 1), F32),
            pltpu.VMEM((nh * t, HEAD_DIM), F32),
        ],
        compiler_params=_cparams(2),
        name="sb_sample",
    )(q_s, ck, cv, kn, vn)


def _tile_col(gain, n, scale):
    return (jnp.tile(gain.astype(F32), n // gain.shape[0]) * scale).reshape(n, 1)


def _heads_major(x, nb, t, nh, hd):
    return x.reshape(nb, t, nh, hd).transpose(0, 2, 1, 3)


def _diff_bias_tiles(rel_bias, tq, t):
    j = jnp.arange(tq)
    valid_d = (j // CHUNK)[:, None] <= (j // CHUNK)[None, :]
    bias_d = _bias_tile(rel_bias, tq, tq, 0, valid_d, True).reshape(DIFF_HEADS, 2, tq, tq)
    bias_p = _bias_tile(rel_bias, NEAR, NEAR, -NEAR, None, True).reshape(DIFF_HEADS, 2, NEAR, NEAR)
    bstats = _bias_stats(rel_bias).reshape(DIFF_HEADS, 2, 2)
    bias_s = jnp.swapaxes(_bias_tile(rel_bias, NEAR + t, t, -NEAR, None, True), 1, 2)
    bias_s = bias_s.reshape(DIFF_HEADS, 2, t, NEAR + t)
    return bias_d, bias_p, bstats, bias_s


def _diff_layer(xp, xs, cache_k, cache_v, layer, n_layers, earlier, bias_tiles, g, w_qkv, q_gain, k_gain, lam_p,
                subln, lam_init):
    bias_d, bias_p, bstats, bias_s = bias_tiles
    n_p = xp.shape[0]
    nb = cache_k.shape[1]
    t = xs.shape[0] // nb
    wt = w_qkv.T.astype(BF16)
    qg = _tile_col(q_gain, D_MODEL, HEAD_DIM ** -0.5 * LOG2E)
    kg = _tile_col(k_gain, D_MODEL, 1.0)
    g = g.reshape(1, D_MODEL)
    kw = dict(nq=D_MODEL, nk=D_MODEL, nv=D_MODEL, head_norm=True)
    tq = min(DIFF_TILE, n_p)
    qT, k32, k16, v32, vT, knorm2 = _project(xp, g, wt, qg, kg, vb=tq, kstat="normsq", slot=layer,
                                             n_slots=n_layers, earlier=earlier, kv_layout="head_rows", **kw)
    qT_s, k32_s, k16_s, v32_s, _ = _project(xs, g, wt, qg, kg, vb=LANES, **kw)
    k32_s, v32_s = k32_s[0], v32_s[0]

    o_p = _diff_prompt(qT, k16, vT, bias_d, bias_p, knorm2.reshape(DIFF_HEADS, 2, 1), bstats, lam_p,
                       subln.reshape(2 * HEAD_DIM, 1), lam_init)
    hm = functools.partial(_heads_major, nb=nb, t=t, nh=DIFF_HEADS, hd=2 * HEAD_DIM)
    o_s = _diff_sample(hm(qT_s.T), cache_k, cache_v, layer, hm(k16_s), hm(v32_s), bias_s, lam_p,
                       subln.reshape(1, 2 * HEAD_DIM), lam_init)
    o_s = o_s.transpose(0, 2, 1, 3).reshape(nb * t, D_MODEL)
    return (o_p, o_s,
            k32, v32,
            k32_s.reshape(nb, t, DIFF_HEADS, 2 * HEAD_DIM), v32_s.reshape(nb, t, DIFF_HEADS, 2 * HEAD_DIM))


def _swa_layer(xp, xs, ck, cv, rel_bias, g, w_qkv, q_gain, k_gain, sinks):
    n_p = xp.shape[0]
    nb, buf = ck.shape[0], ck.shape[1]
    t = xs.shape[0] // nb
    nkv = SWA_KV_HEADS * HEAD_DIM
    wt = w_qkv.T.astype(BF16)
    qg = _tile_col(q_gain, D_MODEL, HEAD_DIM ** -0.5 * LOG2E)
    kg = _tile_col(k_gain, nkv, 1.0)
    g = g.reshape(1, D_MODEL)
    kw = dict(nq=D_MODEL, nk=nkv, nv=nkv, vb=SWA_TILE, head_norm=True)
    qT, k32, k16, v32, vT = _project(xp, g, wt, qg, kg, **kw)
    qT_s, k32_s, _, v32_s, _ = _project(xs, g, wt, qg, kg, **kw)
    k32, v32, k32_s, v32_s = k32[0], v32[0], k32_s[0], v32_s[0]
    sink_col = (sinks.astype(F32) * LOG2E).reshape(SWA_Q_HEADS, 1, 1)
    wchunks = WINDOW // CHUNK

    tq = SWA_TILE
    kchunk = jnp.floor_divide(jnp.arange(2 * tq) - tq, CHUNK)[:, None]
    qchunk = (jnp.arange(tq) // CHUNK)[None, :]
    valid = (kchunk <= qchunk) & (kchunk >= qchunk - wchunks)
    bias = _bias_tile(rel_bias, 2 * tq, tq, -tq, valid, False)
    bias_g = bias.reshape(SWA_KV_HEADS, SWA_GROUP, 2 * tq, tq).transpose(0, 2, 1, 3)
    bias_g = bias_g.reshape(SWA_KV_HEADS, 2 * tq, SWA_GROUP * tq)
    sink_g = jnp.repeat(sink_col.reshape(SWA_KV_HEADS, 1, SWA_GROUP), tq, axis=2)
    o_p = _swa_prompt(qT, k16, vT, bias_g, sink_g)

    sc = jnp.floor_divide(jnp.arange(buf + t) - buf, CHUNK)[:, None]
    tc = (jnp.arange(t) // CHUNK)[None, :]
    valid_s = (sc <= tc) & (sc >= tc - wchunks)
    bias_s = jnp.swapaxes(_bias_tile(rel_bias, buf + t, t, -buf, valid_s, False), 1, 2)
    q_s = _heads_major(qT_s.T, nb, t, SWA_Q_HEADS, HEAD_DIM)
    kn = _heads_major(k32_s, nb, t, SWA_KV_HEADS, HEAD_DIM)
    vn = _heads_major(v32_s, nb, t, SWA_KV_HEADS, HEAD_DIM)
    o_s, ko, vo = _swa_sample(q_s, ck.transpose(0, 2, 1, 3), cv.transpose(0, 2, 1, 3), kn, vn, bias_s, sink_col)
    o_s = o_s.transpose(0, 2, 1, 3).reshape(nb * t, D_MODEL)
    wbuf = min(WINDOW, n_p)
    return (o_p, o_s,
            k32[n_p - wbuf:].reshape(1, wbuf, SWA_KV_HEADS, HEAD_DIM),
            v32[n_p - wbuf:].reshape(1, wbuf, SWA_KV_HEADS, HEAD_DIM),
            ko.transpose(0, 2, 1, 3), vo.transpose(0, 2, 1, 3))


def _sb_layer(xp, xs, cache_k, cache_v, layer, n_layers, earlier, g, w_qkv):
    n_p = xp.shape[0]
    nb = cache_k.shape[1]
    t = xs.shape[0] // nb
    wt = w_qkv.T.astype(BF16)
    qg = jnp.full((D_MODEL, 1), HEAD_DIM ** -0.5 * LOG2E, F32)
    kg = jnp.ones((D_MODEL, 1), F32)
    g = g.reshape(1, D_MODEL)
    kw = dict(nq=D_MODEL, nk=D_MODEL, nv=D_MODEL, head_norm=False)
    tq = min(SB_TILE, n_p)
    qT, k32, k16, v32, vT, kmax = _project(xp, g, wt, qg, kg, vb=tq, kstat="absmax", slot=layer,
                                           n_slots=n_layers, earlier=earlier, kv_layout="feature", **kw)
    qT_s, k32_s, k16_s, v32_s, _ = _project(xs, g, wt, qg, kg, vb=LANES, **kw)
    k32_s, v32_s = k32_s[0], v32_s[0]
    o_p = _sb_prompt(qT, k16, vT, kmax)
    hm = functools.partial(_heads_major, nb=nb, t=t, nh=SB_HEADS, hd=HEAD_DIM)
    o_s = _sb_sample(hm(qT_s.T), cache_k, cache_v, layer, hm(k16_s), hm(v32_s))
    o_s = o_s.transpose(0, 2, 1, 3).reshape(nb * t, D_MODEL)
    return (o_p, o_s,
            k32, v32,
            k32_s.reshape(nb, t, SB_HEADS, HEAD_DIM), v32_s.reshape(nb, t, SB_HEADS, HEAD_DIM))


def kernel(x_prompt, x_sample, cache_diff_k, cache_diff_v, cache_swa_k, cache_swa_v, cache_sb_k, cache_sb_v, rel_bias, norm_mix, norm_mlp, w_up, w_down, diff_w_qkv, diff_w_o, diff_q_norm, diff_k_norm, diff_lambda, diff_subln, swa_w_qkv, swa_w_o, swa_q_norm, swa_k_norm, swa_sinks, sb_w_qkv, sb_w_o):
    bp, n_p, _ = x_prompt.shape
    assert bp == 1
    nb, t, _ = x_sample.shape
    depth = norm_mix.shape[0]
    xp = x_prompt.reshape(n_p, D_MODEL)
    xs = x_sample.reshape(nb * t, D_MODEL)
    n_diff, n_sb = cache_diff_k.shape[0], cache_sb_k.shape[0]
    outs = {name: [] for name in ("pwk", "pwv", "sdk", "sdv", "swk", "swv", "sbk", "sbv")}
    diff_kv = sb_kv = None
    diff_bias = _diff_bias_tiles(rel_bias, min(DIFF_TILE, n_p), t)
    wup16 = w_up.astype(BF16)
    wdn16 = w_down.astype(BF16)
    for i in range(depth):
        j = i // N_MIXERS
        if i % N_MIXERS == 0:
            lam_init = 0.8 - 0.6 * math.exp(-0.3 * i)
            o_p, o_s, kp, vp, kn, vn = _diff_layer(
                xp, xs, cache_diff_k, cache_diff_v, j, n_diff, diff_kv, diff_bias, norm_mix[i], diff_w_qkv[j],
                diff_q_norm[j], diff_k_norm[j], diff_lambda[j], diff_subln[j], lam_init)
            diff_kv = (kp, vp)
            w_o = diff_w_o[j]
            names = ("sdk", "sdv")
        elif i % N_MIXERS == 1:
            o_p, o_s, kp, vp, kn, vn = _swa_layer(
                xp, xs, cache_swa_k[j], cache_swa_v[j], rel_bias, norm_mix[i], swa_w_qkv[j],
                swa_q_norm[j], swa_k_norm[j], swa_sinks[j])
            outs["pwk"].append(kp)
            outs["pwv"].append(vp)
            w_o = swa_w_o[j]
            names = ("swk", "swv")
        else:
            o_p, o_s, kp, vp, kn, vn = _sb_layer(xp, xs, cache_sb_k, cache_sb_v, j, n_sb, sb_kv, norm_mix[i],
                                                 sb_w_qkv[j])
            sb_kv = (kp, vp)
            w_o = sb_w_o[j]
            names = ("sbk", "sbv")
        for name, val in zip(names, (kn, vn)):
            outs[name].append(val)
        wo16 = w_o.astype(BF16)
        g_mlp = norm_mlp[i].reshape(1, D_MODEL)
        xp, xs = _outproj_mlp(xp, o_p, xs, o_s, wo16, g_mlp, wup16, wdn16, i)
    st = {name: jnp.stack(v) for name, v in outs.items()}
    diff_shape = (n_diff, 1, n_p, DIFF_HEADS, 2 * HEAD_DIM)

    def sb_result(a):
        return a.reshape(n_sb, 1, SB_HEADS, HEAD_DIM, n_p).transpose(0, 1, 4, 2, 3)

    return (xp.reshape(1, n_p, D_MODEL), xs.reshape(nb, t, D_MODEL),
            diff_kv[0].reshape(diff_shape), diff_kv[1].reshape(diff_shape), st["pwk"], st["pwv"],
            sb_result(sb_kv[0]), sb_result(sb_kv[1]),
            st["sdk"], st["sdv"], st["swk"], st["swv"], st["sbk"], st["sbv"])
```
